```python
import math
import jax, jax.numpy as jnp
from jax import lax
import numpy as np

D_MODEL = 1024
BATCH = 2
SEQ = 8192
DEPTH = 2

D_MIX = D_MODEL
EPS = 1e-6
MLA_HEADS = 8
MLA_NOPE = 64
MLA_ROPE = 32
MLA_V = 64
MLA_Q_LORA = 256
MLA_KV_LORA = 128
MLA_WIDTH = MLA_HEADS * MLA_V
ROPE_THETA = 10000.0
Q_BLOCK = 128
HY_WIDTH = D_MIX // 4
HY_ORDER = 2
HY_SHORT = 3
HY_BANDS = 8
HY_POS_FEAT = 1 + 2 * HY_BANDS
HY_FILT_HID = 64
HY_DECAY_MIN = 3.07
HY_DECAY_MAX = 15.35
SSM_WIDTH = D_MIX // 4
SSM_HEADDIM = 64
SSM_HEADS = SSM_WIDTH // SSM_HEADDIM
SSM_GROUPS = 2
SSM_STATE = 128
SSM_CONV = 5
SSM_CHUNK = 128
SSM_CONV_DIM = SSM_WIDTH + 2 * SSM_GROUPS * SSM_STATE
N_EXPERTS = 16
EXPERT_FF = 2048
EC_CAPACITY_FACTOR = 2
PLE_DIM = 256
IN_SPLITS = (MLA_Q_LORA, MLA_KV_LORA, MLA_ROPE, (HY_ORDER + 1) * HY_WIDTH, SSM_WIDTH, SSM_CONV_DIM, 2 * SSM_HEADS)
N_IN = MLA_Q_LORA + MLA_KV_LORA + MLA_ROPE + (HY_ORDER + 1) * HY_WIDTH + SSM_WIDTH + SSM_CONV_DIM + 2 * SSM_HEADS

kernel_name = 'hybrid_mla_hyena_ssd_ecmoe_encoder'


def rmsnorm(x, g):
    xf = x.astype(jnp.float32)
    y = xf * lax.rsqrt(jnp.mean(xf * xf, axis=-1, keepdims=True) + EPS)
    return (y * g.astype(jnp.float32)).astype(x.dtype)


def split_cols(a, sizes):
    offs = [int(o) for o in np.cumsum(sizes)[:-1]]
    return jnp.split(a, offs, axis=-1)


def dwconv_centred(u, w, b):
    width = w.shape[0]
    pad = width // 2
    L = u.shape[1]
    up = jnp.pad(u, ((0, 0), (pad, pad), (0, 0)))
    out = up[:, 0:L] * w[0]
    for k in range(1, width):
        out = out + up[:, k:k + L] * w[k]
    return out + b


def apply_rope(t, cos, sin):
    half = t.shape[-1] // 2
    t1 = t[..., :half].astype(jnp.float32)
    t2 = t[..., half:].astype(jnp.float32)
    return jnp.concatenate([t1 * cos - t2 * sin, t2 * cos + t1 * sin], axis=-1).astype(t.dtype)


def mla_mixer(cq_raw, ckv_raw, kr_raw, positions, q_norm, w_uq, kv_norm, w_ukv):
    b, l, _ = cq_raw.shape
    f32 = jnp.float32
    q = (rmsnorm(cq_raw, q_norm) @ w_uq).reshape(b, l, MLA_HEADS, MLA_NOPE + MLA_ROPE)
    kv = (rmsnorm(ckv_raw, kv_norm) @ w_ukv).reshape(b, l, MLA_HEADS, MLA_NOPE + MLA_V)
    q_nope, q_rope = q[..., :MLA_NOPE], q[..., MLA_NOPE:]
    k_nope, v = kv[..., :MLA_NOPE], kv[..., MLA_NOPE:]
    inv_freq = ROPE_THETA ** (-jnp.arange(0, MLA_ROPE, 2, dtype=f32) / MLA_ROPE)
    ang = positions.astype(f32)[..., None] * inv_freq
    cos, sin = jnp.cos(ang), jnp.sin(ang)
    q_rope = apply_rope(q_rope, cos[:, :, None], sin[:, :, None])
    k_rope = apply_rope(kr_raw, cos, sin)
    scale = (MLA_NOPE + MLA_ROPE) ** -0.5
    nb = l // Q_BLOCK

    def blocks(t):
        return jnp.moveaxis(t.reshape(b, nb, Q_BLOCK, *t.shape[2:]), 1, 0)

    def attend(qb):
        qn, qr = qb
        s = jnp.einsum('bqhd,bkhd->bhqk', qn, k_nope) + jnp.einsum('bqhr,bkr->bhqk', qr, k_rope)
        w = jax.nn.softmax(s.astype(f32) * scale, axis=-1).astype(v.dtype)
        return jnp.einsum('bhqk,bkhd->bqhd', w, v)

    o = lax.map(attend, (blocks(q_nope), blocks(q_rope)))
    return jnp.moveaxis(o, 0, 1).reshape(b, l, MLA_WIDTH)


def hyena_filters(L, w1, b1, freq, w2, b2, w3, decay):
    f32 = jnp.float32
    t = jnp.arange(L, dtype=f32) / L
    bands = jnp.arange(1, HY_BANDS + 1, dtype=f32)
    ang = 2.0 * math.pi * t[:, None] * bands[None, :]
    feats = jnp.concatenate([t[:, None], jnp.sin(ang), jnp.cos(ang)], axis=-1)
    fr = freq.astype(f32)
    hdn = jnp.sin(fr * (feats @ w1.astype(f32) + b1.astype(f32)))
    hdn = jnp.sin(fr * (hdn @ w2.astype(f32) + b2.astype(f32)))
    filt = hdn @ w3.astype(f32)
    window = jnp.exp(-t[:, None] * jnp.abs(decay.astype(f32))[None, :])
    return (filt * window).reshape(L, HY_ORDER, 2, HY_WIDTH)


def fftconv_bidir(u, k_fwd, k_bwd, bias):
    b, L, c = u.shape
    k2 = jnp.concatenate([k_fwd, jnp.zeros((1, c), k_fwd.dtype), k_bwd[1:][::-1]], axis=0)
    kf = jnp.fft.rfft(k2, n=2 * L, axis=0)
    uf = jnp.fft.rfft(u.astype(jnp.float32), n=2 * L, axis=1)
    y = jnp.fft.irfft(uf * kf[None], n=2 * L, axis=1)[:, :L]
    return (y + u.astype(jnp.float32) * bias.astype(jnp.float32)).astype(u.dtype)


def hyena_mixer(u_raw, conv_w, conv_b, w1, b1, freq, w2, b2, w3, decay, bias):
    L = u_raw.shape[1]
    u = dwconv_centred(u_raw, conv_w, conv_b)
    parts = jnp.split(u, HY_ORDER + 1, axis=-1)
    gates, v = parts[:-1], parts[-1]
    filt = hyena_filters(L, w1, b1, freq, w2, b2, w3, decay)
    for o in range(HY_ORDER):
        v = fftconv_bidir(v, filt[:, o, 0], filt[:, o, 1], bias[o]) * gates[o]
    return v


def ssd_scan(x, dt, a, bm, cm):
    f32 = jnp.float32
    b, l, h, pdim = x.shape
    n = bm.shape[-1]
    q = SSM_CHUNK
    nc = l // q
    xdt = (x.astype(f32) * dt[..., None]).reshape(b, nc, q, h, pdim)
    bc = bm.astype(f32).reshape(b, nc, q, h, n)
    cc = cm.astype(f32).reshape(b, nc, q, h, n)
    a_cs = jnp.cumsum((dt * a).reshape(b, nc, q, h), axis=2)
    seg = a_cs[:, :, :, None, :] - a_cs[:, :, None, :, :]
    lower = jnp.tril(jnp.ones((q, q), dtype=bool))[None, None, :, :, None]
    decay_in = jnp.exp(jnp.where(lower, seg, -jnp.inf))
    scores = jnp.einsum('bclhn,bcshn->bclsh', cc, bc) * decay_in
    y_diag = jnp.einsum('bclsh,bcshp->bclhp', scores, xdt)
    decay_to_end = jnp.exp(a_cs[:, :, -1:, :] - a_cs)
    chunk_states = jnp.einsum('bcshn,bcshp->bchpn', bc * decay_to_end[..., None], xdt)
    chunk_decay = jnp.exp(a_cs[:, :, -1, :])

    def step(state, inp):
        st, dec = inp
        return state * dec[..., None, None] + st, state

    init = jnp.zeros((b, h, pdim, n), f32)
    _, prev = lax.scan(step, init, (jnp.moveaxis(chunk_states, 1, 0), jnp.moveaxis(chunk_decay, 1, 0)))
    prev = jnp.moveaxis(prev, 0, 1)
    y_off = jnp.einsum('bclhn,bchpn->bclhp', cc * jnp.exp(a_cs)[..., None], prev)
    return (y_diag + y_off).reshape(b, l, h, pdim)


def ssd_mixer(z, xbc_raw, dt_raw, conv_w, conv_b, dt_bias, a_log, d_skip, norm_g):
    f32 = jnp.float32
    b, l, _ = z.shape
    xbc = jax.nn.silu(dwconv_centred(xbc_raw, conv_w, conv_b))
    xs, bm, cm = split_cols(xbc, (SSM_WIDTH, SSM_GROUPS * SSM_STATE, SSM_GROUPS * SSM_STATE))
    xs = xs.reshape(b, l, SSM_HEADS, SSM_HEADDIM)
    rep = SSM_HEADS // SSM_GROUPS
    bm = jnp.repeat(bm.reshape(b, l, SSM_GROUPS, SSM_STATE), rep, axis=2)
    cm = jnp.repeat(cm.reshape(b, l, SSM_GROUPS, SSM_STATE), rep, axis=2)
    dt = jax.nn.softplus(dt_raw.astype(f32).reshape(b, l, 2, SSM_HEADS) + dt_bias.astype(f32))
    a = -jnp.exp(a_log.astype(f32))
    y_f = ssd_scan(xs, dt[:, :, 0], a[0], bm, cm)
    flip = lambda t: jnp.flip(t, axis=1)
    y_b = flip(ssd_scan(flip(xs), flip(dt[:, :, 1]), a[1], flip(bm), flip(cm)))
    y = y_f + y_b + xs.astype(f32) * d_skip.astype(f32)[:, None]
    y = y.reshape(b, l, SSM_WIDTH) * jax.nn.silu(z.astype(f32))
    y = rmsnorm(y.reshape(b, l, SSM_GROUPS, SSM_WIDTH // SSM_GROUPS), norm_g.reshape(SSM_GROUPS, -1))
    return y.reshape(b, l, SSM_WIDTH).astype(z.dtype)


def ec_moe(h, router, w_gate, w_up, w_down):
    b, l, d = h.shape
    cap = EC_CAPACITY_FACTOR * l // N_EXPERTS
    aff = jax.nn.softmax((h @ router).astype(jnp.float32), axis=-1)
    g, idx = lax.top_k(jnp.swapaxes(aff, 1, 2), cap)
    xe = jax.vmap(lambda hb, ib: hb[ib])(h, idx)
    a = jnp.einsum('becd,edf->becf', xe, w_gate)
    u = jnp.einsum('becd,edf->becf', xe, w_up)
    ye = jnp.einsum('becf,efd->becd', jax.nn.silu(a) * u, w_down) * g[..., None].astype(h.dtype)
    return jax.vmap(lambda ib, yb: jnp.zeros((l, d), yb.dtype).at[ib.reshape(-1)].add(yb.reshape(-1, d)))(idx, ye)


def setup_inputs(seed: int = 0) -> dict:
    key = jax.random.key(seed)
    nk = iter(jax.random.split(key, 48))
    f32 = jnp.float32

    def nrm(shape, scale):
        return jax.random.normal(next(nk), shape, f32) * scale

    def gain(shape):
        return 1.0 + nrm(shape, 0.05)

    dt0 = jnp.exp(jax.random.uniform(next(nk), (DEPTH, 2, SSM_HEADS), f32, math.log(1e-3), math.log(1e-1)))
    decay0 = jnp.tile(jnp.linspace(HY_DECAY_MIN, HY_DECAY_MAX, HY_WIDTH, dtype=f32), HY_ORDER * 2)
    return {
        'x': nrm((BATCH, SEQ, D_MODEL), 1.0),
        'p': nrm((DEPTH, BATCH, SEQ, PLE_DIM), 1.0),
        'positions': jnp.broadcast_to(jnp.arange(SEQ, dtype=jnp.int32), (BATCH, SEQ)),
        'norm_mix': gain((DEPTH, D_MODEL)),
        'w_in': nrm((DEPTH, D_MODEL, N_IN), D_MODEL ** -0.5),
        'mla_q_norm': gain((DEPTH, MLA_Q_LORA)),
        'mla_w_uq': nrm((DEPTH, MLA_Q_LORA, MLA_HEADS * (MLA_NOPE + MLA_ROPE)), MLA_Q_LORA ** -0.5),
        'mla_kv_norm': gain((DEPTH, MLA_KV_LORA)),
        'mla_w_ukv': nrm((DEPTH, MLA_KV_LORA, MLA_HEADS * (MLA_NOPE + MLA_V)), MLA_KV_LORA ** -0.5),
        'mla_out_norm': gain((DEPTH, MLA_WIDTH)),
        'hy_conv_w': nrm((DEPTH, HY_SHORT, (HY_ORDER + 1) * HY_WIDTH), HY_SHORT ** -0.5),
        'hy_conv_b': nrm((DEPTH, (HY_ORDER + 1) * HY_WIDTH), 0.01),
        'hy_filt_w1': nrm((DEPTH, HY_POS_FEAT, HY_FILT_HID), HY_POS_FEAT ** -0.5),
        'hy_filt_b1': nrm((DEPTH, HY_FILT_HID), 0.01),
        'hy_filt_freq': gain((DEPTH, HY_FILT_HID)),
        'hy_filt_w2': nrm((DEPTH, HY_FILT_HID, HY_FILT_HID), HY_FILT_HID ** -0.5),
        'hy_filt_b2': nrm((DEPTH, HY_FILT_HID), 0.01),
        'hy_filt_w3': nrm((DEPTH, HY_FILT_HID, HY_ORDER * 2 * HY_WIDTH), HY_FILT_HID ** -0.5),
        'hy_decay': decay0[None, :] + nrm((DEPTH, HY_ORDER * 2 * HY_WIDTH), 0.1),
        'hy_bias': nrm((DEPTH, HY_ORDER, HY_WIDTH), 0.1),
        'hy_out_norm': gain((DEPTH, HY_WIDTH)),
        'ssm_conv_w': nrm((DEPTH, SSM_CONV, SSM_CONV_DIM), SSM_CONV ** -0.5),
        'ssm_conv_b': nrm((DEPTH, SSM_CONV_DIM), 0.01),
        'ssm_dt_bias': dt0 + jnp.log(-jnp.expm1(-dt0)),
        'ssm_a_log': jnp.log(jax.random.uniform(next(nk), (DEPTH, 2, SSM_HEADS), f32, 1.0, 16.0)),
        'ssm_d': gain((DEPTH, SSM_HEADS)),
        'ssm_norm': gain((DEPTH, SSM_WIDTH)),
        'w_out': nrm((DEPTH, D_MIX, D_MODEL), D_MIX ** -0.5),
        'norm_ffn': gain((DEPTH, D_MODEL)),
        'moe_router': nrm((DEPTH, D_MODEL, N_EXPERTS), D_MODEL ** -0.5),
        'moe_w_gate': nrm((DEPTH, N_EXPERTS, D_MODEL, EXPERT_FF), D_MODEL ** -0.5),
        'moe_w_up': nrm((DEPTH, N_EXPERTS, D_MODEL, EXPERT_FF), D_MODEL ** -0.5),
        'moe_w_down': nrm((DEPTH, N_EXPERTS, EXPERT_FF, D_MODEL), EXPERT_FF ** -0.5),
        'ple_norm': gain((DEPTH, D_MODEL)),
        'ple_gate_w': nrm((DEPTH, D_MODEL, D_MODEL), D_MODEL ** -0.5),
        'ple_proj': nrm((DEPTH, PLE_DIM, D_MODEL), PLE_DIM ** -0.5),
        'final_norm': gain((D_MODEL,)),
    }


def reference(x, p, positions, norm_mix, w_in, mla_q_norm, mla_w_uq, mla_kv_norm, mla_w_ukv, mla_out_norm,
              hy_conv_w, hy_conv_b, hy_filt_w1, hy_filt_b1, hy_filt_freq, hy_filt_w2, hy_filt_b2, hy_filt_w3,
              hy_decay, hy_bias, hy_out_norm, ssm_conv_w, ssm_conv_b, ssm_dt_bias, ssm_a_log, ssm_d, ssm_norm,
              w_out, norm_ffn, moe_router, moe_w_gate, moe_w_up, moe_w_down, ple_norm, ple_gate_w, ple_proj,
              final_norm):
    for i in range(DEPTH):
        h = rmsnorm(x, norm_mix[i])
        cq, ckv, kr, hy_u, z, xbc, dt_raw = split_cols(h @ w_in[i], IN_SPLITS)
        o_mla = rmsnorm(mla_mixer(cq, ckv, kr, positions, mla_q_norm[i], mla_w_uq[i], mla_kv_norm[i], mla_w_ukv[i]),
                        mla_out_norm[i])
        o_hy = rmsnorm(hyena_mixer(hy_u, hy_conv_w[i], hy_conv_b[i], hy_filt_w1[i], hy_filt_b1[i], hy_filt_freq[i],
                                   hy_filt_w2[i], hy_filt_b2[i], hy_filt_w3[i], hy_decay[i], hy_bias[i]),
                       hy_out_norm[i])
        o_ssm = ssd_mixer(z, xbc, dt_raw, ssm_conv_w[i], ssm_conv_b[i], ssm_dt_bias[i], ssm_a_log[i], ssm_d[i],
                          ssm_norm[i])
        x = x + jnp.concatenate([o_mla, o_hy, o_ssm], axis=-1) @ w_out[i]
        x = x + ec_moe(rmsnorm(x, norm_ffn[i]), moe_router[i], moe_w_gate[i], moe_w_up[i], moe_w_down[i])
        gate = jax.nn.sigmoid(rmsnorm(x, ple_norm[i]) @ ple_gate_w[i])
        x = x + (p[i] @ ple_proj[i]) * gate
    return rmsnorm(x, final_norm)
```

```python
import functools
import math

import numpy as np
import jax
import jax.numpy as jnp
from jax import lax
from jax.experimental import pallas as pl
from jax.experimental.pallas import tpu as pltpu

F32 = jnp.float32
BF16 = jnp.bfloat16
I32 = jnp.int32
HIGHEST = lax.Precision.HIGHEST

EPS = 1e-6
LANES = 128
SUBLANES = 8
VMEM_LIMIT = 56 * 1024 * 1024

D_MODEL = 1024
MLA_HEADS = 8
MLA_NOPE = 64
MLA_ROPE = 32
MLA_V = 64
MLA_Q_LORA = 256
MLA_KV_LORA = 128
HEAD_PAD = 128
ROPE_THETA = 10000.0
HY_WIDTH = 256
HY_ORDER = 2
HY_SHORT = 3
HY_BANDS = 8
HY_FILT_HID = 64
SSM_WIDTH = 256
SSM_HEADDIM = 64
SSM_HEADS = 4
SSM_GROUPS = 2
SSM_STATE = 128
SSM_CONV = 5
SSM_CHUNK = 128
SSM_CONV_DIM = 768
N_EXPERTS = 16
EXPERT_FF = 2048
EC_CAPACITY_FACTOR = 2
PLE_DIM = 256
FFT_N2 = 128
MANTISSA_STEPS = 40
IN_SPLITS = (MLA_Q_LORA, MLA_KV_LORA, MLA_ROPE, 3 * HY_WIDTH, SSM_WIDTH, SSM_CONV_DIM, 2 * SSM_HEADS)


def _cparams(sem, vmem=None):
    return pltpu.CompilerParams(dimension_semantics=sem, vmem_limit_bytes=vmem or VMEM_LIMIT)


def _rms(x, g):
    ms = jnp.mean(x * x, axis=-1, keepdims=True)
    return x * lax.rsqrt(ms + EPS) * g


def _dot(a, b, precision=None):
    return jnp.dot(a, b, preferred_element_type=F32, precision=precision)


def _dot_nt(a, b, precision=None):
    return lax.dot_general(a, b, (((1,), (1,)), ((), ())), preferred_element_type=F32, precision=precision)


def _dot_tn(a, b, precision=None):
    return lax.dot_general(a, b, (((0,), (0,)), ((), ())), preferred_element_type=F32, precision=precision)


def _full(shape):
    n = len(shape)
    return pl.BlockSpec(shape, lambda *_: (0,) * n)


def _rope_kernel(pos_ref, freq_ref, cos_ref, sin_ref):
    ang = pos_ref[...].astype(F32) * freq_ref[...]
    cos_ref[...] = jnp.cos(ang)
    sin_ref[...] = jnp.sin(ang)


def rope_tables(pos_col, freq_row, tm):
    t = pos_col.shape[0]
    return pl.pallas_call(
        _rope_kernel,
        grid=(t // tm,),
        in_specs=[pl.BlockSpec((tm, 1), lambda i: (i, 0)), _full((1, HEAD_PAD))],
        out_specs=[pl.BlockSpec((tm, HEAD_PAD), lambda i: (i, 0))] * 2,
        out_shape=[jax.ShapeDtypeStruct((t, HEAD_PAD), F32)] * 2,
        compiler_params=_cparams(("parallel",)),
        name="rope_tables",
    )(pos_col, freq_row)


_C_CQ = 0
_C_CKV = 256
_C_KR = 384
_C_KRS = 512
_C_HY = 640
_C_Z = 1408
_C_XBC = 1664
_C_DT = 2432
_C_END = 2560


def _inproj_kernel(x_ref, gmix_ref, wall_ref, qn_ref, wq_ref, wqs_ref, kvn_ref, wk_ref, wv_ref, cos_ref, sin_ref,
                   q_ref, k_ref, v_ref, hy_ref, z_ref, xbc_ref, dt_ref, *, scale):
    h = _rms(x_ref[...], gmix_ref[...]).astype(BF16)
    proj = _dot(h, wall_ref[...])
    hy_ref[...] = proj[:, _C_HY:_C_Z]
    z_ref[...] = proj[:, _C_Z:_C_XBC]
    xbc_ref[...] = proj[:, _C_XBC:_C_DT]
    dt_ref[...] = proj[:, _C_DT:_C_END]
    cos = cos_ref[...]
    sin = sin_ref[...]
    cos8 = jnp.concatenate([cos] * MLA_HEADS, axis=-1)
    sin8 = jnp.concatenate([sin] * MLA_HEADS, axis=-1)
    cqn = _rms(proj[:, _C_CQ:_C_CKV], qn_ref[...]).astype(BF16)
    q = _dot(cqn, wq_ref[...])
    qs = _dot(cqn, wqs_ref[...])
    q_ref[...] = ((q * cos8 + qs * sin8) * scale).astype(BF16)
    ckvn = _rms(proj[:, _C_CKV:_C_KR], kvn_ref[...]).astype(BF16)
    kn = _dot(ckvn, wk_ref[...])
    v_ref[...] = _dot(ckvn, wv_ref[...]).astype(BF16)
    kr = proj[:, _C_KR:_C_KRS] * cos + proj[:, _C_KRS:_C_HY] * sin
    k_ref[...] = (kn + jnp.concatenate([kr] * MLA_HEADS, axis=-1)).astype(BF16)


def inproj(x2d, gmix, wall, qn, wq, wqs, kvn, wk, wv, cos_t, sin_t, tm):
    t = x2d.shape[0]
    hq = MLA_HEADS * HEAD_PAD
    row = lambda w: pl.BlockSpec((tm, w), lambda i: (i, 0))
    outs = [(hq, BF16), (hq, BF16), (MLA_HEADS * MLA_V, BF16), (3 * HY_WIDTH, F32), (SSM_WIDTH, F32),
            (SSM_CONV_DIM, F32), (LANES, F32)]
    return pl.pallas_call(
        functools.partial(_inproj_kernel, scale=(MLA_NOPE + MLA_ROPE) ** -0.5),
        grid=(t // tm,),
        in_specs=[row(D_MODEL), _full(gmix.shape), _full(wall.shape), _full(qn.shape), _full(wq.shape),
                  _full(wqs.shape), _full(kvn.shape), _full(wk.shape), _full(wv.shape), row(HEAD_PAD), row(HEAD_PAD)],
        out_specs=[row(w) for w, _ in outs],
        out_shape=[jax.ShapeDtypeStruct((t, w), d) for w, d in outs],
        compiler_params=_cparams(("parallel",)),
        name="inproj",
    )(x2d, gmix, wall, qn, wq, wqs, kvn, wk, wv, cos_t, sin_t)


def _attn_kernel(q_ref, k_ref, v_ref, o_ref, *, tk):
    seq = k_ref.shape[1]
    tq = q_ref.shape[1]
    outs = []
    for hh in range(2):
        q = q_ref[0, :, hh * HEAD_PAD:(hh + 1) * HEAD_PAD]

        def body(c, carry, hh=hh, q=q):
            m, l, acc = carry
            off = pl.multiple_of(c * tk, tk)
            kc = k_ref[0, pl.ds(off, tk), hh * HEAD_PAD:(hh + 1) * HEAD_PAD]
            vc = v_ref[0, pl.ds(off, tk), hh * MLA_V:(hh + 1) * MLA_V]
            s = _dot_nt(q, kc)
            m_new = jnp.maximum(m, jnp.max(s, axis=-1, keepdims=True))
            alpha = jnp.exp(m - m_new)
            p = jnp.exp(s - m_new)
            l = l * alpha + jnp.sum(p, axis=-1, keepdims=True)
            acc = acc * alpha + _dot(p.astype(BF16), vc)
            return m_new, l, acc

        init = (jnp.full((tq, 1), -jnp.inf, F32), jnp.zeros((tq, 1), F32), jnp.zeros((tq, MLA_V), F32))
        _, l, acc = lax.fori_loop(0, seq // tk, body, init)
        outs.append(acc / l)
    o_ref[0] = jnp.concatenate(outs, axis=-1)


def attention(q, k, v, tq, tk):
    b, seq, _ = q.shape
    return pl.pallas_call(
        functools.partial(_attn_kernel, tk=tk),
        grid=(b, MLA_HEADS // 2, seq // tq),
        in_specs=[pl.BlockSpec((1, tq, 2 * HEAD_PAD), lambda bi, hp, qi: (bi, qi, hp)),
                  pl.BlockSpec((1, seq, 2 * HEAD_PAD), lambda bi, hp, qi: (bi, 0, hp)),
                  pl.BlockSpec((1, seq, 2 * MLA_V), lambda bi, hp, qi: (bi, 0, hp))],
        out_specs=pl.BlockSpec((1, tq, 2 * MLA_V), lambda bi, hp, qi: (bi, qi, hp)),
        out_shape=jax.ShapeDtypeStruct((b, seq, MLA_HEADS * MLA_V), F32),
        compiler_params=_cparams(("parallel", "parallel", "parallel")),
        name="attention",
    )(q, k, v)


def _dwconv_kernel(x_ref, w_ref, b_ref, o_ref, *, width, act, rows):
    seq = x_ref.shape[1]
    pad = width // 2
    nchunks = seq // rows
    w = w_ref[...]
    bias = b_ref[...]

    def body(c, carry):
        r0 = pl.multiple_of(c * rows, rows)
        cur = x_ref[0, pl.ds(r0, rows), :]
        p0 = pl.multiple_of(jnp.maximum(r0 - SUBLANES, 0), SUBLANES)
        n0 = pl.multiple_of(jnp.minimum(r0 + rows, seq - SUBLANES), SUBLANES)
        prev = jnp.where(c > 0, x_ref[0, pl.ds(p0, SUBLANES), :], 0.0)
        nxt = jnp.where(c < nchunks - 1, x_ref[0, pl.ds(n0, SUBLANES), :], 0.0)
        ext = jnp.concatenate([prev, cur, nxt], axis=0)
        acc = bias + ext[SUBLANES - pad:SUBLANES - pad + rows] * w[0:1]
        for kk in range(1, width):
            s0 = SUBLANES - pad + kk
            acc = acc + ext[s0:s0 + rows] * w[kk:kk + 1]
        if act:
            acc = acc * jax.nn.sigmoid(acc)
        o_ref[0, pl.ds(r0, rows), :] = acc
        return carry

    lax.fori_loop(0, nchunks, body, 0)


def dwconv(x, w, bias, col0, ncols, act, rows):
    b, seq, _ = x.shape
    width = w.shape[0]
    cb0 = col0 // LANES
    return pl.pallas_call(
        functools.partial(_dwconv_kernel, width=width, act=act, rows=min(rows, seq)),
        grid=(b, ncols // LANES),
        in_specs=[pl.BlockSpec((1, seq, LANES), lambda bi, ci: (bi, 0, ci + cb0)),
                  pl.BlockSpec((width, LANES), lambda bi, ci: (0, ci)),
                  pl.BlockSpec((1, LANES), lambda bi, ci: (0, ci))],
        out_specs=pl.BlockSpec((1, seq, LANES), lambda bi, ci: (bi, 0, ci)),
        out_shape=jax.ShapeDtypeStruct((b, seq, ncols), F32),
        compiler_params=_cparams(("parallel", "parallel")),
        name="dwconv",
    )(x, w, bias)


def _hyfilt_kernel(mult_ref, w1_ref, b1_ref, fr_ref, w2_ref, b2_ref, w3_ref, dec_ref, bwd_ref, o_ref, *, seq, rows):
    i = pl.program_id(0)
    ridx = lax.broadcasted_iota(I32, (rows, LANES), 0) + i * rows
    lane = lax.broadcasted_iota(I32, (rows, LANES), 1)
    t = ridx.astype(F32) / seq
    ang = t * mult_ref[...]
    feats = jnp.where(lane == 0, t, jnp.where(lane <= HY_BANDS, jnp.sin(ang), jnp.cos(ang)))
    feats = jnp.where(lane < 1 + 2 * HY_BANDS, feats, 0.0)
    fr = fr_ref[...]
    hdn = jnp.sin(fr * (_dot(feats, w1_ref[...], HIGHEST) + b1_ref[...]))
    hdn = jnp.sin(fr * (_dot(hdn, w2_ref[...], HIGHEST) + b2_ref[...]))
    filt = _dot(hdn, w3_ref[...], HIGHEST)
    window = jnp.exp(-t[:, 0:1] * jnp.abs(dec_ref[...]))
    out = filt * window
    keep = jnp.logical_or(ridx[:, 0:1] > 0, bwd_ref[...] < 0.5)
    o_ref[...] = jnp.where(keep, out, 0.0)


def hyena_filters(seq, mult, w1p, b1, fr, w2, b2, w3, dec, bwd_mask, rows):
    ncol = w3.shape[1]
    rows = min(rows, seq)
    args = (mult, w1p, b1, fr, w2, b2, w3, dec, bwd_mask)
    return pl.pallas_call(
        functools.partial(_hyfilt_kernel, seq=seq, rows=rows),
        grid=(seq // rows,),
        in_specs=[_full(a.shape) for a in args],
        out_specs=pl.BlockSpec((rows, ncol), lambda i: (i, 0)),
        out_shape=jax.ShapeDtypeStruct((seq, ncol), F32),
        compiler_params=_cparams(("parallel",)),
        name="hyena_filters",
    )(*args)


def _leftmm_kernel(m_ref, x_ref, o_ref):
    o_ref[...] = _dot(m_ref[...], x_ref[...], HIGHEST)


def leftmm(m, x2d, cb):
    r, kdim = m.shape
    n = x2d.shape[1]
    cb = min(cb, n)
    return pl.pallas_call(
        _leftmm_kernel,
        grid=(n // cb,),
        in_specs=[_full((r, kdim)), pl.BlockSpec((kdim, cb), lambda i: (0, i))],
        out_specs=pl.BlockSpec((r, cb), lambda i: (0, i)),
        out_shape=jax.ShapeDtypeStruct((r, n), F32),
        compiler_params=_cparams(("parallel",)),
        name="dft_outer",
    )(m, x2d)


def _inner_mats(f2r_ref, f2i_ref, twr, twi):
    f2r = f2r_ref[...]
    f2i = f2i_ref[...]
    gr = f2r * twr - f2i * twi
    gi = f2r * twi + f2i * twr
    return gr, gi


def _specfilt_kernel(a_ref, f2r_ref, f2i_ref, twr_ref, twi_ref, o_ref):
    n2 = FFT_N2
    c2 = a_ref.shape[-1]
    gr, gi = _inner_mats(f2r_ref, f2i_ref, twr_ref[0], twi_ref[0])
    gblk = jnp.concatenate([jnp.concatenate([gr, -gi], axis=1), jnp.concatenate([gi, gr], axis=1)], axis=0)
    a = a_ref[:, 0].reshape(2 * n2, c2)
    x = _dot(gblk, a, HIGHEST)
    c = c2 // 2
    o_ref[0, 0, 0] = x[:n2, :c] + x[:n2, c:]
    o_ref[0, 1, 0] = x[n2:, :c] - x[n2:, c:]


def filter_spectrum(a4, f2r, f2i, twr_row, twi_row):
    _, n1, n2, ctot = a4.shape
    c = HY_WIDTH
    return pl.pallas_call(
        _specfilt_kernel,
        grid=(HY_ORDER, n1),
        in_specs=[pl.BlockSpec((2, 1, n2, 2 * c), lambda o, k: (0, k, 0, o)),
                  _full((n2, n2)), _full((n2, n2)),
                  pl.BlockSpec((1, 1, n2), lambda o, k: (k, 0, 0)),
                  pl.BlockSpec((1, 1, n2), lambda o, k: (k, 0, 0))],
        out_specs=pl.BlockSpec((1, 2, 1, n2, c), lambda o, k: (o, 0, k, 0, 0)),
        out_shape=jax.ShapeDtypeStruct((HY_ORDER, 2, n1, n2, c), F32),
        compiler_params=_cparams(("parallel", "parallel")),
        name="filter_spectrum",
    )(a4, f2r, f2i, twr_row, twi_row)


def _specmul_kernel(a_ref, k_ref, f2r_ref, f2i_ref, twr_ref, twi_ref, twrc_ref, twic_ref, o_ref):
    n2 = FFT_N2
    c = a_ref.shape[-1]
    gr, gi = _inner_mats(f2r_ref, f2i_ref, twr_ref[0], twi_ref[0])
    gblk = jnp.concatenate([jnp.concatenate([gr, -gi], axis=1), jnp.concatenate([gi, gr], axis=1)], axis=0)
    grt, git = _inner_mats(f2r_ref, f2i_ref, twrc_ref[0], twic_ref[0])
    gblk_t = jnp.concatenate([jnp.concatenate([grt, git], axis=1), jnp.concatenate([-git, grt], axis=1)], axis=0)
    a = a_ref[:, 0].reshape(2 * n2, c)
    x = _dot(gblk, a, HIGHEST)
    xr, xi = x[:n2], x[n2:]
    kr, ki = k_ref[0, 0, 0], k_ref[0, 1, 0]
    p = jnp.concatenate([xr * kr - xi * ki, xr * ki + xi * kr], axis=0)
    y = _dot(gblk_t, p, HIGHEST)
    o_ref[0, 0] = y[:n2]
    o_ref[1, 0] = y[n2:]


def spectrum_multiply(a4, kspec, order, f2r, f2i, twr_row, twi_row, twr_col, twi_col):
    _, n1, n2, c = a4.shape
    return pl.pallas_call(
        _specmul_kernel,
        grid=(n1,),
        in_specs=[pl.BlockSpec((2, 1, n2, c), lambda k: (0, k, 0, 0)),
                  pl.BlockSpec((1, 2, 1, n2, c), lambda k: (order, 0, k, 0, 0)),
                  _full((n2, n2)), _full((n2, n2)),
                  pl.BlockSpec((1, 1, n2), lambda k: (k, 0, 0)),
                  pl.BlockSpec((1, 1, n2), lambda k: (k, 0, 0)),
                  pl.BlockSpec((1, n2, 1), lambda k: (k, 0, 0)),
                  pl.BlockSpec((1, n2, 1), lambda k: (k, 0, 0))],
        out_specs=pl.BlockSpec((2, 1, n2, c), lambda k: (0, k, 0, 0)),
        out_shape=jax.ShapeDtypeStruct((2, n1, n2, c), F32),
        compiler_params=_cparams(("parallel",)),
        name="spectrum_multiply",
    )(a4, kspec, f2r, f2i, twr_row, twi_row, twr_col, twi_col)


def _convout_kernel(m_ref, y_ref, v_ref, g_ref, bias_ref, o_ref, *, reps):
    y = _dot(m_ref[...], y_ref[...], HIGHEST)
    bias = jnp.concatenate([bias_ref[...]] * reps, axis=-1)
    v = v_ref[...]
    o_ref[...] = (y + v * bias) * g_ref[...]


def conv_output(m3, y2d, v2d, g2d, bias_row, cb):
    r, kdim = m3.shape
    n = y2d.shape[1]
    cb = min(cb, n)
    c = bias_row.shape[1]
    return pl.pallas_call(
        functools.partial(_convout_kernel, reps=cb // c),
        grid=(n // cb,),
        in_specs=[_full((r, kdim)), pl.BlockSpec((kdim, cb), lambda i: (0, i)),
                  pl.BlockSpec((r, cb), lambda i: (0, i)), pl.BlockSpec((r, cb), lambda i: (0, i)),
                  _full((1, c))],
        out_specs=pl.BlockSpec((r, cb), lambda i: (0, i)),
        out_shape=jax.ShapeDtypeStruct((r, n), F32),
        compiler_params=_cparams(("parallel",)),
        name="dft_outer_inverse",
    )(m3, y2d, v2d, g2d, bias_row)


def _dft_tables(seq, batch):
    n2 = FFT_N2
    half = seq // n2
    n1 = 2 * half
    n = n1 * n2
    k1 = np.arange(n1, dtype=np.float64)[:, None]
    nn1 = np.arange(half, dtype=np.float64)[None, :]
    th = 2.0 * np.pi * k1 * nn1 / n1
    c1, s1 = np.cos(th), np.sin(th)
    assert batch == 2, "the two batch entries are packed as real / imaginary parts"
    m1 = np.block([[c1, s1], [-s1, c1]])
    m3 = np.block([[c1.T, -s1.T], [s1.T, c1.T]]) / n
    mk = np.concatenate([c1, -s1], axis=0)
    kk2 = np.arange(n2, dtype=np.float64)
    th2 = 2.0 * np.pi * np.outer(kk2, kk2) / n2
    f2r, f2i = np.cos(th2), -np.sin(th2)
    tht = 2.0 * np.pi * np.outer(np.arange(n1, dtype=np.float64), kk2) / n
    twr, twi = np.cos(tht), -np.sin(tht)
    f = lambda a: jnp.asarray(a, F32)
    return dict(m1=f(m1), m3=f(m3), mk=f(mk), f2r=f(f2r), f2i=f(f2i),
                twr_row=f(twr[:, None, :]), twi_row=f(twi[:, None, :]),
                twr_col=f(twr[:, :, None]), twi_col=f(twi[:, :, None]), n1=n1, half=half)


def _ssd_kernel(xf_ref, bf_ref, cf_ref, dtf_ref, xb_ref, bb_ref, cb_ref, dtb_ref, dtbias_ref, a_ref, tril_ref,
                yf_ref, yb_ref, state_ref):
    q = SSM_CHUNK
    hd = SSM_HEADDIM
    ns = SSM_STATE

    @pl.when(pl.program_id(1) == 0)
    def _():
        state_ref[...] = jnp.zeros_like(state_ref)

    tril = tril_ref[...]
    rows = lax.broadcasted_iota(I32, (q, q), 0)
    cols = lax.broadcasted_iota(I32, (q, q), 1)
    a_row = a_ref[...]
    bias = dtbias_ref[...]

    def direction(x_ref, b_ref, c_ref, dt_ref, y_ref, d):
        dt = jax.nn.softplus(dt_ref[0] + bias)
        dta = dt * a_row
        cs = _dot(tril, dta, HIGHEST)
        ecs = cs - dta
        base = ecs if d else cs
        base_t = jnp.transpose(base)
        total = cs[q - 1:q, :]
        x = x_ref[0]
        ys = []
        for g in range(SSM_GROUPS):
            bm = b_ref[0, :, g * ns:(g + 1) * ns]
            cm = c_ref[0, :, g * ns:(g + 1) * ns]
            cb = _dot_nt(cm.astype(BF16), bm.astype(BF16))
            for hh in range(SSM_HEADS // SSM_GROUPS):
                h = g * (SSM_HEADS // SSM_GROUPS) + hh
                j = d * SSM_HEADS + h
                col = base[:, j:j + 1]
                row = base_t[j:j + 1, :]
                tot = total[:, j:j + 1]
                if d == 0:
                    seg = jnp.where(rows >= cols, col - row, -jnp.inf)
                    c_scale = jnp.exp(col)
                    b_scale = jnp.exp(tot - col)
                else:
                    seg = jnp.where(cols >= rows, row - col, -jnp.inf)
                    c_scale = jnp.exp(tot - col)
                    b_scale = jnp.exp(col)
                scores = cb * jnp.exp(seg)
                xdt = x[:, h * hd:(h + 1) * hd] * dt[:, j:j + 1]
                st = state_ref[j]
                y = _dot(scores.astype(BF16), xdt.astype(BF16)) + _dot((cm * c_scale).astype(BF16), st.astype(BF16))
                state_ref[j] = st * jnp.exp(tot) + _dot_tn((bm * b_scale).astype(BF16), xdt.astype(BF16))
                ys.append(y)
        y_ref[0] = jnp.concatenate(ys, axis=-1)

    direction(xf_ref, bf_ref, cf_ref, dtf_ref, yf_ref, 0)
    direction(xb_ref, bb_ref, cb_ref, dtb_ref, yb_ref, 1)


def ssd_scan(xbc, dt_raw, dtbias_row, a_row, tril):
    b, seq, _ = xbc.shape
    q = SSM_CHUNK
    nc = seq // q
    w = SSM_WIDTH
    fwd = lambda col: pl.BlockSpec((1, q, w), lambda bi, i: (bi, i, col))
    bwd = lambda col: pl.BlockSpec((1, q, w), lambda bi, i: (bi, nc - 1 - i, col))
    return pl.pallas_call(
        _ssd_kernel,
        grid=(b, nc),
        in_specs=[fwd(0), fwd(1), fwd(2), pl.BlockSpec((1, q, LANES), lambda bi, i: (bi, i, 0)),
                  bwd(0), bwd(1), bwd(2), pl.BlockSpec((1, q, LANES), lambda bi, i: (bi, nc - 1 - i, 0)),
                  _full((1, LANES)), _full((1, LANES)), _full((q, q))],
        out_specs=[pl.BlockSpec((1, q, w), lambda bi, i: (bi, i, 0)),
                   pl.BlockSpec((1, q, w), lambda bi, i: (bi, nc - 1 - i, 0))],
        out_shape=[jax.ShapeDtypeStruct((b, seq, w), F32)] * 2,
        scratch_shapes=[pltpu.VMEM((2 * SSM_HEADS, SSM_STATE, SSM_HEADDIM), F32)],
        compiler_params=_cparams(("parallel", "arbitrary")),
        name="ssd_scan",
    )(xbc, xbc, xbc, dt_raw, xbc, xbc, xbc, dt_raw, dtbias_row, a_row, tril)


def _outproj_kernel(om_ref, hy_ref, yf_ref, yb_ref, xs_ref, z_ref, x_ref, gm_ref, gh_ref, dsk_ref, gs_ref, wout_ref,
                    gffn_ref, rt_ref, x1_ref, hffn_ref, aff_ref):
    o1 = _rms(om_ref[...], gm_ref[...])
    o2 = _rms(hy_ref[...], gh_ref[...])
    z = z_ref[...]
    y = (yf_ref[...] + yb_ref[...] + xs_ref[...] * dsk_ref[...]) * (z * jax.nn.sigmoid(z))
    gw = SSM_WIDTH // SSM_GROUPS
    gs = gs_ref[...]
    o3 = jnp.concatenate([_rms(y[:, g * gw:(g + 1) * gw], gs[:, g * gw:(g + 1) * gw]) for g in range(SSM_GROUPS)],
                         axis=-1)
    mix = jnp.concatenate([o1, o2, o3], axis=-1).astype(BF16)
    x1 = x_ref[...] + _dot(mix, wout_ref[...])
    x1_ref[...] = x1
    hf = _rms(x1, gffn_ref[...])
    hffn_ref[...] = hf.astype(BF16)
    logits = _dot_nt(rt_ref[...], hf, HIGHEST)
    mx = jnp.max(logits, axis=0, keepdims=True)
    ex = jnp.exp(logits - mx)
    aff_ref[0] = ex / jnp.sum(ex, axis=0, keepdims=True)


def outproj(om, hy, yf, yb, xbc, z, x2d, gm, gh, dsk, gs, wout, gffn, router_t, batch, tm):
    t = x2d.shape[0]
    seq = t // batch
    nb = seq // tm
    row = lambda w: pl.BlockSpec((tm, w), lambda i: (i, 0))
    return pl.pallas_call(
        _outproj_kernel,
        grid=(t // tm,),
        in_specs=[row(om.shape[1]), row(HY_WIDTH), row(SSM_WIDTH), row(SSM_WIDTH), row(SSM_WIDTH), row(SSM_WIDTH),
                  row(D_MODEL), _full(gm.shape), _full(gh.shape), _full(dsk.shape), _full(gs.shape),
                  _full(wout.shape), _full(gffn.shape), _full(router_t.shape)],
        out_specs=[row(D_MODEL), row(D_MODEL),
                   pl.BlockSpec((1, N_EXPERTS, tm), lambda i: (i // nb, 0, i % nb))],
        out_shape=[jax.ShapeDtypeStruct((t, D_MODEL), F32), jax.ShapeDtypeStruct((t, D_MODEL), BF16),
                   jax.ShapeDtypeStruct((batch, N_EXPERTS, seq), F32)],
        compiler_params=_cparams(("parallel",)),
        name="outproj_router",
    )(om, hy, yf, yb, xbc, z, x2d, gm, gh, dsk, gs, wout, gffn, router_t)


def _select_kernel(aff_ref, tri_ref, ones_ref, blk_ref, pos_ref, g_ref, off_ref, *, cap, nrows):
    aff = aff_ref[0]
    er = aff.shape[0]
    ne = er // nrows
    aff3 = aff.reshape(ne, nrows, LANES)
    capf = jnp.float32(cap)

    def count(mask3):
        return jnp.sum(jnp.where(mask3, 1.0, 0.0), axis=(1, 2), keepdims=True)

    def enough(cand):
        return count(aff3 >= cand) >= capf

    top = jnp.full((ne, 1, 1), 2.0, F32)
    for shift in (64, 32, 16, 8, 4, 2, 1):
        cand = top * (2.0 ** -shift)
        top = jnp.where(enough(cand), top, cand)
    p = top * 0.5

    def refine(_, carry):
        lo, step = carry
        cand = lo + step
        return jnp.where(enough(cand), cand, lo), step * 0.5

    lo, _ = lax.fori_loop(0, MANTISSA_STEPS, refine, (p, p * 0.5))
    thr = jnp.min(jnp.where(aff3 >= lo, aff3, jnp.inf), axis=(1, 2), keepdims=True)
    gt3 = aff3 > thr
    eq3 = aff3 == thr
    need = capf - count(gt3)

    tri = tri_ref[...]
    ones = ones_ref[...]
    blk = blk_ref[...]

    def prefix(maskf):
        mb = maskf.astype(BF16)
        within = _dot(mb, tri)
        rowtot = _dot(mb, ones)
        before = _dot(blk, rowtot.astype(BF16))
        return within + before, before

    eqf = jnp.where(eq3, 1.0, 0.0).reshape(er, LANES)
    tie_incl, _ = prefix(eqf)
    tie_rank = (tie_incl - eqf).reshape(ne, nrows, LANES)
    sel3 = jnp.logical_or(gt3, jnp.logical_and(eq3, tie_rank < need))
    self_ = jnp.where(sel3, 1.0, 0.0).reshape(er, LANES)
    incl, before = prefix(self_)
    sel = self_ > 0.5
    pos_ref[0] = jnp.where(sel, (incl - self_).astype(I32), -1)
    g_ref[0] = jnp.where(sel, aff, 0.0)
    off_ref[0] = before.astype(I32)


def moe_select(aff, cap):
    b, ne, seq = aff.shape
    nrows = seq // LANES
    er = ne * nrows
    tri = jnp.asarray(np.triu(np.ones((LANES, LANES), np.float32)), BF16)
    ones = jnp.ones((LANES, LANES), BF16)
    ridx = np.arange(er)
    blk = (ridx[:, None] // nrows == ridx[None, :] // nrows) & (ridx[None, :] < ridx[:, None])
    blk = jnp.asarray(blk.astype(np.float32), BF16)
    spec = pl.BlockSpec((1, er, LANES), lambda bi: (bi, 0, 0))
    pos, gsel, off = pl.pallas_call(
        functools.partial(_select_kernel, cap=cap, nrows=nrows),
        grid=(b,),
        in_specs=[spec, _full(tri.shape), _full(ones.shape), _full(blk.shape)],
        out_specs=[spec] * 3,
        out_shape=[jax.ShapeDtypeStruct((b, er, LANES), I32), jax.ShapeDtypeStruct((b, er, LANES), F32),
                   jax.ShapeDtypeStruct((b, er, LANES), I32)],
        compiler_params=_cparams(("parallel",)),
        name="moe_select",
    )(aff.reshape(b, er, LANES), tri, ones, blk)
    return pos.reshape(b, ne, seq), gsel.reshape(b, ne, seq), off[:, :, 0].reshape(b, ne, nrows)


def _gather_kernel(offs_ref, h_ref, pos_ref, o_ref, acc_ref, *, tb, cap):
    bi = pl.program_id(0)
    ei = pl.program_id(1)
    seq = h_ref.shape[1]
    win = tb + SUBLANES
    acc_ref[...] = jnp.zeros_like(acc_ref)
    riota = lax.broadcasted_iota(I32, (win, tb), 0)

    def body(j, carry):
        off = pl.multiple_of(offs_ref[bi, ei, j] * SUBLANES, SUBLANES)
        t0 = pl.multiple_of(j * tb, tb)
        pos = pos_ref[0, 0, :, pl.ds(t0, tb)]
        onehot = jnp.where(riota + off == pos, 1.0, 0.0).astype(BF16)
        acc_ref[pl.ds(off, win), :] += _dot(onehot, h_ref[0, pl.ds(t0, tb), :])
        return carry

    lax.fori_loop(0, seq // tb, body, 0)
    o_ref[0, 0] = acc_ref[0:cap, :].astype(BF16)


def moe_gather(hffn, pos_row, offs, cap, tb):
    b, seq, d = hffn.shape
    ne = pos_row.shape[1]
    grid_spec = pltpu.PrefetchScalarGridSpec(
        num_scalar_prefetch=1,
        grid=(b, ne),
        in_specs=[pl.BlockSpec((1, seq, d), lambda bi, ei, offs: (bi, 0, 0)),
                  pl.BlockSpec((1, 1, 1, seq), lambda bi, ei, offs: (bi, ei, 0, 0))],
        out_specs=pl.BlockSpec((1, 1, cap, d), lambda bi, ei, offs: (bi, ei, 0, 0)),
        scratch_shapes=[pltpu.VMEM((cap + tb + SUBLANES, d), F32)],
    )
    return pl.pallas_call(
        functools.partial(_gather_kernel, tb=tb, cap=cap),
        grid_spec=grid_spec,
        out_shape=jax.ShapeDtypeStruct((b, ne, cap, d), BF16),
        compiler_params=_cparams(("parallel", "arbitrary")),
        name="moe_gather",
    )(offs, hffn, pos_row)


def _ffn_kernel(xe_ref, wg_ref, wu_ref, wd_ref, o_ref, acc_ref, *, cap):
    f = pl.program_id(1)
    nb = xe_ref.shape[0]
    d = xe_ref.shape[-1]
    xe = xe_ref[...].reshape(nb * cap, d)
    a = _dot(xe, wg_ref[...].astype(BF16))
    u = _dot(xe, wu_ref[...].astype(BF16))
    hid = (a * jax.nn.sigmoid(a) * u).astype(BF16)
    part = _dot(hid, wd_ref[...].astype(BF16))

    @pl.when(f == 0)
    def _():
        acc_ref[...] = part

    @pl.when(f > 0)
    def _():
        acc_ref[...] += part

    @pl.when(f == pl.num_programs(1) - 1)
    def _():
        o_ref[:, 0:cap, :] = acc_ref[...].reshape(nb, cap, d).astype(BF16)
        o_ref[:, cap:, :] = jnp.zeros((nb, o_ref.shape[1] - cap, d), BF16)


def moe_ffn(xe, w_gate, w_up, w_down, cap, cap_pad, tf):
    b, ne, _, d = xe.shape
    ff = w_gate.shape[-1]
    return pl.pallas_call(
        functools.partial(_ffn_kernel, cap=cap),
        grid=(ne, ff // tf),
        in_specs=[pl.BlockSpec((b, None, cap, d), lambda e, f: (0, e, 0, 0)),
                  pl.BlockSpec((None, d, tf), lambda e, f: (e, 0, f)),
                  pl.BlockSpec((None, d, tf), lambda e, f: (e, 0, f)),
                  pl.BlockSpec((None, tf, d), lambda e, f: (e, f, 0))],
        out_specs=pl.BlockSpec((b, None, cap_pad, d), lambda e, f: (0, e, 0, 0)),
        out_shape=jax.ShapeDtypeStruct((b, ne, cap_pad, d), BF16),
        scratch_shapes=[pltpu.VMEM((b * cap, d), F32)],
        compiler_params=_cparams(("parallel", "arbitrary")),
        name="moe_ffn",
    )(xe, w_gate, w_up, w_down)


def _combine_kernel(offs_ref, ye_ref, pos_ref, g_ref, x1_ref, p_ref, gple_ref, wgate_ref, wproj_ref, o_ref, acc_ref,
                    *, tb):
    bi = pl.program_id(0)
    j = pl.program_id(1)
    e = pl.program_id(2)
    win = ye_ref.shape[2]

    @pl.when(e == 0)
    def _():
        acc_ref[...] = jnp.zeros_like(acc_ref)

    off = offs_ref[bi, e, j] * SUBLANES
    lane_e = lax.broadcasted_iota(I32, pos_ref.shape[1:], 1) == e
    pos = jnp.sum(jnp.where(lane_e, pos_ref[0], 0), axis=1, keepdims=True)
    gate = jnp.sum(jnp.where(lane_e, g_ref[0], 0.0), axis=1, keepdims=True)
    liota = lax.broadcasted_iota(I32, (tb, win), 1)
    onehot = jnp.where(liota + off == pos, 1.0, 0.0).astype(BF16)
    acc_ref[...] += gate * _dot(onehot, ye_ref[0, 0])

    @pl.when(e == pl.num_programs(2) - 1)
    def _():
        x2 = x1_ref[0] + acc_ref[...]
        hp = _rms(x2, gple_ref[...]).astype(BF16)
        gt = jax.nn.sigmoid(_dot(hp, wgate_ref[...]))
        o_ref[0] = x2 + _dot(p_ref[0].astype(BF16), wproj_ref[...]) * gt


def moe_combine(ye, pos_col, g_col, offs, x1, p, gple, wgate, wproj, tb):
    b, ne, _, d = ye.shape
    seq = x1.shape[1]
    win = tb + SUBLANES
    grid_spec = pltpu.PrefetchScalarGridSpec(
        num_scalar_prefetch=1,
        grid=(b, seq // tb, ne),
        in_specs=[pl.BlockSpec((pl.Element(1), pl.Element(1), pl.Element(win), pl.Element(d)),
                               lambda bi, j, e, offs: (bi, e, offs[bi, e, j] * SUBLANES, 0)),
                  pl.BlockSpec((1, tb, ne), lambda bi, j, e, offs: (bi, j, 0)),
                  pl.BlockSpec((1, tb, ne), lambda bi, j, e, offs: (bi, j, 0)),
                  pl.BlockSpec((1, tb, d), lambda bi, j, e, offs: (bi, j, 0)),
                  pl.BlockSpec((1, tb, p.shape[-1]), lambda bi, j, e, offs: (bi, j, 0)),
                  pl.BlockSpec(gple.shape, lambda bi, j, e, offs: (0, 0)),
                  pl.BlockSpec(wgate.shape, lambda bi, j, e, offs: (0, 0)),
                  pl.BlockSpec(wproj.shape, lambda bi, j, e, offs: (0, 0))],
        out_specs=pl.BlockSpec((1, tb, d), lambda bi, j, e, offs: (bi, j, 0)),
        scratch_shapes=[pltpu.VMEM((tb, d), F32)],
    )
    return pl.pallas_call(
        functools.partial(_combine_kernel, tb=tb),
        grid_spec=grid_spec,
        out_shape=jax.ShapeDtypeStruct((b, seq, d), F32),
        compiler_params=_cparams(("parallel", "parallel", "arbitrary")),
        name="moe_combine_ple",
    )(offs, ye, pos_col, g_col, x1, p, gple, wgate, wproj)


def _finalnorm_kernel(x_ref, g_ref, o_ref):
    o_ref[...] = _rms(x_ref[...], g_ref[...])


def final_norm(x2d, g, tm):
    t, d = x2d.shape
    return pl.pallas_call(
        _finalnorm_kernel,
        grid=(t // tm,),
        in_specs=[pl.BlockSpec((tm, d), lambda i: (i, 0)), _full(g.shape)],
        out_specs=pl.BlockSpec((tm, d), lambda i: (i, 0)),
        out_shape=jax.ShapeDtypeStruct((t, d), F32),
        compiler_params=_cparams(("parallel",)),
        name="final_norm",
    )(x2d, g)


def _pad_cols(a, width):
    return jnp.pad(a, ((0, 0), (0, width - a.shape[1])))


def _pack_inproj(w_in):
    offs = np.cumsum((0,) + IN_SPLITS)
    cq, ckv, kr, hy, z, xbc, dt = [w_in[:, offs[i]:offs[i + 1]] for i in range(len(IN_SPLITS))]
    d = w_in.shape[0]
    half = MLA_ROPE // 2
    zeros = lambda n: jnp.zeros((d, n), w_in.dtype)
    kr_pad = jnp.concatenate([zeros(MLA_NOPE), kr, zeros(HEAD_PAD - MLA_NOPE - MLA_ROPE)], axis=1)
    kr_swap = jnp.concatenate([zeros(MLA_NOPE), -kr[:, half:], kr[:, :half], zeros(HEAD_PAD - MLA_NOPE - MLA_ROPE)],
                              axis=1)
    wall = jnp.concatenate([cq, ckv, kr_pad, kr_swap, hy, z, xbc, _pad_cols(dt, LANES)], axis=1)
    assert wall.shape[1] == _C_END
    return wall.astype(BF16)


def _pack_mla(w_uq, w_ukv):
    lq = w_uq.shape[0]
    lkv = w_ukv.shape[0]
    half = MLA_ROPE // 2
    padw = HEAD_PAD - MLA_NOPE - MLA_ROPE
    q3 = w_uq.reshape(lq, MLA_HEADS, MLA_NOPE + MLA_ROPE)
    nope, rope = q3[..., :MLA_NOPE], q3[..., MLA_NOPE:]
    zq = jnp.zeros((lq, MLA_HEADS, padw), w_uq.dtype)
    wq = jnp.concatenate([nope, rope, zq], axis=-1).reshape(lq, MLA_HEADS * HEAD_PAD)
    wqs = jnp.concatenate([jnp.zeros_like(nope), -rope[..., half:], rope[..., :half], zq], axis=-1)
    wqs = wqs.reshape(lq, MLA_HEADS * HEAD_PAD)
    kv3 = w_ukv.reshape(lkv, MLA_HEADS, MLA_NOPE + MLA_V)
    knope, vv = kv3[..., :MLA_NOPE], kv3[..., MLA_NOPE:]
    wk = jnp.concatenate([knope, jnp.zeros((lkv, MLA_HEADS, HEAD_PAD - MLA_NOPE), w_ukv.dtype)], axis=-1)
    wk = wk.reshape(lkv, MLA_HEADS * HEAD_PAD)
    wv = vv.reshape(lkv, MLA_HEADS * MLA_V)
    return wq.astype(BF16), wqs.astype(BF16), wk.astype(BF16), wv.astype(BF16)


def _row(a):
    return a.reshape(1, -1).astype(F32)


TM_PROJ = 512
TQ_ATTN = 256
TK_ATTN = 512
ROWS_CONV = 512
ROWS_FILT = 512
CB_DFT = 2048
TM_OUT = 256
TB_MOE = 256
TF_FFN = 512


def _hyena(hy_u, tabs, kspec, conv_w, conv_b, bias):
    b, seq, _ = hy_u.shape
    c = HY_WIDTH
    parts = [dwconv(hy_u, conv_w[:, i * c:(i + 1) * c], _row(conv_b[i * c:(i + 1) * c]), i * c, c, False, ROWS_CONV)
             for i in range(HY_ORDER + 1)]
    gates, v = parts[:-1], parts[-1]
    half, n1 = tabs["half"], tabs["n1"]
    flat = lambda a: a.reshape(b * half, FFT_N2 * c)
    for o in range(HY_ORDER):
        a = leftmm(tabs["m1"], flat(v), CB_DFT).reshape(2, n1, FFT_N2, c)
        y = spectrum_multiply(a, kspec, o, tabs["f2r"], tabs["f2i"], tabs["twr_row"], tabs["twi_row"],
                              tabs["twr_col"], tabs["twi_col"])
        v = conv_output(tabs["m3"], y.reshape(2 * n1, FFT_N2 * c), flat(v), flat(gates[o]), _row(bias[o]), CB_DFT)
        v = v.reshape(b, seq, c)
    return v


def _hyena_kspec(seq, tabs, w1, b1, freq, w2, b2, w3, decay):
    bands = np.arange(1, HY_BANDS + 1, dtype=np.float64) * 2.0 * np.pi
    mult = np.zeros((1, LANES), np.float32)
    mult[0, 1:1 + HY_BANDS] = bands
    mult[0, 1 + HY_BANDS:1 + 2 * HY_BANDS] = bands
    w1p = jnp.pad(w1.astype(F32), ((0, LANES - w1.shape[0]), (0, 0)))
    ncol = HY_ORDER * 2 * HY_WIDTH
    bwd = (np.arange(ncol) // HY_WIDTH) % 2
    kf = hyena_filters(seq, jnp.asarray(mult), w1p, _row(b1), _row(freq), w2.astype(F32), _row(b2), w3.astype(F32),
                       _row(decay), jnp.asarray(bwd.astype(np.float32)).reshape(1, ncol), ROWS_FILT)
    a = leftmm(tabs["mk"], kf.reshape(tabs["half"], FFT_N2 * ncol), CB_DFT)
    a4 = a.reshape(2, tabs["n1"], FFT_N2, ncol)
    return filter_spectrum(a4, tabs["f2r"], tabs["f2i"], tabs["twr_row"], tabs["twi_row"])


def kernel(x, p, positions, norm_mix, w_in, mla_q_norm, mla_w_uq, mla_kv_norm, mla_w_ukv, mla_out_norm, hy_conv_w,
           hy_conv_b, hy_filt_w1, hy_filt_b1, hy_filt_freq, hy_filt_w2, hy_filt_b2, hy_filt_w3, hy_decay, hy_bias,
           hy_out_norm, ssm_conv_w, ssm_conv_b, ssm_dt_bias, ssm_a_log, ssm_d, ssm_norm, w_out, norm_ffn, moe_router,
           moe_w_gate, moe_w_up, moe_w_down, ple_norm, ple_gate_w, ple_proj, final_norm_g):
    batch, seq, d = x.shape
    depth = w_in.shape[0]
    t = batch * seq
    cap = EC_CAPACITY_FACTOR * seq // N_EXPERTS
    tb = min(TB_MOE, seq)
    cap_pad = cap + 2 * tb
    tm_proj = min(TM_PROJ, seq)
    tm_out = min(TM_OUT, seq)

    freq = np.zeros((1, HEAD_PAD), np.float32)
    inv = ROPE_THETA ** (-np.arange(0, MLA_ROPE, 2, dtype=np.float32) / MLA_ROPE)
    freq[0, MLA_NOPE:MLA_NOPE + MLA_ROPE // 2] = inv
    freq[0, MLA_NOPE + MLA_ROPE // 2:MLA_NOPE + MLA_ROPE] = inv
    cos_t, sin_t = rope_tables(positions.reshape(t, 1), jnp.asarray(freq), tm_proj)

    tabs = _dft_tables(seq, batch)
    tril = jnp.asarray(np.tril(np.ones((SSM_CHUNK, SSM_CHUNK), np.float32)))

    x2d = x.reshape(t, d)
    for i in range(depth):
        wall = _pack_inproj(w_in[i])
        wq, wqs, wk, wv = _pack_mla(mla_w_uq[i], mla_w_ukv[i])
        q, k, v, hy_u, z, xbc_raw, dt_raw = inproj(x2d, _row(norm_mix[i]), wall, _row(mla_q_norm[i]), wq, wqs,
                                                    _row(mla_kv_norm[i]), wk, wv, cos_t, sin_t, tm_proj)
        o_mla = attention(q.reshape(batch, seq, -1), k.reshape(batch, seq, -1), v.reshape(batch, seq, -1),
                          min(TQ_ATTN, seq), min(TK_ATTN, seq))

        kspec = _hyena_kspec(seq, tabs, hy_filt_w1[i], hy_filt_b1[i], hy_filt_freq[i], hy_filt_w2[i], hy_filt_b2[i],
                             hy_filt_w3[i], hy_decay[i])
        o_hy = _hyena(hy_u.reshape(batch, seq, -1), tabs, kspec, hy_conv_w[i], hy_conv_b[i], hy_bias[i])

        xbc = dwconv(xbc_raw.reshape(batch, seq, -1), ssm_conv_w[i], _row(ssm_conv_b[i]), 0, SSM_CONV_DIM, True,
                     ROWS_CONV)
        dtbias_row = _pad_cols(_row(ssm_dt_bias[i]), LANES)
        a_row = _pad_cols(_row(-jnp.exp(ssm_a_log[i].astype(F32))), LANES)
        y_f, y_b = ssd_scan(xbc, dt_raw.reshape(batch, seq, -1), dtbias_row, a_row, tril)

        dsk = _row(jnp.repeat(ssm_d[i].astype(F32), SSM_HEADDIM))
        x1, hffn, aff = outproj(o_mla.reshape(t, -1), o_hy.reshape(t, -1), y_f.reshape(t, -1), y_b.reshape(t, -1),
                                xbc.reshape(t, -1), z, x2d, _row(mla_out_norm[i]), _row(hy_out_norm[i]), dsk,
                                _row(ssm_norm[i]), w_out[i].astype(BF16), _row(norm_ffn[i]),
                                moe_router[i].astype(F32).T, batch, tm_out)

        pos, gsel, rowoff = moe_select(aff, cap)
        step = tb // LANES
        offs = rowoff[:, :, ::step] // SUBLANES
        xe = moe_gather(hffn.reshape(batch, seq, d), pos.reshape(batch, N_EXPERTS, 1, seq), offs, cap, tb)
        ye = moe_ffn(xe, moe_w_gate[i], moe_w_up[i], moe_w_down[i], cap, cap_pad, TF_FFN)
        x3 = moe_combine(ye, jnp.swapaxes(pos, 1, 2), jnp.swapaxes(gsel, 1, 2), offs, x1.reshape(batch, seq, d),
                         p[i], _row(ple_norm[i]), ple_gate_w[i].astype(BF16), ple_proj[i].astype(BF16), tb)
        x2d = x3.reshape(t, d)
    return final_norm(x2d, _row(final_norm_g), tm_proj).reshape(batch, seq, d)
```

```python
import functools
import math

import numpy as np
import jax
import jax.numpy as jnp
from jax import lax
from jax.experimental import pallas as pl
from jax.experimental.pallas import tpu as pltpu

F32 = jnp.float32
BF16 = jnp.bfloat16
I32 = jnp.int32
HIGHEST = lax.Precision.HIGHEST

EPS = 1e-6
LANES = 128
SUBLANES = 8
VMEM_LIMIT = 56 * 1024 * 1024

D_MODEL = 1024
MLA_HEADS = 8
MLA_NOPE = 64
MLA_ROPE = 32
MLA_V = 64
MLA_Q_LORA = 256
MLA_KV_LORA = 128
HEAD_PAD = 128
ROPE_THETA = 10000.0
HY_WIDTH = 256
HY_ORDER = 2
HY_SHORT = 3
HY_BANDS = 8
HY_FILT_HID = 64
SSM_WIDTH = 256
SSM_HEADDIM = 64
SSM_HEADS = 4
SSM_GROUPS = 2
SSM_STATE = 128
SSM_CONV = 5
SSM_CHUNK = 128
SSM_CONV_DIM = 768
N_EXPERTS = 16
EXPERT_FF = 2048
EC_CAPACITY_FACTOR = 2
PLE_DIM = 256
FFT_N2 = 128
MANTISSA_STEPS = 40
IN_SPLITS = (MLA_Q_LORA, MLA_KV_LORA, MLA_ROPE, 3 * HY_WIDTH, SSM_WIDTH, SSM_CONV_DIM, 2 * SSM_HEADS)


def _cparams(sem, vmem=None):
    return pltpu.CompilerParams(dimension_semantics=sem, vmem_limit_bytes=vmem or VMEM_LIMIT)


def _rms(x, g):
    ms = jnp.mean(x * x, axis=-1, keepdims=True)
    return x * lax.rsqrt(ms + EPS) * g


def _dot(a, b, precision=None):
    return jnp.dot(a, b, preferred_element_type=F32, precision=precision)


def _dot_nt(a, b, precision=None):
    return lax.dot_general(a, b, (((1,), (1,)), ((), ())), preferred_element_type=F32, precision=precision)


def _dot_tn(a, b, precision=None):
    return lax.dot_general(a, b, (((0,), (0,)), ((), ())), preferred_element_type=F32, precision=precision)


def _split_bf16(x):
    hi = x.astype(BF16)
    return hi, (x - hi.astype(F32)).astype(BF16)


def _dot3(a, b):
    ah, al = _split_bf16(a)
    bh, bl = _split_bf16(b)
    return _dot(jnp.concatenate([ah, ah, al], axis=1), jnp.concatenate([bh, bl, bh], axis=0))


def _full(shape):
    n = len(shape)
    return pl.BlockSpec(shape, lambda *_: (0,) * n)


def _rope_kernel(pos_ref, freq_ref, cos_ref, sin_ref):
    ang = pos_ref[...].astype(F32) * freq_ref[...]
    cos_ref[...] = jnp.cos(ang)
    sin_ref[...] = jnp.sin(ang)


def rope_tables(pos_col, freq_row, tm):
    t = pos_col.shape[0]
    return pl.pallas_call(
        _rope_kernel,
        grid=(t // tm,),
        in_specs=[pl.BlockSpec((tm, 1), lambda i: (i, 0)), _full((1, HEAD_PAD))],
        out_specs=[pl.BlockSpec((tm, HEAD_PAD), lambda i: (i, 0))] * 2,
        out_shape=[jax.ShapeDtypeStruct((t, HEAD_PAD), F32)] * 2,
        compiler_params=_cparams(("parallel",)),
        name="rope_tables",
    )(pos_col, freq_row)


_C_CQ = 0
_C_CKV = 256
_C_KR = 384
_C_KRS = 512
_C_HY = 640
_C_Z = 1408
_C_XBC = 1664
_C_DT = 2432
_C_END = 2560


def _inproj_kernel(x_ref, gmix_ref, wall_ref, qn_ref, wq_ref, wqs_ref, kvn_ref, wk_ref, wv_ref, cos_ref, sin_ref,
                   q_ref, k_ref, v_ref, hy_ref, z_ref, xbc_ref, dt_ref, *, scale):
    h = _rms(x_ref[...], gmix_ref[...]).astype(BF16)
    proj = _dot(h, wall_ref[...])
    hy_ref[...] = proj[:, _C_HY:_C_Z]
    z_ref[...] = proj[:, _C_Z:_C_XBC]
    xbc_ref[...] = proj[:, _C_XBC:_C_DT]
    dt_ref[...] = proj[:, _C_DT:_C_END]
    cos = cos_ref[...]
    sin = sin_ref[...]
    cos8 = jnp.concatenate([cos] * MLA_HEADS, axis=-1)
    sin8 = jnp.concatenate([sin] * MLA_HEADS, axis=-1)
    cqn = _rms(proj[:, _C_CQ:_C_CKV], qn_ref[...]).astype(BF16)
    q = _dot(cqn, wq_ref[...])
    qs = _dot(cqn, wqs_ref[...])
    q_ref[...] = ((q * cos8 + qs * sin8) * scale).astype(BF16)
    ckvn = _rms(proj[:, _C_CKV:_C_KR], kvn_ref[...]).astype(BF16)
    kn = _dot(ckvn, wk_ref[...])
    v_ref[...] = _dot_nt(wv_ref[...], ckvn).astype(BF16)
    kr = proj[:, _C_KR:_C_KRS] * cos + proj[:, _C_KRS:_C_HY] * sin
    k_ref[...] = (kn + jnp.concatenate([kr] * MLA_HEADS, axis=-1)).astype(BF16)


def inproj(x2d, gmix, wall, qn, wq, wqs, kvn, wk, wv, cos_t, sin_t, tm):
    t = x2d.shape[0]
    hq = MLA_HEADS * HEAD_PAD
    row = lambda w: pl.BlockSpec((tm, w), lambda i: (i, 0))
    outs = [(hq, BF16), (hq, BF16), None, (3 * HY_WIDTH, F32), (SSM_WIDTH, F32), (SSM_CONV_DIM, F32), (LANES, F32)]
    hv = MLA_HEADS * MLA_V
    return pl.pallas_call(
        functools.partial(_inproj_kernel, scale=(MLA_NOPE + MLA_ROPE) ** -0.5 * math.log2(math.e)),
        grid=(t // tm,),
        in_specs=[row(D_MODEL), _full(gmix.shape), _full(wall.shape), _full(qn.shape), _full(wq.shape),
                  _full(wqs.shape), _full(kvn.shape), _full(wk.shape), _full(wv.shape), row(HEAD_PAD), row(HEAD_PAD)],
        out_specs=[row(o[0]) if o else pl.BlockSpec((hv, tm), lambda i: (0, i)) for o in outs],
        out_shape=[jax.ShapeDtypeStruct((t, o[0]), o[1]) if o else jax.ShapeDtypeStruct((hv, t), BF16)
                   for o in outs],
        compiler_params=_cparams(("parallel",)),
        name="inproj",
    )(x2d, gmix, wall, qn, wq, wqs, kvn, wk, wv, cos_t, sin_t)


def _attn_kernel(q_ref, k_ref, vt_ref, o_ref, st_ref, *, tk):
    seq = k_ref.shape[1]
    tq = q_ref.shape[1]
    nh = st_ref.shape[0]
    npairs = seq // (2 * tk)

    def scores(hh, c, slot):
        off = pl.multiple_of(c * tk, tk)
        st = _dot_nt(k_ref[0, pl.ds(off, tk), hh * HEAD_PAD:(hh + 1) * HEAD_PAD],
                     q_ref[0, :, hh * HEAD_PAD:(hh + 1) * HEAD_PAD])
        st_ref[hh, slot] = st
        return jnp.max(st, axis=0, keepdims=True)

    def update(hh, c, slot, m, l, acc, smax):
        off = pl.multiple_of(c * tk, tk)
        vtc = vt_ref[hh * MLA_V:(hh + 1) * MLA_V, pl.ds(off, tk)]
        m_new = jnp.maximum(m, smax)
        alpha = jnp.exp2(m - m_new)
        p = jnp.exp2(st_ref[hh, slot] - m_new)
        l = l * alpha + jnp.sum(p, axis=0, keepdims=True)
        acc = acc * alpha + _dot(vtc, p.astype(BF16))
        return m_new, l, acc

    def pair(i, carry, last):
        new = []
        for hh in range(nh):
            m, l, acc, smax0 = carry[hh]
            smax1 = scores(hh, 2 * i + 1, 1)
            m, l, acc = update(hh, 2 * i, 0, m, l, acc, smax0)
            smax0 = smax1 if last else scores(hh, 2 * i + 2, 0)
            m, l, acc = update(hh, 2 * i + 1, 1, m, l, acc, smax1)
            new.append((m, l, acc, smax0))
        return tuple(new)

    init = tuple((jnp.full((1, tq), -jnp.inf, F32), jnp.zeros((1, tq), F32), jnp.zeros((MLA_V, tq), F32),
                  scores(hh, 0, 0)) for hh in range(nh))
    carry = lax.fori_loop(0, npairs - 1, functools.partial(pair, last=False), init)
    final = pair(npairs - 1, carry, True)
    o_ref[0] = jnp.concatenate([jnp.transpose(acc / l) for _, l, acc, _ in final], axis=-1)


def attention(q, k, vt, tq, tk):
    b, seq, _ = q.shape
    return pl.pallas_call(
        functools.partial(_attn_kernel, tk=tk),
        grid=(b, MLA_HEADS // 2, seq // tq),
        in_specs=[pl.BlockSpec((1, tq, 2 * HEAD_PAD), lambda bi, hp, qi: (bi, qi, hp)),
                  pl.BlockSpec((1, seq, 2 * HEAD_PAD), lambda bi, hp, qi: (bi, 0, hp)),
                  pl.BlockSpec((2 * MLA_V, seq), lambda bi, hp, qi: (hp, bi))],
        out_specs=pl.BlockSpec((1, tq, 2 * MLA_V), lambda bi, hp, qi: (bi, qi, hp)),
        out_shape=jax.ShapeDtypeStruct((b, seq, MLA_HEADS * MLA_V), F32),
        scratch_shapes=[pltpu.VMEM((2, 2, tk, tq), F32)],
        compiler_params=_cparams(("parallel", "parallel", "parallel")),
        name="attention",
    )(q, k, vt)


def _dwconv_kernel(x_ref, w_ref, b_ref, o_ref, *, width, act, rows):
    seq = x_ref.shape[1]
    pad = width // 2
    nchunks = seq // rows
    w = w_ref[...]
    bias = b_ref[...]

    def body(c, carry):
        r0 = pl.multiple_of(c * rows, rows)
        cur = x_ref[0, pl.ds(r0, rows), :]
        p0 = pl.multiple_of(jnp.maximum(r0 - SUBLANES, 0), SUBLANES)
        n0 = pl.multiple_of(jnp.minimum(r0 + rows, seq - SUBLANES), SUBLANES)
        prev = jnp.where(c > 0, x_ref[0, pl.ds(p0, SUBLANES), :], 0.0)
        nxt = jnp.where(c < nchunks - 1, x_ref[0, pl.ds(n0, SUBLANES), :], 0.0)
        ext = jnp.concatenate([prev, cur, nxt], axis=0)
        acc = bias + ext[SUBLANES - pad:SUBLANES - pad + rows] * w[0:1]
        for kk in range(1, width):
            s0 = SUBLANES - pad + kk
            acc = acc + ext[s0:s0 + rows] * w[kk:kk + 1]
        if act:
            acc = acc * jax.nn.sigmoid(acc)
        o_ref[0, pl.ds(r0, rows), :] = acc
        return carry

    lax.fori_loop(0, nchunks, body, 0)


def dwconv(x, w, bias, col0, ncols, act, rows):
    b, seq, _ = x.shape
    width = w.shape[0]
    cb0 = col0 // LANES
    return pl.pallas_call(
        functools.partial(_dwconv_kernel, width=width, act=act, rows=min(rows, seq)),
        grid=(b, ncols // LANES),
        in_specs=[pl.BlockSpec((1, seq, LANES), lambda bi, ci: (bi, 0, ci + cb0)),
                  pl.BlockSpec((width, LANES), lambda bi, ci: (0, ci)),
                  pl.BlockSpec((1, LANES), lambda bi, ci: (0, ci))],
        out_specs=pl.BlockSpec((1, seq, LANES), lambda bi, ci: (bi, 0, ci)),
        out_shape=jax.ShapeDtypeStruct((b, seq, ncols), F32),
        compiler_params=_cparams(("parallel", "parallel")),
        name="dwconv",
    )(x, w, bias)


def _hyfilt_kernel(mult_ref, w1_ref, b1_ref, fr_ref, w2_ref, b2_ref, w3_ref, dec_ref, bwd_ref, o_ref, *, seq, rows):
    i = pl.program_id(0)
    ridx = lax.broadcasted_iota(I32, (rows, LANES), 0) + i * rows
    lane = lax.broadcasted_iota(I32, (rows, LANES), 1)
    t = ridx.astype(F32) / seq
    ang = t * mult_ref[...]
    feats = jnp.where(lane == 0, t, jnp.where(lane <= HY_BANDS, jnp.sin(ang), jnp.cos(ang)))
    feats = jnp.where(lane < 1 + 2 * HY_BANDS, feats, 0.0)
    fr = fr_ref[...]
    hdn = jnp.sin(fr * (_dot(feats, w1_ref[...], HIGHEST) + b1_ref[...]))
    hdn = jnp.sin(fr * (_dot(hdn, w2_ref[...], HIGHEST) + b2_ref[...]))
    filt = _dot(hdn, w3_ref[...], HIGHEST)
    window = jnp.exp(-t[:, 0:1] * jnp.abs(dec_ref[...]))
    out = filt * window
    keep = jnp.logical_or(ridx[:, 0:1] > 0, bwd_ref[...] < 0.5)
    o_ref[...] = jnp.where(keep, out, 0.0)


def hyena_filters(seq, mult, w1p, b1, fr, w2, b2, w3, dec, bwd_mask, rows):
    ncol = w3.shape[1]
    rows = min(rows, seq)
    args = (mult, w1p, b1, fr, w2, b2, w3, dec, bwd_mask)
    return pl.pallas_call(
        functools.partial(_hyfilt_kernel, seq=seq, rows=rows),
        grid=(seq // rows,),
        in_specs=[_full(a.shape) for a in args],
        out_specs=pl.BlockSpec((rows, ncol), lambda i: (i, 0)),
        out_shape=jax.ShapeDtypeStruct((seq, ncol), F32),
        compiler_params=_cparams(("parallel",)),
        name="hyena_filters",
    )(*args)


def _leftmm_kernel(m_ref, x_ref, o_ref):
    o_ref[...] = _dot3(m_ref[...], x_ref[...])


def leftmm(m, x2d, cb):
    r, kdim = m.shape
    n = x2d.shape[1]
    cb = min(cb, n)
    return pl.pallas_call(
        _leftmm_kernel,
        grid=(n // cb,),
        in_specs=[_full((r, kdim)), pl.BlockSpec((kdim, cb), lambda i: (0, i))],
        out_specs=pl.BlockSpec((r, cb), lambda i: (0, i)),
        out_shape=jax.ShapeDtypeStruct((r, n), F32),
        compiler_params=_cparams(("parallel",)),
        name="dft_outer",
    )(m, x2d)


def _inner_mats(f2r_ref, f2i_ref, twr, twi):
    f2r = f2r_ref[...]
    f2i = f2i_ref[...]
    gr = f2r * twr - f2i * twi
    gi = f2r * twi + f2i * twr
    return gr, gi


def _specfilt_kernel(a_ref, f2r_ref, f2i_ref, twr_ref, twi_ref, o_ref):
    n2 = FFT_N2
    c2 = a_ref.shape[-1]
    gr, gi = _inner_mats(f2r_ref, f2i_ref, twr_ref[0], twi_ref[0])
    gblk = jnp.concatenate([jnp.concatenate([gr, -gi], axis=1), jnp.concatenate([gi, gr], axis=1)], axis=0)
    a = a_ref[:, 0].reshape(2 * n2, c2)
    x = _dot3(gblk, a)
    c = c2 // 2
    o_ref[0, 0, 0] = x[:n2, :c] + x[:n2, c:]
    o_ref[0, 1, 0] = x[n2:, :c] - x[n2:, c:]


def filter_spectrum(a4, f2r, f2i, twr_row, twi_row):
    _, n1, n2, ctot = a4.shape
    c = HY_WIDTH
    return pl.pallas_call(
        _specfilt_kernel,
        grid=(HY_ORDER, n1),
        in_specs=[pl.BlockSpec((2, 1, n2, 2 * c), lambda o, k: (0, k, 0, o)),
                  _full((n2, n2)), _full((n2, n2)),
                  pl.BlockSpec((1, 1, n2), lambda o, k: (k, 0, 0)),
                  pl.BlockSpec((1, 1, n2), lambda o, k: (k, 0, 0))],
        out_specs=pl.BlockSpec((1, 2, 1, n2, c), lambda o, k: (o, 0, k, 0, 0)),
        out_shape=jax.ShapeDtypeStruct((HY_ORDER, 2, n1, n2, c), F32),
        compiler_params=_cparams(("parallel", "parallel")),
        name="filter_spectrum",
    )(a4, f2r, f2i, twr_row, twi_row)


def _specmul_kernel(a_ref, k_ref, f2r_ref, f2i_ref, twr_ref, twi_ref, twrc_ref, twic_ref, o_ref):
    n2 = FFT_N2
    c = a_ref.shape[-1]
    gr, gi = _inner_mats(f2r_ref, f2i_ref, twr_ref[0], twi_ref[0])
    gblk = jnp.concatenate([jnp.concatenate([gr, -gi], axis=1), jnp.concatenate([gi, gr], axis=1)], axis=0)
    grt, git = _inner_mats(f2r_ref, f2i_ref, twrc_ref[0], twic_ref[0])
    gblk_t = jnp.concatenate([jnp.concatenate([grt, git], axis=1), jnp.concatenate([-git, grt], axis=1)], axis=0)
    a = a_ref[:, 0].reshape(2 * n2, c)
    x = _dot3(gblk, a)
    xr, xi = x[:n2], x[n2:]
    kr, ki = k_ref[0, 0, 0], k_ref[0, 1, 0]
    p = jnp.concatenate([xr * kr - xi * ki, xr * ki + xi * kr], axis=0)
    y = _dot3(gblk_t, p)
    o_ref[0, 0] = y[:n2]
    o_ref[1, 0] = y[n2:]


def spectrum_multiply(a4, kspec, order, f2r, f2i, twr_row, twi_row, twr_col, twi_col):
    _, n1, n2, c = a4.shape
    return pl.pallas_call(
        _specmul_kernel,
        grid=(n1,),
        in_specs=[pl.BlockSpec((2, 1, n2, c), lambda k: (0, k, 0, 0)),
                  pl.BlockSpec((1, 2, 1, n2, c), lambda k: (order, 0, k, 0, 0)),
                  _full((n2, n2)), _full((n2, n2)),
                  pl.BlockSpec((1, 1, n2), lambda k: (k, 0, 0)),
                  pl.BlockSpec((1, 1, n2), lambda k: (k, 0, 0)),
                  pl.BlockSpec((1, n2, 1), lambda k: (k, 0, 0)),
                  pl.BlockSpec((1, n2, 1), lambda k: (k, 0, 0))],
        out_specs=pl.BlockSpec((2, 1, n2, c), lambda k: (0, k, 0, 0)),
        out_shape=jax.ShapeDtypeStruct((2, n1, n2, c), F32),
        compiler_params=_cparams(("parallel",)),
        name="spectrum_multiply",
    )(a4, kspec, f2r, f2i, twr_row, twi_row, twr_col, twi_col)


def _convout_kernel(m_ref, y_ref, v_ref, g_ref, bias_ref, o_ref, *, reps):
    y = _dot3(m_ref[...], y_ref[...])
    bias = jnp.concatenate([bias_ref[...]] * reps, axis=-1)
    v = v_ref[...]
    o_ref[...] = (y + v * bias) * g_ref[...]


def conv_output(m3, y2d, v2d, g2d, bias_row, cb):
    r, kdim = m3.shape
    n = y2d.shape[1]
    cb = min(cb, n)
    c = bias_row.shape[1]
    return pl.pallas_call(
        functools.partial(_convout_kernel, reps=cb // c),
        grid=(n // cb,),
        in_specs=[_full((r, kdim)), pl.BlockSpec((kdim, cb), lambda i: (0, i)),
                  pl.BlockSpec((r, cb), lambda i: (0, i)), pl.BlockSpec((r, cb), lambda i: (0, i)),
                  _full((1, c))],
        out_specs=pl.BlockSpec((r, cb), lambda i: (0, i)),
        out_shape=jax.ShapeDtypeStruct((r, n), F32),
        compiler_params=_cparams(("parallel",)),
        name="dft_outer_inverse",
    )(m3, y2d, v2d, g2d, bias_row)


def _dft_tables(seq, batch):
    n2 = FFT_N2
    half = seq // n2
    n1 = 2 * half
    n = n1 * n2
    k1 = np.arange(n1, dtype=np.float64)[:, None]
    nn1 = np.arange(half, dtype=np.float64)[None, :]
    th = 2.0 * np.pi * k1 * nn1 / n1
    c1, s1 = np.cos(th), np.sin(th)
    assert batch == 2, "the two batch entries are packed as real / imaginary parts"
    m1 = np.block([[c1, s1], [-s1, c1]])
    m3 = np.block([[c1.T, -s1.T], [s1.T, c1.T]]) / n
    mk = np.concatenate([c1, -s1], axis=0)
    kk2 = np.arange(n2, dtype=np.float64)
    th2 = 2.0 * np.pi * np.outer(kk2, kk2) / n2
    f2r, f2i = np.cos(th2), -np.sin(th2)
    tht = 2.0 * np.pi * np.outer(np.arange(n1, dtype=np.float64), kk2) / n
    twr, twi = np.cos(tht), -np.sin(tht)
    f = lambda a: jnp.asarray(a, F32)
    return dict(m1=f(m1), m3=f(m3), mk=f(mk), f2r=f(f2r), f2i=f(f2i),
                twr_row=f(twr[:, None, :]), twi_row=f(twi[:, None, :]),
                twr_col=f(twr[:, :, None]), twi_col=f(twi[:, :, None]), n1=n1, half=half)


def _ssd_kernel(xf_ref, bf_ref, cf_ref, dtf_ref, xb_ref, bb_ref, cb_ref, dtb_ref, dtbias_ref, a_ref, tril_ref,
                yf_ref, yb_ref, state_ref):
    q = SSM_CHUNK
    hd = SSM_HEADDIM
    ns = SSM_STATE

    @pl.when(pl.program_id(1) == 0)
    def _():
        state_ref[...] = jnp.zeros_like(state_ref)

    tril = tril_ref[...]
    rows = lax.broadcasted_iota(I32, (q, q), 0)
    cols = lax.broadcasted_iota(I32, (q, q), 1)
    a_row = a_ref[...]
    bias = dtbias_ref[...]

    def direction(x_ref, b_ref, c_ref, dt_ref, y_ref, d):
        dt = jax.nn.softplus(dt_ref[0] + bias)
        dta = dt * a_row
        cs = _dot(tril, dta, HIGHEST)
        ecs = cs - dta
        base = ecs if d else cs
        base_t = jnp.transpose(base)
        total = cs[q - 1:q, :]
        x = x_ref[0]
        ys = []
        for g in range(SSM_GROUPS):
            bm = b_ref[0, :, g * ns:(g + 1) * ns]
            cm = c_ref[0, :, g * ns:(g + 1) * ns]
            cb = _dot_nt(cm.astype(BF16), bm.astype(BF16))
            for hh in range(SSM_HEADS // SSM_GROUPS):
                h = g * (SSM_HEADS // SSM_GROUPS) + hh
                j = d * SSM_HEADS + h
                col = base[:, j:j + 1]
                row = base_t[j:j + 1, :]
                tot = total[:, j:j + 1]
                if d == 0:
                    seg = jnp.where(rows >= cols, col - row, -jnp.inf)
                    c_scale = jnp.exp(col)
                    b_scale = jnp.exp(tot - col)
                else:
                    seg = jnp.where(cols >= rows, row - col, -jnp.inf)
                    c_scale = jnp.exp(tot - col)
                    b_scale = jnp.exp(col)
                scores = cb * jnp.exp(seg)
                xdt = x[:, h * hd:(h + 1) * hd] * dt[:, j:j + 1]
                st = state_ref[j]
                y = _dot(scores.astype(BF16), xdt.astype(BF16)) + _dot((cm * c_scale).astype(BF16), st.astype(BF16))
                state_ref[j] = st * jnp.exp(tot) + _dot_tn((bm * b_scale).astype(BF16), xdt.astype(BF16))
                ys.append(y)
        y_ref[0] = jnp.concatenate(ys, axis=-1)

    direction(xf_ref, bf_ref, cf_ref, dtf_ref, yf_ref, 0)
    direction(xb_ref, bb_ref, cb_ref, dtb_ref, yb_ref, 1)


def ssd_scan(xbc, dt_raw, dtbias_row, a_row, tril):
    b, seq, _ = xbc.shape
    q = SSM_CHUNK
    nc = seq // q
    w = SSM_WIDTH
    fwd = lambda col: pl.BlockSpec((1, q, w), lambda bi, i: (bi, i, col))
    bwd = lambda col: pl.BlockSpec((1, q, w), lambda bi, i: (bi, nc - 1 - i, col))
    return pl.pallas_call(
        _ssd_kernel,
        grid=(b, nc),
        in_specs=[fwd(0), fwd(1), fwd(2), pl.BlockSpec((1, q, LANES), lambda bi, i: (bi, i, 0)),
                  bwd(0), bwd(1), bwd(2), pl.BlockSpec((1, q, LANES), lambda bi, i: (bi, nc - 1 - i, 0)),
                  _full((1, LANES)), _full((1, LANES)), _full((q, q))],
        out_specs=[pl.BlockSpec((1, q, w), lambda bi, i: (bi, i, 0)),
                   pl.BlockSpec((1, q, w), lambda bi, i: (bi, nc - 1 - i, 0))],
        out_shape=[jax.ShapeDtypeStruct((b, seq, w), F32)] * 2,
        scratch_shapes=[pltpu.VMEM((2 * SSM_HEADS, SSM_STATE, SSM_HEADDIM), F32)],
        compiler_params=_cparams(("parallel", "arbitrary")),
        name="ssd_scan",
    )(xbc, xbc, xbc, dt_raw, xbc, xbc, xbc, dt_raw, dtbias_row, a_row, tril)


def _outproj_kernel(om_ref, hy_ref, yf_ref, yb_ref, xs_ref, z_ref, x_ref, gm_ref, gh_ref, dsk_ref, gs_ref, wout_ref,
                    gffn_ref, rt_ref, x1_ref, hffn_ref, aff_ref):
    o1 = _rms(om_ref[...], gm_ref[...])
    o2 = _rms(hy_ref[...], gh_ref[...])
    z = z_ref[...]
    y = (yf_ref[...] + yb_ref[...] + xs_ref[...] * dsk_ref[...]) * (z * jax.nn.sigmoid(z))
    gw = SSM_WIDTH // SSM_GROUPS
    gs = gs_ref[...]
    o3 = jnp.concatenate([_rms(y[:, g * gw:(g + 1) * gw], gs[:, g * gw:(g + 1) * gw]) for g in range(SSM_GROUPS)],
                         axis=-1)
    mix = jnp.concatenate([o1, o2, o3], axis=-1).astype(BF16)
    x1 = x_ref[...] + _dot(mix, wout_ref[...])
    x1_ref[...] = x1
    hf = _rms(x1, gffn_ref[...])
    hffn_ref[...] = hf.astype(BF16)
    logits = _dot_nt(rt_ref[...], hf, HIGHEST)
    mx = jnp.max(logits, axis=0, keepdims=True)
    ex = jnp.exp(logits - mx)
    aff_ref[0] = ex / jnp.sum(ex, axis=0, keepdims=True)


def outproj(om, hy, yf, yb, xbc, z, x2d, gm, gh, dsk, gs, wout, gffn, router_t, batch, tm):
    t = x2d.shape[0]
    seq = t // batch
    nb = seq // tm
    row = lambda w: pl.BlockSpec((tm, w), lambda i: (i, 0))
    return pl.pallas_call(
        _outproj_kernel,
        grid=(t // tm,),
        in_specs=[row(om.shape[1]), row(HY_WIDTH), row(SSM_WIDTH), row(SSM_WIDTH), row(SSM_WIDTH), row(SSM_WIDTH),
                  row(D_MODEL), _full(gm.shape), _full(gh.shape), _full(dsk.shape), _full(gs.shape),
                  _full(wout.shape), _full(gffn.shape), _full(router_t.shape)],
        out_specs=[row(D_MODEL), row(D_MODEL),
                   pl.BlockSpec((1, N_EXPERTS, tm), lambda i: (i // nb, 0, i % nb))],
        out_shape=[jax.ShapeDtypeStruct((t, D_MODEL), F32), jax.ShapeDtypeStruct((t, D_MODEL), BF16),
                   jax.ShapeDtypeStruct((batch, N_EXPERTS, seq), F32)],
        compiler_params=_cparams(("parallel",)),
        name="outproj_router",
    )(om, hy, yf, yb, xbc, z, x2d, gm, gh, dsk, gs, wout, gffn, router_t)


def _select_kernel(aff_ref, tri_ref, ones_ref, blk_ref, pos_ref, g_ref, off_ref, *, cap, nrows):
    aff = aff_ref[0]
    er = aff.shape[0]
    ne = er // nrows
    aff3 = aff.reshape(ne, nrows, LANES)
    capf = jnp.float32(cap)

    def count(mask3):
        return jnp.sum(jnp.where(mask3, 1.0, 0.0), axis=(1, 2), keepdims=True)

    def enough(cand):
        return count(aff3 >= cand) >= capf

    top = jnp.full((ne, 1, 1), 2.0, F32)
    for shift in (64, 32, 16, 8, 4, 2, 1):
        cand = top * (2.0 ** -shift)
        top = jnp.where(enough(cand), top, cand)
    p = top * 0.5

    def refine(_, carry):
        lo, step = carry
        cand = lo + step
        return jnp.where(enough(cand), cand, lo), step * 0.5

    lo, _ = lax.fori_loop(0, MANTISSA_STEPS, refine, (p, p * 0.5))
    thr = jnp.min(jnp.where(aff3 >= lo, aff3, jnp.inf), axis=(1, 2), keepdims=True)
    gt3 = aff3 > thr
    eq3 = aff3 == thr
    need = capf - count(gt3)

    tri = tri_ref[...]
    ones = ones_ref[...]
    blk = blk_ref[...]

    def prefix(maskf):
        mb = maskf.astype(BF16)
        within = _dot(mb, tri)
        rowtot = _dot(mb, ones)
        before = _dot(blk, rowtot.astype(BF16))
        return within + before, before

    eqf = jnp.where(eq3, 1.0, 0.0).reshape(er, LANES)
    tie_incl, _ = prefix(eqf)
    tie_rank = (tie_incl - eqf).reshape(ne, nrows, LANES)
    sel3 = jnp.logical_or(gt3, jnp.logical_and(eq3, tie_rank < need))
    self_ = jnp.where(sel3, 1.0, 0.0).reshape(er, LANES)
    incl, before = prefix(self_)
    sel = self_ > 0.5
    pos_ref[0] = jnp.where(sel, (incl - self_).astype(I32), -1)
    g_ref[0] = jnp.where(sel, aff, 0.0)
    off_ref[0] = before.astype(I32)


def moe_select(aff, cap):
    b, ne, seq = aff.shape
    nrows = seq // LANES
    er = ne * nrows
    tri = jnp.asarray(np.triu(np.ones((LANES, LANES), np.float32)), BF16)
    ones = jnp.ones((LANES, LANES), BF16)
    ridx = np.arange(er)
    blk = (ridx[:, None] // nrows == ridx[None, :] // nrows) & (ridx[None, :] < ridx[:, None])
    blk = jnp.asarray(blk.astype(np.float32), BF16)
    spec = pl.BlockSpec((1, er, LANES), lambda bi: (bi, 0, 0))
    pos, gsel, off = pl.pallas_call(
        functools.partial(_select_kernel, cap=cap, nrows=nrows),
        grid=(b,),
        in_specs=[spec, _full(tri.shape), _full(ones.shape), _full(blk.shape)],
        out_specs=[spec] * 3,
        out_shape=[jax.ShapeDtypeStruct((b, er, LANES), I32), jax.ShapeDtypeStruct((b, er, LANES), F32),
                   jax.ShapeDtypeStruct((b, er, LANES), I32)],
        compiler_params=_cparams(("parallel",)),
        name="moe_select",
    )(aff.reshape(b, er, LANES), tri, ones, blk)
    return pos.reshape(b, ne, seq), gsel.reshape(b, ne, seq), off[:, :, 0].reshape(b, ne, nrows)


def _gather_kernel(offs_ref, h_ref, pos_ref, o_ref, acc_ref, *, tb, cap):
    bi = pl.program_id(0)
    ei = pl.program_id(1)
    seq = h_ref.shape[1]
    win = tb + SUBLANES
    acc_ref[...] = jnp.zeros_like(acc_ref)
    riota = lax.broadcasted_iota(I32, (win, tb), 0)

    def body(j, carry):
        off = pl.multiple_of(offs_ref[bi, ei, j] * SUBLANES, SUBLANES)
        t0 = pl.multiple_of(j * tb, tb)
        pos = pos_ref[0, 0, :, pl.ds(t0, tb)]
        onehot = jnp.where(riota + off == pos, 1.0, 0.0).astype(BF16)
        acc_ref[pl.ds(off, win), :] += _dot(onehot, h_ref[0, pl.ds(t0, tb), :])
        return carry

    lax.fori_loop(0, seq // tb, body, 0)
    o_ref[0, 0] = acc_ref[0:cap, :].astype(BF16)


def moe_gather(hffn, pos_row, offs, cap, tb):
    b, seq, d = hffn.shape
    ne = pos_row.shape[1]
    grid_spec = pltpu.PrefetchScalarGridSpec(
        num_scalar_prefetch=1,
        grid=(b, ne),
        in_specs=[pl.BlockSpec((1, seq, d), lambda bi, ei, offs: (bi, 0, 0)),
                  pl.BlockSpec((1, 1, 1, seq), lambda bi, ei, offs: (bi, ei, 0, 0))],
        out_specs=pl.BlockSpec((1, 1, cap, d), lambda bi, ei, offs: (bi, ei, 0, 0)),
        scratch_shapes=[pltpu.VMEM((cap + tb + SUBLANES, d), F32)],
    )
    return pl.pallas_call(
        functools.partial(_gather_kernel, tb=tb, cap=cap),
        grid_spec=grid_spec,
        out_shape=jax.ShapeDtypeStruct((b, ne, cap, d), BF16),
        compiler_params=_cparams(("parallel", "arbitrary")),
        name="moe_gather",
    )(offs, hffn, pos_row)


def _ffn_kernel(xe_ref, wg_ref, wu_ref, wd_ref, o_ref, acc_ref, *, cap):
    f = pl.program_id(1)
    nb = xe_ref.shape[0]
    d = xe_ref.shape[-1]
    xe = xe_ref[...].reshape(nb * cap, d)
    a = _dot(xe, wg_ref[...].astype(BF16))
    u = _dot(xe, wu_ref[...].astype(BF16))
    hid = (a * jax.nn.sigmoid(a) * u).astype(BF16)
    part = _dot(hid, wd_ref[...].astype(BF16))

    @pl.when(f == 0)
    def _():
        acc_ref[...] = part

    @pl.when(f > 0)
    def _():
        acc_ref[...] += part

    @pl.when(f == pl.num_programs(1) - 1)
    def _():
        o_ref[:, 0:cap, :] = acc_ref[...].reshape(nb, cap, d).astype(BF16)
        o_ref[:, cap:, :] = jnp.zeros((nb, o_ref.shape[1] - cap, d), BF16)


def moe_ffn(xe, w_gate, w_up, w_down, layer, cap, cap_pad, tf):
    b, ne, _, d = xe.shape
    ff = w_gate.shape[-1]
    return pl.pallas_call(
        functools.partial(_ffn_kernel, cap=cap),
        grid=(ne, ff // tf),
        in_specs=[pl.BlockSpec((b, None, cap, d), lambda e, f: (0, e, 0, 0)),
                  pl.BlockSpec((None, None, d, tf), lambda e, f: (layer, e, 0, f)),
                  pl.BlockSpec((None, None, d, tf), lambda e, f: (layer, e, 0, f)),
                  pl.BlockSpec((None, None, tf, d), lambda e, f: (layer, e, f, 0))],
        out_specs=pl.BlockSpec((b, None, cap_pad, d), lambda e, f: (0, e, 0, 0)),
        out_shape=jax.ShapeDtypeStruct((b, ne, cap_pad, d), BF16),
        scratch_shapes=[pltpu.VMEM((b * cap, d), F32)],
        compiler_params=_cparams(("parallel", "arbitrary")),
        name="moe_ffn",
    )(xe, w_gate, w_up, w_down)


def _combine_kernel(offs_ref, ye_ref, pos_ref, g_ref, x1_ref, p_ref, gple_ref, wgate_ref, wproj_ref, o_ref, acc_ref,
                    *, tb):
    bi = pl.program_id(0)
    j = pl.program_id(1)
    e = pl.program_id(2)
    win = ye_ref.shape[2]

    @pl.when(e == 0)
    def _():
        acc_ref[...] = jnp.zeros_like(acc_ref)

    off = offs_ref[bi, e, j] * SUBLANES
    lane_e = lax.broadcasted_iota(I32, pos_ref.shape[1:], 1) == e
    pos = jnp.sum(jnp.where(lane_e, pos_ref[0], 0), axis=1, keepdims=True)
    gate = jnp.sum(jnp.where(lane_e, g_ref[0], 0.0), axis=1, keepdims=True)
    liota = lax.broadcasted_iota(I32, (tb, win), 1)
    onehot = jnp.where(liota + off == pos, 1.0, 0.0).astype(BF16)
    acc_ref[...] += gate * _dot(onehot, ye_ref[0, 0])

    @pl.when(e == pl.num_programs(2) - 1)
    def _():
        x2 = x1_ref[0] + acc_ref[...]
        hp = _rms(x2, gple_ref[...]).astype(BF16)
        gt = jax.nn.sigmoid(_dot(hp, wgate_ref[...]))
        o_ref[0] = x2 + _dot(p_ref[0].astype(BF16), wproj_ref[...]) * gt


def moe_combine(ye, pos_col, g_col, offs, x1, p, gple, wgate, wproj, tb):
    b, ne, _, d = ye.shape
    seq = x1.shape[1]
    win = tb + SUBLANES
    grid_spec = pltpu.PrefetchScalarGridSpec(
        num_scalar_prefetch=1,
        grid=(b, seq // tb, ne),
        in_specs=[pl.BlockSpec((pl.Element(1), pl.Element(1), pl.Element(win), pl.Element(d)),
                               lambda bi, j, e, offs: (bi, e, offs[bi, e, j] * SUBLANES, 0)),
                  pl.BlockSpec((1, tb, ne), lambda bi, j, e, offs: (bi, j, 0)),
                  pl.BlockSpec((1, tb, ne), lambda bi, j, e, offs: (bi, j, 0)),
                  pl.BlockSpec((1, tb, d), lambda bi, j, e, offs: (bi, j, 0)),
                  pl.BlockSpec((1, tb, p.shape[-1]), lambda bi, j, e, offs: (bi, j, 0)),
                  pl.BlockSpec(gple.shape, lambda bi, j, e, offs: (0, 0)),
                  pl.BlockSpec(wgate.shape, lambda bi, j, e, offs: (0, 0)),
                  pl.BlockSpec(wproj.shape, lambda bi, j, e, offs: (0, 0))],
        out_specs=pl.BlockSpec((1, tb, d), lambda bi, j, e, offs: (bi, j, 0)),
        scratch_shapes=[pltpu.VMEM((tb, d), F32)],
    )
    return pl.pallas_call(
        functools.partial(_combine_kernel, tb=tb),
        grid_spec=grid_spec,
        out_shape=jax.ShapeDtypeStruct((b, seq, d), F32),
        compiler_params=_cparams(("parallel", "parallel", "arbitrary")),
        name="moe_combine_ple",
    )(offs, ye, pos_col, g_col, x1, p, gple, wgate, wproj)


def _finalnorm_kernel(x_ref, g_ref, o_ref):
    o_ref[...] = _rms(x_ref[...], g_ref[...])


def final_norm(x2d, g, tm):
    t, d = x2d.shape
    return pl.pallas_call(
        _finalnorm_kernel,
        grid=(t // tm,),
        in_specs=[pl.BlockSpec((tm, d), lambda i: (i, 0)), _full(g.shape)],
        out_specs=pl.BlockSpec((tm, d), lambda i: (i, 0)),
        out_shape=jax.ShapeDtypeStruct((t, d), F32),
        compiler_params=_cparams(("parallel",)),
        name="final_norm",
    )(x2d, g)


def _pad_cols(a, width):
    return jnp.pad(a, ((0, 0), (0, width - a.shape[1])))


def _pack_inproj(w_in):
    offs = np.cumsum((0,) + IN_SPLITS)
    cq, ckv, kr, hy, z, xbc, dt = [w_in[:, offs[i]:offs[i + 1]] for i in range(len(IN_SPLITS))]
    d = w_in.shape[0]
    half = MLA_ROPE // 2
    zeros = lambda n: jnp.zeros((d, n), w_in.dtype)
    kr_pad = jnp.concatenate([zeros(MLA_NOPE), kr, zeros(HEAD_PAD - MLA_NOPE - MLA_ROPE)], axis=1)
    kr_swap = jnp.concatenate([zeros(MLA_NOPE), -kr[:, half:], kr[:, :half], zeros(HEAD_PAD - MLA_NOPE - MLA_ROPE)],
                              axis=1)
    wall = jnp.concatenate([cq, ckv, kr_pad, kr_swap, hy, z, xbc, _pad_cols(dt, LANES)], axis=1)
    assert wall.shape[1] == _C_END
    return wall.astype(BF16)


def _pack_mla(w_uq, w_ukv):
    lq = w_uq.shape[0]
    lkv = w_ukv.shape[0]
    half = MLA_ROPE // 2
    padw = HEAD_PAD - MLA_NOPE - MLA_ROPE
    q3 = w_uq.reshape(lq, MLA_HEADS, MLA_NOPE + MLA_ROPE)
    nope, rope = q3[..., :MLA_NOPE], q3[..., MLA_NOPE:]
    zq = jnp.zeros((lq, MLA_HEADS, padw), w_uq.dtype)
    wq = jnp.concatenate([nope, rope, zq], axis=-1).reshape(lq, MLA_HEADS * HEAD_PAD)
    wqs = jnp.concatenate([jnp.zeros_like(nope), -rope[..., half:], rope[..., :half], zq], axis=-1)
    wqs = wqs.reshape(lq, MLA_HEADS * HEAD_PAD)
    kv3 = w_ukv.reshape(lkv, MLA_HEADS, MLA_NOPE + MLA_V)
    knope, vv = kv3[..., :MLA_NOPE], kv3[..., MLA_NOPE:]
    wk = jnp.concatenate([knope, jnp.zeros((lkv, MLA_HEADS, HEAD_PAD - MLA_NOPE), w_ukv.dtype)], axis=-1)
    wk = wk.reshape(lkv, MLA_HEADS * HEAD_PAD)
    wv = vv.reshape(lkv, MLA_HEADS * MLA_V).T
    return wq.astype(BF16), wqs.astype(BF16), wk.astype(BF16), wv.astype(BF16)


def _row(a):
    return a.reshape(1, -1).astype(F32)


TM_PROJ = 512
TQ_ATTN = 256
TK_ATTN = 512
ROWS_CONV = 512
ROWS_FILT = 512
CB_DFT = 2048
TM_OUT = 256
TB_MOE = 256
TF_FFN = 512


def _hyena(hy_u, tabs, kspec, conv_w, conv_b, bias):
    b, seq, _ = hy_u.shape
    c = HY_WIDTH
    parts = [dwconv(hy_u, conv_w[:, i * c:(i + 1) * c], _row(conv_b[i * c:(i + 1) * c]), i * c, c, False, ROWS_CONV)
             for i in range(HY_ORDER + 1)]
    gates, v = parts[:-1], parts[-1]
    half, n1 = tabs["half"], tabs["n1"]
    flat = lambda a: a.reshape(b * half, FFT_N2 * c)
    for o in range(HY_ORDER):
        a = leftmm(tabs["m1"], flat(v), CB_DFT).reshape(2, n1, FFT_N2, c)
        y = spectrum_multiply(a, kspec, o, tabs["f2r"], tabs["f2i"], tabs["twr_row"], tabs["twi_row"],
                              tabs["twr_col"], tabs["twi_col"])
        v = conv_output(tabs["m3"], y.reshape(2 * n1, FFT_N2 * c), flat(v), flat(gates[o]), _row(bias[o]), CB_DFT)
        v = v.reshape(b, seq, c)
    return v


def _hyena_kspec(seq, tabs, w1, b1, freq, w2, b2, w3, decay):
    bands = np.arange(1, HY_BANDS + 1, dtype=np.float64) * 2.0 * np.pi
    mult = np.zeros((1, LANES), np.float32)
    mult[0, 1:1 + HY_BANDS] = bands
    mult[0, 1 + HY_BANDS:1 + 2 * HY_BANDS] = bands
    w1p = jnp.pad(w1.astype(F32), ((0, LANES - w1.shape[0]), (0, 0)))
    ncol = HY_ORDER * 2 * HY_WIDTH
    bwd = (np.arange(ncol) // HY_WIDTH) % 2
    kf = hyena_filters(seq, jnp.asarray(mult), w1p, _row(b1), _row(freq), w2.astype(F32), _row(b2), w3.astype(F32),
                       _row(decay), jnp.asarray(bwd.astype(np.float32)).reshape(1, ncol), ROWS_FILT)
    a = leftmm(tabs["mk"], kf.reshape(tabs["half"], FFT_N2 * ncol), CB_DFT)
    a4 = a.reshape(2, tabs["n1"], FFT_N2, ncol)
    return filter_spectrum(a4, tabs["f2r"], tabs["f2i"], tabs["twr_row"], tabs["twi_row"])


def kernel(x, p, positions, norm_mix, w_in, mla_q_norm, mla_w_uq, mla_kv_norm, mla_w_ukv, mla_out_norm, hy_conv_w,
           hy_conv_b, hy_filt_w1, hy_filt_b1, hy_filt_freq, hy_filt_w2, hy_filt_b2, hy_filt_w3, hy_decay, hy_bias,
           hy_out_norm, ssm_conv_w, ssm_conv_b, ssm_dt_bias, ssm_a_log, ssm_d, ssm_norm, w_out, norm_ffn, moe_router,
           moe_w_gate, moe_w_up, moe_w_down, ple_norm, ple_gate_w, ple_proj, final_norm_g):
    batch, seq, d = x.shape
    depth = w_in.shape[0]
    t = batch * seq
    cap = EC_CAPACITY_FACTOR * seq // N_EXPERTS
    tb = min(TB_MOE, seq)
    cap_pad = cap + 2 * tb
    tm_proj = min(TM_PROJ, seq)
    tm_out = min(TM_OUT, seq)

    freq = np.zeros((1, HEAD_PAD), np.float32)
    inv = ROPE_THETA ** (-np.arange(0, MLA_ROPE, 2, dtype=np.float32) / MLA_ROPE)
    freq[0, MLA_NOPE:MLA_NOPE + MLA_ROPE // 2] = inv
    freq[0, MLA_NOPE + MLA_ROPE // 2:MLA_NOPE + MLA_ROPE] = inv
    cos_t, sin_t = rope_tables(positions.reshape(t, 1), jnp.asarray(freq), tm_proj)

    tabs = _dft_tables(seq, batch)
    tril = jnp.asarray(np.tril(np.ones((SSM_CHUNK, SSM_CHUNK), np.float32)))

    x2d = x.reshape(t, d)
    for i in range(depth):
        wall = _pack_inproj(w_in[i])
        wq, wqs, wk, wv = _pack_mla(mla_w_uq[i], mla_w_ukv[i])
        q, k, v, hy_u, z, xbc_raw, dt_raw = inproj(x2d, _row(norm_mix[i]), wall, _row(mla_q_norm[i]), wq, wqs,
                                                    _row(mla_kv_norm[i]), wk, wv, cos_t, sin_t, tm_proj)
        o_mla = attention(q.reshape(batch, seq, -1), k.reshape(batch, seq, -1), v, min(TQ_ATTN, seq),
                          min(TK_ATTN, seq))

        kspec = _hyena_kspec(seq, tabs, hy_filt_w1[i], hy_filt_b1[i], hy_filt_freq[i], hy_filt_w2[i], hy_filt_b2[i],
                             hy_filt_w3[i], hy_decay[i])
        o_hy = _hyena(hy_u.reshape(batch, seq, -1), tabs, kspec, hy_conv_w[i], hy_conv_b[i], hy_bias[i])

        xbc = dwconv(xbc_raw.reshape(batch, seq, -1), ssm_conv_w[i], _row(ssm_conv_b[i]), 0, SSM_CONV_DIM, True,
                     ROWS_CONV)
        dtbias_row = _pad_cols(_row(ssm_dt_bias[i]), LANES)
        a_row = _pad_cols(_row(-jnp.exp(ssm_a_log[i].astype(F32))), LANES)
        y_f, y_b = ssd_scan(xbc, dt_raw.reshape(batch, seq, -1), dtbias_row, a_row, tril)

        dsk = _row(jnp.repeat(ssm_d[i].astype(F32), SSM_HEADDIM))
        x1, hffn, aff = outproj(o_mla.reshape(t, -1), o_hy.reshape(t, -1), y_f.reshape(t, -1), y_b.reshape(t, -1),
                                xbc.reshape(t, -1), z, x2d, _row(mla_out_norm[i]), _row(hy_out_norm[i]), dsk,
                                _row(ssm_norm[i]), w_out[i].astype(BF16), _row(norm_ffn[i]),
                                moe_router[i].astype(F32).T, batch, tm_out)

        pos, gsel, rowoff = moe_select(aff, cap)
        step = tb // LANES
        offs = rowoff[:, :, ::step] // SUBLANES
        xe = moe_gather(hffn.reshape(batch, seq, d), pos.reshape(batch, N_EXPERTS, 1, seq), offs, cap, tb)
        ye = moe_ffn(xe, moe_w_gate, moe_w_up, moe_w_down, i, cap, cap_pad, TF_FFN)
        x3 = moe_combine(ye, jnp.swapaxes(pos, 1, 2), jnp.swapaxes(gsel, 1, 2), offs, x1.reshape(batch, seq, d),
                         p[i], _row(ple_norm[i]), ple_gate_w[i].astype(BF16), ple_proj[i].astype(BF16), tb)
        x2d = x3.reshape(t, d)
    return final_norm(x2d, _row(final_norm_g), tm_proj).reshape(batch, seq, d)
```

```python
import functools
import math

import numpy as np
import jax
import jax.numpy as jnp
from jax import lax
from jax.experimental import pallas as pl
from jax.experimental.pallas import tpu as pltpu

F32 = jnp.float32
BF16 = jnp.bfloat16
I32 = jnp.int32
HIGHEST = lax.Precision.HIGHEST

EPS = 1e-6
LANES = 128
SUBLANES = 8
VMEM_LIMIT = 56 * 1024 * 1024

D_MODEL = 1024
MLA_HEADS = 8
MLA_NOPE = 64
MLA_ROPE = 32
MLA_V = 64
MLA_Q_LORA = 256
MLA_KV_LORA = 128
HEAD_PAD = 128
ROPE_THETA = 10000.0
HY_WIDTH = 256
HY_ORDER = 2
HY_SHORT = 3
HY_BANDS = 8
HY_FILT_HID = 64
SSM_WIDTH = 256
SSM_HEADDIM = 64
SSM_HEADS = 4
SSM_GROUPS = 2
SSM_STATE = 128
SSM_CONV = 5
SSM_CHUNK = 128
SSM_CONV_DIM = 768
N_EXPERTS = 16
EXPERT_FF = 2048
EC_CAPACITY_FACTOR = 2
PLE_DIM = 256
FFT_N2 = 128
MANTISSA_STEPS = 40
IN_SPLITS = (MLA_Q_LORA, MLA_KV_LORA, MLA_ROPE, 3 * HY_WIDTH, SSM_WIDTH, SSM_CONV_DIM, 2 * SSM_HEADS)


def _cparams(sem, vmem=None):
    return pltpu.CompilerParams(dimension_semantics=sem, vmem_limit_bytes=vmem or VMEM_LIMIT)


def _rms(x, g):
    ms = jnp.mean(x * x, axis=-1, keepdims=True)
    return x * lax.rsqrt(ms + EPS) * g


def _dot(a, b, precision=None):
    return jnp.dot(a, b, preferred_element_type=F32, precision=precision)


def _dot_nt(a, b, precision=None):
    return lax.dot_general(a, b, (((1,), (1,)), ((), ())), preferred_element_type=F32, precision=precision)


def _dot_tn(a, b, precision=None):
    return lax.dot_general(a, b, (((0,), (0,)), ((), ())), preferred_element_type=F32, precision=precision)


def _split_bf16(x):
    hi = x.astype(BF16)
    return hi, (x - hi.astype(F32)).astype(BF16)


def _dot3(a, b):
    ah, al = _split_bf16(a)
    bh, bl = _split_bf16(b)
    return _dot(jnp.concatenate([ah, ah, al], axis=1), jnp.concatenate([bh, bl, bh], axis=0))


def _full(shape):
    n = len(shape)
    return pl.BlockSpec(shape, lambda *_: (0,) * n)


def _rope_kernel(pos_ref, freq_ref, cos_ref, sin_ref):
    ang = pos_ref[...].astype(F32) * freq_ref[...]
    cos_ref[...] = jnp.cos(ang)
    sin_ref[...] = jnp.sin(ang)


def rope_tables(pos_col, freq_row, tm):
    t = pos_col.shape[0]
    return pl.pallas_call(
        _rope_kernel,
        grid=(t // tm,),
        in_specs=[pl.BlockSpec((tm, 1), lambda i: (i, 0)), _full((1, HEAD_PAD))],
        out_specs=[pl.BlockSpec((tm, HEAD_PAD), lambda i: (i, 0))] * 2,
        out_shape=[jax.ShapeDtypeStruct((t, HEAD_PAD), F32)] * 2,
        compiler_params=_cparams(("parallel",)),
        name="rope_tables",
    )(pos_col, freq_row)


_C_CQ = 0
_C_CKV = 256
_C_KR = 384
_C_KRS = 512
_C_HY = 640
_C_Z = 1408
_C_XBC = 1664
_C_DT = 2432
_C_END = 2560


def _inproj_kernel(x_ref, gmix_ref, wall_ref, qn_ref, wq_ref, wqs_ref, kvn_ref, wk_ref, wv_ref, cos_ref, sin_ref,
                   q_ref, k_ref, v_ref, hy_ref, z_ref, xbc_ref, dt_ref, *, scale):
    h = _rms(x_ref[...], gmix_ref[...]).astype(BF16)
    proj = _dot(h, wall_ref[...])
    hy_ref[...] = proj[:, _C_HY:_C_Z]
    z_ref[...] = proj[:, _C_Z:_C_XBC]
    xbc_ref[...] = proj[:, _C_XBC:_C_DT]
    dt_ref[...] = proj[:, _C_DT:_C_END]
    cos = cos_ref[...]
    sin = sin_ref[...]
    cos8 = jnp.concatenate([cos] * MLA_HEADS, axis=-1)
    sin8 = jnp.concatenate([sin] * MLA_HEADS, axis=-1)
    cqn = _rms(proj[:, _C_CQ:_C_CKV], qn_ref[...]).astype(BF16)
    q = _dot(cqn, wq_ref[...])
    qs = _dot(cqn, wqs_ref[...])
    q_ref[...] = ((q * cos8 + qs * sin8) * scale).astype(BF16)
    ckvn = _rms(proj[:, _C_CKV:_C_KR], kvn_ref[...]).astype(BF16)
    kn = _dot(ckvn, wk_ref[...])
    v_ref[...] = _dot_nt(wv_ref[...], ckvn).astype(BF16)
    kr = proj[:, _C_KR:_C_KRS] * cos + proj[:, _C_KRS:_C_HY] * sin
    k_ref[...] = (kn + jnp.concatenate([kr] * MLA_HEADS, axis=-1)).astype(BF16)


def inproj(x2d, gmix, wall, qn, wq, wqs, kvn, wk, wv, cos_t, sin_t, tm):
    t = x2d.shape[0]
    hq = MLA_HEADS * HEAD_PAD
    row = lambda w: pl.BlockSpec((tm, w), lambda i: (i, 0))
    outs = [(hq, BF16), (hq, BF16), None, (3 * HY_WIDTH, F32), (SSM_WIDTH, F32), (SSM_CONV_DIM, F32), (LANES, F32)]
    hv = MLA_HEADS * MLA_V
    return pl.pallas_call(
        functools.partial(_inproj_kernel, scale=(MLA_NOPE + MLA_ROPE) ** -0.5 * math.log2(math.e)),
        grid=(t // tm,),
        in_specs=[row(D_MODEL), _full(gmix.shape), _full(wall.shape), _full(qn.shape), _full(wq.shape),
                  _full(wqs.shape), _full(kvn.shape), _full(wk.shape), _full(wv.shape), row(HEAD_PAD), row(HEAD_PAD)],
        out_specs=[row(o[0]) if o else pl.BlockSpec((hv, tm), lambda i: (0, i)) for o in outs],
        out_shape=[jax.ShapeDtypeStruct((t, o[0]), o[1]) if o else jax.ShapeDtypeStruct((hv, t), BF16)
                   for o in outs],
        compiler_params=_cparams(("parallel",)),
        name="inproj",
    )(x2d, gmix, wall, qn, wq, wqs, kvn, wk, wv, cos_t, sin_t)


def _attn_kernel(q_ref, k_ref, vt_ref, o_ref, st_ref, *, tk):
    seq = k_ref.shape[1]
    tq = q_ref.shape[1]
    nh = st_ref.shape[0]
    npairs = seq // (2 * tk)

    def scores(hh, c, slot):
        off = pl.multiple_of(c * tk, tk)
        st = _dot_nt(k_ref[0, pl.ds(off, tk), hh * HEAD_PAD:(hh + 1) * HEAD_PAD],
                     q_ref[0, :, hh * HEAD_PAD:(hh + 1) * HEAD_PAD])
        st_ref[hh, slot] = st
        return jnp.max(st, axis=0, keepdims=True)

    def update(hh, c, slot, m, l, acc, smax):
        off = pl.multiple_of(c * tk, tk)
        vtc = vt_ref[hh * MLA_V:(hh + 1) * MLA_V, pl.ds(off, tk)]
        m_new = jnp.maximum(m, smax)
        alpha = jnp.exp2(m - m_new)
        p = jnp.exp2(st_ref[hh, slot] - m_new)
        l = l * alpha + jnp.sum(p, axis=0, keepdims=True)
        acc = acc * alpha + _dot(vtc, p.astype(BF16))
        return m_new, l, acc

    def pair(i, carry, last):
        new = []
        for hh in range(nh):
            m, l, acc, smax0 = carry[hh]
            smax1 = scores(hh, 2 * i + 1, 1)
            m, l, acc = update(hh, 2 * i, 0, m, l, acc, smax0)
            smax0 = smax1 if last else scores(hh, 2 * i + 2, 0)
            m, l, acc = update(hh, 2 * i + 1, 1, m, l, acc, smax1)
            new.append((m, l, acc, smax0))
        return tuple(new)

    init = tuple((jnp.full((1, tq), -jnp.inf, F32), jnp.zeros((1, tq), F32), jnp.zeros((MLA_V, tq), F32),
                  scores(hh, 0, 0)) for hh in range(nh))
    carry = lax.fori_loop(0, npairs - 1, functools.partial(pair, last=False), init)
    final = pair(npairs - 1, carry, True)
    o_ref[0] = jnp.concatenate([jnp.transpose(acc / l) for _, l, acc, _ in final], axis=-1)


def attention(q, k, vt, tq, tk):
    b, seq, _ = q.shape
    return pl.pallas_call(
        functools.partial(_attn_kernel, tk=tk),
        grid=(b, MLA_HEADS // 2, seq // tq),
        in_specs=[pl.BlockSpec((1, tq, 2 * HEAD_PAD), lambda bi, hp, qi: (bi, qi, hp)),
                  pl.BlockSpec((1, seq, 2 * HEAD_PAD), lambda bi, hp, qi: (bi, 0, hp)),
                  pl.BlockSpec((2 * MLA_V, seq), lambda bi, hp, qi: (hp, bi))],
        out_specs=pl.BlockSpec((1, tq, 2 * MLA_V), lambda bi, hp, qi: (bi, qi, hp)),
        out_shape=jax.ShapeDtypeStruct((b, seq, MLA_HEADS * MLA_V), F32),
        scratch_shapes=[pltpu.VMEM((2, 2, tk, tq), F32)],
        compiler_params=_cparams(("parallel", "parallel", "parallel")),
        name="attention",
    )(q, k, vt)


def _dwconv_kernel(x_ref, w_ref, b_ref, o_ref, *, width, act, rows):
    seq = x_ref.shape[1]
    pad = width // 2
    nchunks = seq // rows
    w = w_ref[...]
    bias = b_ref[...]

    def body(c, carry):
        r0 = pl.multiple_of(c * rows, rows)
        cur = x_ref[0, pl.ds(r0, rows), :]
        p0 = pl.multiple_of(jnp.maximum(r0 - SUBLANES, 0), SUBLANES)
        n0 = pl.multiple_of(jnp.minimum(r0 + rows, seq - SUBLANES), SUBLANES)
        prev = jnp.where(c > 0, x_ref[0, pl.ds(p0, SUBLANES), :], 0.0)
        nxt = jnp.where(c < nchunks - 1, x_ref[0, pl.ds(n0, SUBLANES), :], 0.0)
        ext = jnp.concatenate([prev, cur, nxt], axis=0)
        acc = bias + ext[SUBLANES - pad:SUBLANES - pad + rows] * w[0:1]
        for kk in range(1, width):
            s0 = SUBLANES - pad + kk
            acc = acc + ext[s0:s0 + rows] * w[kk:kk + 1]
        if act:
            acc = acc * jax.nn.sigmoid(acc)
        o_ref[0, pl.ds(r0, rows), :] = acc
        return carry

    lax.fori_loop(0, nchunks, body, 0)


def dwconv(x, w, bias, col0, ncols, act, rows):
    b, seq, _ = x.shape
    width = w.shape[0]
    cb0 = col0 // LANES
    return pl.pallas_call(
        functools.partial(_dwconv_kernel, width=width, act=act, rows=min(rows, seq)),
        grid=(b, ncols // LANES),
        in_specs=[pl.BlockSpec((1, seq, LANES), lambda bi, ci: (bi, 0, ci + cb0)),
                  pl.BlockSpec((width, LANES), lambda bi, ci: (0, ci)),
                  pl.BlockSpec((1, LANES), lambda bi, ci: (0, ci))],
        out_specs=pl.BlockSpec((1, seq, LANES), lambda bi, ci: (bi, 0, ci)),
        out_shape=jax.ShapeDtypeStruct((b, seq, ncols), F32),
        compiler_params=_cparams(("parallel", "parallel")),
        name="dwconv",
    )(x, w, bias)


def _hyfilt_kernel(mult_ref, w1_ref, b1_ref, fr_ref, w2_ref, b2_ref, w3_ref, dec_ref, bwd_ref, o_ref, *, seq, rows):
    i = pl.program_id(0)
    ridx = lax.broadcasted_iota(I32, (rows, LANES), 0) + i * rows
    lane = lax.broadcasted_iota(I32, (rows, LANES), 1)
    t = ridx.astype(F32) / seq
    ang = t * mult_ref[...]
    feats = jnp.where(lane == 0, t, jnp.where(lane <= HY_BANDS, jnp.sin(ang), jnp.cos(ang)))
    feats = jnp.where(lane < 1 + 2 * HY_BANDS, feats, 0.0)
    fr = fr_ref[...]
    hdn = jnp.sin(fr * (_dot(feats, w1_ref[...], HIGHEST) + b1_ref[...]))
    hdn = jnp.sin(fr * (_dot(hdn, w2_ref[...], HIGHEST) + b2_ref[...]))
    filt = _dot(hdn, w3_ref[...], HIGHEST)
    window = jnp.exp(-t[:, 0:1] * jnp.abs(dec_ref[...]))
    out = filt * window
    keep = jnp.logical_or(ridx[:, 0:1] > 0, bwd_ref[...] < 0.5)
    o_ref[...] = jnp.where(keep, out, 0.0)


def hyena_filters(seq, mult, w1p, b1, fr, w2, b2, w3, dec, bwd_mask, rows):
    ncol = w3.shape[1]
    rows = min(rows, seq)
    args = (mult, w1p, b1, fr, w2, b2, w3, dec, bwd_mask)
    return pl.pallas_call(
        functools.partial(_hyfilt_kernel, seq=seq, rows=rows),
        grid=(seq // rows,),
        in_specs=[_full(a.shape) for a in args],
        out_specs=pl.BlockSpec((rows, ncol), lambda i: (i, 0)),
        out_shape=jax.ShapeDtypeStruct((seq, ncol), F32),
        compiler_params=_cparams(("parallel",)),
        name="hyena_filters",
    )(*args)


def _leftmm_kernel(m_ref, x_ref, o_ref):
    o_ref[...] = _dot3(m_ref[...], x_ref[...])


def leftmm(m, x2d, cb):
    r, kdim = m.shape
    n = x2d.shape[1]
    cb = min(cb, n)
    return pl.pallas_call(
        _leftmm_kernel,
        grid=(n // cb,),
        in_specs=[_full((r, kdim)), pl.BlockSpec((kdim, cb), lambda i: (0, i))],
        out_specs=pl.BlockSpec((r, cb), lambda i: (0, i)),
        out_shape=jax.ShapeDtypeStruct((r, n), F32),
        compiler_params=_cparams(("parallel",)),
        name="dft_outer",
    )(m, x2d)


def _inner_mats(f2r_ref, f2i_ref, twr, twi):
    f2r = f2r_ref[...]
    f2i = f2i_ref[...]
    gr = f2r * twr - f2i * twi
    gi = f2r * twi + f2i * twr
    return gr, gi


def _specfilt_kernel(a_ref, f2r_ref, f2i_ref, twr_ref, twi_ref, o_ref):
    n2 = FFT_N2
    c2 = a_ref.shape[-1]
    gr, gi = _inner_mats(f2r_ref, f2i_ref, twr_ref[0], twi_ref[0])
    gblk = jnp.concatenate([jnp.concatenate([gr, -gi], axis=1), jnp.concatenate([gi, gr], axis=1)], axis=0)
    a = a_ref[:, 0].reshape(2 * n2, c2)
    x = _dot3(gblk, a)
    c = c2 // 2
    o_ref[0, 0, 0] = x[:n2, :c] + x[:n2, c:]
    o_ref[0, 1, 0] = x[n2:, :c] - x[n2:, c:]


def filter_spectrum(a4, f2r, f2i, twr_row, twi_row):
    _, n1, n2, ctot = a4.shape
    c = HY_WIDTH
    return pl.pallas_call(
        _specfilt_kernel,
        grid=(HY_ORDER, n1),
        in_specs=[pl.BlockSpec((2, 1, n2, 2 * c), lambda o, k: (0, k, 0, o)),
                  _full((n2, n2)), _full((n2, n2)),
                  pl.BlockSpec((1, 1, n2), lambda o, k: (k, 0, 0)),
                  pl.BlockSpec((1, 1, n2), lambda o, k: (k, 0, 0))],
        out_specs=pl.BlockSpec((1, 2, 1, n2, c), lambda o, k: (o, 0, k, 0, 0)),
        out_shape=jax.ShapeDtypeStruct((HY_ORDER, 2, n1, n2, c), F32),
        compiler_params=_cparams(("parallel", "parallel")),
        name="filter_spectrum",
    )(a4, f2r, f2i, twr_row, twi_row)


def _specmul_kernel(a_ref, k_ref, f2r_ref, f2i_ref, twr_ref, twi_ref, twrc_ref, twic_ref, o_ref):
    n2 = FFT_N2
    c = a_ref.shape[-1]
    gr, gi = _inner_mats(f2r_ref, f2i_ref, twr_ref[0], twi_ref[0])
    gblk = jnp.concatenate([jnp.concatenate([gr, -gi], axis=1), jnp.concatenate([gi, gr], axis=1)], axis=0)
    grt, git = _inner_mats(f2r_ref, f2i_ref, twrc_ref[0], twic_ref[0])
    gblk_t = jnp.concatenate([jnp.concatenate([grt, git], axis=1), jnp.concatenate([-git, grt], axis=1)], axis=0)
    a = a_ref[:, 0].reshape(2 * n2, c)
    x = _dot3(gblk, a)
    xr, xi = x[:n2], x[n2:]
    kr, ki = k_ref[0, 0, 0], k_ref[0, 1, 0]
    p = jnp.concatenate([xr * kr - xi * ki, xr * ki + xi * kr], axis=0)
    y = _dot3(gblk_t, p)
    o_ref[0, 0] = y[:n2]
    o_ref[1, 0] = y[n2:]


def spectrum_multiply(a4, kspec, order, f2r, f2i, twr_row, twi_row, twr_col, twi_col):
    _, n1, n2, c = a4.shape
    return pl.pallas_call(
        _specmul_kernel,
        grid=(n1,),
        in_specs=[pl.BlockSpec((2, 1, n2, c), lambda k: (0, k, 0, 0)),
                  pl.BlockSpec((1, 2, 1, n2, c), lambda k: (order, 0, k, 0, 0)),
                  _full((n2, n2)), _full((n2, n2)),
                  pl.BlockSpec((1, 1, n2), lambda k: (k, 0, 0)),
                  pl.BlockSpec((1, 1, n2), lambda k: (k, 0, 0)),
                  pl.BlockSpec((1, n2, 1), lambda k: (k, 0, 0)),
                  pl.BlockSpec((1, n2, 1), lambda k: (k, 0, 0))],
        out_specs=pl.BlockSpec((2, 1, n2, c), lambda k: (0, k, 0, 0)),
        out_shape=jax.ShapeDtypeStruct((2, n1, n2, c), F32),
        compiler_params=_cparams(("parallel",)),
        name="spectrum_multiply",
    )(a4, kspec, f2r, f2i, twr_row, twi_row, twr_col, twi_col)


def _convout_kernel(m_ref, y_ref, v_ref, g_ref, bias_ref, o_ref, *, reps):
    y = _dot3(m_ref[...], y_ref[...])
    bias = jnp.concatenate([bias_ref[...]] * reps, axis=-1)
    v = v_ref[...]
    o_ref[...] = (y + v * bias) * g_ref[...]


def conv_output(m3, y2d, v2d, g2d, bias_row, cb):
    r, kdim = m3.shape
    n = y2d.shape[1]
    cb = min(cb, n)
    c = bias_row.shape[1]
    return pl.pallas_call(
        functools.partial(_convout_kernel, reps=cb // c),
        grid=(n // cb,),
        in_specs=[_full((r, kdim)), pl.BlockSpec((kdim, cb), lambda i: (0, i)),
                  pl.BlockSpec((r, cb), lambda i: (0, i)), pl.BlockSpec((r, cb), lambda i: (0, i)),
                  _full((1, c))],
        out_specs=pl.BlockSpec((r, cb), lambda i: (0, i)),
        out_shape=jax.ShapeDtypeStruct((r, n), F32),
        compiler_params=_cparams(("parallel",)),
        name="dft_outer_inverse",
    )(m3, y2d, v2d, g2d, bias_row)


def _dft_tables(seq, batch):
    n2 = FFT_N2
    half = seq // n2
    n1 = 2 * half
    n = n1 * n2
    k1 = np.arange(n1, dtype=np.float64)[:, None]
    nn1 = np.arange(half, dtype=np.float64)[None, :]
    th = 2.0 * np.pi * k1 * nn1 / n1
    c1, s1 = np.cos(th), np.sin(th)
    assert batch == 2, "the two batch entries are packed as real / imaginary parts"
    m1 = np.block([[c1, s1], [-s1, c1]])
    m3 = np.block([[c1.T, -s1.T], [s1.T, c1.T]]) / n
    mk = np.concatenate([c1, -s1], axis=0)
    kk2 = np.arange(n2, dtype=np.float64)
    th2 = 2.0 * np.pi * np.outer(kk2, kk2) / n2
    f2r, f2i = np.cos(th2), -np.sin(th2)
    tht = 2.0 * np.pi * np.outer(np.arange(n1, dtype=np.float64), kk2) / n
    twr, twi = np.cos(tht), -np.sin(tht)
    f = lambda a: jnp.asarray(a, F32)
    return dict(m1=f(m1), m3=f(m3), mk=f(mk), f2r=f(f2r), f2i=f(f2i),
                twr_row=f(twr[:, None, :]), twi_row=f(twi[:, None, :]),
                twr_col=f(twr[:, :, None]), twi_col=f(twi[:, :, None]), n1=n1, half=half)


def _ssd_kernel(xf_ref, bf_ref, cf_ref, dtf_ref, xb_ref, bb_ref, cb_ref, dtb_ref, dtbias_ref, a_ref, tril_ref,
                yf_ref, yb_ref, state_ref):
    q = SSM_CHUNK
    hd = SSM_HEADDIM
    ns = SSM_STATE

    @pl.when(pl.program_id(1) == 0)
    def _():
        state_ref[...] = jnp.zeros_like(state_ref)

    tril = tril_ref[...]
    rows = lax.broadcasted_iota(I32, (q, q), 0)
    cols = lax.broadcasted_iota(I32, (q, q), 1)
    a_row = a_ref[...]
    bias = dtbias_ref[...]

    def direction(x_ref, b_ref, c_ref, dt_ref, y_ref, d):
        dt = jax.nn.softplus(dt_ref[0] + bias)
        dta = dt * a_row
        cs = _dot(tril, dta, HIGHEST)
        ecs = cs - dta
        base = ecs if d else cs
        base_t = jnp.transpose(base)
        total = cs[q - 1:q, :]
        x = x_ref[0]
        ys = []
        for g in range(SSM_GROUPS):
            bm = b_ref[0, :, g * ns:(g + 1) * ns]
            cm = c_ref[0, :, g * ns:(g + 1) * ns]
            cb = _dot_nt(cm.astype(BF16), bm.astype(BF16))
            for hh in range(SSM_HEADS // SSM_GROUPS):
                h = g * (SSM_HEADS // SSM_GROUPS) + hh
                j = d * SSM_HEADS + h
                col = base[:, j:j + 1]
                row = base_t[j:j + 1, :]
                tot = total[:, j:j + 1]
                if d == 0:
                    seg = jnp.where(rows >= cols, col - row, -jnp.inf)
                    c_scale = jnp.exp(col)
                    b_scale = jnp.exp(tot - col)
                else:
                    seg = jnp.where(cols >= rows, row - col, -jnp.inf)
                    c_scale = jnp.exp(tot - col)
                    b_scale = jnp.exp(col)
                scores = cb * jnp.exp(seg)
                xdt = x[:, h * hd:(h + 1) * hd] * dt[:, j:j + 1]
                st = state_ref[j]
                y = _dot(scores.astype(BF16), xdt.astype(BF16)) + _dot((cm * c_scale).astype(BF16), st.astype(BF16))
                state_ref[j] = st * jnp.exp(tot) + _dot_tn((bm * b_scale).astype(BF16), xdt.astype(BF16))
                ys.append(y)
        y_ref[0] = jnp.concatenate(ys, axis=-1)

    direction(xf_ref, bf_ref, cf_ref, dtf_ref, yf_ref, 0)
    direction(xb_ref, bb_ref, cb_ref, dtb_ref, yb_ref, 1)


def ssd_scan(xbc, dt_raw, dtbias_row, a_row, tril):
    b, seq, _ = xbc.shape
    q = SSM_CHUNK
    nc = seq // q
    w = SSM_WIDTH
    fwd = lambda col: pl.BlockSpec((1, q, w), lambda bi, i: (bi, i, col))
    bwd = lambda col: pl.BlockSpec((1, q, w), lambda bi, i: (bi, nc - 1 - i, col))
    return pl.pallas_call(
        _ssd_kernel,
        grid=(b, nc),
        in_specs=[fwd(0), fwd(1), fwd(2), pl.BlockSpec((1, q, LANES), lambda bi, i: (bi, i, 0)),
                  bwd(0), bwd(1), bwd(2), pl.BlockSpec((1, q, LANES), lambda bi, i: (bi, nc - 1 - i, 0)),
                  _full((1, LANES)), _full((1, LANES)), _full((q, q))],
        out_specs=[pl.BlockSpec((1, q, w), lambda bi, i: (bi, i, 0)),
                   pl.BlockSpec((1, q, w), lambda bi, i: (bi, nc - 1 - i, 0))],
        out_shape=[jax.ShapeDtypeStruct((b, seq, w), F32)] * 2,
        scratch_shapes=[pltpu.VMEM((2 * SSM_HEADS, SSM_STATE, SSM_HEADDIM), F32)],
        compiler_params=_cparams(("parallel", "arbitrary")),
        name="ssd_scan",
    )(xbc, xbc, xbc, dt_raw, xbc, xbc, xbc, dt_raw, dtbias_row, a_row, tril)


def _outproj_kernel(om_ref, hy_ref, yf_ref, yb_ref, xs_ref, z_ref, x_ref, gm_ref, gh_ref, dsk_ref, gs_ref, wout_ref,
                    gffn_ref, rt_ref, x1_ref, hffn_ref, aff_ref):
    o1 = _rms(om_ref[...], gm_ref[...])
    o2 = _rms(hy_ref[...], gh_ref[...])
    z = z_ref[...]
    y = (yf_ref[...] + yb_ref[...] + xs_ref[...] * dsk_ref[...]) * (z * jax.nn.sigmoid(z))
    gw = SSM_WIDTH // SSM_GROUPS
    gs = gs_ref[...]
    o3 = jnp.concatenate([_rms(y[:, g * gw:(g + 1) * gw], gs[:, g * gw:(g + 1) * gw]) for g in range(SSM_GROUPS)],
                         axis=-1)
    mix = jnp.concatenate([o1, o2, o3], axis=-1).astype(BF16)
    x1 = x_ref[...] + _dot(mix, wout_ref[...])
    x1_ref[...] = x1
    hf = _rms(x1, gffn_ref[...])
    hffn_ref[...] = hf.astype(BF16)
    logits = _dot_nt(rt_ref[...], hf, HIGHEST)
    mx = jnp.max(logits, axis=0, keepdims=True)
    ex = jnp.exp(logits - mx)
    aff_ref[0] = ex / jnp.sum(ex, axis=0, keepdims=True)


def outproj(om, hy, yf, yb, xbc, z, x2d, gm, gh, dsk, gs, wout, gffn, router_t, batch, tm):
    t = x2d.shape[0]
    seq = t // batch
    nb = seq // tm
    row = lambda w: pl.BlockSpec((tm, w), lambda i: (i, 0))
    return pl.pallas_call(
        _outproj_kernel,
        grid=(t // tm,),
        in_specs=[row(om.shape[1]), row(HY_WIDTH), row(SSM_WIDTH), row(SSM_WIDTH), row(SSM_WIDTH), row(SSM_WIDTH),
                  row(D_MODEL), _full(gm.shape), _full(gh.shape), _full(dsk.shape), _full(gs.shape),
                  _full(wout.shape), _full(gffn.shape), _full(router_t.shape)],
        out_specs=[row(D_MODEL), row(D_MODEL),
                   pl.BlockSpec((1, N_EXPERTS, tm), lambda i: (i // nb, 0, i % nb))],
        out_shape=[jax.ShapeDtypeStruct((t, D_MODEL), F32), jax.ShapeDtypeStruct((t, D_MODEL), BF16),
                   jax.ShapeDtypeStruct((batch, N_EXPERTS, seq), F32)],
        compiler_params=_cparams(("parallel",)),
        name="outproj_router",
    )(om, hy, yf, yb, xbc, z, x2d, gm, gh, dsk, gs, wout, gffn, router_t)


def _select_kernel(aff_ref, tri_ref, ones_ref, blk_ref, pos_ref, g_ref, off_ref, *, cap, nrows):
    aff = aff_ref[0]
    er = aff.shape[0]
    ne = er // nrows
    aff3 = aff.reshape(ne, nrows, LANES)
    capf = jnp.float32(cap)

    def count(mask3):
        return jnp.sum(jnp.where(mask3, 1.0, 0.0), axis=(1, 2), keepdims=True)

    def enough(cand):
        return count(aff3 >= cand) >= capf

    top = jnp.full((ne, 1, 1), 2.0, F32)
    for shift in (64, 32, 16, 8, 4, 2, 1):
        cand = top * (2.0 ** -shift)
        top = jnp.where(enough(cand), top, cand)
    p = top * 0.5

    def refine(_, carry):
        lo, step = carry
        cand = lo + step
        return jnp.where(enough(cand), cand, lo), step * 0.5

    lo, _ = lax.fori_loop(0, MANTISSA_STEPS, refine, (p, p * 0.5))
    thr = jnp.min(jnp.where(aff3 >= lo, aff3, jnp.inf), axis=(1, 2), keepdims=True)
    gt3 = aff3 > thr
    eq3 = aff3 == thr
    need = capf - count(gt3)

    tri = tri_ref[...]
    ones = ones_ref[...]
    blk = blk_ref[...]

    def prefix(maskf):
        mb = maskf.astype(BF16)
        within = _dot(mb, tri)
        rowtot = _dot(mb, ones)
        before = _dot(blk, rowtot.astype(BF16))
        return within + before, before

    eqf = jnp.where(eq3, 1.0, 0.0).reshape(er, LANES)
    tie_incl, _ = prefix(eqf)
    tie_rank = (tie_incl - eqf).reshape(ne, nrows, LANES)
    sel3 = jnp.logical_or(gt3, jnp.logical_and(eq3, tie_rank < need))
    self_ = jnp.where(sel3, 1.0, 0.0).reshape(er, LANES)
    incl, before = prefix(self_)
    sel = self_ > 0.5
    pos_ref[0] = jnp.where(sel, (incl - self_).astype(I32), -1)
    g_ref[0] = jnp.where(sel, aff, 0.0)
    off_ref[0] = before.astype(I32)


def moe_select(aff, cap):
    b, ne, seq = aff.shape
    nrows = seq // LANES
    er = ne * nrows
    tri = jnp.asarray(np.triu(np.ones((LANES, LANES), np.float32)), BF16)
    ones = jnp.ones((LANES, LANES), BF16)
    ridx = np.arange(er)
    blk = (ridx[:, None] // nrows == ridx[None, :] // nrows) & (ridx[None, :] < ridx[:, None])
    blk = jnp.asarray(blk.astype(np.float32), BF16)
    spec = pl.BlockSpec((1, er, LANES), lambda bi: (bi, 0, 0))
    pos, gsel, off = pl.pallas_call(
        functools.partial(_select_kernel, cap=cap, nrows=nrows),
        grid=(b,),
        in_specs=[spec, _full(tri.shape), _full(ones.shape), _full(blk.shape)],
        out_specs=[spec] * 3,
        out_shape=[jax.ShapeDtypeStruct((b, er, LANES), I32), jax.ShapeDtypeStruct((b, er, LANES), F32),
                   jax.ShapeDtypeStruct((b, er, LANES), I32)],
        compiler_params=_cparams(("parallel",)),
        name="moe_select",
    )(aff.reshape(b, er, LANES), tri, ones, blk)
    return pos.reshape(b, ne, seq), gsel.reshape(b, ne, seq), off[:, :, 0].reshape(b, ne, nrows)


def _gather_kernel(offs_ref, h_ref, pos_ref, o_ref, acc_ref, *, tb, cap):
    bi = pl.program_id(0)
    ei = pl.program_id(1)
    seq = h_ref.shape[1]
    win = tb + SUBLANES
    acc_ref[0:SUBLANES, :] = jnp.zeros((SUBLANES, acc_ref.shape[1]), F32)
    riota = lax.broadcasted_iota(I32, (win, tb), 0)

    def body(j, carry):
        off = pl.multiple_of(offs_ref[bi, ei, j] * SUBLANES, SUBLANES)
        t0 = pl.multiple_of(j * tb, tb)
        pos = pos_ref[0, 0, :, pl.ds(t0, tb)]
        onehot = jnp.where(riota + off == pos, 1.0, 0.0).astype(BF16)
        rows = _dot(onehot, h_ref[0, pl.ds(t0, tb), :])
        acc_ref[pl.ds(off, SUBLANES), :] += rows[0:SUBLANES]
        acc_ref[pl.ds(off + SUBLANES, tb), :] = rows[SUBLANES:]
        return carry

    lax.fori_loop(0, seq // tb, body, 0)
    o_ref[0, 0] = acc_ref[0:cap, :].astype(BF16)


def moe_gather(hffn, pos_row, offs, cap, tb):
    b, seq, d = hffn.shape
    ne = pos_row.shape[1]
    grid_spec = pltpu.PrefetchScalarGridSpec(
        num_scalar_prefetch=1,
        grid=(b, ne),
        in_specs=[pl.BlockSpec((1, seq, d), lambda bi, ei, offs: (bi, 0, 0)),
                  pl.BlockSpec((1, 1, 1, seq), lambda bi, ei, offs: (bi, ei, 0, 0))],
        out_specs=pl.BlockSpec((1, 1, cap, d), lambda bi, ei, offs: (bi, ei, 0, 0)),
        scratch_shapes=[pltpu.VMEM((cap + tb + SUBLANES, d), F32)],
    )
    return pl.pallas_call(
        functools.partial(_gather_kernel, tb=tb, cap=cap),
        grid_spec=grid_spec,
        out_shape=jax.ShapeDtypeStruct((b, ne, cap, d), BF16),
        compiler_params=_cparams(("parallel", "arbitrary")),
        name="moe_gather",
    )(offs, hffn, pos_row)


def _ffn_kernel(xe_ref, wg_ref, wu_ref, wd_ref, o_ref, acc_ref, *, cap):
    f = pl.program_id(1)
    nb = xe_ref.shape[0]
    d = xe_ref.shape[-1]
    xe = xe_ref[...].reshape(nb * cap, d)
    a = _dot(xe, wg_ref[...].astype(BF16))
    u = _dot(xe, wu_ref[...].astype(BF16))
    hid = (a * jax.nn.sigmoid(a) * u).astype(BF16)
    part = _dot(hid, wd_ref[...].astype(BF16))

    @pl.when(f == 0)
    def _():
        acc_ref[...] = part

    @pl.when(f > 0)
    def _():
        acc_ref[...] += part

    @pl.when(f == pl.num_programs(1) - 1)
    def _():
        o_ref[:, 0:cap, :] = acc_ref[...].reshape(nb, cap, d).astype(BF16)
        o_ref[:, cap:, :] = jnp.zeros((nb, o_ref.shape[1] - cap, d), BF16)


def moe_ffn(xe, w_gate, w_up, w_down, layer, cap, cap_pad, tf):
    b, ne, _, d = xe.shape
    ff = w_gate.shape[-1]
    return pl.pallas_call(
        functools.partial(_ffn_kernel, cap=cap),
        grid=(ne, ff // tf),
        in_specs=[pl.BlockSpec((b, None, cap, d), lambda e, f: (0, e, 0, 0)),
                  pl.BlockSpec((None, None, d, tf), lambda e, f: (layer, e, 0, f)),
                  pl.BlockSpec((None, None, d, tf), lambda e, f: (layer, e, 0, f)),
                  pl.BlockSpec((None, None, tf, d), lambda e, f: (layer, e, f, 0))],
        out_specs=pl.BlockSpec((b, None, cap_pad, d), lambda e, f: (0, e, 0, 0)),
        out_shape=jax.ShapeDtypeStruct((b, ne, cap_pad, d), BF16),
        scratch_shapes=[pltpu.VMEM((b * cap, d), F32)],
        compiler_params=_cparams(("parallel", "arbitrary")),
        name="moe_ffn",
    )(xe, w_gate, w_up, w_down)


def _combine_kernel(offs_ref, ye_hbm, pos_ref, g_ref, x1_ref, p_ref, gple_ref, wgate_ref, wproj_ref, o_ref, buf_ref,
                    sem_ref, *, tb):
    bi = pl.program_id(0)
    j = pl.program_id(1)
    ne = pos_ref.shape[-1]
    win = buf_ref.shape[1]

    def window_copy(e, slot):
        off = pl.multiple_of(offs_ref[bi, e, j] * SUBLANES, SUBLANES)
        return pltpu.make_async_copy(ye_hbm.at[bi, e, pl.ds(off, win), :], buf_ref.at[slot], sem_ref.at[slot])

    window_copy(0, 0).start()
    pos_all = pos_ref[0]
    g_all = g_ref[0]
    liota = lax.broadcasted_iota(I32, (tb, win), 1)
    acc = x1_ref[0]
    for e in range(ne):
        slot = e % 2
        if e + 1 < ne:
            window_copy(e + 1, 1 - slot).start()
        off = offs_ref[bi, e, j] * SUBLANES
        onehot = jnp.where(liota + off == pos_all[:, e:e + 1], 1.0, 0.0).astype(BF16)
        window_copy(e, slot).wait()
        acc = acc + g_all[:, e:e + 1] * _dot(onehot, buf_ref[slot])
    hp = _rms(acc, gple_ref[...]).astype(BF16)
    gt = jax.nn.sigmoid(_dot(hp, wgate_ref[...]))
    o_ref[0] = acc + _dot(p_ref[0].astype(BF16), wproj_ref[...]) * gt


def moe_combine(ye, pos_col, g_col, offs, x1, p, gple, wgate, wproj, tb):
    b, ne, _, d = ye.shape
    seq = x1.shape[1]
    win = tb + SUBLANES
    grid_spec = pltpu.PrefetchScalarGridSpec(
        num_scalar_prefetch=1,
        grid=(b, seq // tb),
        in_specs=[pl.BlockSpec(memory_space=pl.ANY),
                  pl.BlockSpec((1, tb, ne), lambda bi, j, offs: (bi, j, 0)),
                  pl.BlockSpec((1, tb, ne), lambda bi, j, offs: (bi, j, 0)),
                  pl.BlockSpec((1, tb, d), lambda bi, j, offs: (bi, j, 0)),
                  pl.BlockSpec((1, tb, p.shape[-1]), lambda bi, j, offs: (bi, j, 0)),
                  pl.BlockSpec(gple.shape, lambda bi, j, offs: (0, 0)),
                  pl.BlockSpec(wgate.shape, lambda bi, j, offs: (0, 0)),
                  pl.BlockSpec(wproj.shape, lambda bi, j, offs: (0, 0))],
        out_specs=pl.BlockSpec((1, tb, d), lambda bi, j, offs: (bi, j, 0)),
        scratch_shapes=[pltpu.VMEM((2, win, d), BF16), pltpu.SemaphoreType.DMA((2,))],
    )
    return pl.pallas_call(
        functools.partial(_combine_kernel, tb=tb),
        grid_spec=grid_spec,
        out_shape=jax.ShapeDtypeStruct((b, seq, d), F32),
        compiler_params=_cparams(("parallel", "parallel")),
        name="moe_combine_ple",
    )(offs, ye, pos_col, g_col, x1, p, gple, wgate, wproj)


def _finalnorm_kernel(x_ref, g_ref, o_ref):
    o_ref[...] = _rms(x_ref[...], g_ref[...])


def final_norm(x2d, g, tm):
    t, d = x2d.shape
    return pl.pallas_call(
        _finalnorm_kernel,
        grid=(t // tm,),
        in_specs=[pl.BlockSpec((tm, d), lambda i: (i, 0)), _full(g.shape)],
        out_specs=pl.BlockSpec((tm, d), lambda i: (i, 0)),
        out_shape=jax.ShapeDtypeStruct((t, d), F32),
        compiler_params=_cparams(("parallel",)),
        name="final_norm",
    )(x2d, g)


def _pad_cols(a, width):
    return jnp.pad(a, ((0, 0), (0, width - a.shape[1])))


def _pack_inproj(w_in):
    offs = np.cumsum((0,) + IN_SPLITS)
    cq, ckv, kr, hy, z, xbc, dt = [w_in[:, offs[i]:offs[i + 1]] for i in range(len(IN_SPLITS))]
    d = w_in.shape[0]
    half = MLA_ROPE // 2
    zeros = lambda n: jnp.zeros((d, n), w_in.dtype)
    kr_pad = jnp.concatenate([zeros(MLA_NOPE), kr, zeros(HEAD_PAD - MLA_NOPE - MLA_ROPE)], axis=1)
    kr_swap = jnp.concatenate([zeros(MLA_NOPE), -kr[:, half:], kr[:, :half], zeros(HEAD_PAD - MLA_NOPE - MLA_ROPE)],
                              axis=1)
    wall = jnp.concatenate([cq, ckv, kr_pad, kr_swap, hy, z, xbc, _pad_cols(dt, LANES)], axis=1)
    assert wall.shape[1] == _C_END
    return wall.astype(BF16)


def _pack_mla(w_uq, w_ukv):
    lq = w_uq.shape[0]
    lkv = w_ukv.shape[0]
    half = MLA_ROPE // 2
    padw = HEAD_PAD - MLA_NOPE - MLA_ROPE
    q3 = w_uq.reshape(lq, MLA_HEADS, MLA_NOPE + MLA_ROPE)
    nope, rope = q3[..., :MLA_NOPE], q3[..., MLA_NOPE:]
    zq = jnp.zeros((lq, MLA_HEADS, padw), w_uq.dtype)
    wq = jnp.concatenate([nope, rope, zq], axis=-1).reshape(lq, MLA_HEADS * HEAD_PAD)
    wqs = jnp.concatenate([jnp.zeros_like(nope), -rope[..., half:], rope[..., :half], zq], axis=-1)
    wqs = wqs.reshape(lq, MLA_HEADS * HEAD_PAD)
    kv3 = w_ukv.reshape(lkv, MLA_HEADS, MLA_NOPE + MLA_V)
    knope, vv = kv3[..., :MLA_NOPE], kv3[..., MLA_NOPE:]
    wk = jnp.concatenate([knope, jnp.zeros((lkv, MLA_HEADS, HEAD_PAD - MLA_NOPE), w_ukv.dtype)], axis=-1)
    wk = wk.reshape(lkv, MLA_HEADS * HEAD_PAD)
    wv = vv.reshape(lkv, MLA_HEADS * MLA_V).T
    return wq.astype(BF16), wqs.astype(BF16), wk.astype(BF16), wv.astype(BF16)


def _row(a):
    return a.reshape(1, -1).astype(F32)


TM_PROJ = 512
TQ_ATTN = 1024
TK_ATTN = 512
ROWS_CONV = 512
ROWS_FILT = 512
CB_DFT = 2048
TM_OUT = 256
TB_MOE = 256
TB_COMBINE = 128
TF_FFN = 512


def _hyena(hy_u, tabs, kspec, conv_w, conv_b, bias):
    b, seq, _ = hy_u.shape
    c = HY_WIDTH
    parts = [dwconv(hy_u, conv_w[:, i * c:(i + 1) * c], _row(conv_b[i * c:(i + 1) * c]), i * c, c, False, ROWS_CONV)
             for i in range(HY_ORDER + 1)]
    gates, v = parts[:-1], parts[-1]
    half, n1 = tabs["half"], tabs["n1"]
    flat = lambda a: a.reshape(b * half, FFT_N2 * c)
    for o in range(HY_ORDER):
        a = leftmm(tabs["m1"], flat(v), CB_DFT).reshape(2, n1, FFT_N2, c)
        y = spectrum_multiply(a, kspec, o, tabs["f2r"], tabs["f2i"], tabs["twr_row"], tabs["twi_row"],
                              tabs["twr_col"], tabs["twi_col"])
        v = conv_output(tabs["m3"], y.reshape(2 * n1, FFT_N2 * c), flat(v), flat(gates[o]), _row(bias[o]), CB_DFT)
        v = v.reshape(b, seq, c)
    return v


def _hyena_kspec(seq, tabs, w1, b1, freq, w2, b2, w3, decay):
    bands = np.arange(1, HY_BANDS + 1, dtype=np.float64) * 2.0 * np.pi
    mult = np.zeros((1, LANES), np.float32)
    mult[0, 1:1 + HY_BANDS] = bands
    mult[0, 1 + HY_BANDS:1 + 2 * HY_BANDS] = bands
    w1p = jnp.pad(w1.astype(F32), ((0, LANES - w1.shape[0]), (0, 0)))
    ncol = HY_ORDER * 2 * HY_WIDTH
    bwd = (np.arange(ncol) // HY_WIDTH) % 2
    kf = hyena_filters(seq, jnp.asarray(mult), w1p, _row(b1), _row(freq), w2.astype(F32), _row(b2), w3.astype(F32),
                       _row(decay), jnp.asarray(bwd.astype(np.float32)).reshape(1, ncol), ROWS_FILT)
    a = leftmm(tabs["mk"], kf.reshape(tabs["half"], FFT_N2 * ncol), CB_DFT)
    a4 = a.reshape(2, tabs["n1"], FFT_N2, ncol)
    return filter_spectrum(a4, tabs["f2r"], tabs["f2i"], tabs["twr_row"], tabs["twi_row"])


def kernel(x, p, positions, norm_mix, w_in, mla_q_norm, mla_w_uq, mla_kv_norm, mla_w_ukv, mla_out_norm, hy_conv_w,
           hy_conv_b, hy_filt_w1, hy_filt_b1, hy_filt_freq, hy_filt_w2, hy_filt_b2, hy_filt_w3, hy_decay, hy_bias,
           hy_out_norm, ssm_conv_w, ssm_conv_b, ssm_dt_bias, ssm_a_log, ssm_d, ssm_norm, w_out, norm_ffn, moe_router,
           moe_w_gate, moe_w_up, moe_w_down, ple_norm, ple_gate_w, ple_proj, final_norm_g):
    batch, seq, d = x.shape
    depth = w_in.shape[0]
    t = batch * seq
    cap = EC_CAPACITY_FACTOR * seq // N_EXPERTS
    tb = min(TB_MOE, seq)
    tbc = min(TB_COMBINE, seq)
    cap_pad = cap + 2 * tb
    tm_proj = min(TM_PROJ, seq)
    tm_out = min(TM_OUT, seq)

    freq = np.zeros((1, HEAD_PAD), np.float32)
    inv = ROPE_THETA ** (-np.arange(0, MLA_ROPE, 2, dtype=np.float32) / MLA_ROPE)
    freq[0, MLA_NOPE:MLA_NOPE + MLA_ROPE // 2] = inv
    freq[0, MLA_NOPE + MLA_ROPE // 2:MLA_NOPE + MLA_ROPE] = inv
    cos_t, sin_t = rope_tables(positions.reshape(t, 1), jnp.asarray(freq), tm_proj)

    tabs = _dft_tables(seq, batch)
    tril = jnp.asarray(np.tril(np.ones((SSM_CHUNK, SSM_CHUNK), np.float32)))

    x2d = x.reshape(t, d)
    for i in range(depth):
        wall = _pack_inproj(w_in[i])
        wq, wqs, wk, wv = _pack_mla(mla_w_uq[i], mla_w_ukv[i])
        q, k, v, hy_u, z, xbc_raw, dt_raw = inproj(x2d, _row(norm_mix[i]), wall, _row(mla_q_norm[i]), wq, wqs,
                                                    _row(mla_kv_norm[i]), wk, wv, cos_t, sin_t, tm_proj)
        o_mla = attention(q.reshape(batch, seq, -1), k.reshape(batch, seq, -1), v, min(TQ_ATTN, seq),
                          min(TK_ATTN, seq))

        kspec = _hyena_kspec(seq, tabs, hy_filt_w1[i], hy_filt_b1[i], hy_filt_freq[i], hy_filt_w2[i], hy_filt_b2[i],
                             hy_filt_w3[i], hy_decay[i])
        o_hy = _hyena(hy_u.reshape(batch, seq, -1), tabs, kspec, hy_conv_w[i], hy_conv_b[i], hy_bias[i])

        xbc = dwconv(xbc_raw.reshape(batch, seq, -1), ssm_conv_w[i], _row(ssm_conv_b[i]), 0, SSM_CONV_DIM, True,
                     ROWS_CONV)
        dtbias_row = _pad_cols(_row(ssm_dt_bias[i]), LANES)
        a_row = _pad_cols(_row(-jnp.exp(ssm_a_log[i].astype(F32))), LANES)
        y_f, y_b = ssd_scan(xbc, dt_raw.reshape(batch, seq, -1), dtbias_row, a_row, tril)

        dsk = _row(jnp.repeat(ssm_d[i].astype(F32), SSM_HEADDIM))
        x1, hffn, aff = outproj(o_mla.reshape(t, -1), o_hy.reshape(t, -1), y_f.reshape(t, -1), y_b.reshape(t, -1),
                                xbc.reshape(t, -1), z, x2d, _row(mla_out_norm[i]), _row(hy_out_norm[i]), dsk,
                                _row(ssm_norm[i]), w_out[i].astype(BF16), _row(norm_ffn[i]),
                                moe_router[i].astype(F32).T, batch, tm_out)

        pos, gsel, rowoff = moe_select(aff, cap)
        offs_g = rowoff[:, :, ::tb // LANES] // SUBLANES
        offs_c = rowoff[:, :, ::tbc // LANES] // SUBLANES
        xe = moe_gather(hffn.reshape(batch, seq, d), pos.reshape(batch, N_EXPERTS, 1, seq), offs_g, cap, tb)
        ye = moe_ffn(xe, moe_w_gate, moe_w_up, moe_w_down, i, cap, cap_pad, TF_FFN)
        x3 = moe_combine(ye, jnp.swapaxes(pos, 1, 2), jnp.swapaxes(gsel, 1, 2), offs_c, x1.reshape(batch, seq, d),
                         p[i], _row(ple_norm[i]), ple_gate_w[i].astype(BF16), ple_proj[i].astype(BF16), tbc)
        x2d = x3.reshape(t, d)
    return final_norm(x2d, _row(final_norm_g), tm_proj).reshape(batch, seq, d)
```

```python
import functools
import math

import numpy as np
import jax
import jax.numpy as jnp
from jax import lax
from jax.experimental import pallas as pl
from jax.experimental.pallas import tpu as pltpu

F32 = jnp.float32
BF16 = jnp.bfloat16
I32 = jnp.int32
HIGHEST = lax.Precision.HIGHEST

EPS = 1e-6
LANES = 128
SUBLANES = 8
VMEM_LIMIT = 56 * 1024 * 1024

D_MODEL = 1024
MLA_HEADS = 8
MLA_NOPE = 64
MLA_ROPE = 32
MLA_V = 64
MLA_Q_LORA = 256
MLA_KV_LORA = 128
HEAD_PAD = 128
ATTN_SUM_ROWS = 16
ROPE_THETA = 10000.0
HY_WIDTH = 256
HY_ORDER = 2
HY_SHORT = 3
HY_BANDS = 8
HY_FILT_HID = 64
SSM_WIDTH = 256
SSM_HEADDIM = 64
SSM_HEADS = 4
SSM_GROUPS = 2
SSM_STATE = 128
SSM_CONV = 5
SSM_CHUNK = 128
SSM_CONV_DIM = 768
N_EXPERTS = 16
EXPERT_FF = 2048
EC_CAPACITY_FACTOR = 2
PLE_DIM = 256
FFT_N2 = 128
MANTISSA_STEPS = 40
IN_SPLITS = (MLA_Q_LORA, MLA_KV_LORA, MLA_ROPE, 3 * HY_WIDTH, SSM_WIDTH, SSM_CONV_DIM, 2 * SSM_HEADS)


def _cparams(sem, vmem=None):
    return pltpu.CompilerParams(dimension_semantics=sem, vmem_limit_bytes=vmem or VMEM_LIMIT)


def _rms(x, g):
    ms = jnp.mean(x * x, axis=-1, keepdims=True)
    return x * lax.rsqrt(ms + EPS) * g


def _dot(a, b, precision=None):
    return jnp.dot(a, b, preferred_element_type=F32, precision=precision)


def _dot_nt(a, b, precision=None):
    return lax.dot_general(a, b, (((1,), (1,)), ((), ())), preferred_element_type=F32, precision=precision)


def _dot_tn(a, b, precision=None):
    return lax.dot_general(a, b, (((0,), (0,)), ((), ())), preferred_element_type=F32, precision=precision)


def _split_bf16(x):
    hi = x.astype(BF16)
    return hi, (x - hi.astype(F32)).astype(BF16)


def _dot3(a, b):
    ah, al = _split_bf16(a)
    bh, bl = _split_bf16(b)
    return _dot(jnp.concatenate([ah, ah, al], axis=1), jnp.concatenate([bh, bl, bh], axis=0))


def _full(shape):
    n = len(shape)
    return pl.BlockSpec(shape, lambda *_: (0,) * n)


def _rope_kernel(pos_ref, freq_ref, cos_ref, sin_ref):
    ang = pos_ref[...].astype(F32) * freq_ref[...]
    cos_ref[...] = jnp.cos(ang)
    sin_ref[...] = jnp.sin(ang)


def rope_tables(pos_col, freq_row, tm):
    t = pos_col.shape[0]
    return pl.pallas_call(
        _rope_kernel,
        grid=(t // tm,),
        in_specs=[pl.BlockSpec((tm, 1), lambda i: (i, 0)), _full((1, HEAD_PAD))],
        out_specs=[pl.BlockSpec((tm, HEAD_PAD), lambda i: (i, 0))] * 2,
        out_shape=[jax.ShapeDtypeStruct((t, HEAD_PAD), F32)] * 2,
        compiler_params=_cparams(("parallel",)),
        name="rope_tables",
    )(pos_col, freq_row)


_C_CQ = 0
_C_CKV = 256
_C_KR = 384
_C_KRS = 512
_C_HY = 640
_C_Z = 1408
_C_XBC = 1664
_C_DT = 2432
_C_END = 2560


def _inproj_kernel(x_ref, gmix_ref, wall_ref, qn_ref, wq_ref, wqs_ref, kvn_ref, wk_ref, wv_ref, cos_ref, sin_ref,
                   q_ref, k_ref, v_ref, hy_ref, z_ref, xbc_ref, dt_ref, *, scale):
    h = _rms(x_ref[...], gmix_ref[...]).astype(BF16)
    proj = _dot(h, wall_ref[...])
    hy_ref[...] = proj[:, _C_HY:_C_Z]
    z_ref[...] = proj[:, _C_Z:_C_XBC]
    xbc_ref[...] = proj[:, _C_XBC:_C_DT]
    dt_ref[...] = proj[:, _C_DT:_C_END]
    cos = cos_ref[...]
    sin = sin_ref[...]
    cos8 = jnp.concatenate([cos] * MLA_HEADS, axis=-1)
    sin8 = jnp.concatenate([sin] * MLA_HEADS, axis=-1)
    cqn = _rms(proj[:, _C_CQ:_C_CKV], qn_ref[...]).astype(BF16)
    q = _dot(cqn, wq_ref[...])
    qs = _dot(cqn, wqs_ref[...])
    q_ref[...] = ((q * cos8 + qs * sin8) * scale).astype(BF16)
    ckvn = _rms(proj[:, _C_CKV:_C_KR], kvn_ref[...]).astype(BF16)
    kn = _dot(ckvn, wk_ref[...])
    v_ref[...] = _dot_nt(wv_ref[...], ckvn).astype(BF16)
    kr = proj[:, _C_KR:_C_KRS] * cos + proj[:, _C_KRS:_C_HY] * sin
    k_ref[...] = (kn + jnp.concatenate([kr] * MLA_HEADS, axis=-1)).astype(BF16)


def inproj(x2d, gmix, wall, qn, wq, wqs, kvn, wk, wv, cos_t, sin_t, tm):
    t = x2d.shape[0]
    hq = MLA_HEADS * HEAD_PAD
    row = lambda w: pl.BlockSpec((tm, w), lambda i: (i, 0))
    outs = [(hq, BF16), (hq, BF16), None, (3 * HY_WIDTH, F32), (SSM_WIDTH, F32), (SSM_CONV_DIM, F32), (LANES, F32)]
    hv = MLA_HEADS * MLA_V
    return pl.pallas_call(
        functools.partial(_inproj_kernel, scale=(MLA_NOPE + MLA_ROPE) ** -0.5 * math.log2(math.e)),
        grid=(t // tm,),
        in_specs=[row(D_MODEL), _full(gmix.shape), _full(wall.shape), _full(qn.shape), _full(wq.shape),
                  _full(wqs.shape), _full(kvn.shape), _full(wk.shape), _full(wv.shape), row(HEAD_PAD), row(HEAD_PAD)],
        out_specs=[row(o[0]) if o else pl.BlockSpec((hv, tm), lambda i: (0, i)) for o in outs],
        out_shape=[jax.ShapeDtypeStruct((t, o[0]), o[1]) if o else jax.ShapeDtypeStruct((hv, t), BF16)
                   for o in outs],
        compiler_params=_cparams(("parallel",)),
        name="inproj",
    )(x2d, gmix, wall, qn, wq, wqs, kvn, wk, wv, cos_t, sin_t)


def _attn_kernel(q_ref, k_ref, vt_ref, o_ref, st_ref, *, tk):
    seq = k_ref.shape[1]
    tq = q_ref.shape[1]
    nh = st_ref.shape[0]
    npairs = seq // (2 * tk)

    def scores(hh, c, slot):
        off = pl.multiple_of(c * tk, tk)
        st = _dot_nt(k_ref[0, pl.ds(off, tk), hh * HEAD_PAD:(hh + 1) * HEAD_PAD],
                     q_ref[0, :, hh * HEAD_PAD:(hh + 1) * HEAD_PAD])
        st_ref[hh, slot] = st
        return jnp.max(st, axis=0, keepdims=True)

    ones_rows = jnp.ones((ATTN_SUM_ROWS, tk), BF16)

    def update(hh, c, slot, m, acc, smax):
        off = pl.multiple_of(c * tk, tk)
        vtc = jnp.concatenate([vt_ref[hh * MLA_V:(hh + 1) * MLA_V, pl.ds(off, tk)], ones_rows], axis=0)
        m_new = jnp.maximum(m, smax)
        alpha = jnp.exp2(m - m_new)
        p = jnp.exp2(st_ref[hh, slot] - m_new)
        acc = acc * alpha + _dot(vtc, p.astype(BF16))
        return m_new, acc

    def pair(i, carry, last):
        new = []
        for hh in range(nh):
            m, acc, smax0 = carry[hh]
            smax1 = scores(hh, 2 * i + 1, 1)
            m, acc = update(hh, 2 * i, 0, m, acc, smax0)
            smax0 = smax1 if last else scores(hh, 2 * i + 2, 0)
            m, acc = update(hh, 2 * i + 1, 1, m, acc, smax1)
            new.append((m, acc, smax0))
        return tuple(new)

    init = tuple((jnp.full((1, tq), -jnp.inf, F32), jnp.zeros((MLA_V + ATTN_SUM_ROWS, tq), F32), scores(hh, 0, 0))
                 for hh in range(nh))
    carry = lax.fori_loop(0, npairs - 1, functools.partial(pair, last=False), init)
    final = pair(npairs - 1, carry, True)
    o_ref[0] = jnp.concatenate([jnp.transpose(acc[:MLA_V] / acc[MLA_V:MLA_V + 1]) for _, acc, _ in final], axis=-1)


def attention(q, k, vt, tq, tk):
    b, seq, _ = q.shape
    return pl.pallas_call(
        functools.partial(_attn_kernel, tk=tk),
        grid=(b, MLA_HEADS // 2, seq // tq),
        in_specs=[pl.BlockSpec((1, tq, 2 * HEAD_PAD), lambda bi, hp, qi: (bi, qi, hp)),
                  pl.BlockSpec((1, seq, 2 * HEAD_PAD), lambda bi, hp, qi: (bi, 0, hp)),
                  pl.BlockSpec((2 * MLA_V, seq), lambda bi, hp, qi: (hp, bi))],
        out_specs=pl.BlockSpec((1, tq, 2 * MLA_V), lambda bi, hp, qi: (bi, qi, hp)),
        out_shape=jax.ShapeDtypeStruct((b, seq, MLA_HEADS * MLA_V), F32),
        scratch_shapes=[pltpu.VMEM((2, 2, tk, tq), F32)],
        compiler_params=_cparams(("parallel", "parallel", "parallel")),
        name="attention",
    )(q, k, vt)


def _dwconv_kernel(x_ref, w_ref, b_ref, o_ref, *, width, act, rows):
    seq = x_ref.shape[1]
    pad = width // 2
    nchunks = seq // rows
    w = w_ref[...]
    bias = b_ref[...]

    def body(c, carry):
        r0 = pl.multiple_of(c * rows, rows)
        cur = x_ref[0, pl.ds(r0, rows), :]
        p0 = pl.multiple_of(jnp.maximum(r0 - SUBLANES, 0), SUBLANES)
        n0 = pl.multiple_of(jnp.minimum(r0 + rows, seq - SUBLANES), SUBLANES)
        prev = jnp.where(c > 0, x_ref[0, pl.ds(p0, SUBLANES), :], 0.0)
        nxt = jnp.where(c < nchunks - 1, x_ref[0, pl.ds(n0, SUBLANES), :], 0.0)
        ext = jnp.concatenate([prev, cur, nxt], axis=0)
        acc = bias + ext[SUBLANES - pad:SUBLANES - pad + rows] * w[0:1]
        for kk in range(1, width):
            s0 = SUBLANES - pad + kk
            acc = acc + ext[s0:s0 + rows] * w[kk:kk + 1]
        if act:
            acc = acc * jax.nn.sigmoid(acc)
        o_ref[0, pl.ds(r0, rows), :] = acc
        return carry

    lax.fori_loop(0, nchunks, body, 0)


def dwconv(x, w, bias, col0, ncols, act, rows):
    b, seq, _ = x.shape
    width = w.shape[0]
    cb0 = col0 // LANES
    return pl.pallas_call(
        functools.partial(_dwconv_kernel, width=width, act=act, rows=min(rows, seq)),
        grid=(b, ncols // LANES),
        in_specs=[pl.BlockSpec((1, seq, LANES), lambda bi, ci: (bi, 0, ci + cb0)),
                  pl.BlockSpec((width, LANES), lambda bi, ci: (0, ci)),
                  pl.BlockSpec((1, LANES), lambda bi, ci: (0, ci))],
        out_specs=pl.BlockSpec((1, seq, LANES), lambda bi, ci: (bi, 0, ci)),
        out_shape=jax.ShapeDtypeStruct((b, seq, ncols), F32),
        compiler_params=_cparams(("parallel", "parallel")),
        name="dwconv",
    )(x, w, bias)


def _hyfilt_kernel(mult_ref, w1_ref, b1_ref, fr_ref, w2_ref, b2_ref, w3_ref, dec_ref, bwd_ref, o_ref, *, seq, rows):
    i = pl.program_id(0)
    ridx = lax.broadcasted_iota(I32, (rows, LANES), 0) + i * rows
    lane = lax.broadcasted_iota(I32, (rows, LANES), 1)
    t = ridx.astype(F32) / seq
    ang = t * mult_ref[...]
    feats = jnp.where(lane == 0, t, jnp.where(lane <= HY_BANDS, jnp.sin(ang), jnp.cos(ang)))
    feats = jnp.where(lane < 1 + 2 * HY_BANDS, feats, 0.0)
    fr = fr_ref[...]
    hdn = jnp.sin(fr * (_dot(feats, w1_ref[...], HIGHEST) + b1_ref[...]))
    hdn = jnp.sin(fr * (_dot(hdn, w2_ref[...], HIGHEST) + b2_ref[...]))
    filt = _dot(hdn, w3_ref[...], HIGHEST)
    window = jnp.exp(-t[:, 0:1] * jnp.abs(dec_ref[...]))
    out = filt * window
    keep = jnp.logical_or(ridx[:, 0:1] > 0, bwd_ref[...] < 0.5)
    o_ref[...] = jnp.where(keep, out, 0.0)


def hyena_filters(seq, mult, w1p, b1, fr, w2, b2, w3, dec, bwd_mask, rows):
    ncol = w3.shape[1]
    rows = min(rows, seq)
    args = (mult, w1p, b1, fr, w2, b2, w3, dec, bwd_mask)
    return pl.pallas_call(
        functools.partial(_hyfilt_kernel, seq=seq, rows=rows),
        grid=(seq // rows,),
        in_specs=[_full(a.shape) for a in args],
        out_specs=pl.BlockSpec((rows, ncol), lambda i: (i, 0)),
        out_shape=jax.ShapeDtypeStruct((seq, ncol), F32),
        compiler_params=_cparams(("parallel",)),
        name="hyena_filters",
    )(*args)


def _leftmm_kernel(m_ref, x_ref, o_ref):
    o_ref[...] = _dot3(m_ref[...], x_ref[...])


def leftmm(m, x2d, cb):
    r, kdim = m.shape
    n = x2d.shape[1]
    cb = min(cb, n)
    return pl.pallas_call(
        _leftmm_kernel,
        grid=(n // cb,),
        in_specs=[_full((r, kdim)), pl.BlockSpec((kdim, cb), lambda i: (0, i))],
        out_specs=pl.BlockSpec((r, cb), lambda i: (0, i)),
        out_shape=jax.ShapeDtypeStruct((r, n), F32),
        compiler_params=_cparams(("parallel",)),
        name="dft_outer",
    )(m, x2d)


def _stack3_lhs(f):
    hi, lo = _split_bf16(f)
    return jnp.concatenate([hi, hi, lo], axis=1)


def _stack3_rhs(a):
    hi, lo = _split_bf16(a)
    return jnp.concatenate([hi, lo, hi], axis=0)


def _twiddle(ar, ai, twr, twi, conj):
    if conj:
        return ar * twr + ai * twi, ai * twr - ar * twi
    return ar * twr - ai * twi, ai * twr + ar * twi


def _specfilt_kernel(a_ref, fblk_ref, twr_ref, twi_ref, o_ref, lhs_ref):
    n2 = FFT_N2
    c = a_ref.shape[-1] // 2

    @pl.when((pl.program_id(0) == 0) & (pl.program_id(1) == 0))
    def _():
        lhs_ref[...] = _stack3_lhs(fblk_ref[...])

    for kk in range(a_ref.shape[1]):
        br, bi = _twiddle(a_ref[0, kk], a_ref[1, kk], twr_ref[kk], twi_ref[kk], False)
        x = _dot(lhs_ref[...], _stack3_rhs(jnp.concatenate([br, bi], axis=0)))
        o_ref[0, 0, kk] = x[:n2, :c] + x[:n2, c:]
        o_ref[0, 1, kk] = x[n2:, :c] - x[n2:, c:]


def filter_spectrum(a4, fblk, twr_col, twi_col, k1s):
    _, n1, n2, ctot = a4.shape
    c = HY_WIDTH
    return pl.pallas_call(
        _specfilt_kernel,
        grid=(HY_ORDER, n1 // k1s),
        in_specs=[pl.BlockSpec((2, k1s, n2, 2 * c), lambda o, k: (0, k, 0, o)),
                  _full(fblk.shape),
                  pl.BlockSpec((k1s, n2, 1), lambda o, k: (k, 0, 0)),
                  pl.BlockSpec((k1s, n2, 1), lambda o, k: (k, 0, 0))],
        out_specs=pl.BlockSpec((1, 2, k1s, n2, c), lambda o, k: (o, 0, k, 0, 0)),
        out_shape=jax.ShapeDtypeStruct((HY_ORDER, 2, n1, n2, c), F32),
        scratch_shapes=[pltpu.VMEM((2 * n2, 6 * n2), BF16)],
        compiler_params=_cparams(("arbitrary", "arbitrary")),
        name="filter_spectrum",
    )(a4, fblk, twr_col, twi_col)


def _specmul_kernel(a_ref, k_ref, fblk_ref, fblk_t_ref, twr_ref, twi_ref, o_ref, lhs_ref):
    n2 = FFT_N2

    @pl.when(pl.program_id(0) == 0)
    def _():
        lhs_ref[0] = _stack3_lhs(fblk_ref[...])
        lhs_ref[1] = _stack3_lhs(fblk_t_ref[...])

    for kk in range(a_ref.shape[1]):
        twr, twi = twr_ref[kk], twi_ref[kk]
        br, bi = _twiddle(a_ref[0, kk], a_ref[1, kk], twr, twi, False)
        x = _dot(lhs_ref[0], _stack3_rhs(jnp.concatenate([br, bi], axis=0)))
        xr, xi = x[:n2], x[n2:]
        kr, ki = k_ref[0, 0, kk], k_ref[0, 1, kk]
        p = jnp.concatenate([xr * kr - xi * ki, xr * ki + xi * kr], axis=0)
        y = _dot(lhs_ref[1], _stack3_rhs(p))
        yr, yi = _twiddle(y[:n2], y[n2:], twr, twi, True)
        o_ref[0, kk] = yr
        o_ref[1, kk] = yi


def spectrum_multiply(a4, kspec, order, fblk, fblk_t, twr_col, twi_col, k1s):
    _, n1, n2, c = a4.shape
    return pl.pallas_call(
        _specmul_kernel,
        grid=(n1 // k1s,),
        in_specs=[pl.BlockSpec((2, k1s, n2, c), lambda k: (0, k, 0, 0)),
                  pl.BlockSpec((1, 2, k1s, n2, c), lambda k: (order, 0, k, 0, 0)),
                  _full(fblk.shape), _full(fblk_t.shape),
                  pl.BlockSpec((k1s, n2, 1), lambda k: (k, 0, 0)),
                  pl.BlockSpec((k1s, n2, 1), lambda k: (k, 0, 0))],
        out_specs=pl.BlockSpec((2, k1s, n2, c), lambda k: (0, k, 0, 0)),
        out_shape=jax.ShapeDtypeStruct((2, n1, n2, c), F32),
        scratch_shapes=[pltpu.VMEM((2, 2 * n2, 6 * n2), BF16)],
        compiler_params=_cparams(("arbitrary",)),
        name="spectrum_multiply",
    )(a4, kspec, fblk, fblk_t, twr_col, twi_col)


def _convout_kernel(m_ref, y_ref, v_ref, g_ref, bias_ref, o_ref, *, reps):
    y = _dot3(m_ref[...], y_ref[...])
    bias = jnp.concatenate([bias_ref[...]] * reps, axis=-1)
    v = v_ref[...]
    o_ref[...] = (y + v * bias) * g_ref[...]


def conv_output(m3, y2d, v2d, g2d, bias_row, cb):
    r, kdim = m3.shape
    n = y2d.shape[1]
    cb = min(cb, n)
    c = bias_row.shape[1]
    return pl.pallas_call(
        functools.partial(_convout_kernel, reps=cb // c),
        grid=(n // cb,),
        in_specs=[_full((r, kdim)), pl.BlockSpec((kdim, cb), lambda i: (0, i)),
                  pl.BlockSpec((r, cb), lambda i: (0, i)), pl.BlockSpec((r, cb), lambda i: (0, i)),
                  _full((1, c))],
        out_specs=pl.BlockSpec((r, cb), lambda i: (0, i)),
        out_shape=jax.ShapeDtypeStruct((r, n), F32),
        compiler_params=_cparams(("parallel",)),
        name="dft_outer_inverse",
    )(m3, y2d, v2d, g2d, bias_row)


def _dft_tables(seq, batch):
    n2 = FFT_N2
    half = seq // n2
    n1 = 2 * half
    n = n1 * n2
    k1 = np.arange(n1, dtype=np.float64)[:, None]
    nn1 = np.arange(half, dtype=np.float64)[None, :]
    th = 2.0 * np.pi * k1 * nn1 / n1
    c1, s1 = np.cos(th), np.sin(th)
    assert batch == 2, "the two batch entries are packed as real / imaginary parts"
    m1 = np.block([[c1, s1], [-s1, c1]])
    m3 = np.block([[c1.T, -s1.T], [s1.T, c1.T]]) / n
    mk = np.concatenate([c1, -s1], axis=0)
    kk2 = np.arange(n2, dtype=np.float64)
    th2 = 2.0 * np.pi * np.outer(kk2, kk2) / n2
    c2, s2 = np.cos(th2), np.sin(th2)
    fblk = np.block([[c2, s2], [-s2, c2]])
    tht = 2.0 * np.pi * np.outer(np.arange(n1, dtype=np.float64), kk2) / n
    twr, twi = np.cos(tht), -np.sin(tht)
    f = lambda a: jnp.asarray(a, F32)
    return dict(m1=f(m1), m3=f(m3), mk=f(mk), fblk=f(fblk), fblk_t=f(fblk.T),
                twr_col=f(twr[:, :, None]), twi_col=f(twi[:, :, None]), n1=n1, half=half)


def _ssd_kernel(xf_ref, bf_ref, cf_ref, dtf_ref, xb_ref, bb_ref, cb_ref, dtb_ref, dtbias_ref, a_ref, tril_ref,
                yf_ref, yb_ref, state_ref):
    q = SSM_CHUNK
    hd = SSM_HEADDIM
    ns = SSM_STATE

    @pl.when(pl.program_id(1) == 0)
    def _():
        state_ref[...] = jnp.zeros_like(state_ref)

    tril = tril_ref[...]
    rows = lax.broadcasted_iota(I32, (q, q), 0)
    cols = lax.broadcasted_iota(I32, (q, q), 1)
    a_row = a_ref[...]
    bias = dtbias_ref[...]

    def direction(x_ref, b_ref, c_ref, dt_ref, y_ref, d):
        dt = jax.nn.softplus(dt_ref[0] + bias)
        dta = dt * a_row
        cs = _dot(tril, dta, HIGHEST)
        ecs = cs - dta
        base = ecs if d else cs
        base_t = jnp.transpose(base)
        total = cs[q - 1:q, :]
        x = x_ref[0]
        ys = []
        for g in range(SSM_GROUPS):
            bm = b_ref[0, :, g * ns:(g + 1) * ns]
            cm = c_ref[0, :, g * ns:(g + 1) * ns]
            cb = _dot_nt(cm.astype(BF16), bm.astype(BF16))
            for hh in range(SSM_HEADS // SSM_GROUPS):
                h = g * (SSM_HEADS // SSM_GROUPS) + hh
                j = d * SSM_HEADS + h
                col = base[:, j:j + 1]
                row = base_t[j:j + 1, :]
                tot = total[:, j:j + 1]
                if d == 0:
                    seg = jnp.where(rows >= cols, col - row, -jnp.inf)
                    c_scale = jnp.exp(col)
                    b_scale = jnp.exp(tot - col)
                else:
                    seg = jnp.where(cols >= rows, row - col, -jnp.inf)
                    c_scale = jnp.exp(tot - col)
                    b_scale = jnp.exp(col)
                scores = cb * jnp.exp(seg)
                xdt = x[:, h * hd:(h + 1) * hd] * dt[:, j:j + 1]
                st = state_ref[j]
                y = _dot(scores.astype(BF16), xdt.astype(BF16)) + _dot((cm * c_scale).astype(BF16), st.astype(BF16))
                state_ref[j] = st * jnp.exp(tot) + _dot_tn((bm * b_scale).astype(BF16), xdt.astype(BF16))
                ys.append(y)
        y_ref[0] = jnp.concatenate(ys, axis=-1)

    direction(xf_ref, bf_ref, cf_ref, dtf_ref, yf_ref, 0)
    direction(xb_ref, bb_ref, cb_ref, dtb_ref, yb_ref, 1)


def ssd_scan(xbc, dt_raw, dtbias_row, a_row, tril):
    b, seq, _ = xbc.shape
    q = SSM_CHUNK
    nc = seq // q
    w = SSM_WIDTH
    fwd = lambda col: pl.BlockSpec((1, q, w), lambda bi, i: (bi, i, col))
    bwd = lambda col: pl.BlockSpec((1, q, w), lambda bi, i: (bi, nc - 1 - i, col))
    return pl.pallas_call(
        _ssd_kernel,
        grid=(b, nc),
        in_specs=[fwd(0), fwd(1), fwd(2), pl.BlockSpec((1, q, LANES), lambda bi, i: (bi, i, 0)),
                  bwd(0), bwd(1), bwd(2), pl.BlockSpec((1, q, LANES), lambda bi, i: (bi, nc - 1 - i, 0)),
                  _full((1, LANES)), _full((1, LANES)), _full((q, q))],
        out_specs=[pl.BlockSpec((1, q, w), lambda bi, i: (bi, i, 0)),
                   pl.BlockSpec((1, q, w), lambda bi, i: (bi, nc - 1 - i, 0))],
        out_shape=[jax.ShapeDtypeStruct((b, seq, w), F32)] * 2,
        scratch_shapes=[pltpu.VMEM((2 * SSM_HEADS, SSM_STATE, SSM_HEADDIM), F32)],
        compiler_params=_cparams(("parallel", "arbitrary")),
        name="ssd_scan",
    )(xbc, xbc, xbc, dt_raw, xbc, xbc, xbc, dt_raw, dtbias_row, a_row, tril)


def _outproj_kernel(om_ref, hy_ref, yf_ref, yb_ref, xs_ref, z_ref, x_ref, gm_ref, gh_ref, dsk_ref, gs_ref, wout_ref,
                    gffn_ref, rt_ref, x1_ref, hffn_ref, aff_ref):
    o1 = _rms(om_ref[...], gm_ref[...])
    o2 = _rms(hy_ref[...], gh_ref[...])
    z = z_ref[...]
    y = (yf_ref[...] + yb_ref[...] + xs_ref[...] * dsk_ref[...]) * (z * jax.nn.sigmoid(z))
    gw = SSM_WIDTH // SSM_GROUPS
    gs = gs_ref[...]
    o3 = jnp.concatenate([_rms(y[:, g * gw:(g + 1) * gw], gs[:, g * gw:(g + 1) * gw]) for g in range(SSM_GROUPS)],
                         axis=-1)
    mix = jnp.concatenate([o1, o2, o3], axis=-1).astype(BF16)
    x1 = x_ref[...] + _dot(mix, wout_ref[...])
    x1_ref[...] = x1
    hf = _rms(x1, gffn_ref[...])
    hffn_ref[...] = hf.astype(BF16)
    logits = _dot_nt(rt_ref[...], hf, HIGHEST)
    mx = jnp.max(logits, axis=0, keepdims=True)
    ex = jnp.exp(logits - mx)
    aff_ref[0] = ex / jnp.sum(ex, axis=0, keepdims=True)


def outproj(om, hy, yf, yb, xbc, z, x2d, gm, gh, dsk, gs, wout, gffn, router_t, batch, tm):
    t = x2d.shape[0]
    seq = t // batch
    nb = seq // tm
    row = lambda w: pl.BlockSpec((tm, w), lambda i: (i, 0))
    return pl.pallas_call(
        _outproj_kernel,
        grid=(t // tm,),
        in_specs=[row(om.shape[1]), row(HY_WIDTH), row(SSM_WIDTH), row(SSM_WIDTH), row(SSM_WIDTH), row(SSM_WIDTH),
                  row(D_MODEL), _full(gm.shape), _full(gh.shape), _full(dsk.shape), _full(gs.shape),
                  _full(wout.shape), _full(gffn.shape), _full(router_t.shape)],
        out_specs=[row(D_MODEL), row(D_MODEL),
                   pl.BlockSpec((1, N_EXPERTS, tm), lambda i: (i // nb, 0, i % nb))],
        out_shape=[jax.ShapeDtypeStruct((t, D_MODEL), F32), jax.ShapeDtypeStruct((t, D_MODEL), BF16),
                   jax.ShapeDtypeStruct((batch, N_EXPERTS, seq), F32)],
        compiler_params=_cparams(("parallel",)),
        name="outproj_router",
    )(om, hy, yf, yb, xbc, z, x2d, gm, gh, dsk, gs, wout, gffn, router_t)


def _select_kernel(aff_ref, tri_ref, ones_ref, blk_ref, pos_ref, g_ref, off_ref, *, cap, nrows):
    aff = aff_ref[0]
    er = aff.shape[0]
    ne = er // nrows
    aff3 = aff.reshape(ne, nrows, LANES)
    capf = jnp.float32(cap)

    def count(mask3):
        return jnp.sum(jnp.where(mask3, 1.0, 0.0), axis=(1, 2), keepdims=True)

    def enough(cand):
        return count(aff3 >= cand) >= capf

    top = jnp.full((ne, 1, 1), 2.0, F32)
    for shift in (64, 32, 16, 8, 4, 2, 1):
        cand = top * (2.0 ** -shift)
        top = jnp.where(enough(cand), top, cand)
    p = top * 0.5

    def refine(_, carry):
        lo, step = carry
        cand = lo + step
        return jnp.where(enough(cand), cand, lo), step * 0.5

    lo, _ = lax.fori_loop(0, MANTISSA_STEPS, refine, (p, p * 0.5))
    thr = jnp.min(jnp.where(aff3 >= lo, aff3, jnp.inf), axis=(1, 2), keepdims=True)
    gt3 = aff3 > thr
    eq3 = aff3 == thr
    need = capf - count(gt3)

    tri = tri_ref[...]
    ones = ones_ref[...]
    blk = blk_ref[...]

    def prefix(maskf):
        mb = maskf.astype(BF16)
        within = _dot(mb, tri)
        rowtot = _dot(mb, ones)
        before = _dot(blk, rowtot.astype(BF16))
        return within + before, before

    eqf = jnp.where(eq3, 1.0, 0.0).reshape(er, LANES)
    tie_incl, _ = prefix(eqf)
    tie_rank = (tie_incl - eqf).reshape(ne, nrows, LANES)
    sel3 = jnp.logical_or(gt3, jnp.logical_and(eq3, tie_rank < need))
    self_ = jnp.where(sel3, 1.0, 0.0).reshape(er, LANES)
    incl, before = prefix(self_)
    sel = self_ > 0.5
    pos_ref[0] = jnp.where(sel, (incl - self_).astype(I32), -1)
    g_ref[0] = jnp.where(sel, aff, 0.0)
    off_ref[0] = before.astype(I32)


def moe_select(aff, cap):
    b, ne, seq = aff.shape
    nrows = seq // LANES
    er = ne * nrows
    tri = jnp.asarray(np.triu(np.ones((LANES, LANES), np.float32)), BF16)
    ones = jnp.ones((LANES, LANES), BF16)
    ridx = np.arange(er)
    blk = (ridx[:, None] // nrows == ridx[None, :] // nrows) & (ridx[None, :] < ridx[:, None])
    blk = jnp.asarray(blk.astype(np.float32), BF16)
    spec = pl.BlockSpec((1, er, LANES), lambda bi: (bi, 0, 0))
    pos, gsel, off = pl.pallas_call(
        functools.partial(_select_kernel, cap=cap, nrows=nrows),
        grid=(b,),
        in_specs=[spec, _full(tri.shape), _full(ones.shape), _full(blk.shape)],
        out_specs=[spec] * 3,
        out_shape=[jax.ShapeDtypeStruct((b, er, LANES), I32), jax.ShapeDtypeStruct((b, er, LANES), F32),
                   jax.ShapeDtypeStruct((b, er, LANES), I32)],
        compiler_params=_cparams(("parallel",)),
        name="moe_select",
    )(aff.reshape(b, er, LANES), tri, ones, blk)
    return pos.reshape(b, ne, seq), gsel.reshape(b, ne, seq), off[:, :, 0].reshape(b, ne, nrows)


def _gather_kernel(offs_ref, h_ref, pos_ref, o_ref, acc_ref, *, tb, cap):
    bi = pl.program_id(0)
    ei = pl.program_id(1)
    seq = h_ref.shape[1]
    win = tb + SUBLANES
    acc_ref[0:SUBLANES, :] = jnp.zeros((SUBLANES, acc_ref.shape[1]), F32)
    riota = lax.broadcasted_iota(I32, (win, tb), 0)

    def body(j, carry):
        off = pl.multiple_of(offs_ref[bi, ei, j] * SUBLANES, SUBLANES)
        t0 = pl.multiple_of(j * tb, tb)
        pos = pos_ref[0, 0, :, pl.ds(t0, tb)]
        onehot = jnp.where(riota + off == pos, 1.0, 0.0).astype(BF16)
        rows = _dot(onehot, h_ref[0, pl.ds(t0, tb), :])
        acc_ref[pl.ds(off, SUBLANES), :] += rows[0:SUBLANES]
        acc_ref[pl.ds(off + SUBLANES, tb), :] = rows[SUBLANES:]
        return carry

    lax.fori_loop(0, seq // tb, body, 0)
    o_ref[0, 0] = acc_ref[0:cap, :].astype(BF16)


def moe_gather(hffn, pos_row, offs, cap, tb):
    b, seq, d = hffn.shape
    ne = pos_row.shape[1]
    grid_spec = pltpu.PrefetchScalarGridSpec(
        num_scalar_prefetch=1,
        grid=(b, ne),
        in_specs=[pl.BlockSpec((1, seq, d), lambda bi, ei, offs: (bi, 0, 0)),
                  pl.BlockSpec((1, 1, 1, seq), lambda bi, ei, offs: (bi, ei, 0, 0))],
        out_specs=pl.BlockSpec((1, 1, cap, d), lambda bi, ei, offs: (bi, ei, 0, 0)),
        scratch_shapes=[pltpu.VMEM((cap + tb + SUBLANES, d), F32)],
    )
    return pl.pallas_call(
        functools.partial(_gather_kernel, tb=tb, cap=cap),
        grid_spec=grid_spec,
        out_shape=jax.ShapeDtypeStruct((b, ne, cap, d), BF16),
        compiler_params=_cparams(("parallel", "arbitrary")),
        name="moe_gather",
    )(offs, hffn, pos_row)


def _ffn_kernel(xe_ref, wg_ref, wu_ref, wd_ref, o_ref, acc_ref, *, cap):
    f = pl.program_id(1)
    nb = xe_ref.shape[0]
    d = xe_ref.shape[-1]
    xe = xe_ref[...].reshape(nb * cap, d)
    a = _dot(xe, wg_ref[...].astype(BF16))
    u = _dot(xe, wu_ref[...].astype(BF16))
    hid = (a * jax.nn.sigmoid(a) * u).astype(BF16)
    part = _dot(hid, wd_ref[...].astype(BF16))

    @pl.when(f == 0)
    def _():
        acc_ref[...] = part

    @pl.when(f > 0)
    def _():
        acc_ref[...] += part

    @pl.when(f == pl.num_programs(1) - 1)
    def _():
        o_ref[:, 0:cap, :] = acc_ref[...].reshape(nb, cap, d).astype(BF16)
        o_ref[:, cap:, :] = jnp.zeros((nb, o_ref.shape[1] - cap, d), BF16)


def moe_ffn(xe, w_gate, w_up, w_down, layer, cap, cap_pad, tf):
    b, ne, _, d = xe.shape
    ff = w_gate.shape[-1]
    return pl.pallas_call(
        functools.partial(_ffn_kernel, cap=cap),
        grid=(ne, ff // tf),
        in_specs=[pl.BlockSpec((b, None, cap, d), lambda e, f: (0, e, 0, 0)),
                  pl.BlockSpec((None, None, d, tf), lambda e, f: (layer, e, 0, f)),
                  pl.BlockSpec((None, None, d, tf), lambda e, f: (layer, e, 0, f)),
                  pl.BlockSpec((None, None, tf, d), lambda e, f: (layer, e, f, 0))],
        out_specs=pl.BlockSpec((b, None, cap_pad, d), lambda e, f: (0, e, 0, 0)),
        out_shape=jax.ShapeDtypeStruct((b, ne, cap_pad, d), BF16),
        scratch_shapes=[pltpu.VMEM((b * cap, d), F32)],
        compiler_params=_cparams(("parallel", "arbitrary")),
        name="moe_ffn",
    )(xe, w_gate, w_up, w_down)


def _combine_kernel(offs_ref, ye_hbm, pos_ref, g_ref, x1_ref, p_ref, gple_ref, wgate_ref, wproj_ref, o_ref, buf_ref,
                    sem_ref, *, tb):
    bi = pl.program_id(0)
    j = pl.program_id(1)
    nj = pl.num_programs(1)
    ne = pos_ref.shape[-1]
    win = buf_ref.shape[2]
    step = bi * nj + j
    slot = step % 2

    def window_copy(b_, j_, e, slot_):
        off = pl.multiple_of(offs_ref[b_, e, j_] * SUBLANES, SUBLANES)
        return pltpu.make_async_copy(ye_hbm.at[b_, e, pl.ds(off, win), :], buf_ref.at[slot_, e],
                                     sem_ref.at[slot_, e])

    @pl.when(step == 0)
    def _():
        for e in range(ne):
            window_copy(bi, j, e, slot).start()

    @pl.when(step + 1 < pl.num_programs(0) * nj)
    def _():
        wrap = j + 1 == nj
        b_next = jnp.where(wrap, bi + 1, bi)
        j_next = jnp.where(wrap, 0, j + 1)
        for e in range(ne):
            window_copy(b_next, j_next, e, 1 - slot).start()

    pos_all = pos_ref[0]
    g_all = g_ref[0]
    liota = lax.broadcasted_iota(I32, (tb, win), 1)
    acc = x1_ref[0]
    for e in range(ne):
        off = offs_ref[bi, e, j] * SUBLANES
        onehot = jnp.where(liota + off == pos_all[:, e:e + 1], 1.0, 0.0).astype(BF16)
        window_copy(bi, j, e, slot).wait()
        acc = acc + g_all[:, e:e + 1] * _dot(onehot, buf_ref[slot, e])
    hp = _rms(acc, gple_ref[...]).astype(BF16)
    gt = jax.nn.sigmoid(_dot(hp, wgate_ref[...]))
    o_ref[0] = acc + _dot(p_ref[0].astype(BF16), wproj_ref[...]) * gt


def moe_combine(ye, pos_col, g_col, offs, x1, p, gple, wgate, wproj, tb):
    b, ne, _, d = ye.shape
    seq = x1.shape[1]
    win = tb + SUBLANES
    grid_spec = pltpu.PrefetchScalarGridSpec(
        num_scalar_prefetch=1,
        grid=(b, seq // tb),
        in_specs=[pl.BlockSpec(memory_space=pl.ANY),
                  pl.BlockSpec((1, tb, ne), lambda bi, j, offs: (bi, j, 0)),
                  pl.BlockSpec((1, tb, ne), lambda bi, j, offs: (bi, j, 0)),
                  pl.BlockSpec((1, tb, d), lambda bi, j, offs: (bi, j, 0)),
                  pl.BlockSpec((1, tb, p.shape[-1]), lambda bi, j, offs: (bi, j, 0)),
                  pl.BlockSpec(gple.shape, lambda bi, j, offs: (0, 0)),
                  pl.BlockSpec(wgate.shape, lambda bi, j, offs: (0, 0)),
                  pl.BlockSpec(wproj.shape, lambda bi, j, offs: (0, 0))],
        out_specs=pl.BlockSpec((1, tb, d), lambda bi, j, offs: (bi, j, 0)),
        scratch_shapes=[pltpu.VMEM((2, ne, win, d), BF16), pltpu.SemaphoreType.DMA((2, ne))],
    )
    return pl.pallas_call(
        functools.partial(_combine_kernel, tb=tb),
        grid_spec=grid_spec,
        out_shape=jax.ShapeDtypeStruct((b, seq, d), F32),
        compiler_params=_cparams(("arbitrary", "arbitrary")),
        name="moe_combine_ple",
    )(offs, ye, pos_col, g_col, x1, p, gple, wgate, wproj)


def _finalnorm_kernel(x_ref, g_ref, o_ref):
    o_ref[...] = _rms(x_ref[...], g_ref[...])


def final_norm(x2d, g, tm):
    t, d = x2d.shape
    return pl.pallas_call(
        _finalnorm_kernel,
        grid=(t // tm,),
        in_specs=[pl.BlockSpec((tm, d), lambda i: (i, 0)), _full(g.shape)],
        out_specs=pl.BlockSpec((tm, d), lambda i: (i, 0)),
        out_shape=jax.ShapeDtypeStruct((t, d), F32),
        compiler_params=_cparams(("parallel",)),
        name="final_norm",
    )(x2d, g)


def _pad_cols(a, width):
    return jnp.pad(a, ((0, 0), (0, width - a.shape[1])))


def _pack_inproj(w_in):
    offs = np.cumsum((0,) + IN_SPLITS)
    cq, ckv, kr, hy, z, xbc, dt = [w_in[:, offs[i]:offs[i + 1]] for i in range(len(IN_SPLITS))]
    d = w_in.shape[0]
    half = MLA_ROPE // 2
    zeros = lambda n: jnp.zeros((d, n), w_in.dtype)
    kr_pad = jnp.concatenate([zeros(MLA_NOPE), kr, zeros(HEAD_PAD - MLA_NOPE - MLA_ROPE)], axis=1)
    kr_swap = jnp.concatenate([zeros(MLA_NOPE), -kr[:, half:], kr[:, :half], zeros(HEAD_PAD - MLA_NOPE - MLA_ROPE)],
                              axis=1)
    wall = jnp.concatenate([cq, ckv, kr_pad, kr_swap, hy, z, xbc, _pad_cols(dt, LANES)], axis=1)
    assert wall.shape[1] == _C_END
    return wall.astype(BF16)


def _pack_mla(w_uq, w_ukv):
    lq = w_uq.shape[0]
    lkv = w_ukv.shape[0]
    half = MLA_ROPE // 2
    padw = HEAD_PAD - MLA_NOPE - MLA_ROPE
    q3 = w_uq.reshape(lq, MLA_HEADS, MLA_NOPE + MLA_ROPE)
    nope, rope = q3[..., :MLA_NOPE], q3[..., MLA_NOPE:]
    zq = jnp.zeros((lq, MLA_HEADS, padw), w_uq.dtype)
    wq = jnp.concatenate([nope, rope, zq], axis=-1).reshape(lq, MLA_HEADS * HEAD_PAD)
    wqs = jnp.concatenate([jnp.zeros_like(nope), -rope[..., half:], rope[..., :half], zq], axis=-1)
    wqs = wqs.reshape(lq, MLA_HEADS * HEAD_PAD)
    kv3 = w_ukv.reshape(lkv, MLA_HEADS, MLA_NOPE + MLA_V)
    knope, vv = kv3[..., :MLA_NOPE], kv3[..., MLA_NOPE:]
    wk = jnp.concatenate([knope, jnp.zeros((lkv, MLA_HEADS, HEAD_PAD - MLA_NOPE), w_ukv.dtype)], axis=-1)
    wk = wk.reshape(lkv, MLA_HEADS * HEAD_PAD)
    wv = vv.reshape(lkv, MLA_HEADS * MLA_V).T
    return wq.astype(BF16), wqs.astype(BF16), wk.astype(BF16), wv.astype(BF16)


def _row(a):
    return a.reshape(1, -1).astype(F32)


TM_PROJ = 512
TQ_ATTN = 1024
TK_ATTN = 512
ROWS_CONV = 512
ROWS_FILT = 512
CB_DFT = 2048
TM_OUT = 256
K1_PER_STEP = 2
TB_MOE = 256
TB_COMBINE = 128
TF_FFN = 512


def _hyena(hy_u, tabs, kspec, conv_w, conv_b, bias):
    b, seq, _ = hy_u.shape
    c = HY_WIDTH
    parts = [dwconv(hy_u, conv_w[:, i * c:(i + 1) * c], _row(conv_b[i * c:(i + 1) * c]), i * c, c, False, ROWS_CONV)
             for i in range(HY_ORDER + 1)]
    gates, v = parts[:-1], parts[-1]
    half, n1 = tabs["half"], tabs["n1"]
    flat = lambda a: a.reshape(b * half, FFT_N2 * c)
    for o in range(HY_ORDER):
        a = leftmm(tabs["m1"], flat(v), CB_DFT).reshape(2, n1, FFT_N2, c)
        y = spectrum_multiply(a, kspec, o, tabs["fblk"], tabs["fblk_t"], tabs["twr_col"], tabs["twi_col"],
                              K1_PER_STEP)
        v = conv_output(tabs["m3"], y.reshape(2 * n1, FFT_N2 * c), flat(v), flat(gates[o]), _row(bias[o]), CB_DFT)
        v = v.reshape(b, seq, c)
    return v


def _hyena_kspec(seq, tabs, w1, b1, freq, w2, b2, w3, decay):
    bands = np.arange(1, HY_BANDS + 1, dtype=np.float64) * 2.0 * np.pi
    mult = np.zeros((1, LANES), np.float32)
    mult[0, 1:1 + HY_BANDS] = bands
    mult[0, 1 + HY_BANDS:1 + 2 * HY_BANDS] = bands
    w1p = jnp.pad(w1.astype(F32), ((0, LANES - w1.shape[0]), (0, 0)))
    ncol = HY_ORDER * 2 * HY_WIDTH
    bwd = (np.arange(ncol) // HY_WIDTH) % 2
    kf = hyena_filters(seq, jnp.asarray(mult), w1p, _row(b1), _row(freq), w2.astype(F32), _row(b2), w3.astype(F32),
                       _row(decay), jnp.asarray(bwd.astype(np.float32)).reshape(1, ncol), ROWS_FILT)
    a = leftmm(tabs["mk"], kf.reshape(tabs["half"], FFT_N2 * ncol), CB_DFT)
    a4 = a.reshape(2, tabs["n1"], FFT_N2, ncol)
    return filter_spectrum(a4, tabs["fblk"], tabs["twr_col"], tabs["twi_col"], K1_PER_STEP)


def kernel(x, p, positions, norm_mix, w_in, mla_q_norm, mla_w_uq, mla_kv_norm, mla_w_ukv, mla_out_norm, hy_conv_w,
           hy_conv_b, hy_filt_w1, hy_filt_b1, hy_filt_freq, hy_filt_w2, hy_filt_b2, hy_filt_w3, hy_decay, hy_bias,
           hy_out_norm, ssm_conv_w, ssm_conv_b, ssm_dt_bias, ssm_a_log, ssm_d, ssm_norm, w_out, norm_ffn, moe_router,
           moe_w_gate, moe_w_up, moe_w_down, ple_norm, ple_gate_w, ple_proj, final_norm_g):
    batch, seq, d = x.shape
    depth = w_in.shape[0]
    t = batch * seq
    cap = EC_CAPACITY_FACTOR * seq // N_EXPERTS
    tb = min(TB_MOE, seq)
    tbc = min(TB_COMBINE, seq)
    cap_pad = cap + 2 * tb
    tm_proj = min(TM_PROJ, seq)
    tm_out = min(TM_OUT, seq)

    freq = np.zeros((1, HEAD_PAD), np.float32)
    inv = ROPE_THETA ** (-np.arange(0, MLA_ROPE, 2, dtype=np.float32) / MLA_ROPE)
    freq[0, MLA_NOPE:MLA_NOPE + MLA_ROPE // 2] = inv
    freq[0, MLA_NOPE + MLA_ROPE // 2:MLA_NOPE + MLA_ROPE] = inv
    cos_t, sin_t = rope_tables(positions.reshape(t, 1), jnp.asarray(freq), tm_proj)

    tabs = _dft_tables(seq, batch)
    tril = jnp.asarray(np.tril(np.ones((SSM_CHUNK, SSM_CHUNK), np.float32)))

    x2d = x.reshape(t, d)
    for i in range(depth):
        wall = _pack_inproj(w_in[i])
        wq, wqs, wk, wv = _pack_mla(mla_w_uq[i], mla_w_ukv[i])
        q, k, v, hy_u, z, xbc_raw, dt_raw = inproj(x2d, _row(norm_mix[i]), wall, _row(mla_q_norm[i]), wq, wqs,
                                                    _row(mla_kv_norm[i]), wk, wv, cos_t, sin_t, tm_proj)
        o_mla = attention(q.reshape(batch, seq, -1), k.reshape(batch, seq, -1), v, min(TQ_ATTN, seq),
                          min(TK_ATTN, seq))

        kspec = _hyena_kspec(seq, tabs, hy_filt_w1[i], hy_filt_b1[i], hy_filt_freq[i], hy_filt_w2[i], hy_filt_b2[i],
                             hy_filt_w3[i], hy_decay[i])
        o_hy = _hyena(hy_u.reshape(batch, seq, -1), tabs, kspec, hy_conv_w[i], hy_conv_b[i], hy_bias[i])

        xbc = dwconv(xbc_raw.reshape(batch, seq, -1), ssm_conv_w[i], _row(ssm_conv_b[i]), 0, SSM_CONV_DIM, True,
                     ROWS_CONV)
        dtbias_row = _pad_cols(_row(ssm_dt_bias[i]), LANES)
        a_row = _pad_cols(_row(-jnp.exp(ssm_a_log[i].astype(F32))), LANES)
        y_f, y_b = ssd_scan(xbc, dt_raw.reshape(batch, seq, -1), dtbias_row, a_row, tril)

        dsk = _row(jnp.repeat(ssm_d[i].astype(F32), SSM_HEADDIM))
        x1, hffn, aff = outproj(o_mla.reshape(t, -1), o_hy.reshape(t, -1), y_f.reshape(t, -1), y_b.reshape(t, -1),
                                xbc.reshape(t, -1), z, x2d, _row(mla_out_norm[i]), _row(hy_out_norm[i]), dsk,
                                _row(ssm_norm[i]), w_out[i].astype(BF16), _row(norm_ffn[i]),
                                moe_router[i].astype(F32).T, batch, tm_out)

        pos, gsel, rowoff = moe_select(aff, cap)
        offs_g = rowoff[:, :, ::tb // LANES] // SUBLANES
        offs_c = rowoff[:, :, ::tbc // LANES] // SUBLANES
        xe = moe_gather(hffn.reshape(batch, seq, d), pos.reshape(batch, N_EXPERTS, 1, seq), offs_g, cap, tb)
        ye = moe_ffn(xe, moe_w_gate, moe_w_up, moe_w_down, i, cap, cap_pad, TF_FFN)
        x3 = moe_combine(ye, jnp.swapaxes(pos, 1, 2), jnp.swapaxes(gsel, 1, 2), offs_c, x1.reshape(batch, seq, d),
                         p[i], _row(ple_norm[i]), ple_gate_w[i].astype(BF16), ple_proj[i].astype(BF16), tbc)
        x2d = x3.reshape(t, d)
    return final_norm(x2d, _row(final_norm_g), tm_proj).reshape(batch, seq, d)
```

```python
import functools
import math

import numpy as np
import jax
import jax.numpy as jnp
from jax import lax
from jax.experimental import pallas as pl
from jax.experimental.pallas import tpu as pltpu

F32 = jnp.float32
BF16 = jnp.bfloat16
I32 = jnp.int32
HIGHEST = lax.Precision.HIGHEST

EPS = 1e-6
LANES = 128
SUBLANES = 8
VMEM_LIMIT = 56 * 1024 * 1024

D_MODEL = 1024
MLA_HEADS = 8
MLA_NOPE = 64
MLA_ROPE = 32
MLA_V = 64
MLA_Q_LORA = 256
MLA_KV_LORA = 128
HEAD_PAD = 128
ATTN_SUM_ROWS = 16
ROPE_THETA = 10000.0
HY_WIDTH = 256
HY_ORDER = 2
HY_SHORT = 3
HY_BANDS = 8
HY_FILT_HID = 64
SSM_WIDTH = 256
SSM_HEADDIM = 64
SSM_HEADS = 4
SSM_GROUPS = 2
SSM_STATE = 128
SSM_CONV = 5
SSM_CHUNK = 128
SSM_CONV_DIM = 768
N_EXPERTS = 16
EXPERT_FF = 2048
EC_CAPACITY_FACTOR = 2
PLE_DIM = 256
FFT_N2 = 128
MANTISSA_STEPS = 40
IN_SPLITS = (MLA_Q_LORA, MLA_KV_LORA, MLA_ROPE, 3 * HY_WIDTH, SSM_WIDTH, SSM_CONV_DIM, 2 * SSM_HEADS)


def _cparams(sem, vmem=None):
    return pltpu.CompilerParams(dimension_semantics=sem, vmem_limit_bytes=vmem or VMEM_LIMIT)


def _rms(x, g):
    ms = jnp.mean(x * x, axis=-1, keepdims=True)
    return x * lax.rsqrt(ms + EPS) * g


def _dot(a, b, precision=None):
    return jnp.dot(a, b, preferred_element_type=F32, precision=precision)


def _dot_nt(a, b, precision=None):
    return lax.dot_general(a, b, (((1,), (1,)), ((), ())), preferred_element_type=F32, precision=precision)


def _dot_tn(a, b, precision=None):
    return lax.dot_general(a, b, (((0,), (0,)), ((), ())), preferred_element_type=F32, precision=precision)


def _split_bf16(x):
    hi = x.astype(BF16)
    return hi, (x - hi.astype(F32)).astype(BF16)


def _dot3(a, b):
    ah, al = _split_bf16(a)
    bh, bl = _split_bf16(b)
    return _dot(jnp.concatenate([ah, ah, al], axis=1), jnp.concatenate([bh, bl, bh], axis=0))


def _full(shape):
    n = len(shape)
    return pl.BlockSpec(shape, lambda *_: (0,) * n)


def _rope_kernel(pos_ref, freq_ref, cos_ref, sin_ref):
    ang = pos_ref[...].astype(F32) * freq_ref[...]
    cos_ref[...] = jnp.cos(ang)
    sin_ref[...] = jnp.sin(ang)


def rope_tables(pos_col, freq_row, tm):
    t = pos_col.shape[0]
    return pl.pallas_call(
        _rope_kernel,
        grid=(t // tm,),
        in_specs=[pl.BlockSpec((tm, 1), lambda i: (i, 0)), _full((1, HEAD_PAD))],
        out_specs=[pl.BlockSpec((tm, HEAD_PAD), lambda i: (i, 0))] * 2,
        out_shape=[jax.ShapeDtypeStruct((t, HEAD_PAD), F32)] * 2,
        compiler_params=_cparams(("parallel",)),
        name="rope_tables",
    )(pos_col, freq_row)


_C_CQ = 0
_C_CKV = 256
_C_KR = 384
_C_KRS = 512
_C_HY = 640
_C_Z = 1408
_C_XBC = 1664
_C_DT = 2432
_C_END = 2560


def _inproj_kernel(x_ref, gmix_ref, wall_ref, qn_ref, wq_ref, wqs_ref, kvn_ref, wk_ref, wv_ref, cos_ref, sin_ref,
                   q_ref, k_ref, v_ref, hy_ref, z_ref, xbc_ref, dt_ref, *, scale):
    h = _rms(x_ref[...], gmix_ref[...]).astype(BF16)
    proj = _dot(h, wall_ref[...])
    hy_ref[...] = proj[:, _C_HY:_C_Z]
    z_ref[...] = proj[:, _C_Z:_C_XBC]
    xbc_ref[...] = proj[:, _C_XBC:_C_DT]
    dt_ref[...] = proj[:, _C_DT:_C_END]
    cos = cos_ref[...]
    sin = sin_ref[...]
    cos8 = jnp.concatenate([cos] * MLA_HEADS, axis=-1)
    sin8 = jnp.concatenate([sin] * MLA_HEADS, axis=-1)
    cqn = _rms(proj[:, _C_CQ:_C_CKV], qn_ref[...]).astype(BF16)
    q = _dot(cqn, wq_ref[...])
    qs = _dot(cqn, wqs_ref[...])
    q_ref[...] = ((q * cos8 + qs * sin8) * scale).astype(BF16)
    ckvn = _rms(proj[:, _C_CKV:_C_KR], kvn_ref[...]).astype(BF16)
    kn = _dot(ckvn, wk_ref[...])
    v_ref[...] = _dot_nt(wv_ref[...], ckvn).astype(BF16)
    kr = proj[:, _C_KR:_C_KRS] * cos + proj[:, _C_KRS:_C_HY] * sin
    k_ref[...] = (kn + jnp.concatenate([kr] * MLA_HEADS, axis=-1)).astype(BF16)


def inproj(x2d, gmix, wall, qn, wq, wqs, kvn, wk, wv, cos_t, sin_t, tm):
    t = x2d.shape[0]
    hq = MLA_HEADS * HEAD_PAD
    row = lambda w: pl.BlockSpec((tm, w), lambda i: (i, 0))
    outs = [(hq, BF16), (hq, BF16), None, (3 * HY_WIDTH, F32), (SSM_WIDTH, F32), (SSM_CONV_DIM, F32), (LANES, F32)]
    hv = MLA_HEADS * MLA_V
    return pl.pallas_call(
        functools.partial(_inproj_kernel, scale=(MLA_NOPE + MLA_ROPE) ** -0.5 * math.log2(math.e)),
        grid=(t // tm,),
        in_specs=[row(D_MODEL), _full(gmix.shape), _full(wall.shape), _full(qn.shape), _full(wq.shape),
                  _full(wqs.shape), _full(kvn.shape), _full(wk.shape), _full(wv.shape), row(HEAD_PAD), row(HEAD_PAD)],
        out_specs=[row(o[0]) if o else pl.BlockSpec((hv, tm), lambda i: (0, i)) for o in outs],
        out_shape=[jax.ShapeDtypeStruct((t, o[0]), o[1]) if o else jax.ShapeDtypeStruct((hv, t), BF16)
                   for o in outs],
        compiler_params=_cparams(("parallel",)),
        name="inproj",
    )(x2d, gmix, wall, qn, wq, wqs, kvn, wk, wv, cos_t, sin_t)


def _attn_kernel(q_ref, k_ref, vt_ref, o_ref, st_ref, *, tk):
    seq = k_ref.shape[1]
    tq = q_ref.shape[1]
    nh = st_ref.shape[0]
    npairs = seq // (2 * tk)

    def scores(hh, c, slot):
        off = pl.multiple_of(c * tk, tk)
        st = _dot_nt(k_ref[0, pl.ds(off, tk), hh * HEAD_PAD:(hh + 1) * HEAD_PAD],
                     q_ref[0, :, hh * HEAD_PAD:(hh + 1) * HEAD_PAD])
        st_ref[hh, slot] = st
        return jnp.max(st, axis=0, keepdims=True)

    ones_rows = jnp.ones((ATTN_SUM_ROWS, tk), BF16)

    def update(hh, c, slot, m, acc, smax):
        off = pl.multiple_of(c * tk, tk)
        vtc = jnp.concatenate([vt_ref[hh * MLA_V:(hh + 1) * MLA_V, pl.ds(off, tk)], ones_rows], axis=0)
        m_new = jnp.maximum(m, smax)
        alpha = jnp.exp2(m - m_new)
        p = jnp.exp2(st_ref[hh, slot] - m_new)
        acc = acc * alpha + _dot(vtc, p.astype(BF16))
        return m_new, acc

    def pair(i, carry, last):
        new = []
        for hh in range(nh):
            m, acc, smax0 = carry[hh]
            smax1 = scores(hh, 2 * i + 1, 1)
            m, acc = update(hh, 2 * i, 0, m, acc, smax0)
            smax0 = smax1 if last else scores(hh, 2 * i + 2, 0)
            m, acc = update(hh, 2 * i + 1, 1, m, acc, smax1)
            new.append((m, acc, smax0))
        return tuple(new)

    init = tuple((jnp.full((1, tq), -jnp.inf, F32), jnp.zeros((MLA_V + ATTN_SUM_ROWS, tq), F32), scores(hh, 0, 0))
                 for hh in range(nh))
    carry = lax.fori_loop(0, npairs - 1, functools.partial(pair, last=False), init)
    final = pair(npairs - 1, carry, True)
    o_ref[0] = jnp.concatenate([jnp.transpose(acc[:MLA_V] / acc[MLA_V:MLA_V + 1]) for _, acc, _ in final], axis=-1)


def attention(q, k, vt, tq, tk):
    b, seq, _ = q.shape
    return pl.pallas_call(
        functools.partial(_attn_kernel, tk=tk),
        grid=(b, MLA_HEADS // 2, seq // tq),
        in_specs=[pl.BlockSpec((1, tq, 2 * HEAD_PAD), lambda bi, hp, qi: (bi, qi, hp)),
                  pl.BlockSpec((1, seq, 2 * HEAD_PAD), lambda bi, hp, qi: (bi, 0, hp)),
                  pl.BlockSpec((2 * MLA_V, seq), lambda bi, hp, qi: (hp, bi))],
        out_specs=pl.BlockSpec((1, tq, 2 * MLA_V), lambda bi, hp, qi: (bi, qi, hp)),
        out_shape=jax.ShapeDtypeStruct((b, seq, MLA_HEADS * MLA_V), F32),
        scratch_shapes=[pltpu.VMEM((2, 2, tk, tq), F32)],
        compiler_params=_cparams(("parallel", "parallel", "parallel")),
        name="attention",
    )(q, k, vt)


def _dwconv_kernel(x_ref, w_ref, b_ref, o_ref, *, width, act, rows):
    seq = x_ref.shape[1]
    pad = width // 2
    nchunks = seq // rows
    w = w_ref[...]
    bias = b_ref[...]

    def body(c, carry):
        r0 = pl.multiple_of(c * rows, rows)
        cur = x_ref[0, pl.ds(r0, rows), :]
        p0 = pl.multiple_of(jnp.maximum(r0 - SUBLANES, 0), SUBLANES)
        n0 = pl.multiple_of(jnp.minimum(r0 + rows, seq - SUBLANES), SUBLANES)
        prev = jnp.where(c > 0, x_ref[0, pl.ds(p0, SUBLANES), :], 0.0)
        nxt = jnp.where(c < nchunks - 1, x_ref[0, pl.ds(n0, SUBLANES), :], 0.0)
        ext = jnp.concatenate([prev, cur, nxt], axis=0)
        acc = bias + ext[SUBLANES - pad:SUBLANES - pad + rows] * w[0:1]
        for kk in range(1, width):
            s0 = SUBLANES - pad + kk
            acc = acc + ext[s0:s0 + rows] * w[kk:kk + 1]
        if act:
            acc = acc * jax.nn.sigmoid(acc)
        o_ref[0, pl.ds(r0, rows), :] = acc
        return carry

    lax.fori_loop(0, nchunks, body, 0)


def dwconv(x, w, bias, col0, ncols, act, rows):
    b, seq, _ = x.shape
    width = w.shape[0]
    cb0 = col0 // LANES
    return pl.pallas_call(
        functools.partial(_dwconv_kernel, width=width, act=act, rows=min(rows, seq)),
        grid=(b, ncols // LANES),
        in_specs=[pl.BlockSpec((1, seq, LANES), lambda bi, ci: (bi, 0, ci + cb0)),
                  pl.BlockSpec((width, LANES), lambda bi, ci: (0, ci)),
                  pl.BlockSpec((1, LANES), lambda bi, ci: (0, ci))],
        out_specs=pl.BlockSpec((1, seq, LANES), lambda bi, ci: (bi, 0, ci)),
        out_shape=jax.ShapeDtypeStruct((b, seq, ncols), F32),
        compiler_params=_cparams(("parallel", "parallel")),
        name="dwconv",
    )(x, w, bias)


def _hyfilt_kernel(mult_ref, w1_ref, b1_ref, fr_ref, w2_ref, b2_ref, w3_ref, dec_ref, bwd_ref, o_ref, *, seq, rows):
    i = pl.program_id(0)
    ridx = lax.broadcasted_iota(I32, (rows, LANES), 0) + i * rows
    lane = lax.broadcasted_iota(I32, (rows, LANES), 1)
    t = ridx.astype(F32) / seq
    ang = t * mult_ref[...]
    feats = jnp.where(lane == 0, t, jnp.where(lane <= HY_BANDS, jnp.sin(ang), jnp.cos(ang)))
    feats = jnp.where(lane < 1 + 2 * HY_BANDS, feats, 0.0)
    fr = fr_ref[...]
    hdn = jnp.sin(fr * (_dot(feats, w1_ref[...], HIGHEST) + b1_ref[...]))
    hdn = jnp.sin(fr * (_dot(hdn, w2_ref[...], HIGHEST) + b2_ref[...]))
    filt = _dot(hdn, w3_ref[...], HIGHEST)
    window = jnp.exp(-t[:, 0:1] * jnp.abs(dec_ref[...]))
    out = filt * window
    keep = jnp.logical_or(ridx[:, 0:1] > 0, bwd_ref[...] < 0.5)
    o_ref[...] = jnp.where(keep, out, 0.0)


def hyena_filters(seq, mult, w1p, b1, fr, w2, b2, w3, dec, bwd_mask, rows):
    ncol = w3.shape[1]
    rows = min(rows, seq)
    args = (mult, w1p, b1, fr, w2, b2, w3, dec, bwd_mask)
    return pl.pallas_call(
        functools.partial(_hyfilt_kernel, seq=seq, rows=rows),
        grid=(seq // rows,),
        in_specs=[_full(a.shape) for a in args],
        out_specs=pl.BlockSpec((rows, ncol), lambda i: (i, 0)),
        out_shape=jax.ShapeDtypeStruct((seq, ncol), F32),
        compiler_params=_cparams(("parallel",)),
        name="hyena_filters",
    )(*args)


def _leftmm_kernel(m_ref, x_ref, o_ref):
    o_ref[...] = _dot3(m_ref[...], x_ref[...])


def leftmm(m, x2d, cb):
    r, kdim = m.shape
    n = x2d.shape[1]
    cb = min(cb, n)
    return pl.pallas_call(
        _leftmm_kernel,
        grid=(n // cb,),
        in_specs=[_full((r, kdim)), pl.BlockSpec((kdim, cb), lambda i: (0, i))],
        out_specs=pl.BlockSpec((r, cb), lambda i: (0, i)),
        out_shape=jax.ShapeDtypeStruct((r, n), F32),
        compiler_params=_cparams(("parallel",)),
        name="dft_outer",
    )(m, x2d)


def _stack3_lhs(f):
    hi, lo = _split_bf16(f)
    return jnp.concatenate([hi, hi, lo], axis=1)


def _stack3_rhs(a):
    hi, lo = _split_bf16(a)
    return jnp.concatenate([hi, lo, hi], axis=0)


def _twiddle(ar, ai, twr, twi, conj):
    if conj:
        return ar * twr + ai * twi, ai * twr - ar * twi
    return ar * twr - ai * twi, ai * twr + ar * twi


def _specfilt_kernel(a_ref, fblk_ref, twr_ref, twi_ref, o_ref, lhs_ref):
    n2 = FFT_N2
    c = a_ref.shape[-1] // 2

    @pl.when((pl.program_id(0) == 0) & (pl.program_id(1) == 0))
    def _():
        lhs_ref[...] = _stack3_lhs(fblk_ref[...])

    for kk in range(a_ref.shape[1]):
        br, bi = _twiddle(a_ref[0, kk], a_ref[1, kk], twr_ref[kk], twi_ref[kk], False)
        x = _dot(lhs_ref[...], _stack3_rhs(jnp.concatenate([br, bi], axis=0)))
        o_ref[0, 0, kk] = x[:n2, :c] + x[:n2, c:]
        o_ref[0, 1, kk] = x[n2:, :c] - x[n2:, c:]


def filter_spectrum(a4, fblk, twr_col, twi_col, k1s):
    _, n1, n2, ctot = a4.shape
    c = HY_WIDTH
    return pl.pallas_call(
        _specfilt_kernel,
        grid=(HY_ORDER, n1 // k1s),
        in_specs=[pl.BlockSpec((2, k1s, n2, 2 * c), lambda o, k: (0, k, 0, o)),
                  _full(fblk.shape),
                  pl.BlockSpec((k1s, n2, 1), lambda o, k: (k, 0, 0)),
                  pl.BlockSpec((k1s, n2, 1), lambda o, k: (k, 0, 0))],
        out_specs=pl.BlockSpec((1, 2, k1s, n2, c), lambda o, k: (o, 0, k, 0, 0)),
        out_shape=jax.ShapeDtypeStruct((HY_ORDER, 2, n1, n2, c), F32),
        scratch_shapes=[pltpu.VMEM((2 * n2, 6 * n2), BF16)],
        compiler_params=_cparams(("arbitrary", "arbitrary")),
        name="filter_spectrum",
    )(a4, fblk, twr_col, twi_col)


def _specmul_kernel(a_ref, k_ref, fblk_ref, fblk_t_ref, twr_ref, twi_ref, o_ref, lhs_ref):
    n2 = FFT_N2

    @pl.when(pl.program_id(0) == 0)
    def _():
        lhs_ref[0] = _stack3_lhs(fblk_ref[...])
        lhs_ref[1] = _stack3_lhs(fblk_t_ref[...])

    for kk in range(a_ref.shape[1]):
        twr, twi = twr_ref[kk], twi_ref[kk]
        br, bi = _twiddle(a_ref[0, kk], a_ref[1, kk], twr, twi, False)
        x = _dot(lhs_ref[0], _stack3_rhs(jnp.concatenate([br, bi], axis=0)))
        xr, xi = x[:n2], x[n2:]
        kr, ki = k_ref[0, 0, kk], k_ref[0, 1, kk]
        p = jnp.concatenate([xr * kr - xi * ki, xr * ki + xi * kr], axis=0)
        y = _dot(lhs_ref[1], _stack3_rhs(p))
        yr, yi = _twiddle(y[:n2], y[n2:], twr, twi, True)
        o_ref[0, kk] = yr
        o_ref[1, kk] = yi


def spectrum_multiply(a4, kspec, order, fblk, fblk_t, twr_col, twi_col, k1s):
    _, n1, n2, c = a4.shape
    return pl.pallas_call(
        _specmul_kernel,
        grid=(n1 // k1s,),
        in_specs=[pl.BlockSpec((2, k1s, n2, c), lambda k: (0, k, 0, 0)),
                  pl.BlockSpec((1, 2, k1s, n2, c), lambda k: (order, 0, k, 0, 0)),
                  _full(fblk.shape), _full(fblk_t.shape),
                  pl.BlockSpec((k1s, n2, 1), lambda k: (k, 0, 0)),
                  pl.BlockSpec((k1s, n2, 1), lambda k: (k, 0, 0))],
        out_specs=pl.BlockSpec((2, k1s, n2, c), lambda k: (0, k, 0, 0)),
        out_shape=jax.ShapeDtypeStruct((2, n1, n2, c), F32),
        scratch_shapes=[pltpu.VMEM((2, 2 * n2, 6 * n2), BF16)],
        compiler_params=_cparams(("arbitrary",)),
        name="spectrum_multiply",
    )(a4, kspec, fblk, fblk_t, twr_col, twi_col)


def _convout_kernel(m_ref, y_ref, v_ref, g_ref, bias_ref, o_ref, *, reps):
    y = _dot3(m_ref[...], y_ref[...])
    bias = jnp.concatenate([bias_ref[...]] * reps, axis=-1)
    v = v_ref[...]
    o_ref[...] = (y + v * bias) * g_ref[...]


def conv_output(m3, y2d, v2d, g2d, bias_row, cb):
    r, kdim = m3.shape
    n = y2d.shape[1]
    cb = min(cb, n)
    c = bias_row.shape[1]
    return pl.pallas_call(
        functools.partial(_convout_kernel, reps=cb // c),
        grid=(n // cb,),
        in_specs=[_full((r, kdim)), pl.BlockSpec((kdim, cb), lambda i: (0, i)),
                  pl.BlockSpec((r, cb), lambda i: (0, i)), pl.BlockSpec((r, cb), lambda i: (0, i)),
                  _full((1, c))],
        out_specs=pl.BlockSpec((r, cb), lambda i: (0, i)),
        out_shape=jax.ShapeDtypeStruct((r, n), F32),
        compiler_params=_cparams(("parallel",)),
        name="dft_outer_inverse",
    )(m3, y2d, v2d, g2d, bias_row)


def _dft_tables(seq, batch):
    n2 = FFT_N2
    half = seq // n2
    n1 = 2 * half
    n = n1 * n2
    k1 = np.arange(n1, dtype=np.float64)[:, None]
    nn1 = np.arange(half, dtype=np.float64)[None, :]
    th = 2.0 * np.pi * k1 * nn1 / n1
    c1, s1 = np.cos(th), np.sin(th)
    assert batch == 2, "the two batch entries are packed as real / imaginary parts"
    m1 = np.block([[c1, s1], [-s1, c1]])
    m3 = np.block([[c1.T, -s1.T], [s1.T, c1.T]]) / n
    mk = np.concatenate([c1, -s1], axis=0)
    kk2 = np.arange(n2, dtype=np.float64)
    th2 = 2.0 * np.pi * np.outer(kk2, kk2) / n2
    c2, s2 = np.cos(th2), np.sin(th2)
    fblk = np.block([[c2, s2], [-s2, c2]])
    tht = 2.0 * np.pi * np.outer(np.arange(n1, dtype=np.float64), kk2) / n
    twr, twi = np.cos(tht), -np.sin(tht)
    f = lambda a: jnp.asarray(a, F32)
    return dict(m1=f(m1), m3=f(m3), mk=f(mk), fblk=f(fblk), fblk_t=f(fblk.T),
                twr_col=f(twr[:, :, None]), twi_col=f(twi[:, :, None]), n1=n1, half=half)


def _ssd_kernel(xf_ref, bf_ref, cf_ref, dtf_ref, xb_ref, bb_ref, cb_ref, dtb_ref, dtbias_ref, a_ref, tril_ref,
                yf_ref, yb_ref, state_ref):
    q = SSM_CHUNK
    hd = SSM_HEADDIM
    ns = SSM_STATE

    @pl.when(pl.program_id(1) == 0)
    def _():
        state_ref[...] = jnp.zeros_like(state_ref)

    tril = tril_ref[...]
    rows = lax.broadcasted_iota(I32, (q, q), 0)
    cols = lax.broadcasted_iota(I32, (q, q), 1)
    a_row = a_ref[...]
    bias = dtbias_ref[...]

    def direction(x_ref, b_ref, c_ref, dt_ref, y_ref, d):
        dt = jax.nn.softplus(dt_ref[0] + bias)
        dta = dt * a_row
        cs = _dot(tril, dta, HIGHEST)
        ecs = cs - dta
        base = ecs if d else cs
        base_t = jnp.transpose(base)
        total = cs[q - 1:q, :]
        x = x_ref[0]
        ys = []
        for g in range(SSM_GROUPS):
            bm = b_ref[0, :, g * ns:(g + 1) * ns]
            cm = c_ref[0, :, g * ns:(g + 1) * ns]
            cb = _dot_nt(cm.astype(BF16), bm.astype(BF16))
            for hh in range(SSM_HEADS // SSM_GROUPS):
                h = g * (SSM_HEADS // SSM_GROUPS) + hh
                j = d * SSM_HEADS + h
                col = base[:, j:j + 1]
                row = base_t[j:j + 1, :]
                tot = total[:, j:j + 1]
                if d == 0:
                    seg = jnp.where(rows >= cols, col - row, -jnp.inf)
                    c_scale = jnp.exp(col)
                    b_scale = jnp.exp(tot - col)
                else:
                    seg = jnp.where(cols >= rows, row - col, -jnp.inf)
                    c_scale = jnp.exp(tot - col)
                    b_scale = jnp.exp(col)
                scores = cb * jnp.exp(seg)
                xdt = x[:, h * hd:(h + 1) * hd] * dt[:, j:j + 1]
                st = state_ref[j]
                y = _dot(scores.astype(BF16), xdt.astype(BF16)) + _dot((cm * c_scale).astype(BF16), st.astype(BF16))
                state_ref[j] = st * jnp.exp(tot) + _dot_tn((bm * b_scale).astype(BF16), xdt.astype(BF16))
                ys.append(y)
        y_ref[0] = jnp.concatenate(ys, axis=-1)

    direction(xf_ref, bf_ref, cf_ref, dtf_ref, yf_ref, 0)
    direction(xb_ref, bb_ref, cb_ref, dtb_ref, yb_ref, 1)


def ssd_scan(xbc, dt_raw, dtbias_row, a_row, tril):
    b, seq, _ = xbc.shape
    q = SSM_CHUNK
    nc = seq // q
    w = SSM_WIDTH
    fwd = lambda col: pl.BlockSpec((1, q, w), lambda bi, i: (bi, i, col))
    bwd = lambda col: pl.BlockSpec((1, q, w), lambda bi, i: (bi, nc - 1 - i, col))
    return pl.pallas_call(
        _ssd_kernel,
        grid=(b, nc),
        in_specs=[fwd(0), fwd(1), fwd(2), pl.BlockSpec((1, q, LANES), lambda bi, i: (bi, i, 0)),
                  bwd(0), bwd(1), bwd(2), pl.BlockSpec((1, q, LANES), lambda bi, i: (bi, nc - 1 - i, 0)),
                  _full((1, LANES)), _full((1, LANES)), _full((q, q))],
        out_specs=[pl.BlockSpec((1, q, w), lambda bi, i: (bi, i, 0)),
                   pl.BlockSpec((1, q, w), lambda bi, i: (bi, nc - 1 - i, 0))],
        out_shape=[jax.ShapeDtypeStruct((b, seq, w), F32)] * 2,
        scratch_shapes=[pltpu.VMEM((2 * SSM_HEADS, SSM_STATE, SSM_HEADDIM), F32)],
        compiler_params=_cparams(("parallel", "arbitrary")),
        name="ssd_scan",
    )(xbc, xbc, xbc, dt_raw, xbc, xbc, xbc, dt_raw, dtbias_row, a_row, tril)


def _outproj_kernel(om_ref, hy_ref, yf_ref, yb_ref, xs_ref, z_ref, x_ref, gm_ref, gh_ref, dsk_ref, gs_ref, wout_ref,
                    gffn_ref, rt_ref, x1_ref, hffn_ref, aff_ref):
    o1 = _rms(om_ref[...], gm_ref[...])
    o2 = _rms(hy_ref[...], gh_ref[...])
    z = z_ref[...]
    y = (yf_ref[...] + yb_ref[...] + xs_ref[...] * dsk_ref[...]) * (z * jax.nn.sigmoid(z))
    gw = SSM_WIDTH // SSM_GROUPS
    gs = gs_ref[...]
    o3 = jnp.concatenate([_rms(y[:, g * gw:(g + 1) * gw], gs[:, g * gw:(g + 1) * gw]) for g in range(SSM_GROUPS)],
                         axis=-1)
    mix = jnp.concatenate([o1, o2, o3], axis=-1).astype(BF16)
    x1 = x_ref[...] + _dot(mix, wout_ref[...])
    x1_ref[...] = x1
    hf = _rms(x1, gffn_ref[...])
    hi, lo = _split_bf16(hf)
    hffn_ref[...] = hi
    logits = _dot_nt(rt_ref[...], jnp.concatenate([hi, lo, hi], axis=1))
    mx = jnp.max(logits, axis=0, keepdims=True)
    ex = jnp.exp(logits - mx)
    aff_ref[0] = ex / jnp.sum(ex, axis=0, keepdims=True)


def outproj(om, hy, yf, yb, xbc, z, x2d, gm, gh, dsk, gs, wout, gffn, router_t, batch, tm):
    t = x2d.shape[0]
    seq = t // batch
    nb = seq // tm
    row = lambda w: pl.BlockSpec((tm, w), lambda i: (i, 0))
    return pl.pallas_call(
        _outproj_kernel,
        grid=(t // tm,),
        in_specs=[row(om.shape[1]), row(HY_WIDTH), row(SSM_WIDTH), row(SSM_WIDTH), row(SSM_WIDTH), row(SSM_WIDTH),
                  row(D_MODEL), _full(gm.shape), _full(gh.shape), _full(dsk.shape), _full(gs.shape),
                  _full(wout.shape), _full(gffn.shape), _full(router_t.shape)],
        out_specs=[row(D_MODEL), row(D_MODEL),
                   pl.BlockSpec((1, N_EXPERTS, tm), lambda i: (i // nb, 0, i % nb))],
        out_shape=[jax.ShapeDtypeStruct((t, D_MODEL), F32), jax.ShapeDtypeStruct((t, D_MODEL), BF16),
                   jax.ShapeDtypeStruct((batch, N_EXPERTS, seq), F32)],
        compiler_params=_cparams(("parallel",)),
        name="outproj_router",
    )(om, hy, yf, yb, xbc, z, x2d, gm, gh, dsk, gs, wout, gffn, router_t)


def _select_kernel(aff_ref, tri_ref, ones_ref, blk_ref, pos_ref, g_ref, off_ref, *, cap, nrows):
    aff = aff_ref[0]
    er = aff.shape[0]
    ne = er // nrows
    aff3 = aff.reshape(ne, nrows, LANES)
    capf = jnp.float32(cap)

    def count(mask3):
        return jnp.sum(jnp.where(mask3, 1.0, 0.0), axis=(1, 2), keepdims=True)

    def enough(cand):
        return count(aff3 >= cand) >= capf

    top = jnp.full((ne, 1, 1), 2.0, F32)
    for shift in (64, 32, 16, 8, 4, 2, 1):
        cand = top * (2.0 ** -shift)
        top = jnp.where(enough(cand), top, cand)
    p = top * 0.5

    def refine(_, carry):
        lo, step = carry
        cand = lo + step
        return jnp.where(enough(cand), cand, lo), step * 0.5

    lo, _ = lax.fori_loop(0, MANTISSA_STEPS, refine, (p, p * 0.5))
    thr = jnp.min(jnp.where(aff3 >= lo, aff3, jnp.inf), axis=(1, 2), keepdims=True)
    gt3 = aff3 > thr
    eq3 = aff3 == thr
    need = capf - count(gt3)

    tri = tri_ref[...]
    ones = ones_ref[...]
    blk = blk_ref[...]

    def prefix(maskf):
        mb = maskf.astype(BF16)
        within = _dot(mb, tri)
        rowtot = _dot(mb, ones)
        before = _dot(blk, rowtot.astype(BF16))
        return within + before, before

    eqf = jnp.where(eq3, 1.0, 0.0).reshape(er, LANES)
    tie_incl, _ = prefix(eqf)
    tie_rank = (tie_incl - eqf).reshape(ne, nrows, LANES)
    sel3 = jnp.logical_or(gt3, jnp.logical_and(eq3, tie_rank < need))
    self_ = jnp.where(sel3, 1.0, 0.0).reshape(er, LANES)
    incl, before = prefix(self_)
    sel = self_ > 0.5
    pos_ref[0] = jnp.where(sel, (incl - self_).astype(I32), -1)
    g_ref[0] = jnp.where(sel, aff, 0.0)
    off_ref[0] = before.astype(I32)


def moe_select(aff, cap):
    b, ne, seq = aff.shape
    nrows = seq // LANES
    er = ne * nrows
    tri = jnp.asarray(np.triu(np.ones((LANES, LANES), np.float32)), BF16)
    ones = jnp.ones((LANES, LANES), BF16)
    ridx = np.arange(er)
    blk = (ridx[:, None] // nrows == ridx[None, :] // nrows) & (ridx[None, :] < ridx[:, None])
    blk = jnp.asarray(blk.astype(np.float32), BF16)
    spec = pl.BlockSpec((1, er, LANES), lambda bi: (bi, 0, 0))
    pos, gsel, off = pl.pallas_call(
        functools.partial(_select_kernel, cap=cap, nrows=nrows),
        grid=(b,),
        in_specs=[spec, _full(tri.shape), _full(ones.shape), _full(blk.shape)],
        out_specs=[spec] * 3,
        out_shape=[jax.ShapeDtypeStruct((b, er, LANES), I32), jax.ShapeDtypeStruct((b, er, LANES), F32),
                   jax.ShapeDtypeStruct((b, er, LANES), I32)],
        compiler_params=_cparams(("parallel",)),
        name="moe_select",
    )(aff.reshape(b, er, LANES), tri, ones, blk)
    return pos.reshape(b, ne, seq), gsel.reshape(b, ne, seq), off[:, :, 0].reshape(b, ne, nrows)


def _gather_kernel(offs_ref, h_ref, pos_ref, o_ref, acc_ref, *, tb, cap):
    bi = pl.program_id(0)
    ei = pl.program_id(1)
    seq = h_ref.shape[1]
    acc_ref[0:SUBLANES, :] = jnp.zeros((SUBLANES, acc_ref.shape[1]), F32)
    wins = sorted({min(64, tb + SUBLANES), min(128, tb + SUBLANES), tb + SUBLANES})

    def body(j, carry):
        off8 = offs_ref[bi, ei, j]
        off = pl.multiple_of(off8 * SUBLANES, SUBLANES)
        need = (offs_ref[bi, ei, j + 1] - off8 + 1) * SUBLANES
        t0 = pl.multiple_of(j * tb, tb)

        def place(win):
            pos = pos_ref[0, 0, :, pl.ds(t0, tb)]
            riota = lax.broadcasted_iota(I32, (win, tb), 0)
            onehot = jnp.where(riota + off == pos, 1.0, 0.0).astype(BF16)
            rows = _dot(onehot, h_ref[0, pl.ds(t0, tb), :])
            acc_ref[pl.ds(off, SUBLANES), :] += rows[0:SUBLANES]
            acc_ref[pl.ds(off + SUBLANES, win - SUBLANES), :] = rows[SUBLANES:]

        lo = 0
        for win in wins:
            fits = need <= win if win != wins[-1] else True
            pl.when(jnp.logical_and(need > lo, fits))(functools.partial(place, win))
            lo = win
        return carry

    lax.fori_loop(0, seq // tb, body, 0)
    o_ref[0, 0] = acc_ref[0:cap, :].astype(BF16)


def moe_gather(hffn, pos_row, offs, cap, tb):
    b, seq, d = hffn.shape
    ne = pos_row.shape[1]
    grid_spec = pltpu.PrefetchScalarGridSpec(
        num_scalar_prefetch=1,
        grid=(b, ne),
        in_specs=[pl.BlockSpec((1, seq, d), lambda bi, ei, offs: (bi, 0, 0)),
                  pl.BlockSpec((1, 1, 1, seq), lambda bi, ei, offs: (bi, ei, 0, 0))],
        out_specs=pl.BlockSpec((1, 1, cap, d), lambda bi, ei, offs: (bi, ei, 0, 0)),
        scratch_shapes=[pltpu.VMEM((cap + tb + SUBLANES, d), F32)],
    )
    return pl.pallas_call(
        functools.partial(_gather_kernel, tb=tb, cap=cap),
        grid_spec=grid_spec,
        out_shape=jax.ShapeDtypeStruct((b, ne, cap, d), BF16),
        compiler_params=_cparams(("parallel", "arbitrary")),
        name="moe_gather",
    )(offs, hffn, pos_row)


def _ffn_kernel(xe_ref, wg_ref, wu_ref, wd_ref, o_ref, acc_ref, *, cap):
    f = pl.program_id(1)
    nb = xe_ref.shape[0]
    d = xe_ref.shape[-1]
    xe = xe_ref[...].reshape(nb * cap, d)
    a = _dot(xe, wg_ref[...].astype(BF16))
    u = _dot(xe, wu_ref[...].astype(BF16))
    hid = (a * jax.nn.sigmoid(a) * u).astype(BF16)
    part = _dot(hid, wd_ref[...].astype(BF16))

    @pl.when(f == 0)
    def _():
        acc_ref[...] = part

    @pl.when(f > 0)
    def _():
        acc_ref[...] += part

    @pl.when(f == pl.num_programs(1) - 1)
    def _():
        o_ref[:, 0:cap, :] = acc_ref[...].reshape(nb, cap, d).astype(BF16)
        o_ref[:, cap:, :] = jnp.zeros((nb, o_ref.shape[1] - cap, d), BF16)


def moe_ffn(xe, w_gate, w_up, w_down, layer, cap, cap_pad, tf):
    b, ne, _, d = xe.shape
    ff = w_gate.shape[-1]
    return pl.pallas_call(
        functools.partial(_ffn_kernel, cap=cap),
        grid=(ne, ff // tf),
        in_specs=[pl.BlockSpec((b, None, cap, d), lambda e, f: (0, e, 0, 0)),
                  pl.BlockSpec((None, None, d, tf), lambda e, f: (layer, e, 0, f)),
                  pl.BlockSpec((None, None, d, tf), lambda e, f: (layer, e, 0, f)),
                  pl.BlockSpec((None, None, tf, d), lambda e, f: (layer, e, f, 0))],
        out_specs=pl.BlockSpec((b, None, cap_pad, d), lambda e, f: (0, e, 0, 0)),
        out_shape=jax.ShapeDtypeStruct((b, ne, cap_pad, d), BF16),
        scratch_shapes=[pltpu.VMEM((b * cap, d), F32)],
        compiler_params=_cparams(("parallel", "arbitrary")),
        name="moe_ffn",
    )(xe, w_gate, w_up, w_down)


def _combine_kernel(offs_ref, ye_hbm, pos_ref, g_ref, x1_ref, p_ref, gple_ref, wgate_ref, wproj_ref, gfin_ref, o_ref,
                    buf_ref, sem_ref, sel_ref, gate_ref, x2_ref, *, tb, final):
    bi = pl.program_id(0)
    j = pl.program_id(1)
    nj = pl.num_programs(1)
    ne = pos_ref.shape[-1]
    win = buf_ref.shape[2]
    step = bi * nj + j
    slot = step % 2

    def window_copy(b_, j_, e, slot_):
        off = pl.multiple_of(offs_ref[b_, e, j_] * SUBLANES, SUBLANES)
        return pltpu.make_async_copy(ye_hbm.at[b_, e, pl.ds(off, win), :], buf_ref.at[slot_, e],
                                     sem_ref.at[slot_, e])

    @pl.when(step == 0)
    def _():
        for e in range(ne):
            window_copy(bi, j, e, slot).start()

    @pl.when(step + 1 < pl.num_programs(0) * nj)
    def _():
        wrap = j + 1 == nj
        b_next = jnp.where(wrap, bi + 1, bi)
        j_next = jnp.where(wrap, 0, j + 1)
        for e in range(ne):
            window_copy(b_next, j_next, e, 1 - slot).start()

    pos_all = pos_ref[0]
    g_all = g_ref[0]
    liota = lax.broadcasted_iota(I32, (tb, win), 1)
    chunk = gate_ref.shape[-1]
    for e in range(ne):
        off = offs_ref[bi, e, j] * SUBLANES
        sel_ref[e] = jnp.where(liota + off == pos_all[:, e:e + 1], 1.0, 0.0).astype(BF16)
        gate_ref[e] = jnp.broadcast_to(g_all[:, e:e + 1], (tb, chunk))
        window_copy(bi, j, e, slot).wait()
    for c0 in range(0, x2_ref.shape[-1], chunk):
        acc = x1_ref[0, :, c0:c0 + chunk]
        for e in range(ne):
            acc = acc + gate_ref[e] * _dot(sel_ref[e], buf_ref[slot, e, :, c0:c0 + chunk])
        x2_ref[:, c0:c0 + chunk] = acc
    x2 = x2_ref[...]
    hp = _rms(x2, gple_ref[...]).astype(BF16)
    gt = jax.nn.sigmoid(_dot(hp, wgate_ref[...]))
    x3 = x2 + _dot(p_ref[0].astype(BF16), wproj_ref[...]) * gt
    o_ref[0] = _rms(x3, gfin_ref[...]) if final else x3


def moe_combine(ye, pos_col, g_col, offs, x1, p, gple, wgate, wproj, gfin, final, tb):
    b, ne, _, d = ye.shape
    seq = x1.shape[1]
    win = tb + SUBLANES
    chunk = 2 * LANES
    grid_spec = pltpu.PrefetchScalarGridSpec(
        num_scalar_prefetch=1,
        grid=(b, seq // tb),
        in_specs=[pl.BlockSpec(memory_space=pl.ANY),
                  pl.BlockSpec((1, tb, ne), lambda bi, j, offs: (bi, j, 0)),
                  pl.BlockSpec((1, tb, ne), lambda bi, j, offs: (bi, j, 0)),
                  pl.BlockSpec((1, tb, d), lambda bi, j, offs: (bi, j, 0)),
                  pl.BlockSpec((1, tb, p.shape[-1]), lambda bi, j, offs: (bi, j, 0)),
                  pl.BlockSpec(gple.shape, lambda bi, j, offs: (0, 0)),
                  pl.BlockSpec(wgate.shape, lambda bi, j, offs: (0, 0)),
                  pl.BlockSpec(wproj.shape, lambda bi, j, offs: (0, 0)),
                  pl.BlockSpec(gfin.shape, lambda bi, j, offs: (0, 0))],
        out_specs=pl.BlockSpec((1, tb, d), lambda bi, j, offs: (bi, j, 0)),
        scratch_shapes=[pltpu.VMEM((2, ne, win, d), BF16), pltpu.SemaphoreType.DMA((2, ne)),
                        pltpu.VMEM((ne, tb, win), BF16), pltpu.VMEM((ne, tb, chunk), F32), pltpu.VMEM((tb, d), F32)],
    )
    return pl.pallas_call(
        functools.partial(_combine_kernel, tb=tb, final=final),
        grid_spec=grid_spec,
        out_shape=jax.ShapeDtypeStruct((b, seq, d), F32),
        compiler_params=_cparams(("arbitrary", "arbitrary")),
        name="moe_combine_ple",
    )(offs, ye, pos_col, g_col, x1, p, gple, wgate, wproj, gfin)


def _pad_cols(a, width):
    return jnp.pad(a, ((0, 0), (0, width - a.shape[1])))


def _pack_inproj(w_in):
    offs = np.cumsum((0,) + IN_SPLITS)
    cq, ckv, kr, hy, z, xbc, dt = [w_in[:, offs[i]:offs[i + 1]] for i in range(len(IN_SPLITS))]
    d = w_in.shape[0]
    half = MLA_ROPE // 2
    zeros = lambda n: jnp.zeros((d, n), w_in.dtype)
    kr_pad = jnp.concatenate([zeros(MLA_NOPE), kr, zeros(HEAD_PAD - MLA_NOPE - MLA_ROPE)], axis=1)
    kr_swap = jnp.concatenate([zeros(MLA_NOPE), -kr[:, half:], kr[:, :half], zeros(HEAD_PAD - MLA_NOPE - MLA_ROPE)],
                              axis=1)
    wall = jnp.concatenate([cq, ckv, kr_pad, kr_swap, hy, z, xbc, _pad_cols(dt, LANES)], axis=1)
    assert wall.shape[1] == _C_END
    return wall.astype(BF16)


def _pack_mla(w_uq, w_ukv):
    lq = w_uq.shape[0]
    lkv = w_ukv.shape[0]
    half = MLA_ROPE // 2
    padw = HEAD_PAD - MLA_NOPE - MLA_ROPE
    q3 = w_uq.reshape(lq, MLA_HEADS, MLA_NOPE + MLA_ROPE)
    nope, rope = q3[..., :MLA_NOPE], q3[..., MLA_NOPE:]
    zq = jnp.zeros((lq, MLA_HEADS, padw), w_uq.dtype)
    wq = jnp.concatenate([nope, rope, zq], axis=-1).reshape(lq, MLA_HEADS * HEAD_PAD)
    wqs = jnp.concatenate([jnp.zeros_like(nope), -rope[..., half:], rope[..., :half], zq], axis=-1)
    wqs = wqs.reshape(lq, MLA_HEADS * HEAD_PAD)
    kv3 = w_ukv.reshape(lkv, MLA_HEADS, MLA_NOPE + MLA_V)
    knope, vv = kv3[..., :MLA_NOPE], kv3[..., MLA_NOPE:]
    wk = jnp.concatenate([knope, jnp.zeros((lkv, MLA_HEADS, HEAD_PAD - MLA_NOPE), w_ukv.dtype)], axis=-1)
    wk = wk.reshape(lkv, MLA_HEADS * HEAD_PAD)
    wv = vv.reshape(lkv, MLA_HEADS * MLA_V).T
    return wq.astype(BF16), wqs.astype(BF16), wk.astype(BF16), wv.astype(BF16)


def _row(a):
    return a.reshape(1, -1).astype(F32)


TM_PROJ = 512
TQ_ATTN = 1024
TK_ATTN = 512
ROWS_CONV = 512
ROWS_FILT = 512
CB_DFT = 2048
TM_OUT = 512
K1_PER_STEP = 2
TB_MOE = 256
TB_COMBINE = 128
TF_FFN = 512


def _hyena(hy_u, tabs, kspec, conv_w, conv_b, bias):
    b, seq, _ = hy_u.shape
    c = HY_WIDTH
    parts = [dwconv(hy_u, conv_w[:, i * c:(i + 1) * c], _row(conv_b[i * c:(i + 1) * c]), i * c, c, False, ROWS_CONV)
             for i in range(HY_ORDER + 1)]
    gates, v = parts[:-1], parts[-1]
    half, n1 = tabs["half"], tabs["n1"]
    flat = lambda a: a.reshape(b * half, FFT_N2 * c)
    for o in range(HY_ORDER):
        a = leftmm(tabs["m1"], flat(v), CB_DFT).reshape(2, n1, FFT_N2, c)
        y = spectrum_multiply(a, kspec, o, tabs["fblk"], tabs["fblk_t"], tabs["twr_col"], tabs["twi_col"],
                              K1_PER_STEP)
        v = conv_output(tabs["m3"], y.reshape(2 * n1, FFT_N2 * c), flat(v), flat(gates[o]), _row(bias[o]), CB_DFT)
        v = v.reshape(b, seq, c)
    return v


def _hyena_kspec(seq, tabs, w1, b1, freq, w2, b2, w3, decay):
    bands = np.arange(1, HY_BANDS + 1, dtype=np.float64) * 2.0 * np.pi
    mult = np.zeros((1, LANES), np.float32)
    mult[0, 1:1 + HY_BANDS] = bands
    mult[0, 1 + HY_BANDS:1 + 2 * HY_BANDS] = bands
    w1p = jnp.pad(w1.astype(F32), ((0, LANES - w1.shape[0]), (0, 0)))
    ncol = HY_ORDER * 2 * HY_WIDTH
    bwd = (np.arange(ncol) // HY_WIDTH) % 2
    kf = hyena_filters(seq, jnp.asarray(mult), w1p, _row(b1), _row(freq), w2.astype(F32), _row(b2), w3.astype(F32),
                       _row(decay), jnp.asarray(bwd.astype(np.float32)).reshape(1, ncol), ROWS_FILT)
    a = leftmm(tabs["mk"], kf.reshape(tabs["half"], FFT_N2 * ncol), CB_DFT)
    a4 = a.reshape(2, tabs["n1"], FFT_N2, ncol)
    return filter_spectrum(a4, tabs["fblk"], tabs["twr_col"], tabs["twi_col"], K1_PER_STEP)


def kernel(x, p, positions, norm_mix, w_in, mla_q_norm, mla_w_uq, mla_kv_norm, mla_w_ukv, mla_out_norm, hy_conv_w,
           hy_conv_b, hy_filt_w1, hy_filt_b1, hy_filt_freq, hy_filt_w2, hy_filt_b2, hy_filt_w3, hy_decay, hy_bias,
           hy_out_norm, ssm_conv_w, ssm_conv_b, ssm_dt_bias, ssm_a_log, ssm_d, ssm_norm, w_out, norm_ffn, moe_router,
           moe_w_gate, moe_w_up, moe_w_down, ple_norm, ple_gate_w, ple_proj, final_norm_g):
    batch, seq, d = x.shape
    depth = w_in.shape[0]
    t = batch * seq
    cap = EC_CAPACITY_FACTOR * seq // N_EXPERTS
    tb = min(TB_MOE, seq)
    tbc = min(TB_COMBINE, seq)
    cap_pad = cap + tb
    tm_proj = min(TM_PROJ, seq)
    tm_out = min(TM_OUT, seq)

    freq = np.zeros((1, HEAD_PAD), np.float32)
    inv = ROPE_THETA ** (-np.arange(0, MLA_ROPE, 2, dtype=np.float32) / MLA_ROPE)
    freq[0, MLA_NOPE:MLA_NOPE + MLA_ROPE // 2] = inv
    freq[0, MLA_NOPE + MLA_ROPE // 2:MLA_NOPE + MLA_ROPE] = inv
    cos_t, sin_t = rope_tables(positions.reshape(t, 1), jnp.asarray(freq), tm_proj)

    tabs = _dft_tables(seq, batch)
    tril = jnp.asarray(np.tril(np.ones((SSM_CHUNK, SSM_CHUNK), np.float32)))

    x2d = x.reshape(t, d)
    for i in range(depth):
        wall = _pack_inproj(w_in[i])
        wq, wqs, wk, wv = _pack_mla(mla_w_uq[i], mla_w_ukv[i])
        q, k, v, hy_u, z, xbc_raw, dt_raw = inproj(x2d, _row(norm_mix[i]), wall, _row(mla_q_norm[i]), wq, wqs,
                                                    _row(mla_kv_norm[i]), wk, wv, cos_t, sin_t, tm_proj)
        o_mla = attention(q.reshape(batch, seq, -1), k.reshape(batch, seq, -1), v, min(TQ_ATTN, seq),
                          min(TK_ATTN, seq))

        kspec = _hyena_kspec(seq, tabs, hy_filt_w1[i], hy_filt_b1[i], hy_filt_freq[i], hy_filt_w2[i], hy_filt_b2[i],
                             hy_filt_w3[i], hy_decay[i])
        o_hy = _hyena(hy_u.reshape(batch, seq, -1), tabs, kspec, hy_conv_w[i], hy_conv_b[i], hy_bias[i])

        xbc = dwconv(xbc_raw.reshape(batch, seq, -1), ssm_conv_w[i], _row(ssm_conv_b[i]), 0, SSM_CONV_DIM, True,
                     ROWS_CONV)
        dtbias_row = _pad_cols(_row(ssm_dt_bias[i]), LANES)
        a_row = _pad_cols(_row(-jnp.exp(ssm_a_log[i].astype(F32))), LANES)
        y_f, y_b = ssd_scan(xbc, dt_raw.reshape(batch, seq, -1), dtbias_row, a_row, tril)

        dsk = _row(jnp.repeat(ssm_d[i].astype(F32), SSM_HEADDIM))
        x1, hffn, aff = outproj(o_mla.reshape(t, -1), o_hy.reshape(t, -1), y_f.reshape(t, -1), y_b.reshape(t, -1),
                                xbc.reshape(t, -1), z, x2d, _row(mla_out_norm[i]), _row(hy_out_norm[i]), dsk,
                                _row(ssm_norm[i]), w_out[i].astype(BF16), _row(norm_ffn[i]),
                                _stack3_lhs(moe_router[i].astype(F32).T), batch, tm_out)

        pos, gsel, rowoff = moe_select(aff, cap)
        offs_g = jnp.concatenate([rowoff[:, :, ::tb // LANES] // SUBLANES,
                                  jnp.full((batch, N_EXPERTS, 1), pl.cdiv(cap, SUBLANES), I32)], axis=-1)
        offs_c = rowoff[:, :, ::tbc // LANES] // SUBLANES
        xe = moe_gather(hffn.reshape(batch, seq, d), pos.reshape(batch, N_EXPERTS, 1, seq), offs_g, cap, tb)
        ye = moe_ffn(xe, moe_w_gate, moe_w_up, moe_w_down, i, cap, cap_pad, TF_FFN)
        x3 = moe_combine(ye, jnp.swapaxes(pos, 1, 2), jnp.swapaxes(gsel, 1, 2), offs_c, x1.reshape(batch, seq, d),
                         p[i], _row(ple_norm[i]), ple_gate_w[i].astype(BF16), ple_proj[i].astype(BF16),
                         _row(final_norm_g), i == depth - 1, tbc)
        x2d = x3.reshape(t, d)
    return x2d.reshape(batch, seq, d)
```

```python
import functools
import math

import numpy as np
import jax
import jax.numpy as jnp
from jax import lax
from jax.experimental import pallas as pl
from jax.experimental.pallas import tpu as pltpu

F32 = jnp.float32
BF16 = jnp.bfloat16
I32 = jnp.int32
HIGHEST = lax.Precision.HIGHEST

EPS = 1e-6
LANES = 128
SUBLANES = 8
VMEM_LIMIT = 56 * 1024 * 1024

D_MODEL = 1024
MLA_HEADS = 8
MLA_NOPE = 64
MLA_ROPE = 32
MLA_V = 64
MLA_Q_LORA = 256
MLA_KV_LORA = 128
HEAD_PAD = 128
ATTN_SUM_ROWS = 16
ROPE_THETA = 10000.0
HY_WIDTH = 256
HY_ORDER = 2
HY_SHORT = 3
HY_BANDS = 8
HY_FILT_HID = 64
SSM_WIDTH = 256
SSM_HEADDIM = 64
SSM_HEADS = 4
SSM_GROUPS = 2
SSM_STATE = 128
SSM_CONV = 5
SSM_CHUNK = 128
SSM_CONV_DIM = 768
N_EXPERTS = 16
EXPERT_FF = 2048
EC_CAPACITY_FACTOR = 2
PLE_DIM = 256
FFT_N2 = 128
MANTISSA_STEPS = 40
IN_SPLITS = (MLA_Q_LORA, MLA_KV_LORA, MLA_ROPE, 3 * HY_WIDTH, SSM_WIDTH, SSM_CONV_DIM, 2 * SSM_HEADS)


def _cparams(sem, vmem=None):
    return pltpu.CompilerParams(dimension_semantics=sem, vmem_limit_bytes=vmem or VMEM_LIMIT)


def _rms(x, g):
    ms = jnp.mean(x * x, axis=-1, keepdims=True)
    return x * lax.rsqrt(ms + EPS) * g


def _dot(a, b, precision=None):
    return jnp.dot(a, b, preferred_element_type=F32, precision=precision)


def _dot_nt(a, b, precision=None):
    return lax.dot_general(a, b, (((1,), (1,)), ((), ())), preferred_element_type=F32, precision=precision)


def _dot_tn(a, b, precision=None):
    return lax.dot_general(a, b, (((0,), (0,)), ((), ())), preferred_element_type=F32, precision=precision)


def _split_bf16(x):
    hi = x.astype(BF16)
    return hi, (x - hi.astype(F32)).astype(BF16)


def _dot3(a, b):
    ah, al = _split_bf16(a)
    bh, bl = _split_bf16(b)
    return _dot(jnp.concatenate([ah, ah, al], axis=1), jnp.concatenate([bh, bl, bh], axis=0))


def _full(shape):
    n = len(shape)
    return pl.BlockSpec(shape, lambda *_: (0,) * n)


def _rope_kernel(pos_ref, freq_ref, cos_ref, sin_ref):
    ang = pos_ref[...].astype(F32) * freq_ref[...]
    cos_ref[...] = jnp.cos(ang)
    sin_ref[...] = jnp.sin(ang)


def rope_tables(pos_col, freq_row, tm):
    t = pos_col.shape[0]
    return pl.pallas_call(
        _rope_kernel,
        grid=(t // tm,),
        in_specs=[pl.BlockSpec((tm, 1), lambda i: (i, 0)), _full((1, HEAD_PAD))],
        out_specs=[pl.BlockSpec((tm, HEAD_PAD), lambda i: (i, 0))] * 2,
        out_shape=[jax.ShapeDtypeStruct((t, HEAD_PAD), F32)] * 2,
        compiler_params=_cparams(("parallel",)),
        name="rope_tables",
    )(pos_col, freq_row)


_C_CQ = 0
_C_CKV = 256
_C_KR = 384
_C_KRS = 512
_C_HY = 640
_C_Z = 1408
_C_XBC = 1664
_C_DT = 2432
_C_END = 2560


def _inproj_kernel(x_ref, gmix_ref, wall_ref, qn_ref, wq_ref, wqs_ref, kvn_ref, wk_ref, wv_ref, cos_ref, sin_ref,
                   q_ref, k_ref, v_ref, hy_ref, z_ref, xbc_ref, dt_ref, *, scale):
    h = _rms(x_ref[...], gmix_ref[...]).astype(BF16)
    proj = _dot(h, wall_ref[...])
    hy_ref[...] = proj[:, _C_HY:_C_Z]
    z_ref[...] = proj[:, _C_Z:_C_XBC]
    xbc_ref[...] = proj[:, _C_XBC:_C_DT]
    dt_ref[...] = proj[:, _C_DT:_C_END]
    cos = cos_ref[...]
    sin = sin_ref[...]
    cos8 = jnp.concatenate([cos] * MLA_HEADS, axis=-1)
    sin8 = jnp.concatenate([sin] * MLA_HEADS, axis=-1)
    cqn = _rms(proj[:, _C_CQ:_C_CKV], qn_ref[...]).astype(BF16)
    q = _dot(cqn, wq_ref[...])
    qs = _dot(cqn, wqs_ref[...])
    q_ref[...] = ((q * cos8 + qs * sin8) * scale).astype(BF16)
    ckvn = _rms(proj[:, _C_CKV:_C_KR], kvn_ref[...]).astype(BF16)
    kn = _dot(ckvn, wk_ref[...])
    v_ref[...] = _dot_nt(wv_ref[...], ckvn).astype(BF16)
    kr = proj[:, _C_KR:_C_KRS] * cos + proj[:, _C_KRS:_C_HY] * sin
    k_ref[...] = (kn + jnp.concatenate([kr] * MLA_HEADS, axis=-1)).astype(BF16)


def inproj(x2d, gmix, wall, qn, wq, wqs, kvn, wk, wv, cos_t, sin_t, tm):
    t = x2d.shape[0]
    hq = MLA_HEADS * HEAD_PAD
    row = lambda w: pl.BlockSpec((tm, w), lambda i: (i, 0))
    outs = [(hq, BF16), (hq, BF16), None, (3 * HY_WIDTH, F32), (SSM_WIDTH, F32), (SSM_CONV_DIM, F32), (LANES, F32)]
    hv = MLA_HEADS * MLA_V
    return pl.pallas_call(
        functools.partial(_inproj_kernel, scale=(MLA_NOPE + MLA_ROPE) ** -0.5 * math.log2(math.e)),
        grid=(t // tm,),
        in_specs=[row(D_MODEL), _full(gmix.shape), _full(wall.shape), _full(qn.shape), _full(wq.shape),
                  _full(wqs.shape), _full(kvn.shape), _full(wk.shape), _full(wv.shape), row(HEAD_PAD), row(HEAD_PAD)],
        out_specs=[row(o[0]) if o else pl.BlockSpec((hv, tm), lambda i: (0, i)) for o in outs],
        out_shape=[jax.ShapeDtypeStruct((t, o[0]), o[1]) if o else jax.ShapeDtypeStruct((hv, t), BF16)
                   for o in outs],
        compiler_params=_cparams(("parallel",)),
        name="inproj",
    )(x2d, gmix, wall, qn, wq, wqs, kvn, wk, wv, cos_t, sin_t)


def _attn_kernel(q_ref, k_ref, vt_ref, o_ref, st_ref, *, tk):
    seq = k_ref.shape[1]
    tq = q_ref.shape[1]
    nh = st_ref.shape[0]
    npairs = seq // (2 * tk)

    def scores(hh, c, slot):
        off = pl.multiple_of(c * tk, tk)
        st = _dot_nt(k_ref[0, pl.ds(off, tk), hh * HEAD_PAD:(hh + 1) * HEAD_PAD],
                     q_ref[0, :, hh * HEAD_PAD:(hh + 1) * HEAD_PAD])
        st_ref[hh, slot] = st
        return jnp.max(st, axis=0, keepdims=True)

    ones_rows = jnp.ones((ATTN_SUM_ROWS, tk), BF16)

    def update(hh, c, slot, m, acc, smax):
        off = pl.multiple_of(c * tk, tk)
        vtc = jnp.concatenate([vt_ref[hh * MLA_V:(hh + 1) * MLA_V, pl.ds(off, tk)], ones_rows], axis=0)
        m_new = jnp.maximum(m, smax)
        alpha = jnp.exp2(m - m_new)
        p = jnp.exp2(st_ref[hh, slot] - m_new)
        acc = acc * alpha + _dot(vtc, p.astype(BF16))
        return m_new, acc

    def pair(i, carry, last):
        new = []
        for hh in range(nh):
            m, acc, smax0 = carry[hh]
            smax1 = scores(hh, 2 * i + 1, 1)
            m, acc = update(hh, 2 * i, 0, m, acc, smax0)
            smax0 = smax1 if last else scores(hh, 2 * i + 2, 0)
            m, acc = update(hh, 2 * i + 1, 1, m, acc, smax1)
            new.append((m, acc, smax0))
        return tuple(new)

    init = tuple((jnp.full((1, tq), -jnp.inf, F32), jnp.zeros((MLA_V + ATTN_SUM_ROWS, tq), F32), scores(hh, 0, 0))
                 for hh in range(nh))
    carry = lax.fori_loop(0, npairs - 1, functools.partial(pair, last=False), init)
    final = pair(npairs - 1, carry, True)
    o_ref[0] = jnp.concatenate([jnp.transpose(acc[:MLA_V] / acc[MLA_V:MLA_V + 1]) for _, acc, _ in final], axis=-1)


def attention(q, k, vt, tq, tk):
    b, seq, _ = q.shape
    return pl.pallas_call(
        functools.partial(_attn_kernel, tk=tk),
        grid=(b, MLA_HEADS // 2, seq // tq),
        in_specs=[pl.BlockSpec((1, tq, 2 * HEAD_PAD), lambda bi, hp, qi: (bi, qi, hp)),
                  pl.BlockSpec((1, seq, 2 * HEAD_PAD), lambda bi, hp, qi: (bi, 0, hp)),
                  pl.BlockSpec((2 * MLA_V, seq), lambda bi, hp, qi: (hp, bi))],
        out_specs=pl.BlockSpec((1, tq, 2 * MLA_V), lambda bi, hp, qi: (bi, qi, hp)),
        out_shape=jax.ShapeDtypeStruct((b, seq, MLA_HEADS * MLA_V), F32),
        scratch_shapes=[pltpu.VMEM((2, 2, tk, tq), F32)],
        compiler_params=_cparams(("parallel", "parallel", "parallel")),
        name="attention",
    )(q, k, vt)


def _dwconv_kernel(x_ref, w_ref, b_ref, o_ref, *, width, act, rows):
    seq = x_ref.shape[1]
    pad = width // 2
    nchunks = seq // rows
    w = w_ref[...]
    bias = b_ref[...]

    def body(c, carry):
        r0 = pl.multiple_of(c * rows, rows)
        cur = x_ref[0, pl.ds(r0, rows), :]
        p0 = pl.multiple_of(jnp.maximum(r0 - SUBLANES, 0), SUBLANES)
        n0 = pl.multiple_of(jnp.minimum(r0 + rows, seq - SUBLANES), SUBLANES)
        prev = jnp.where(c > 0, x_ref[0, pl.ds(p0, SUBLANES), :], 0.0)
        nxt = jnp.where(c < nchunks - 1, x_ref[0, pl.ds(n0, SUBLANES), :], 0.0)
        ext = jnp.concatenate([prev, cur, nxt], axis=0)
        acc = bias + ext[SUBLANES - pad:SUBLANES - pad + rows] * w[0:1]
        for kk in range(1, width):
            s0 = SUBLANES - pad + kk
            acc = acc + ext[s0:s0 + rows] * w[kk:kk + 1]
        if act:
            acc = acc * jax.nn.sigmoid(acc)
        o_ref[0, pl.ds(r0, rows), :] = acc
        return carry

    lax.fori_loop(0, nchunks, body, 0)


def dwconv(x, w, bias, col0, ncols, act, rows):
    b, seq, _ = x.shape
    width = w.shape[0]
    cb0 = col0 // LANES
    return pl.pallas_call(
        functools.partial(_dwconv_kernel, width=width, act=act, rows=min(rows, seq)),
        grid=(b, ncols // LANES),
        in_specs=[pl.BlockSpec((1, seq, LANES), lambda bi, ci: (bi, 0, ci + cb0)),
                  pl.BlockSpec((width, LANES), lambda bi, ci: (0, ci)),
                  pl.BlockSpec((1, LANES), lambda bi, ci: (0, ci))],
        out_specs=pl.BlockSpec((1, seq, LANES), lambda bi, ci: (bi, 0, ci)),
        out_shape=jax.ShapeDtypeStruct((b, seq, ncols), F32),
        compiler_params=_cparams(("parallel", "parallel")),
        name="dwconv",
    )(x, w, bias)


def _hyfilt_kernel(mult_ref, w1_ref, b1_ref, fr_ref, w2_ref, b2_ref, w3_ref, dec_ref, bwd_ref, o_ref, *, seq, rows):
    i = pl.program_id(0)
    ridx = lax.broadcasted_iota(I32, (rows, LANES), 0) + i * rows
    lane = lax.broadcasted_iota(I32, (rows, LANES), 1)
    t = ridx.astype(F32) / seq
    ang = t * mult_ref[...]
    feats = jnp.where(lane == 0, t, jnp.where(lane <= HY_BANDS, jnp.sin(ang), jnp.cos(ang)))
    feats = jnp.where(lane < 1 + 2 * HY_BANDS, feats, 0.0)
    fr = fr_ref[...]
    hdn = jnp.sin(fr * (_dot(feats, w1_ref[...], HIGHEST) + b1_ref[...]))
    hdn = jnp.sin(fr * (_dot(hdn, w2_ref[...], HIGHEST) + b2_ref[...]))
    hh, hl = _split_bf16(hdn)
    wh, wl = _split_bf16(w3_ref[...])
    filt = (_dot(hh, wh) + _dot(hl, wh)) + _dot(hh, wl)
    window = jnp.exp(-t[:, 0:1] * jnp.abs(dec_ref[...]))
    out = filt * window
    keep = jnp.logical_or(ridx[:, 0:1] > 0, bwd_ref[...] < 0.5)
    o_ref[...] = jnp.where(keep, out, 0.0)


def hyena_filters(seq, mult, w1p, b1, fr, w2, b2, w3, dec, bwd_mask, rows):
    ncol = w3.shape[1]
    rows = min(rows, seq)
    args = (mult, w1p, b1, fr, w2, b2, w3, dec, bwd_mask)
    return pl.pallas_call(
        functools.partial(_hyfilt_kernel, seq=seq, rows=rows),
        grid=(seq // rows,),
        in_specs=[_full(a.shape) for a in args],
        out_specs=pl.BlockSpec((rows, ncol), lambda i: (i, 0)),
        out_shape=jax.ShapeDtypeStruct((seq, ncol), F32),
        compiler_params=_cparams(("parallel",)),
        name="hyena_filters",
    )(*args)


def _leftmm_kernel(m_ref, x_ref, o_ref):
    o_ref[...] = _dot3(m_ref[...], x_ref[...])


def leftmm(m, x2d, cb):
    r, kdim = m.shape
    n = x2d.shape[1]
    cb = min(cb, n)
    return pl.pallas_call(
        _leftmm_kernel,
        grid=(n // cb,),
        in_specs=[_full((r, kdim)), pl.BlockSpec((kdim, cb), lambda i: (0, i))],
        out_specs=pl.BlockSpec((r, cb), lambda i: (0, i)),
        out_shape=jax.ShapeDtypeStruct((r, n), F32),
        compiler_params=_cparams(("parallel",)),
        name="dft_outer",
    )(m, x2d)


def _stack3_lhs(f):
    hi, lo = _split_bf16(f)
    return jnp.concatenate([hi, hi, lo], axis=1)


def _stack3_rhs(a):
    hi, lo = _split_bf16(a)
    return jnp.concatenate([hi, lo, hi], axis=0)


def _twiddle(ar, ai, twr, twi, conj):
    if conj:
        return ar * twr + ai * twi, ai * twr - ar * twi
    return ar * twr - ai * twi, ai * twr + ar * twi


def _specfilt_kernel(a_ref, fblk_ref, twr_ref, twi_ref, o_ref, lhs_ref):
    n2 = FFT_N2
    c = a_ref.shape[-1] // 2

    @pl.when((pl.program_id(0) == 0) & (pl.program_id(1) == 0))
    def _():
        lhs_ref[...] = _stack3_lhs(fblk_ref[...])

    for kk in range(a_ref.shape[1]):
        br, bi = _twiddle(a_ref[0, kk], a_ref[1, kk], twr_ref[kk], twi_ref[kk], False)
        x = _dot(lhs_ref[...], _stack3_rhs(jnp.concatenate([br, bi], axis=0)))
        o_ref[0, 0, kk] = x[:n2, :c] + x[:n2, c:]
        o_ref[0, 1, kk] = x[n2:, :c] - x[n2:, c:]


def filter_spectrum(a4, fblk, twr_col, twi_col, k1s):
    _, n1, n2, ctot = a4.shape
    c = HY_WIDTH
    return pl.pallas_call(
        _specfilt_kernel,
        grid=(HY_ORDER, n1 // k1s),
        in_specs=[pl.BlockSpec((2, k1s, n2, 2 * c), lambda o, k: (0, k, 0, o)),
                  _full(fblk.shape),
                  pl.BlockSpec((k1s, n2, 1), lambda o, k: (k, 0, 0)),
                  pl.BlockSpec((k1s, n2, 1), lambda o, k: (k, 0, 0))],
        out_specs=pl.BlockSpec((1, 2, k1s, n2, c), lambda o, k: (o, 0, k, 0, 0)),
        out_shape=jax.ShapeDtypeStruct((HY_ORDER, 2, n1, n2, c), F32),
        scratch_shapes=[pltpu.VMEM((2 * n2, 6 * n2), BF16)],
        compiler_params=_cparams(("arbitrary", "arbitrary")),
        name="filter_spectrum",
    )(a4, fblk, twr_col, twi_col)


def _specmul_kernel(a_ref, k_ref, fblk_ref, fblk_t_ref, twr_ref, twi_ref, o_ref, lhs_ref):
    n2 = FFT_N2

    @pl.when(pl.program_id(0) == 0)
    def _():
        lhs_ref[0] = _stack3_lhs(fblk_ref[...])
        lhs_ref[1] = _stack3_lhs(fblk_t_ref[...])

    for kk in range(a_ref.shape[1]):
        twr, twi = twr_ref[kk], twi_ref[kk]
        br, bi = _twiddle(a_ref[0, kk], a_ref[1, kk], twr, twi, False)
        x = _dot(lhs_ref[0], _stack3_rhs(jnp.concatenate([br, bi], axis=0)))
        xr, xi = x[:n2], x[n2:]
        kr, ki = k_ref[0, 0, kk], k_ref[0, 1, kk]
        p = jnp.concatenate([xr * kr - xi * ki, xr * ki + xi * kr], axis=0)
        y = _dot(lhs_ref[1], _stack3_rhs(p))
        yr, yi = _twiddle(y[:n2], y[n2:], twr, twi, True)
        o_ref[0, kk] = yr
        o_ref[1, kk] = yi


def spectrum_multiply(a4, kspec, order, fblk, fblk_t, twr_col, twi_col, k1s):
    _, n1, n2, c = a4.shape
    return pl.pallas_call(
        _specmul_kernel,
        grid=(n1 // k1s,),
        in_specs=[pl.BlockSpec((2, k1s, n2, c), lambda k: (0, k, 0, 0)),
                  pl.BlockSpec((1, 2, k1s, n2, c), lambda k: (order, 0, k, 0, 0)),
                  _full(fblk.shape), _full(fblk_t.shape),
                  pl.BlockSpec((k1s, n2, 1), lambda k: (k, 0, 0)),
                  pl.BlockSpec((k1s, n2, 1), lambda k: (k, 0, 0))],
        out_specs=pl.BlockSpec((2, k1s, n2, c), lambda k: (0, k, 0, 0)),
        out_shape=jax.ShapeDtypeStruct((2, n1, n2, c), F32),
        scratch_shapes=[pltpu.VMEM((2, 2 * n2, 6 * n2), BF16)],
        compiler_params=_cparams(("arbitrary",)),
        name="spectrum_multiply",
    )(a4, kspec, fblk, fblk_t, twr_col, twi_col)


def _convout_kernel(m_ref, y_ref, v_ref, g_ref, bias_ref, o_ref, *, reps):
    y = _dot3(m_ref[...], y_ref[...])
    bias = jnp.concatenate([bias_ref[...]] * reps, axis=-1)
    v = v_ref[...]
    o_ref[...] = (y + v * bias) * g_ref[...]


def conv_output(m3, y2d, v2d, g2d, bias_row, cb):
    r, kdim = m3.shape
    n = y2d.shape[1]
    cb = min(cb, n)
    c = bias_row.shape[1]
    return pl.pallas_call(
        functools.partial(_convout_kernel, reps=cb // c),
        grid=(n // cb,),
        in_specs=[_full((r, kdim)), pl.BlockSpec((kdim, cb), lambda i: (0, i)),
                  pl.BlockSpec((r, cb), lambda i: (0, i)), pl.BlockSpec((r, cb), lambda i: (0, i)),
                  _full((1, c))],
        out_specs=pl.BlockSpec((r, cb), lambda i: (0, i)),
        out_shape=jax.ShapeDtypeStruct((r, n), F32),
        compiler_params=_cparams(("parallel",)),
        name="dft_outer_inverse",
    )(m3, y2d, v2d, g2d, bias_row)


def _dft_tables(seq, batch):
    n2 = FFT_N2
    half = seq // n2
    n1 = 2 * half
    n = n1 * n2
    k1 = np.arange(n1, dtype=np.float64)[:, None]
    nn1 = np.arange(half, dtype=np.float64)[None, :]
    th = 2.0 * np.pi * k1 * nn1 / n1
    c1, s1 = np.cos(th), np.sin(th)
    assert batch == 2, "the two batch entries are packed as real / imaginary parts"
    m1 = np.block([[c1, s1], [-s1, c1]])
    m3 = np.block([[c1.T, -s1.T], [s1.T, c1.T]]) / n
    mk = np.concatenate([c1, -s1], axis=0)
    kk2 = np.arange(n2, dtype=np.float64)
    th2 = 2.0 * np.pi * np.outer(kk2, kk2) / n2
    c2, s2 = np.cos(th2), np.sin(th2)
    fblk = np.block([[c2, s2], [-s2, c2]])
    tht = 2.0 * np.pi * np.outer(np.arange(n1, dtype=np.float64), kk2) / n
    twr, twi = np.cos(tht), -np.sin(tht)
    f = lambda a: jnp.asarray(a, F32)
    return dict(m1=f(m1), m3=f(m3), mk=f(mk), fblk=f(fblk), fblk_t=f(fblk.T),
                twr_col=f(twr[:, :, None]), twi_col=f(twi[:, :, None]), n1=n1, half=half)


def _ssd_kernel(xf_ref, bf_ref, cf_ref, dtf_ref, xb_ref, bb_ref, cb_ref, dtb_ref, dtbias_ref, a_ref, tril_ref,
                yf_ref, yb_ref, state_ref):
    q = SSM_CHUNK
    hd = SSM_HEADDIM
    ns = SSM_STATE

    @pl.when(pl.program_id(1) == 0)
    def _():
        state_ref[...] = jnp.zeros_like(state_ref)

    tril = tril_ref[...]
    rows = lax.broadcasted_iota(I32, (q, q), 0)
    cols = lax.broadcasted_iota(I32, (q, q), 1)
    a_row = a_ref[...]
    bias = dtbias_ref[...]

    def direction(x_ref, b_ref, c_ref, dt_ref, y_ref, d):
        dt = jax.nn.softplus(dt_ref[0] + bias)
        dta = dt * a_row
        cs = _dot(tril, dta, HIGHEST)
        ecs = cs - dta
        base = ecs if d else cs
        base_t = jnp.transpose(base)
        total = cs[q - 1:q, :]
        x = x_ref[0]
        ys = []
        for g in range(SSM_GROUPS):
            bm = b_ref[0, :, g * ns:(g + 1) * ns]
            cm = c_ref[0, :, g * ns:(g + 1) * ns]
            cb = _dot_nt(cm.astype(BF16), bm.astype(BF16))
            for hh in range(SSM_HEADS // SSM_GROUPS):
                h = g * (SSM_HEADS // SSM_GROUPS) + hh
                j = d * SSM_HEADS + h
                col = jnp.broadcast_to(base[:, j:j + 1], (q, q))
                coln = col if ns == q else jnp.broadcast_to(base[:, j:j + 1], (q, ns))
                row = base_t[j:j + 1, :]
                tot = total[:, j:j + 1]
                if d == 0:
                    seg = jnp.where(rows >= cols, col - row, -jnp.inf)
                    c_scale = jnp.exp(coln)
                    b_scale = jnp.exp(tot - coln)
                else:
                    seg = jnp.where(cols >= rows, row - col, -jnp.inf)
                    c_scale = jnp.exp(tot - coln)
                    b_scale = jnp.exp(coln)
                scores = cb * jnp.exp(seg)
                xdt = (x[:, h * hd:(h + 1) * hd] * jnp.broadcast_to(dt[:, j:j + 1], (q, hd))).astype(BF16)
                st = state_ref[j]
                y = _dot(scores.astype(BF16), xdt) + _dot_nt((cm * c_scale).astype(BF16), st.astype(BF16))
                state_ref[j] = st * jnp.exp(tot) + _dot_tn(xdt, (bm * b_scale).astype(BF16))
                ys.append(y)
        y_ref[0] = jnp.concatenate(ys, axis=-1)

    direction(xf_ref, bf_ref, cf_ref, dtf_ref, yf_ref, 0)
    direction(xb_ref, bb_ref, cb_ref, dtb_ref, yb_ref, 1)


def ssd_scan(xbc, dt_raw, dtbias_row, a_row, tril):
    b, seq, _ = xbc.shape
    q = SSM_CHUNK
    nc = seq // q
    w = SSM_WIDTH
    fwd = lambda col: pl.BlockSpec((1, q, w), lambda bi, i: (bi, i, col))
    bwd = lambda col: pl.BlockSpec((1, q, w), lambda bi, i: (bi, nc - 1 - i, col))
    return pl.pallas_call(
        _ssd_kernel,
        grid=(b, nc),
        in_specs=[fwd(0), fwd(1), fwd(2), pl.BlockSpec((1, q, LANES), lambda bi, i: (bi, i, 0)),
                  bwd(0), bwd(1), bwd(2), pl.BlockSpec((1, q, LANES), lambda bi, i: (bi, nc - 1 - i, 0)),
                  _full((1, LANES)), _full((1, LANES)), _full((q, q))],
        out_specs=[pl.BlockSpec((1, q, w), lambda bi, i: (bi, i, 0)),
                   pl.BlockSpec((1, q, w), lambda bi, i: (bi, nc - 1 - i, 0))],
        out_shape=[jax.ShapeDtypeStruct((b, seq, w), F32)] * 2,
        scratch_shapes=[pltpu.VMEM((2 * SSM_HEADS, SSM_HEADDIM, SSM_STATE), F32)],
        compiler_params=_cparams(("parallel", "arbitrary")),
        name="ssd_scan",
    )(xbc, xbc, xbc, dt_raw, xbc, xbc, xbc, dt_raw, dtbias_row, a_row, tril)


def _outproj_kernel(om_ref, hy_ref, yf_ref, yb_ref, xs_ref, z_ref, x_ref, gm_ref, gh_ref, dsk_ref, gs_ref, wout_ref,
                    gffn_ref, rt_ref, x1_ref, hffn_ref, aff_ref):
    o1 = _rms(om_ref[...], gm_ref[...])
    o2 = _rms(hy_ref[...], gh_ref[...])
    z = z_ref[...]
    y = (yf_ref[...] + yb_ref[...] + xs_ref[...] * dsk_ref[...]) * (z * jax.nn.sigmoid(z))
    gw = SSM_WIDTH // SSM_GROUPS
    gs = gs_ref[...]
    o3 = jnp.concatenate([_rms(y[:, g * gw:(g + 1) * gw], gs[:, g * gw:(g + 1) * gw]) for g in range(SSM_GROUPS)],
                         axis=-1)
    mix = jnp.concatenate([o1, o2, o3], axis=-1).astype(BF16)
    x1 = x_ref[...] + _dot(mix, wout_ref[...])
    x1_ref[...] = x1
    hf = _rms(x1, gffn_ref[...])
    hi, lo = _split_bf16(hf)
    hffn_ref[...] = hi
    logits = _dot_nt(rt_ref[...], jnp.concatenate([hi, lo, hi], axis=1))
    mx = jnp.max(logits, axis=0, keepdims=True)
    ex = jnp.exp(logits - mx)
    aff_ref[0] = ex / jnp.sum(ex, axis=0, keepdims=True)


def outproj(om, hy, yf, yb, xbc, z, x2d, gm, gh, dsk, gs, wout, gffn, router_t, batch, tm):
    t = x2d.shape[0]
    seq = t // batch
    nb = seq // tm
    row = lambda w: pl.BlockSpec((tm, w), lambda i: (i, 0))
    return pl.pallas_call(
        _outproj_kernel,
        grid=(t // tm,),
        in_specs=[row(om.shape[1]), row(HY_WIDTH), row(SSM_WIDTH), row(SSM_WIDTH), row(SSM_WIDTH), row(SSM_WIDTH),
                  row(D_MODEL), _full(gm.shape), _full(gh.shape), _full(dsk.shape), _full(gs.shape),
                  _full(wout.shape), _full(gffn.shape), _full(router_t.shape)],
        out_specs=[row(D_MODEL), row(D_MODEL),
                   pl.BlockSpec((1, N_EXPERTS, tm), lambda i: (i // nb, 0, i % nb))],
        out_shape=[jax.ShapeDtypeStruct((t, D_MODEL), F32), jax.ShapeDtypeStruct((t, D_MODEL), BF16),
                   jax.ShapeDtypeStruct((batch, N_EXPERTS, seq), F32)],
        compiler_params=_cparams(("parallel",)),
        name="outproj_router",
    )(om, hy, yf, yb, xbc, z, x2d, gm, gh, dsk, gs, wout, gffn, router_t)


def _select_kernel(aff_ref, tri_ref, ones_ref, blk_ref, pos_ref, g_ref, off_ref, *, cap, nrows):
    aff = aff_ref[0]
    er = aff.shape[0]
    ne = er // nrows
    aff3 = aff.reshape(ne, nrows, LANES)
    capf = jnp.float32(cap)

    def count(mask3):
        return jnp.sum(jnp.where(mask3, 1.0, 0.0), axis=(1, 2), keepdims=True)

    def enough(cand):
        return count(aff3 >= cand) >= capf

    top = jnp.full((ne, 1, 1), 2.0, F32)
    for shift in (64, 32, 16, 8, 4, 2, 1):
        cand = top * (2.0 ** -shift)
        top = jnp.where(enough(cand), top, cand)
    p = top * 0.5

    def refine(_, carry):
        lo, step = carry
        cand = lo + step
        return jnp.where(enough(cand), cand, lo), step * 0.5

    lo, _ = lax.fori_loop(0, MANTISSA_STEPS, refine, (p, p * 0.5))
    thr = jnp.min(jnp.where(aff3 >= lo, aff3, jnp.inf), axis=(1, 2), keepdims=True)
    gt3 = aff3 > thr
    eq3 = aff3 == thr
    need = capf - count(gt3)

    tri = tri_ref[...]
    ones = ones_ref[...]
    blk = blk_ref[...]

    def prefix(maskf):
        mb = maskf.astype(BF16)
        within = _dot(mb, tri)
        rowtot = _dot(mb, ones)
        before = _dot(blk, rowtot.astype(BF16))
        return within + before, before

    eqf = jnp.where(eq3, 1.0, 0.0).reshape(er, LANES)
    tie_incl, _ = prefix(eqf)
    tie_rank = (tie_incl - eqf).reshape(ne, nrows, LANES)
    sel3 = jnp.logical_or(gt3, jnp.logical_and(eq3, tie_rank < need))
    self_ = jnp.where(sel3, 1.0, 0.0).reshape(er, LANES)
    incl, before = prefix(self_)
    sel = self_ > 0.5
    pos_ref[0] = jnp.where(sel, (incl - self_).astype(I32), -1)
    g_ref[0] = jnp.where(sel, aff, 0.0)
    off_ref[0] = before.astype(I32)


def moe_select(aff, cap):
    b, ne, seq = aff.shape
    nrows = seq // LANES
    er = ne * nrows
    tri = jnp.asarray(np.triu(np.ones((LANES, LANES), np.float32)), BF16)
    ones = jnp.ones((LANES, LANES), BF16)
    ridx = np.arange(er)
    blk = (ridx[:, None] // nrows == ridx[None, :] // nrows) & (ridx[None, :] < ridx[:, None])
    blk = jnp.asarray(blk.astype(np.float32), BF16)
    spec = pl.BlockSpec((1, er, LANES), lambda bi: (bi, 0, 0))
    pos, gsel, off = pl.pallas_call(
        functools.partial(_select_kernel, cap=cap, nrows=nrows),
        grid=(b,),
        in_specs=[spec, _full(tri.shape), _full(ones.shape), _full(blk.shape)],
        out_specs=[spec] * 3,
        out_shape=[jax.ShapeDtypeStruct((b, er, LANES), I32), jax.ShapeDtypeStruct((b, er, LANES), F32),
                   jax.ShapeDtypeStruct((b, er, LANES), I32)],
        compiler_params=_cparams(("parallel",)),
        name="moe_select",
    )(aff.reshape(b, er, LANES), tri, ones, blk)
    return pos.reshape(b, ne, seq), gsel.reshape(b, ne, seq), off[:, :, 0].reshape(b, ne, nrows)


def _gather_kernel(offs_ref, h_ref, pos_ref, o_ref, acc_ref, *, tb, cap):
    bi = pl.program_id(0)
    ei = pl.program_id(1)
    seq = h_ref.shape[1]
    acc_ref[0:SUBLANES, :] = jnp.zeros((SUBLANES, acc_ref.shape[1]), F32)
    wins = sorted({min(64, tb + SUBLANES), min(128, tb + SUBLANES), tb + SUBLANES})

    def body(j, carry):
        off8 = offs_ref[bi, ei, j]
        off = pl.multiple_of(off8 * SUBLANES, SUBLANES)
        need = (offs_ref[bi, ei, j + 1] - off8 + 1) * SUBLANES
        t0 = pl.multiple_of(j * tb, tb)

        def place(win):
            pos = pos_ref[0, 0, :, pl.ds(t0, tb)]
            riota = lax.broadcasted_iota(I32, (win, tb), 0)
            onehot = jnp.where(riota + off == pos, 1.0, 0.0).astype(BF16)
            rows = _dot(onehot, h_ref[0, pl.ds(t0, tb), :])
            acc_ref[pl.ds(off, SUBLANES), :] += rows[0:SUBLANES]
            acc_ref[pl.ds(off + SUBLANES, win - SUBLANES), :] = rows[SUBLANES:]

        lo = 0
        for win in wins:
            fits = need <= win if win != wins[-1] else True
            pl.when(jnp.logical_and(need > lo, fits))(functools.partial(place, win))
            lo = win
        return carry

    lax.fori_loop(0, seq // tb, body, 0)
    o_ref[0, 0] = acc_ref[0:cap, :].astype(BF16)


def moe_gather(hffn, pos_row, offs, cap, tb):
    b, seq, d = hffn.shape
    ne = pos_row.shape[1]
    grid_spec = pltpu.PrefetchScalarGridSpec(
        num_scalar_prefetch=1,
        grid=(b, ne),
        in_specs=[pl.BlockSpec((1, seq, d), lambda bi, ei, offs: (bi, 0, 0)),
                  pl.BlockSpec((1, 1, 1, seq), lambda bi, ei, offs: (bi, ei, 0, 0))],
        out_specs=pl.BlockSpec((1, 1, cap, d), lambda bi, ei, offs: (bi, ei, 0, 0)),
        scratch_shapes=[pltpu.VMEM((cap + tb + SUBLANES, d), F32)],
    )
    return pl.pallas_call(
        functools.partial(_gather_kernel, tb=tb, cap=cap),
        grid_spec=grid_spec,
        out_shape=jax.ShapeDtypeStruct((b, ne, cap, d), BF16),
        compiler_params=_cparams(("parallel", "arbitrary")),
        name="moe_gather",
    )(offs, hffn, pos_row)


def _ffn_kernel(xe_ref, wg_ref, wu_ref, wd_ref, o_ref, acc_ref, *, cap):
    f = pl.program_id(1)
    nb = xe_ref.shape[0]
    d = xe_ref.shape[-1]
    xe = xe_ref[...].reshape(nb * cap, d)
    a = _dot(xe, wg_ref[...].astype(BF16))
    u = _dot(xe, wu_ref[...].astype(BF16))
    hid = (a * jax.nn.sigmoid(a) * u).astype(BF16)
    part = _dot(hid, wd_ref[...].astype(BF16))

    @pl.when(f == 0)
    def _():
        acc_ref[...] = part

    @pl.when(f > 0)
    def _():
        acc_ref[...] += part

    @pl.when(f == pl.num_programs(1) - 1)
    def _():
        o_ref[:, 0:cap, :] = acc_ref[...].reshape(nb, cap, d).astype(BF16)
        o_ref[:, cap:, :] = jnp.zeros((nb, o_ref.shape[1] - cap, d), BF16)


def moe_ffn(xe, w_gate, w_up, w_down, layer, cap, cap_pad, tf):
    b, ne, _, d = xe.shape
    ff = w_gate.shape[-1]
    return pl.pallas_call(
        functools.partial(_ffn_kernel, cap=cap),
        grid=(ne, ff // tf),
        in_specs=[pl.BlockSpec((b, None, cap, d), lambda e, f: (0, e, 0, 0)),
                  pl.BlockSpec((None, None, d, tf), lambda e, f: (layer, e, 0, f)),
                  pl.BlockSpec((None, None, d, tf), lambda e, f: (layer, e, 0, f)),
                  pl.BlockSpec((None, None, tf, d), lambda e, f: (layer, e, f, 0))],
        out_specs=pl.BlockSpec((b, None, cap_pad, d), lambda e, f: (0, e, 0, 0)),
        out_shape=jax.ShapeDtypeStruct((b, ne, cap_pad, d), BF16),
        scratch_shapes=[pltpu.VMEM((b * cap, d), F32)],
        compiler_params=_cparams(("parallel", "arbitrary")),
        name="moe_ffn",
    )(xe, w_gate, w_up, w_down)


def _combine_kernel(offs_ref, ye_hbm, pos_ref, g_ref, x1_ref, p_ref, gple_ref, wgate_ref, wproj_ref, gfin_ref, o_ref,
                    buf_ref, sem_ref, sel_ref, gate_ref, x2_ref, *, tb, final):
    bi = pl.program_id(0)
    j = pl.program_id(1)
    nj = pl.num_programs(1)
    ne = pos_ref.shape[1]
    win = buf_ref.shape[2]
    step = bi * nj + j
    slot = step % 2

    def window_copy(b_, j_, e, slot_):
        off = pl.multiple_of(offs_ref[b_, e, j_] * SUBLANES, SUBLANES)
        return pltpu.make_async_copy(ye_hbm.at[b_, e, pl.ds(off, win), :], buf_ref.at[slot_, e],
                                     sem_ref.at[slot_, e])

    @pl.when(step == 0)
    def _():
        for e in range(ne):
            window_copy(bi, j, e, slot).start()

    @pl.when(step + 1 < pl.num_programs(0) * nj)
    def _():
        wrap = j + 1 == nj
        b_next = jnp.where(wrap, bi + 1, bi)
        j_next = jnp.where(wrap, 0, j + 1)
        for e in range(ne):
            window_copy(b_next, j_next, e, 1 - slot).start()

    def token_major(a):
        pad = jnp.zeros((tb - ne, tb), a.dtype)
        return jnp.transpose(jnp.concatenate([a, pad], axis=0))[:, :ne]

    pos_all = token_major(pos_ref[0])
    g_all = token_major(g_ref[0])
    liota = lax.broadcasted_iota(I32, (tb, win), 1)
    chunk = gate_ref.shape[-1]
    for e in range(ne):
        off = offs_ref[bi, e, j] * SUBLANES
        sel_ref[e] = jnp.where(liota + off == pos_all[:, e:e + 1], 1.0, 0.0).astype(BF16)
        gate_ref[e] = jnp.broadcast_to(g_all[:, e:e + 1], (tb, chunk))
        window_copy(bi, j, e, slot).wait()
    for c0 in range(0, x2_ref.shape[-1], chunk):
        acc = x1_ref[0, :, c0:c0 + chunk]
        for e in range(ne):
            acc = acc + gate_ref[e] * _dot(sel_ref[e], buf_ref[slot, e, :, c0:c0 + chunk])
        x2_ref[:, c0:c0 + chunk] = acc
    x2 = x2_ref[...]
    hp = _rms(x2, gple_ref[...]).astype(BF16)
    gt = jax.nn.sigmoid(_dot(hp, wgate_ref[...]))
    x3 = x2 + _dot(p_ref[0].astype(BF16), wproj_ref[...]) * gt
    o_ref[0] = _rms(x3, gfin_ref[...]) if final else x3


def moe_combine(ye, pos, gsel, offs, x1, p, layer, gple, wgate, wproj, gfin, final, tb):
    b, ne, _, d = ye.shape
    seq = x1.shape[1]
    win = tb + SUBLANES
    chunk = 2 * LANES
    grid_spec = pltpu.PrefetchScalarGridSpec(
        num_scalar_prefetch=1,
        grid=(b, seq // tb),
        in_specs=[pl.BlockSpec(memory_space=pl.ANY),
                  pl.BlockSpec((1, ne, tb), lambda bi, j, offs: (bi, 0, j)),
                  pl.BlockSpec((1, ne, tb), lambda bi, j, offs: (bi, 0, j)),
                  pl.BlockSpec((1, tb, d), lambda bi, j, offs: (bi, j, 0)),
                  pl.BlockSpec((None, 1, tb, p.shape[-1]), lambda bi, j, offs: (layer, bi, j, 0)),
                  pl.BlockSpec(gple.shape, lambda bi, j, offs: (0, 0)),
                  pl.BlockSpec(wgate.shape, lambda bi, j, offs: (0, 0)),
                  pl.BlockSpec(wproj.shape, lambda bi, j, offs: (0, 0)),
                  pl.BlockSpec(gfin.shape, lambda bi, j, offs: (0, 0))],
        out_specs=pl.BlockSpec((1, tb, d), lambda bi, j, offs: (bi, j, 0)),
        scratch_shapes=[pltpu.VMEM((2, ne, win, d), BF16), pltpu.SemaphoreType.DMA((2, ne)),
                        pltpu.VMEM((ne, tb, win), BF16), pltpu.VMEM((ne, tb, chunk), F32), pltpu.VMEM((tb, d), F32)],
    )
    return pl.pallas_call(
        functools.partial(_combine_kernel, tb=tb, final=final),
        grid_spec=grid_spec,
        out_shape=jax.ShapeDtypeStruct((b, seq, d), F32),
        compiler_params=_cparams(("arbitrary", "arbitrary")),
        name="moe_combine_ple",
    )(offs, ye, pos, gsel, x1, p, gple, wgate, wproj, gfin)


def _pad_cols(a, width):
    return jnp.pad(a, ((0, 0), (0, width - a.shape[1])))


def _pack_inproj(w_in):
    offs = np.cumsum((0,) + IN_SPLITS)
    cq, ckv, kr, hy, z, xbc, dt = [w_in[:, offs[i]:offs[i + 1]] for i in range(len(IN_SPLITS))]
    d = w_in.shape[0]
    half = MLA_ROPE // 2
    zeros = lambda n: jnp.zeros((d, n), w_in.dtype)
    kr_pad = jnp.concatenate([zeros(MLA_NOPE), kr, zeros(HEAD_PAD - MLA_NOPE - MLA_ROPE)], axis=1)
    kr_swap = jnp.concatenate([zeros(MLA_NOPE), -kr[:, half:], kr[:, :half], zeros(HEAD_PAD - MLA_NOPE - MLA_ROPE)],
                              axis=1)
    wall = jnp.concatenate([cq, ckv, kr_pad, kr_swap, hy, z, xbc, _pad_cols(dt, LANES)], axis=1)
    assert wall.shape[1] == _C_END
    return wall.astype(BF16)


def _pack_mla(w_uq, w_ukv):
    lq = w_uq.shape[0]
    lkv = w_ukv.shape[0]
    half = MLA_ROPE // 2
    padw = HEAD_PAD - MLA_NOPE - MLA_ROPE
    q3 = w_uq.reshape(lq, MLA_HEADS, MLA_NOPE + MLA_ROPE)
    nope, rope = q3[..., :MLA_NOPE], q3[..., MLA_NOPE:]
    zq = jnp.zeros((lq, MLA_HEADS, padw), w_uq.dtype)
    wq = jnp.concatenate([nope, rope, zq], axis=-1).reshape(lq, MLA_HEADS * HEAD_PAD)
    wqs = jnp.concatenate([jnp.zeros_like(nope), -rope[..., half:], rope[..., :half], zq], axis=-1)
    wqs = wqs.reshape(lq, MLA_HEADS * HEAD_PAD)
    kv3 = w_ukv.reshape(lkv, MLA_HEADS, MLA_NOPE + MLA_V)
    knope, vv = kv3[..., :MLA_NOPE], kv3[..., MLA_NOPE:]
    wk = jnp.concatenate([knope, jnp.zeros((lkv, MLA_HEADS, HEAD_PAD - MLA_NOPE), w_ukv.dtype)], axis=-1)
    wk = wk.reshape(lkv, MLA_HEADS * HEAD_PAD)
    wv = vv.reshape(lkv, MLA_HEADS * MLA_V).T
    return wq.astype(BF16), wqs.astype(BF16), wk.astype(BF16), wv.astype(BF16)


def _row(a):
    return a.reshape(1, -1).astype(F32)


TM_PROJ = 512
TQ_ATTN = 1024
TK_ATTN = 512
ROWS_CONV = 512
ROWS_FILT = 512
CB_DFT = 2048
TM_OUT = 512
K1_PER_STEP = 4
TB_MOE = 256
TB_COMBINE = 128
TF_FFN = 512


def _hyena(hy_u, tabs, kspec, conv_w, conv_b, bias):
    b, seq, _ = hy_u.shape
    c = HY_WIDTH
    parts = [dwconv(hy_u, conv_w[:, i * c:(i + 1) * c], _row(conv_b[i * c:(i + 1) * c]), i * c, c, False, ROWS_CONV)
             for i in range(HY_ORDER + 1)]
    gates, v = parts[:-1], parts[-1]
    half, n1 = tabs["half"], tabs["n1"]
    flat = lambda a: a.reshape(b * half, FFT_N2 * c)
    for o in range(HY_ORDER):
        a = leftmm(tabs["m1"], flat(v), CB_DFT).reshape(2, n1, FFT_N2, c)
        y = spectrum_multiply(a, kspec, o, tabs["fblk"], tabs["fblk_t"], tabs["twr_col"], tabs["twi_col"],
                              K1_PER_STEP)
        v = conv_output(tabs["m3"], y.reshape(2 * n1, FFT_N2 * c), flat(v), flat(gates[o]), _row(bias[o]), CB_DFT)
        v = v.reshape(b, seq, c)
    return v


def _hyena_kspec(seq, tabs, w1, b1, freq, w2, b2, w3, decay):
    bands = np.arange(1, HY_BANDS + 1, dtype=np.float64) * 2.0 * np.pi
    mult = np.zeros((1, LANES), np.float32)
    mult[0, 1:1 + HY_BANDS] = bands
    mult[0, 1 + HY_BANDS:1 + 2 * HY_BANDS] = bands
    w1p = jnp.pad(w1.astype(F32), ((0, LANES - w1.shape[0]), (0, 0)))
    ncol = HY_ORDER * 2 * HY_WIDTH
    bwd = (np.arange(ncol) // HY_WIDTH) % 2
    kf = hyena_filters(seq, jnp.asarray(mult), w1p, _row(b1), _row(freq), w2.astype(F32), _row(b2), w3.astype(F32),
                       _row(decay), jnp.asarray(bwd.astype(np.float32)).reshape(1, ncol), ROWS_FILT)
    a = leftmm(tabs["mk"], kf.reshape(tabs["half"], FFT_N2 * ncol), CB_DFT)
    a4 = a.reshape(2, tabs["n1"], FFT_N2, ncol)
    return filter_spectrum(a4, tabs["fblk"], tabs["twr_col"], tabs["twi_col"], K1_PER_STEP)


def kernel(x, p, positions, norm_mix, w_in, mla_q_norm, mla_w_uq, mla_kv_norm, mla_w_ukv, mla_out_norm, hy_conv_w,
           hy_conv_b, hy_filt_w1, hy_filt_b1, hy_filt_freq, hy_filt_w2, hy_filt_b2, hy_filt_w3, hy_decay, hy_bias,
           hy_out_norm, ssm_conv_w, ssm_conv_b, ssm_dt_bias, ssm_a_log, ssm_d, ssm_norm, w_out, norm_ffn, moe_router,
           moe_w_gate, moe_w_up, moe_w_down, ple_norm, ple_gate_w, ple_proj, final_norm_g):
    batch, seq, d = x.shape
    depth = w_in.shape[0]
    t = batch * seq
    cap = EC_CAPACITY_FACTOR * seq // N_EXPERTS
    tb = min(TB_MOE, seq)
    tbc = min(TB_COMBINE, seq)
    cap_pad = cap + tb
    tm_proj = min(TM_PROJ, seq)
    tm_out = min(TM_OUT, seq)

    freq = np.zeros((1, HEAD_PAD), np.float32)
    inv = ROPE_THETA ** (-np.arange(0, MLA_ROPE, 2, dtype=np.float32) / MLA_ROPE)
    freq[0, MLA_NOPE:MLA_NOPE + MLA_ROPE // 2] = inv
    freq[0, MLA_NOPE + MLA_ROPE // 2:MLA_NOPE + MLA_ROPE] = inv
    cos_t, sin_t = rope_tables(positions.reshape(t, 1), jnp.asarray(freq), tm_proj)

    tabs = _dft_tables(seq, batch)
    tril = jnp.asarray(np.tril(np.ones((SSM_CHUNK, SSM_CHUNK), np.float32)))

    x2d = x.reshape(t, d)
    for i in range(depth):
        wall = _pack_inproj(w_in[i])
        wq, wqs, wk, wv = _pack_mla(mla_w_uq[i], mla_w_ukv[i])
        q, k, v, hy_u, z, xbc_raw, dt_raw = inproj(x2d, _row(norm_mix[i]), wall, _row(mla_q_norm[i]), wq, wqs,
                                                    _row(mla_kv_norm[i]), wk, wv, cos_t, sin_t, tm_proj)
        o_mla = attention(q.reshape(batch, seq, -1), k.reshape(batch, seq, -1), v, min(TQ_ATTN, seq),
                          min(TK_ATTN, seq))

        kspec = _hyena_kspec(seq, tabs, hy_filt_w1[i], hy_filt_b1[i], hy_filt_freq[i], hy_filt_w2[i], hy_filt_b2[i],
                             hy_filt_w3[i], hy_decay[i])
        o_hy = _hyena(hy_u.reshape(batch, seq, -1), tabs, kspec, hy_conv_w[i], hy_conv_b[i], hy_bias[i])

        xbc = dwconv(xbc_raw.reshape(batch, seq, -1), ssm_conv_w[i], _row(ssm_conv_b[i]), 0, SSM_CONV_DIM, True,
                     ROWS_CONV)
        dtbias_row = _pad_cols(_row(ssm_dt_bias[i]), LANES)
        a_row = _pad_cols(_row(-jnp.exp(ssm_a_log[i].astype(F32))), LANES)
        y_f, y_b = ssd_scan(xbc, dt_raw.reshape(batch, seq, -1), dtbias_row, a_row, tril)

        dsk = _row(jnp.repeat(ssm_d[i].astype(F32), SSM_HEADDIM))
        x1, hffn, aff = outproj(o_mla.reshape(t, -1), o_hy.reshape(t, -1), y_f.reshape(t, -1), y_b.reshape(t, -1),
                                xbc.reshape(t, -1), z, x2d, _row(mla_out_norm[i]), _row(hy_out_norm[i]), dsk,
                                _row(ssm_norm[i]), w_out[i].astype(BF16), _row(norm_ffn[i]),
                                _stack3_lhs(moe_router[i].astype(F32).T), batch, tm_out)

        pos, gsel, rowoff = moe_select(aff, cap)
        offs_g = jnp.concatenate([rowoff[:, :, ::tb // LANES] // SUBLANES,
                                  jnp.full((batch, N_EXPERTS, 1), pl.cdiv(cap, SUBLANES), I32)], axis=-1)
        offs_c = rowoff[:, :, ::tbc // LANES] // SUBLANES
        xe = moe_gather(hffn.reshape(batch, seq, d), pos.reshape(batch, N_EXPERTS, 1, seq), offs_g, cap, tb)
        ye = moe_ffn(xe, moe_w_gate, moe_w_up, moe_w_down, i, cap, cap_pad, TF_FFN)
        x3 = moe_combine(ye, pos, gsel, offs_c, x1.reshape(batch, seq, d), p, i, _row(ple_norm[i]),
                         ple_gate_w[i].astype(BF16), ple_proj[i].astype(BF16), _row(final_norm_g), i == depth - 1, tbc)
        x2d = x3.reshape(t, d)
    return x2d.reshape(batch, seq, d)
```

```python
import functools
import math

import numpy as np
import jax
import jax.numpy as jnp
from jax import lax
from jax.experimental import pallas as pl
from jax.experimental.pallas import tpu as pltpu

F32 = jnp.float32
BF16 = jnp.bfloat16
I32 = jnp.int32
HIGHEST = lax.Precision.HIGHEST

EPS = 1e-6
LANES = 128
SUBLANES = 8
COMBINE_ALIGN = 16
VMEM_LIMIT = 56 * 1024 * 1024

D_MODEL = 1024
MLA_HEADS = 8
MLA_NOPE = 64
MLA_ROPE = 32
MLA_V = 64
MLA_Q_LORA = 256
MLA_KV_LORA = 128
HEAD_PAD = 128
ATTN_SUM_ROWS = 16
ROPE_THETA = 10000.0
HY_WIDTH = 256
HY_ORDER = 2
HY_SHORT = 3
HY_BANDS = 8
HY_FILT_HID = 64
SSM_WIDTH = 256
SSM_HEADDIM = 64
SSM_HEADS = 4
SSM_GROUPS = 2
SSM_STATE = 128
SSM_CONV = 5
SSM_CHUNK = 128
SSM_CONV_DIM = 768
N_EXPERTS = 16
EXPERT_FF = 2048
EC_CAPACITY_FACTOR = 2
PLE_DIM = 256
FFT_N2 = 128
MANTISSA_STEPS = 40
IN_SPLITS = (MLA_Q_LORA, MLA_KV_LORA, MLA_ROPE, 3 * HY_WIDTH, SSM_WIDTH, SSM_CONV_DIM, 2 * SSM_HEADS)


def _cparams(sem, vmem=None):
    return pltpu.CompilerParams(dimension_semantics=sem, vmem_limit_bytes=vmem or VMEM_LIMIT)


def _rms(x, g):
    ms = jnp.mean(x * x, axis=-1, keepdims=True)
    return x * lax.rsqrt(ms + EPS) * g


def _dot(a, b, precision=None):
    return jnp.dot(a, b, preferred_element_type=F32, precision=precision)


def _dot_nt(a, b, precision=None):
    return lax.dot_general(a, b, (((1,), (1,)), ((), ())), preferred_element_type=F32, precision=precision)


def _dot_tn(a, b, precision=None):
    return lax.dot_general(a, b, (((0,), (0,)), ((), ())), preferred_element_type=F32, precision=precision)


def _split_bf16(x):
    hi = x.astype(BF16)
    return hi, (x - hi.astype(F32)).astype(BF16)


def _dot3(a, b):
    ah, al = _split_bf16(a)
    bh, bl = _split_bf16(b)
    return _dot(jnp.concatenate([ah, ah, al], axis=1), jnp.concatenate([bh, bl, bh], axis=0))


def _full(shape):
    n = len(shape)
    return pl.BlockSpec(shape, lambda *_: (0,) * n)


def _rope_kernel(pos_ref, freq_ref, cos_ref, sin_ref):
    ang = pos_ref[...].astype(F32) * freq_ref[...]
    cos_ref[...] = jnp.cos(ang)
    sin_ref[...] = jnp.sin(ang)


def rope_tables(pos_col, freq_row, tm):
    t = pos_col.shape[0]
    return pl.pallas_call(
        _rope_kernel,
        grid=(t // tm,),
        in_specs=[pl.BlockSpec((tm, 1), lambda i: (i, 0)), _full((1, HEAD_PAD))],
        out_specs=[pl.BlockSpec((tm, HEAD_PAD), lambda i: (i, 0))] * 2,
        out_shape=[jax.ShapeDtypeStruct((t, HEAD_PAD), F32)] * 2,
        compiler_params=_cparams(("parallel",)),
        name="rope_tables",
    )(pos_col, freq_row)


_C_CQ = 0
_C_CKV = 256
_C_KR = 384
_C_KRS = 512
_C_HY = 640
_C_Z = 1408
_C_XBC = 1664
_C_DT = 2432
_C_END = 2560


def _inproj_kernel(x_ref, gmix_ref, wall_ref, qn_ref, wq_ref, wqs_ref, kvn_ref, wk_ref, wv_ref, cos_ref, sin_ref,
                   q_ref, k_ref, v_ref, hy_ref, z_ref, xbc_ref, dt_ref, *, scale):
    h = _rms(x_ref[...], gmix_ref[...]).astype(BF16)
    proj = _dot(h, wall_ref[...])
    hy_ref[...] = proj[:, _C_HY:_C_Z]
    z_ref[...] = proj[:, _C_Z:_C_XBC]
    xbc_ref[...] = proj[:, _C_XBC:_C_DT]
    dt_ref[...] = proj[:, _C_DT:_C_END]
    cos = cos_ref[...]
    sin = sin_ref[...]
    cos8 = jnp.concatenate([cos] * MLA_HEADS, axis=-1)
    sin8 = jnp.concatenate([sin] * MLA_HEADS, axis=-1)
    cqn = _rms(proj[:, _C_CQ:_C_CKV], qn_ref[...]).astype(BF16)
    q = _dot(cqn, wq_ref[...])
    qs = _dot(cqn, wqs_ref[...])
    q_ref[...] = ((q * cos8 + qs * sin8) * scale).astype(BF16)
    ckvn = _rms(proj[:, _C_CKV:_C_KR], kvn_ref[...]).astype(BF16)
    kn = _dot(ckvn, wk_ref[...])
    v_ref[...] = _dot_nt(wv_ref[...], ckvn).astype(BF16)
    kr = proj[:, _C_KR:_C_KRS] * cos + proj[:, _C_KRS:_C_HY] * sin
    k_ref[...] = (kn + jnp.concatenate([kr] * MLA_HEADS, axis=-1)).astype(BF16)


def inproj(x2d, gmix, wall, qn, wq, wqs, kvn, wk, wv, cos_t, sin_t, tm):
    t = x2d.shape[0]
    hq = MLA_HEADS * HEAD_PAD
    row = lambda w: pl.BlockSpec((tm, w), lambda i: (i, 0))
    outs = [(hq, BF16), (hq, BF16), None, (3 * HY_WIDTH, F32), (SSM_WIDTH, F32), (SSM_CONV_DIM, F32), (LANES, F32)]
    hv = MLA_HEADS * MLA_V
    return pl.pallas_call(
        functools.partial(_inproj_kernel, scale=(MLA_NOPE + MLA_ROPE) ** -0.5 * math.log2(math.e)),
        grid=(t // tm,),
        in_specs=[row(D_MODEL), _full(gmix.shape), _full(wall.shape), _full(qn.shape), _full(wq.shape),
                  _full(wqs.shape), _full(kvn.shape), _full(wk.shape), _full(wv.shape), row(HEAD_PAD), row(HEAD_PAD)],
        out_specs=[row(o[0]) if o else pl.BlockSpec((hv, tm), lambda i: (0, i)) for o in outs],
        out_shape=[jax.ShapeDtypeStruct((t, o[0]), o[1]) if o else jax.ShapeDtypeStruct((hv, t), BF16)
                   for o in outs],
        compiler_params=_cparams(("parallel",)),
        name="inproj",
    )(x2d, gmix, wall, qn, wq, wqs, kvn, wk, wv, cos_t, sin_t)


def _attn_kernel(q_ref, k_ref, vt_ref, o_ref, st_ref, *, tk):
    seq = k_ref.shape[1]
    tq = q_ref.shape[1]
    nh = st_ref.shape[0]
    npairs = seq // (2 * tk)

    def scores(hh, c, slot):
        off = pl.multiple_of(c * tk, tk)
        st = _dot_nt(k_ref[0, pl.ds(off, tk), hh * HEAD_PAD:(hh + 1) * HEAD_PAD],
                     q_ref[0, :, hh * HEAD_PAD:(hh + 1) * HEAD_PAD])
        st_ref[hh, slot] = st
        return jnp.max(st, axis=0, keepdims=True)

    ones_rows = jnp.ones((ATTN_SUM_ROWS, tk), BF16)

    def update(hh, c, slot, m, acc, smax):
        off = pl.multiple_of(c * tk, tk)
        vtc = jnp.concatenate([vt_ref[hh * MLA_V:(hh + 1) * MLA_V, pl.ds(off, tk)], ones_rows], axis=0)
        m_new = jnp.maximum(m, smax)
        alpha = jnp.exp2(m - m_new)
        p = jnp.exp2(st_ref[hh, slot] - m_new)
        acc = acc * alpha + _dot(vtc, p.astype(BF16))
        return m_new, acc

    def pair(i, carry, last):
        new = []
        for hh in range(nh):
            m, acc, smax0 = carry[hh]
            smax1 = scores(hh, 2 * i + 1, 1)
            m, acc = update(hh, 2 * i, 0, m, acc, smax0)
            smax0 = smax1 if last else scores(hh, 2 * i + 2, 0)
            m, acc = update(hh, 2 * i + 1, 1, m, acc, smax1)
            new.append((m, acc, smax0))
        return tuple(new)

    init = tuple((jnp.full((1, tq), -jnp.inf, F32), jnp.zeros((MLA_V + ATTN_SUM_ROWS, tq), F32), scores(hh, 0, 0))
                 for hh in range(nh))
    carry = lax.fori_loop(0, npairs - 1, functools.partial(pair, last=False), init)
    final = pair(npairs - 1, carry, True)
    o_ref[0] = jnp.concatenate([jnp.transpose(acc[:MLA_V] / acc[MLA_V:MLA_V + 1]) for _, acc, _ in final], axis=-1)


def attention(q, k, vt, tq, tk):
    b, seq, _ = q.shape
    return pl.pallas_call(
        functools.partial(_attn_kernel, tk=tk),
        grid=(b, MLA_HEADS // 2, seq // tq),
        in_specs=[pl.BlockSpec((1, tq, 2 * HEAD_PAD), lambda bi, hp, qi: (bi, qi, hp)),
                  pl.BlockSpec((1, seq, 2 * HEAD_PAD), lambda bi, hp, qi: (bi, 0, hp)),
                  pl.BlockSpec((2 * MLA_V, seq), lambda bi, hp, qi: (hp, bi))],
        out_specs=pl.BlockSpec((1, tq, 2 * MLA_V), lambda bi, hp, qi: (bi, qi, hp)),
        out_shape=jax.ShapeDtypeStruct((b, seq, MLA_HEADS * MLA_V), F32),
        scratch_shapes=[pltpu.VMEM((2, 2, tk, tq), F32)],
        compiler_params=_cparams(("parallel", "parallel", "parallel")),
        name="attention",
    )(q, k, vt)


def _dwconv_kernel(x_ref, w_ref, b_ref, o_ref, *, width, act, rows):
    seq = x_ref.shape[1]
    pad = width // 2
    nchunks = seq // rows
    w = w_ref[...]
    bias = b_ref[...]

    def body(c, carry):
        r0 = pl.multiple_of(c * rows, rows)
        cur = x_ref[0, pl.ds(r0, rows), :]
        p0 = pl.multiple_of(jnp.maximum(r0 - SUBLANES, 0), SUBLANES)
        n0 = pl.multiple_of(jnp.minimum(r0 + rows, seq - SUBLANES), SUBLANES)
        prev = jnp.where(c > 0, x_ref[0, pl.ds(p0, SUBLANES), :], 0.0)
        nxt = jnp.where(c < nchunks - 1, x_ref[0, pl.ds(n0, SUBLANES), :], 0.0)
        ext = jnp.concatenate([prev, cur, nxt], axis=0)
        acc = bias + ext[SUBLANES - pad:SUBLANES - pad + rows] * w[0:1]
        for kk in range(1, width):
            s0 = SUBLANES - pad + kk
            acc = acc + ext[s0:s0 + rows] * w[kk:kk + 1]
        if act:
            acc = acc * jax.nn.sigmoid(acc)
        o_ref[0, pl.ds(r0, rows), :] = acc
        return carry

    lax.fori_loop(0, nchunks, body, 0)


def dwconv(x, w, bias, col0, ncols, act, rows):
    b, seq, _ = x.shape
    width = w.shape[0]
    cb0 = col0 // LANES
    return pl.pallas_call(
        functools.partial(_dwconv_kernel, width=width, act=act, rows=min(rows, seq)),
        grid=(b, ncols // LANES),
        in_specs=[pl.BlockSpec((1, seq, LANES), lambda bi, ci: (bi, 0, ci + cb0)),
                  pl.BlockSpec((width, LANES), lambda bi, ci: (0, ci)),
                  pl.BlockSpec((1, LANES), lambda bi, ci: (0, ci))],
        out_specs=pl.BlockSpec((1, seq, LANES), lambda bi, ci: (bi, 0, ci)),
        out_shape=jax.ShapeDtypeStruct((b, seq, ncols), F32),
        compiler_params=_cparams(("parallel", "parallel")),
        name="dwconv",
    )(x, w, bias)


def _hyfilt_kernel(mult_ref, w1_ref, b1_ref, fr_ref, w2_ref, b2_ref, w3_ref, dec_ref, bwd_ref, o_ref, *, seq, rows):
    i = pl.program_id(0)
    ridx = lax.broadcasted_iota(I32, (rows, LANES), 0) + i * rows
    lane = lax.broadcasted_iota(I32, (rows, LANES), 1)
    t = ridx.astype(F32) / seq
    ang = t * mult_ref[...]
    feats = jnp.where(lane == 0, t, jnp.where(lane <= HY_BANDS, jnp.sin(ang), jnp.cos(ang)))
    feats = jnp.where(lane < 1 + 2 * HY_BANDS, feats, 0.0)
    fr = fr_ref[...]
    hdn = jnp.sin(fr * (_dot(feats, w1_ref[...], HIGHEST) + b1_ref[...]))
    hdn = jnp.sin(fr * (_dot(hdn, w2_ref[...], HIGHEST) + b2_ref[...]))
    hh, hl = _split_bf16(hdn)
    wh, wl = _split_bf16(w3_ref[...])
    filt = (_dot(hh, wh) + _dot(hl, wh)) + _dot(hh, wl)
    window = jnp.exp(-t[:, 0:1] * jnp.abs(dec_ref[...]))
    out = filt * window
    keep = jnp.logical_or(ridx[:, 0:1] > 0, bwd_ref[...] < 0.5)
    o_ref[...] = jnp.where(keep, out, 0.0)


def hyena_filters(seq, mult, w1p, b1, fr, w2, b2, w3, dec, bwd_mask, rows):
    ncol = w3.shape[1]
    rows = min(rows, seq)
    args = (mult, w1p, b1, fr, w2, b2, w3, dec, bwd_mask)
    return pl.pallas_call(
        functools.partial(_hyfilt_kernel, seq=seq, rows=rows),
        grid=(seq // rows,),
        in_specs=[_full(a.shape) for a in args],
        out_specs=pl.BlockSpec((rows, ncol), lambda i: (i, 0)),
        out_shape=jax.ShapeDtypeStruct((seq, ncol), F32),
        compiler_params=_cparams(("parallel",)),
        name="hyena_filters",
    )(*args)


def _leftmm_kernel(m_ref, x_ref, o_ref):
    o_ref[...] = _dot3(m_ref[...], x_ref[...])


def leftmm(m, x2d, cb):
    r, kdim = m.shape
    n = x2d.shape[1]
    cb = min(cb, n)
    return pl.pallas_call(
        _leftmm_kernel,
        grid=(n // cb,),
        in_specs=[_full((r, kdim)), pl.BlockSpec((kdim, cb), lambda i: (0, i))],
        out_specs=pl.BlockSpec((r, cb), lambda i: (0, i)),
        out_shape=jax.ShapeDtypeStruct((r, n), F32),
        compiler_params=_cparams(("parallel",)),
        name="dft_outer",
    )(m, x2d)


def _stack3_lhs(f):
    hi, lo = _split_bf16(f)
    return jnp.concatenate([hi, hi, lo], axis=1)


def _stack3_rhs(a):
    hi, lo = _split_bf16(a)
    return jnp.concatenate([hi, lo, hi], axis=0)


def _twiddle(ar, ai, twr, twi, conj):
    if conj:
        return ar * twr + ai * twi, ai * twr - ar * twi
    return ar * twr - ai * twi, ai * twr + ar * twi


def _specfilt_kernel(a_ref, fblk_ref, twr_ref, twi_ref, o_ref, lhs_ref):
    n2 = FFT_N2
    c = a_ref.shape[-1] // 2

    @pl.when((pl.program_id(0) == 0) & (pl.program_id(1) == 0))
    def _():
        lhs_ref[...] = _stack3_lhs(fblk_ref[...])

    for kk in range(a_ref.shape[1]):
        br, bi = _twiddle(a_ref[0, kk], a_ref[1, kk], twr_ref[kk], twi_ref[kk], False)
        x = _dot(lhs_ref[...], _stack3_rhs(jnp.concatenate([br, bi], axis=0)))
        o_ref[0, 0, kk] = x[:n2, :c] + x[:n2, c:]
        o_ref[0, 1, kk] = x[n2:, :c] - x[n2:, c:]


def filter_spectrum(a4, fblk, twr_col, twi_col, k1s):
    _, n1, n2, ctot = a4.shape
    c = HY_WIDTH
    return pl.pallas_call(
        _specfilt_kernel,
        grid=(HY_ORDER, n1 // k1s),
        in_specs=[pl.BlockSpec((2, k1s, n2, 2 * c), lambda o, k: (0, k, 0, o)),
                  _full(fblk.shape),
                  pl.BlockSpec((k1s, n2, 1), lambda o, k: (k, 0, 0)),
                  pl.BlockSpec((k1s, n2, 1), lambda o, k: (k, 0, 0))],
        out_specs=pl.BlockSpec((1, 2, k1s, n2, c), lambda o, k: (o, 0, k, 0, 0)),
        out_shape=jax.ShapeDtypeStruct((HY_ORDER, 2, n1, n2, c), F32),
        scratch_shapes=[pltpu.VMEM((2 * n2, 6 * n2), BF16)],
        compiler_params=_cparams(("arbitrary", "arbitrary")),
        name="filter_spectrum",
    )(a4, fblk, twr_col, twi_col)


def _specmul_kernel(a_ref, k_ref, fblk_ref, fblk_t_ref, twr_ref, twi_ref, o_ref, lhs_ref):
    n2 = FFT_N2

    @pl.when(pl.program_id(0) == 0)
    def _():
        lhs_ref[0] = _stack3_lhs(fblk_ref[...])
        lhs_ref[1] = _stack3_lhs(fblk_t_ref[...])

    for kk in range(a_ref.shape[1]):
        twr, twi = twr_ref[kk], twi_ref[kk]
        br, bi = _twiddle(a_ref[0, kk], a_ref[1, kk], twr, twi, False)
        x = _dot(lhs_ref[0], _stack3_rhs(jnp.concatenate([br, bi], axis=0)))
        xr, xi = x[:n2], x[n2:]
        kr, ki = k_ref[0, 0, kk], k_ref[0, 1, kk]
        p = jnp.concatenate([xr * kr - xi * ki, xr * ki + xi * kr], axis=0)
        y = _dot(lhs_ref[1], _stack3_rhs(p))
        yr, yi = _twiddle(y[:n2], y[n2:], twr, twi, True)
        o_ref[0, kk] = yr
        o_ref[1, kk] = yi


def spectrum_multiply(a4, kspec, order, fblk, fblk_t, twr_col, twi_col, k1s):
    _, n1, n2, c = a4.shape
    return pl.pallas_call(
        _specmul_kernel,
        grid=(n1 // k1s,),
        in_specs=[pl.BlockSpec((2, k1s, n2, c), lambda k: (0, k, 0, 0)),
                  pl.BlockSpec((1, 2, k1s, n2, c), lambda k: (order, 0, k, 0, 0)),
                  _full(fblk.shape), _full(fblk_t.shape),
                  pl.BlockSpec((k1s, n2, 1), lambda k: (k, 0, 0)),
                  pl.BlockSpec((k1s, n2, 1), lambda k: (k, 0, 0))],
        out_specs=pl.BlockSpec((2, k1s, n2, c), lambda k: (0, k, 0, 0)),
        out_shape=jax.ShapeDtypeStruct((2, n1, n2, c), F32),
        scratch_shapes=[pltpu.VMEM((2, 2 * n2, 6 * n2), BF16)],
        compiler_params=_cparams(("arbitrary",)),
        name="spectrum_multiply",
    )(a4, kspec, fblk, fblk_t, twr_col, twi_col)


def _convout_kernel(m_ref, y_ref, v_ref, g_ref, bias_ref, o_ref, *, reps):
    y = _dot3(m_ref[...], y_ref[...])
    bias = jnp.concatenate([bias_ref[...]] * reps, axis=-1)
    v = v_ref[...]
    o_ref[...] = (y + v * bias) * g_ref[...]


def conv_output(m3, y2d, v2d, g2d, bias_row, cb):
    r, kdim = m3.shape
    n = y2d.shape[1]
    cb = min(cb, n)
    c = bias_row.shape[1]
    return pl.pallas_call(
        functools.partial(_convout_kernel, reps=cb // c),
        grid=(n // cb,),
        in_specs=[_full((r, kdim)), pl.BlockSpec((kdim, cb), lambda i: (0, i)),
                  pl.BlockSpec((r, cb), lambda i: (0, i)), pl.BlockSpec((r, cb), lambda i: (0, i)),
                  _full((1, c))],
        out_specs=pl.BlockSpec((r, cb), lambda i: (0, i)),
        out_shape=jax.ShapeDtypeStruct((r, n), F32),
        compiler_params=_cparams(("parallel",)),
        name="dft_outer_inverse",
    )(m3, y2d, v2d, g2d, bias_row)


def _dft_tables(seq, batch):
    n2 = FFT_N2
    half = seq // n2
    n1 = 2 * half
    n = n1 * n2
    k1 = np.arange(n1, dtype=np.float64)[:, None]
    nn1 = np.arange(half, dtype=np.float64)[None, :]
    th = 2.0 * np.pi * k1 * nn1 / n1
    c1, s1 = np.cos(th), np.sin(th)
    assert batch == 2, "the two batch entries are packed as real / imaginary parts"
    m1 = np.block([[c1, s1], [-s1, c1]])
    m3 = np.block([[c1.T, -s1.T], [s1.T, c1.T]]) / n
    mk = np.concatenate([c1, -s1], axis=0)
    kk2 = np.arange(n2, dtype=np.float64)
    th2 = 2.0 * np.pi * np.outer(kk2, kk2) / n2
    c2, s2 = np.cos(th2), np.sin(th2)
    fblk = np.block([[c2, s2], [-s2, c2]])
    tht = 2.0 * np.pi * np.outer(np.arange(n1, dtype=np.float64), kk2) / n
    twr, twi = np.cos(tht), -np.sin(tht)
    f = lambda a: jnp.asarray(a, F32)
    return dict(m1=f(m1), m3=f(m3), mk=f(mk), fblk=f(fblk), fblk_t=f(fblk.T),
                twr_col=f(twr[:, :, None]), twi_col=f(twi[:, :, None]), n1=n1, half=half)


def _ssd_kernel(xf_ref, bf_ref, cf_ref, dtf_ref, xb_ref, bb_ref, cb_ref, dtb_ref, dtbias_ref, a_ref, tril_ref,
                yf_ref, yb_ref, state_ref):
    q = SSM_CHUNK
    hd = SSM_HEADDIM
    ns = SSM_STATE

    @pl.when(pl.program_id(1) == 0)
    def _():
        state_ref[...] = jnp.zeros_like(state_ref)

    tril = tril_ref[...]
    rows = lax.broadcasted_iota(I32, (q, q), 0)
    cols = lax.broadcasted_iota(I32, (q, q), 1)
    a_row = a_ref[...]
    bias = dtbias_ref[...]

    def direction(x_ref, b_ref, c_ref, dt_ref, y_ref, d):
        dt = jax.nn.softplus(dt_ref[0] + bias)
        dta = dt * a_row
        cs = _dot(tril, dta, HIGHEST)
        ecs = cs - dta
        base = ecs if d else cs
        base_t = jnp.transpose(base)
        total = cs[q - 1:q, :]
        x = x_ref[0]
        ys = []
        for g in range(SSM_GROUPS):
            bm = b_ref[0, :, g * ns:(g + 1) * ns]
            cm = c_ref[0, :, g * ns:(g + 1) * ns]
            cb = _dot_nt(cm.astype(BF16), bm.astype(BF16))
            for hh in range(SSM_HEADS // SSM_GROUPS):
                h = g * (SSM_HEADS // SSM_GROUPS) + hh
                j = d * SSM_HEADS + h
                col = jnp.broadcast_to(base[:, j:j + 1], (q, q))
                coln = col if ns == q else jnp.broadcast_to(base[:, j:j + 1], (q, ns))
                row = base_t[j:j + 1, :]
                tot = total[:, j:j + 1]
                if d == 0:
                    seg = jnp.where(rows >= cols, col - row, -jnp.inf)
                    c_scale = jnp.exp(coln)
                    b_scale = jnp.exp(tot - coln)
                else:
                    seg = jnp.where(cols >= rows, row - col, -jnp.inf)
                    c_scale = jnp.exp(tot - coln)
                    b_scale = jnp.exp(coln)
                scores = cb * jnp.exp(seg)
                xdt = (x[:, h * hd:(h + 1) * hd] * jnp.broadcast_to(dt[:, j:j + 1], (q, hd))).astype(BF16)
                st = state_ref[j]
                y = _dot(scores.astype(BF16), xdt) + _dot_nt((cm * c_scale).astype(BF16), st.astype(BF16))
                state_ref[j] = st * jnp.exp(tot) + _dot_tn(xdt, (bm * b_scale).astype(BF16))
                ys.append(y)
        y_ref[0] = jnp.concatenate(ys, axis=-1)

    direction(xf_ref, bf_ref, cf_ref, dtf_ref, yf_ref, 0)
    direction(xb_ref, bb_ref, cb_ref, dtb_ref, yb_ref, 1)


def ssd_scan(xbc, dt_raw, dtbias_row, a_row, tril):
    b, seq, _ = xbc.shape
    q = SSM_CHUNK
    nc = seq // q
    w = SSM_WIDTH
    fwd = lambda col: pl.BlockSpec((1, q, w), lambda bi, i: (bi, i, col))
    bwd = lambda col: pl.BlockSpec((1, q, w), lambda bi, i: (bi, nc - 1 - i, col))
    return pl.pallas_call(
        _ssd_kernel,
        grid=(b, nc),
        in_specs=[fwd(0), fwd(1), fwd(2), pl.BlockSpec((1, q, LANES), lambda bi, i: (bi, i, 0)),
                  bwd(0), bwd(1), bwd(2), pl.BlockSpec((1, q, LANES), lambda bi, i: (bi, nc - 1 - i, 0)),
                  _full((1, LANES)), _full((1, LANES)), _full((q, q))],
        out_specs=[pl.BlockSpec((1, q, w), lambda bi, i: (bi, i, 0)),
                   pl.BlockSpec((1, q, w), lambda bi, i: (bi, nc - 1 - i, 0))],
        out_shape=[jax.ShapeDtypeStruct((b, seq, w), F32)] * 2,
        scratch_shapes=[pltpu.VMEM((2 * SSM_HEADS, SSM_HEADDIM, SSM_STATE), F32)],
        compiler_params=_cparams(("parallel", "arbitrary")),
        name="ssd_scan",
    )(xbc, xbc, xbc, dt_raw, xbc, xbc, xbc, dt_raw, dtbias_row, a_row, tril)


def _outproj_kernel(om_ref, hy_ref, yf_ref, yb_ref, xs_ref, z_ref, x_ref, gm_ref, gh_ref, dsk_ref, gs_ref, wout_ref,
                    gffn_ref, rt_ref, x1_ref, hffn_ref, aff_ref):
    o1 = _rms(om_ref[...], gm_ref[...])
    o2 = _rms(hy_ref[...], gh_ref[...])
    z = z_ref[...]
    y = (yf_ref[...] + yb_ref[...] + xs_ref[...] * dsk_ref[...]) * (z * jax.nn.sigmoid(z))
    gw = SSM_WIDTH // SSM_GROUPS
    gs = gs_ref[...]
    o3 = jnp.concatenate([_rms(y[:, g * gw:(g + 1) * gw], gs[:, g * gw:(g + 1) * gw]) for g in range(SSM_GROUPS)],
                         axis=-1)
    mix = jnp.concatenate([o1, o2, o3], axis=-1).astype(BF16)
    x1 = x_ref[...] + _dot(mix, wout_ref[...])
    x1_ref[...] = x1
    hf = _rms(x1, gffn_ref[...])
    hi, lo = _split_bf16(hf)
    hffn_ref[...] = hi
    logits = _dot_nt(rt_ref[...], jnp.concatenate([hi, lo, hi], axis=1))
    mx = jnp.max(logits, axis=0, keepdims=True)
    ex = jnp.exp(logits - mx)
    aff_ref[0] = ex / jnp.sum(ex, axis=0, keepdims=True)


def outproj(om, hy, yf, yb, xbc, z, x2d, gm, gh, dsk, gs, wout, gffn, router_t, batch, tm):
    t = x2d.shape[0]
    seq = t // batch
    nb = seq // tm
    row = lambda w: pl.BlockSpec((tm, w), lambda i: (i, 0))
    return pl.pallas_call(
        _outproj_kernel,
        grid=(t // tm,),
        in_specs=[row(om.shape[1]), row(HY_WIDTH), row(SSM_WIDTH), row(SSM_WIDTH), row(SSM_WIDTH), row(SSM_WIDTH),
                  row(D_MODEL), _full(gm.shape), _full(gh.shape), _full(dsk.shape), _full(gs.shape),
                  _full(wout.shape), _full(gffn.shape), _full(router_t.shape)],
        out_specs=[row(D_MODEL), row(D_MODEL),
                   pl.BlockSpec((1, N_EXPERTS, tm), lambda i: (i // nb, 0, i % nb))],
        out_shape=[jax.ShapeDtypeStruct((t, D_MODEL), F32), jax.ShapeDtypeStruct((t, D_MODEL), BF16),
                   jax.ShapeDtypeStruct((batch, N_EXPERTS, seq), F32)],
        compiler_params=_cparams(("parallel",)),
        name="outproj_router",
    )(om, hy, yf, yb, xbc, z, x2d, gm, gh, dsk, gs, wout, gffn, router_t)


def _select_kernel(aff_ref, tri_ref, ones_ref, blk_ref, pos_ref, g_ref, off_ref, *, cap, nrows):
    aff = aff_ref[0]
    er = aff.shape[0]
    ne = er // nrows
    aff3 = aff.reshape(ne, nrows, LANES)
    capf = jnp.float32(cap)

    def count(mask3):
        return jnp.sum(jnp.where(mask3, 1.0, 0.0), axis=(1, 2), keepdims=True)

    def enough(cand):
        return count(aff3 >= cand) >= capf

    top = jnp.full((ne, 1, 1), 2.0, F32)
    for shift in (64, 32, 16, 8, 4, 2, 1):
        cand = top * (2.0 ** -shift)
        top = jnp.where(enough(cand), top, cand)
    p = top * 0.5

    def refine(_, carry):
        lo, step = carry
        cand = lo + step
        return jnp.where(enough(cand), cand, lo), step * 0.5

    lo, _ = lax.fori_loop(0, MANTISSA_STEPS, refine, (p, p * 0.5))
    thr = jnp.min(jnp.where(aff3 >= lo, aff3, jnp.inf), axis=(1, 2), keepdims=True)
    gt3 = aff3 > thr
    eq3 = aff3 == thr
    need = capf - count(gt3)

    tri = tri_ref[...]
    ones = ones_ref[...]
    blk = blk_ref[...]

    def prefix(maskf):
        mb = maskf.astype(BF16)
        within = _dot(mb, tri)
        rowtot = _dot(mb, ones)
        before = _dot(blk, rowtot.astype(BF16))
        return within + before, before

    eqf = jnp.where(eq3, 1.0, 0.0).reshape(er, LANES)
    tie_incl, _ = prefix(eqf)
    tie_rank = (tie_incl - eqf).reshape(ne, nrows, LANES)
    sel3 = jnp.logical_or(gt3, jnp.logical_and(eq3, tie_rank < need))
    self_ = jnp.where(sel3, 1.0, 0.0).reshape(er, LANES)
    incl, before = prefix(self_)
    sel = self_ > 0.5
    pos_ref[0] = jnp.where(sel, (incl - self_).astype(I32), -1)
    g_ref[0] = jnp.where(sel, aff, 0.0)
    off_ref[0] = before.astype(I32)


def moe_select(aff, cap):
    b, ne, seq = aff.shape
    nrows = seq // LANES
    er = ne * nrows
    tri = jnp.asarray(np.triu(np.ones((LANES, LANES), np.float32)), BF16)
    ones = jnp.ones((LANES, LANES), BF16)
    ridx = np.arange(er)
    blk = (ridx[:, None] // nrows == ridx[None, :] // nrows) & (ridx[None, :] < ridx[:, None])
    blk = jnp.asarray(blk.astype(np.float32), BF16)
    spec = pl.BlockSpec((1, er, LANES), lambda bi: (bi, 0, 0))
    pos, gsel, off = pl.pallas_call(
        functools.partial(_select_kernel, cap=cap, nrows=nrows),
        grid=(b,),
        in_specs=[spec, _full(tri.shape), _full(ones.shape), _full(blk.shape)],
        out_specs=[spec] * 3,
        out_shape=[jax.ShapeDtypeStruct((b, er, LANES), I32), jax.ShapeDtypeStruct((b, er, LANES), F32),
                   jax.ShapeDtypeStruct((b, er, LANES), I32)],
        compiler_params=_cparams(("parallel",)),
        name="moe_select",
    )(aff.reshape(b, er, LANES), tri, ones, blk)
    return pos.reshape(b, ne, seq), gsel.reshape(b, ne, seq), off[:, :, 0].reshape(b, ne, nrows)


def _gather_kernel(offs_ref, h_ref, pos_ref, g_ref, o_ref, og_ref, acc_ref, gacc_ref, *, tb, cap):
    bi = pl.program_id(0)
    ei = pl.program_id(1)
    seq = h_ref.shape[1]
    acc_ref[0:SUBLANES, :] = jnp.zeros((SUBLANES, acc_ref.shape[1]), F32)
    gacc_ref[0:SUBLANES, :] = jnp.zeros((SUBLANES, 1), F32)
    wins = sorted({min(64, tb + SUBLANES), min(128, tb + SUBLANES), tb + SUBLANES})

    def body(j, carry):
        off8 = offs_ref[bi, ei, j]
        off = pl.multiple_of(off8 * SUBLANES, SUBLANES)
        need = (offs_ref[bi, ei, j + 1] - off8 + 1) * SUBLANES
        t0 = pl.multiple_of(j * tb, tb)

        def place(win):
            pos = pos_ref[0, 0, :, pl.ds(t0, tb)]
            riota = lax.broadcasted_iota(I32, (win, tb), 0)
            hit = riota + off == pos
            rows = _dot(jnp.where(hit, 1.0, 0.0).astype(BF16), h_ref[0, pl.ds(t0, tb), :])
            gates = jnp.sum(jnp.where(hit, g_ref[0, 0, :, pl.ds(t0, tb)], 0.0), axis=1, keepdims=True)
            acc_ref[pl.ds(off, SUBLANES), :] += rows[0:SUBLANES]
            acc_ref[pl.ds(off + SUBLANES, win - SUBLANES), :] = rows[SUBLANES:]
            gacc_ref[pl.ds(off, SUBLANES), :] += gates[0:SUBLANES]
            gacc_ref[pl.ds(off + SUBLANES, win - SUBLANES), :] = gates[SUBLANES:]

        lo = 0
        for win in wins:
            fits = need <= win if win != wins[-1] else True
            pl.when(jnp.logical_and(need > lo, fits))(functools.partial(place, win))
            lo = win
        return carry

    lax.fori_loop(0, seq // tb, body, 0)
    o_ref[0, 0] = acc_ref[0:cap, :].astype(BF16)
    og_ref[0, 0] = gacc_ref[0:cap, :]


def moe_gather(hffn, pos_row, g_row, offs, cap, tb):
    b, seq, d = hffn.shape
    ne = pos_row.shape[1]
    row_spec = pl.BlockSpec((1, 1, 1, seq), lambda bi, ei, offs: (bi, ei, 0, 0))
    grid_spec = pltpu.PrefetchScalarGridSpec(
        num_scalar_prefetch=1,
        grid=(b, ne),
        in_specs=[pl.BlockSpec((1, seq, d), lambda bi, ei, offs: (bi, 0, 0)), row_spec, row_spec],
        out_specs=[pl.BlockSpec((1, 1, cap, d), lambda bi, ei, offs: (bi, ei, 0, 0)),
                   pl.BlockSpec((1, 1, cap, 1), lambda bi, ei, offs: (bi, ei, 0, 0))],
        scratch_shapes=[pltpu.VMEM((cap + tb + SUBLANES, d), F32), pltpu.VMEM((cap + tb + SUBLANES, 1), F32)],
    )
    return pl.pallas_call(
        functools.partial(_gather_kernel, tb=tb, cap=cap),
        grid_spec=grid_spec,
        out_shape=[jax.ShapeDtypeStruct((b, ne, cap, d), BF16), jax.ShapeDtypeStruct((b, ne, cap, 1), F32)],
        compiler_params=_cparams(("parallel", "arbitrary")),
        name="moe_gather",
    )(offs, hffn, pos_row, g_row)


def _ffn_kernel(xe_ref, gs_ref, wg_ref, wu_ref, wd_ref, o_ref, acc_ref, *, cap):
    f = pl.program_id(1)
    nb = xe_ref.shape[0]
    d = xe_ref.shape[-1]
    xe = xe_ref[...].reshape(nb * cap, d)
    a = _dot(xe, wg_ref[...].astype(BF16))
    u = _dot(xe, wu_ref[...].astype(BF16))
    hid = (a * jax.nn.sigmoid(a) * u).astype(BF16)
    part = _dot(hid, wd_ref[...].astype(BF16))

    @pl.when(f == 0)
    def _():
        acc_ref[...] = part

    @pl.when(f > 0)
    def _():
        acc_ref[...] += part

    @pl.when(f == pl.num_programs(1) - 1)
    def _():
        gated = acc_ref[...] * gs_ref[...].reshape(nb * cap, 1)
        o_ref[:, 0:cap, :] = gated.reshape(nb, cap, d).astype(BF16)
        o_ref[:, cap:, :] = jnp.zeros((nb, o_ref.shape[1] - cap, d), BF16)


def moe_ffn(xe, gslot, w_gate, w_up, w_down, layer, cap, cap_pad, tf):
    b, ne, _, d = xe.shape
    ff = w_gate.shape[-1]
    return pl.pallas_call(
        functools.partial(_ffn_kernel, cap=cap),
        grid=(ne, ff // tf),
        in_specs=[pl.BlockSpec((b, None, cap, d), lambda e, f: (0, e, 0, 0)),
                  pl.BlockSpec((b, None, cap, 1), lambda e, f: (0, e, 0, 0)),
                  pl.BlockSpec((None, None, d, tf), lambda e, f: (layer, e, 0, f)),
                  pl.BlockSpec((None, None, d, tf), lambda e, f: (layer, e, 0, f)),
                  pl.BlockSpec((None, None, tf, d), lambda e, f: (layer, e, f, 0))],
        out_specs=pl.BlockSpec((b, None, cap_pad, d), lambda e, f: (0, e, 0, 0)),
        out_shape=jax.ShapeDtypeStruct((b, ne, cap_pad, d), BF16),
        scratch_shapes=[pltpu.VMEM((b * cap, d), F32)],
        compiler_params=_cparams(("parallel", "arbitrary")),
        name="moe_ffn",
    )(xe, gslot, w_gate, w_up, w_down)


def _combine_kernel(offs_ref, ye_hbm, pos_ref, x1_ref, p_ref, gple_ref, wgate_ref, wproj_ref, gfin_ref, o_ref,
                    buf_ref, sem_ref, *, tb, win, final):
    bi = pl.program_id(0)
    j = pl.program_id(1)
    nj = pl.num_programs(1)
    ne = pos_ref.shape[1]
    step = bi * nj + j
    slot = step % 2

    def window_copy(b_, j_, e, slot_):
        off = pl.multiple_of(offs_ref[b_, e, j_] * COMBINE_ALIGN, COMBINE_ALIGN)
        return pltpu.make_async_copy(ye_hbm.at[b_, e, pl.ds(off, win), :],
                                     buf_ref.at[slot_, pl.ds(e * win, win), :], sem_ref.at[slot_, e])

    @pl.when(step == 0)
    def _():
        for e in range(ne):
            window_copy(bi, j, e, slot).start()

    @pl.when(step + 1 < pl.num_programs(0) * nj)
    def _():
        wrap = j + 1 == nj
        b_next = jnp.where(wrap, bi + 1, bi)
        j_next = jnp.where(wrap, 0, j + 1)
        for e in range(ne):
            window_copy(b_next, j_next, e, 1 - slot).start()

    def token_major(a):
        pad = jnp.zeros((tb - ne, tb), a.dtype)
        return jnp.transpose(jnp.concatenate([a, pad], axis=0))[:, :ne]

    pos_all = token_major(pos_ref[0])
    lane = lax.broadcasted_iota(I32, (tb, LANES), 1)
    targets = []
    for e in range(ne):
        col = pos_all[:, e:e + 1]
        delta = e * win - offs_ref[bi, e, j] * COMBINE_ALIGN
        targets.append(jnp.broadcast_to(jnp.where(col >= 0, col + delta, -1), (tb, LANES)))
    tiles = []
    for k in range(ne * win // LANES):
        hits = [jnp.where(targets[e] == lane + k * LANES, 1.0, 0.0) for e in range(ne)
                if e * win < (k + 1) * LANES and (e + 1) * win > k * LANES]
        tiles.append(functools.reduce(lambda a, b: a + b, hits).astype(BF16))
    sel = jnp.concatenate(tiles, axis=1)
    for e in range(ne):
        window_copy(bi, j, e, slot).wait()
    x2 = x1_ref[0] + _dot(sel, buf_ref[slot])
    hp = _rms(x2, gple_ref[...]).astype(BF16)
    gt = jax.nn.sigmoid(_dot(hp, wgate_ref[...]))
    x3 = x2 + _dot(p_ref[0].astype(BF16), wproj_ref[...]) * gt
    o_ref[0] = _rms(x3, gfin_ref[...]) if final else x3


def moe_combine(ye, pos, offs, x1, p, layer, gple, wgate, wproj, gfin, final, tb):
    b, ne, _, d = ye.shape
    seq = x1.shape[1]
    win = tb + COMBINE_ALIGN
    assert (ne * win) % LANES == 0
    grid_spec = pltpu.PrefetchScalarGridSpec(
        num_scalar_prefetch=1,
        grid=(b, seq // tb),
        in_specs=[pl.BlockSpec(memory_space=pl.ANY),
                  pl.BlockSpec((1, ne, tb), lambda bi, j, offs: (bi, 0, j)),
                  pl.BlockSpec((1, tb, d), lambda bi, j, offs: (bi, j, 0)),
                  pl.BlockSpec((None, 1, tb, p.shape[-1]), lambda bi, j, offs: (layer, bi, j, 0)),
                  pl.BlockSpec(gple.shape, lambda bi, j, offs: (0, 0)),
                  pl.BlockSpec(wgate.shape, lambda bi, j, offs: (0, 0)),
                  pl.BlockSpec(wproj.shape, lambda bi, j, offs: (0, 0)),
                  pl.BlockSpec(gfin.shape, lambda bi, j, offs: (0, 0))],
        out_specs=pl.BlockSpec((1, tb, d), lambda bi, j, offs: (bi, j, 0)),
        scratch_shapes=[pltpu.VMEM((2, ne * win, d), BF16), pltpu.SemaphoreType.DMA((2, ne))],
    )
    return pl.pallas_call(
        functools.partial(_combine_kernel, tb=tb, win=win, final=final),
        grid_spec=grid_spec,
        out_shape=jax.ShapeDtypeStruct((b, seq, d), F32),
        compiler_params=_cparams(("arbitrary", "arbitrary")),
        name="moe_combine_ple",
    )(offs, ye, pos, x1, p, gple, wgate, wproj, gfin)


def _pad_cols(a, width):
    return jnp.pad(a, ((0, 0), (0, width - a.shape[1])))


def _pack_inproj(w_in):
    offs = np.cumsum((0,) + IN_SPLITS)
    cq, ckv, kr, hy, z, xbc, dt = [w_in[:, offs[i]:offs[i + 1]] for i in range(len(IN_SPLITS))]
    d = w_in.shape[0]
    half = MLA_ROPE // 2
    zeros = lambda n: jnp.zeros((d, n), w_in.dtype)
    kr_pad = jnp.concatenate([zeros(MLA_NOPE), kr, zeros(HEAD_PAD - MLA_NOPE - MLA_ROPE)], axis=1)
    kr_swap = jnp.concatenate([zeros(MLA_NOPE), -kr[:, half:], kr[:, :half], zeros(HEAD_PAD - MLA_NOPE - MLA_ROPE)],
                              axis=1)
    wall = jnp.concatenate([cq, ckv, kr_pad, kr_swap, hy, z, xbc, _pad_cols(dt, LANES)], axis=1)
    assert wall.shape[1] == _C_END
    return wall.astype(BF16)


def _pack_mla(w_uq, w_ukv):
    lq = w_uq.shape[0]
    lkv = w_ukv.shape[0]
    half = MLA_ROPE // 2
    padw = HEAD_PAD - MLA_NOPE - MLA_ROPE
    q3 = w_uq.reshape(lq, MLA_HEADS, MLA_NOPE + MLA_ROPE)
    nope, rope = q3[..., :MLA_NOPE], q3[..., MLA_NOPE:]
    zq = jnp.zeros((lq, MLA_HEADS, padw), w_uq.dtype)
    wq = jnp.concatenate([nope, rope, zq], axis=-1).reshape(lq, MLA_HEADS * HEAD_PAD)
    wqs = jnp.concatenate([jnp.zeros_like(nope), -rope[..., half:], rope[..., :half], zq], axis=-1)
    wqs = wqs.reshape(lq, MLA_HEADS * HEAD_PAD)
    kv3 = w_ukv.reshape(lkv, MLA_HEADS, MLA_NOPE + MLA_V)
    knope, vv = kv3[..., :MLA_NOPE], kv3[..., MLA_NOPE:]
    wk = jnp.concatenate([knope, jnp.zeros((lkv, MLA_HEADS, HEAD_PAD - MLA_NOPE), w_ukv.dtype)], axis=-1)
    wk = wk.reshape(lkv, MLA_HEADS * HEAD_PAD)
    wv = vv.reshape(lkv, MLA_HEADS * MLA_V).T
    return wq.astype(BF16), wqs.astype(BF16), wk.astype(BF16), wv.astype(BF16)


def _row(a):
    return a.reshape(1, -1).astype(F32)


TM_PROJ = 512
TQ_ATTN = 1024
TK_ATTN = 512
ROWS_CONV = 512
ROWS_FILT = 512
CB_DFT = 2048
TM_OUT = 512
K1_PER_STEP = 4
TB_MOE = 256
TB_COMBINE = 128
TF_FFN = 512


def _hyena(hy_u, tabs, kspec, conv_w, conv_b, bias):
    b, seq, _ = hy_u.shape
    c = HY_WIDTH
    parts = [dwconv(hy_u, conv_w[:, i * c:(i + 1) * c], _row(conv_b[i * c:(i + 1) * c]), i * c, c, False, ROWS_CONV)
             for i in range(HY_ORDER + 1)]
    gates, v = parts[:-1], parts[-1]
    half, n1 = tabs["half"], tabs["n1"]
    flat = lambda a: a.reshape(b * half, FFT_N2 * c)
    for o in range(HY_ORDER):
        a = leftmm(tabs["m1"], flat(v), CB_DFT).reshape(2, n1, FFT_N2, c)
        y = spectrum_multiply(a, kspec, o, tabs["fblk"], tabs["fblk_t"], tabs["twr_col"], tabs["twi_col"],
                              K1_PER_STEP)
        v = conv_output(tabs["m3"], y.reshape(2 * n1, FFT_N2 * c), flat(v), flat(gates[o]), _row(bias[o]), CB_DFT)
        v = v.reshape(b, seq, c)
    return v


def _hyena_kspec(seq, tabs, w1, b1, freq, w2, b2, w3, decay):
    bands = np.arange(1, HY_BANDS + 1, dtype=np.float64) * 2.0 * np.pi
    mult = np.zeros((1, LANES), np.float32)
    mult[0, 1:1 + HY_BANDS] = bands
    mult[0, 1 + HY_BANDS:1 + 2 * HY_BANDS] = bands
    w1p = jnp.pad(w1.astype(F32), ((0, LANES - w1.shape[0]), (0, 0)))
    ncol = HY_ORDER * 2 * HY_WIDTH
    bwd = (np.arange(ncol) // HY_WIDTH) % 2
    kf = hyena_filters(seq, jnp.asarray(mult), w1p, _row(b1), _row(freq), w2.astype(F32), _row(b2), w3.astype(F32),
                       _row(decay), jnp.asarray(bwd.astype(np.float32)).reshape(1, ncol), ROWS_FILT)
    a = leftmm(tabs["mk"], kf.reshape(tabs["half"], FFT_N2 * ncol), CB_DFT)
    a4 = a.reshape(2, tabs["n1"], FFT_N2, ncol)
    return filter_spectrum(a4, tabs["fblk"], tabs["twr_col"], tabs["twi_col"], K1_PER_STEP)


def kernel(x, p, positions, norm_mix, w_in, mla_q_norm, mla_w_uq, mla_kv_norm, mla_w_ukv, mla_out_norm, hy_conv_w,
           hy_conv_b, hy_filt_w1, hy_filt_b1, hy_filt_freq, hy_filt_w2, hy_filt_b2, hy_filt_w3, hy_decay, hy_bias,
           hy_out_norm, ssm_conv_w, ssm_conv_b, ssm_dt_bias, ssm_a_log, ssm_d, ssm_norm, w_out, norm_ffn, moe_router,
           moe_w_gate, moe_w_up, moe_w_down, ple_norm, ple_gate_w, ple_proj, final_norm_g):
    batch, seq, d = x.shape
    depth = w_in.shape[0]
    t = batch * seq
    cap = EC_CAPACITY_FACTOR * seq // N_EXPERTS
    tb = min(TB_MOE, seq)
    tbc = min(TB_COMBINE, seq)
    cap_pad = cap + tb
    tm_proj = min(TM_PROJ, seq)
    tm_out = min(TM_OUT, seq)

    freq = np.zeros((1, HEAD_PAD), np.float32)
    inv = ROPE_THETA ** (-np.arange(0, MLA_ROPE, 2, dtype=np.float32) / MLA_ROPE)
    freq[0, MLA_NOPE:MLA_NOPE + MLA_ROPE // 2] = inv
    freq[0, MLA_NOPE + MLA_ROPE // 2:MLA_NOPE + MLA_ROPE] = inv
    cos_t, sin_t = rope_tables(positions.reshape(t, 1), jnp.asarray(freq), tm_proj)

    tabs = _dft_tables(seq, batch)
    tril = jnp.asarray(np.tril(np.ones((SSM_CHUNK, SSM_CHUNK), np.float32)))

    x2d = x.reshape(t, d)
    for i in range(depth):
        wall = _pack_inproj(w_in[i])
        wq, wqs, wk, wv = _pack_mla(mla_w_uq[i], mla_w_ukv[i])
        q, k, v, hy_u, z, xbc_raw, dt_raw = inproj(x2d, _row(norm_mix[i]), wall, _row(mla_q_norm[i]), wq, wqs,
                                                    _row(mla_kv_norm[i]), wk, wv, cos_t, sin_t, tm_proj)
        o_mla = attention(q.reshape(batch, seq, -1), k.reshape(batch, seq, -1), v, min(TQ_ATTN, seq),
                          min(TK_ATTN, seq))

        kspec = _hyena_kspec(seq, tabs, hy_filt_w1[i], hy_filt_b1[i], hy_filt_freq[i], hy_filt_w2[i], hy_filt_b2[i],
                             hy_filt_w3[i], hy_decay[i])
        o_hy = _hyena(hy_u.reshape(batch, seq, -1), tabs, kspec, hy_conv_w[i], hy_conv_b[i], hy_bias[i])

        xbc = dwconv(xbc_raw.reshape(batch, seq, -1), ssm_conv_w[i], _row(ssm_conv_b[i]), 0, SSM_CONV_DIM, True,
                     ROWS_CONV)
        dtbias_row = _pad_cols(_row(ssm_dt_bias[i]), LANES)
        a_row = _pad_cols(_row(-jnp.exp(ssm_a_log[i].astype(F32))), LANES)
        y_f, y_b = ssd_scan(xbc, dt_raw.reshape(batch, seq, -1), dtbias_row, a_row, tril)

        dsk = _row(jnp.repeat(ssm_d[i].astype(F32), SSM_HEADDIM))
        x1, hffn, aff = outproj(o_mla.reshape(t, -1), o_hy.reshape(t, -1), y_f.reshape(t, -1), y_b.reshape(t, -1),
                                xbc.reshape(t, -1), z, x2d, _row(mla_out_norm[i]), _row(hy_out_norm[i]), dsk,
                                _row(ssm_norm[i]), w_out[i].astype(BF16), _row(norm_ffn[i]),
                                _stack3_lhs(moe_router[i].astype(F32).T), batch, tm_out)

        pos, gsel, rowoff = moe_select(aff, cap)
        offs_g = jnp.concatenate([rowoff[:, :, ::tb // LANES] // SUBLANES,
                                  jnp.full((batch, N_EXPERTS, 1), pl.cdiv(cap, SUBLANES), I32)], axis=-1)
        offs_c = rowoff[:, :, ::tbc // LANES] // COMBINE_ALIGN
        xe, gslot = moe_gather(hffn.reshape(batch, seq, d), pos.reshape(batch, N_EXPERTS, 1, seq),
                               gsel.reshape(batch, N_EXPERTS, 1, seq), offs_g, cap, tb)
        ye = moe_ffn(xe, gslot, moe_w_gate, moe_w_up, moe_w_down, i, cap, cap_pad, TF_FFN)
        x3 = moe_combine(ye, pos, offs_c, x1.reshape(batch, seq, d), p, i, _row(ple_norm[i]),
                         ple_gate_w[i].astype(BF16), ple_proj[i].astype(BF16), _row(final_norm_g), i == depth - 1, tbc)
        x2d = x3.reshape(t, d)
    return x2d.reshape(batch, seq, d)
```

```python
import functools
import math

import numpy as np
import jax
import jax.numpy as jnp
from jax import lax
from jax.experimental import pallas as pl
from jax.experimental.pallas import tpu as pltpu

F32 = jnp.float32
BF16 = jnp.bfloat16
I32 = jnp.int32
HIGHEST = lax.Precision.HIGHEST

EPS = 1e-6
LANES = 128
SUBLANES = 8
COMBINE_ALIGN = 16
VMEM_LIMIT = 56 * 1024 * 1024

D_MODEL = 1024
MLA_HEADS = 8
MLA_NOPE = 64
MLA_ROPE = 32
MLA_V = 64
MLA_Q_LORA = 256
MLA_KV_LORA = 128
HEAD_PAD = 128
ATTN_SUM_ROWS = 16
ROPE_THETA = 10000.0
HY_WIDTH = 256
HY_ORDER = 2
HY_SHORT = 3
HY_BANDS = 8
HY_FILT_HID = 64
SSM_WIDTH = 256
SSM_HEADDIM = 64
SSM_HEADS = 4
SSM_GROUPS = 2
SSM_STATE = 128
SSM_CONV = 5
SSM_CHUNK = 128
SSM_CONV_DIM = 768
N_EXPERTS = 16
EXPERT_FF = 2048
EC_CAPACITY_FACTOR = 2
PLE_DIM = 256
FFT_N2 = 128
MANTISSA_STEPS = 40
IN_SPLITS = (MLA_Q_LORA, MLA_KV_LORA, MLA_ROPE, 3 * HY_WIDTH, SSM_WIDTH, SSM_CONV_DIM, 2 * SSM_HEADS)


def _cparams(sem, vmem=None):
    return pltpu.CompilerParams(dimension_semantics=sem, vmem_limit_bytes=vmem or VMEM_LIMIT)


def _rms(x, g):
    ms = jnp.mean(x * x, axis=-1, keepdims=True)
    return x * lax.rsqrt(ms + EPS) * g


def _dot(a, b, precision=None):
    return jnp.dot(a, b, preferred_element_type=F32, precision=precision)


def _dot_nt(a, b, precision=None):
    return lax.dot_general(a, b, (((1,), (1,)), ((), ())), preferred_element_type=F32, precision=precision)


def _dot_tn(a, b, precision=None):
    return lax.dot_general(a, b, (((0,), (0,)), ((), ())), preferred_element_type=F32, precision=precision)


def _split_bf16(x):
    hi = x.astype(BF16)
    return hi, (x - hi.astype(F32)).astype(BF16)


def _dot3(a, b):
    ah, al = _split_bf16(a)
    bh, bl = _split_bf16(b)
    return _dot(jnp.concatenate([ah, ah, al], axis=1), jnp.concatenate([bh, bl, bh], axis=0))


def _full(shape):
    n = len(shape)
    return pl.BlockSpec(shape, lambda *_: (0,) * n)


def _rope_kernel(pos_ref, freq_ref, cos_ref, sin_ref):
    ang = pos_ref[...].astype(F32) * freq_ref[...]
    cos_ref[...] = jnp.cos(ang)
    sin_ref[...] = jnp.sin(ang)


def rope_tables(pos_col, freq_row, tm):
    t = pos_col.shape[0]
    return pl.pallas_call(
        _rope_kernel,
        grid=(t // tm,),
        in_specs=[pl.BlockSpec((tm, 1), lambda i: (i, 0)), _full((1, HEAD_PAD))],
        out_specs=[pl.BlockSpec((tm, HEAD_PAD), lambda i: (i, 0))] * 2,
        out_shape=[jax.ShapeDtypeStruct((t, HEAD_PAD), F32)] * 2,
        compiler_params=_cparams(("parallel",)),
        name="rope_tables",
    )(pos_col, freq_row)


_C_CQ = 0
_C_CKV = 256
_C_KR = 384
_C_KRS = 512
_C_HY = 640
_C_Z = 1408
_C_XBC = 1664
_C_DT = 2432
_C_END = 2560


def _inproj_kernel(x_ref, gmix_ref, wall_ref, qn_ref, wq_ref, wqs_ref, kvn_ref, wk_ref, wv_ref, cos_ref, sin_ref,
                   q_ref, k_ref, v_ref, hy_ref, z_ref, xbc_ref, dt_ref, *, scale):
    h = _rms(x_ref[...], gmix_ref[...]).astype(BF16)
    proj = _dot(h, wall_ref[...])
    hy_ref[...] = proj[:, _C_HY:_C_Z]
    z_ref[...] = proj[:, _C_Z:_C_XBC]
    xbc_ref[...] = proj[:, _C_XBC:_C_DT]
    dt_ref[...] = proj[:, _C_DT:_C_END]
    cos = cos_ref[...]
    sin = sin_ref[...]
    cos8 = jnp.concatenate([cos] * MLA_HEADS, axis=-1)
    sin8 = jnp.concatenate([sin] * MLA_HEADS, axis=-1)
    cqn = _rms(proj[:, _C_CQ:_C_CKV], qn_ref[...]).astype(BF16)
    q = _dot(cqn, wq_ref[...])
    qs = _dot(cqn, wqs_ref[...])
    q_ref[...] = ((q * cos8 + qs * sin8) * scale).astype(BF16)
    ckvn = _rms(proj[:, _C_CKV:_C_KR], kvn_ref[...]).astype(BF16)
    kn = _dot(ckvn, wk_ref[...])
    v_ref[...] = _dot_nt(wv_ref[...], ckvn).astype(BF16)
    kr = proj[:, _C_KR:_C_KRS] * cos + proj[:, _C_KRS:_C_HY] * sin
    k_ref[...] = (kn + jnp.concatenate([kr] * MLA_HEADS, axis=-1)).astype(BF16)


def inproj(x2d, gmix, wall, qn, wq, wqs, kvn, wk, wv, cos_t, sin_t, tm):
    t = x2d.shape[0]
    hq = MLA_HEADS * HEAD_PAD
    row = lambda w: pl.BlockSpec((tm, w), lambda i: (i, 0))
    outs = [(hq, BF16), (hq, BF16), None, (3 * HY_WIDTH, F32), (SSM_WIDTH, F32), (SSM_CONV_DIM, F32), (LANES, F32)]
    hv = MLA_HEADS * MLA_V
    return pl.pallas_call(
        functools.partial(_inproj_kernel, scale=(MLA_NOPE + MLA_ROPE) ** -0.5 * math.log2(math.e)),
        grid=(t // tm,),
        in_specs=[row(D_MODEL), _full(gmix.shape), _full(wall.shape), _full(qn.shape), _full(wq.shape),
                  _full(wqs.shape), _full(kvn.shape), _full(wk.shape), _full(wv.shape), row(HEAD_PAD), row(HEAD_PAD)],
        out_specs=[row(o[0]) if o else pl.BlockSpec((hv, tm), lambda i: (0, i)) for o in outs],
        out_shape=[jax.ShapeDtypeStruct((t, o[0]), o[1]) if o else jax.ShapeDtypeStruct((hv, t), BF16)
                   for o in outs],
        compiler_params=_cparams(("parallel",)),
        name="inproj",
    )(x2d, gmix, wall, qn, wq, wqs, kvn, wk, wv, cos_t, sin_t)


def _attn_kernel(q_ref, k_ref, vt_ref, o_ref, st_ref, *, tk):
    seq = k_ref.shape[1]
    tq = q_ref.shape[1]
    nh = st_ref.shape[0]
    npairs = seq // (2 * tk)

    def scores(hh, c, slot):
        off = pl.multiple_of(c * tk, tk)
        st = _dot_nt(k_ref[0, pl.ds(off, tk), hh * HEAD_PAD:(hh + 1) * HEAD_PAD],
                     q_ref[0, :, hh * HEAD_PAD:(hh + 1) * HEAD_PAD])
        st_ref[hh, slot] = st
        return jnp.max(st, axis=0, keepdims=True)

    ones_rows = jnp.ones((ATTN_SUM_ROWS, tk), BF16)

    def update(hh, c, slot, m, acc, smax):
        off = pl.multiple_of(c * tk, tk)
        vtc = jnp.concatenate([vt_ref[hh * MLA_V:(hh + 1) * MLA_V, pl.ds(off, tk)], ones_rows], axis=0)
        m_new = jnp.maximum(m, smax)
        alpha = jnp.exp2(m - m_new)
        p = jnp.exp2(st_ref[hh, slot] - m_new)
        acc = acc * alpha + _dot(vtc, p.astype(BF16))
        return m_new, acc

    def pair(i, carry, last):
        new = []
        for hh in range(nh):
            m, acc, smax0 = carry[hh]
            smax1 = scores(hh, 2 * i + 1, 1)
            m, acc = update(hh, 2 * i, 0, m, acc, smax0)
            smax0 = smax1 if last else scores(hh, 2 * i + 2, 0)
            m, acc = update(hh, 2 * i + 1, 1, m, acc, smax1)
            new.append((m, acc, smax0))
        return tuple(new)

    init = tuple((jnp.full((1, tq), -jnp.inf, F32), jnp.zeros((MLA_V + ATTN_SUM_ROWS, tq), F32), scores(hh, 0, 0))
                 for hh in range(nh))
    carry = lax.fori_loop(0, npairs - 1, functools.partial(pair, last=False), init)
    final = pair(npairs - 1, carry, True)
    o_ref[0] = jnp.concatenate([jnp.transpose(acc[:MLA_V] / acc[MLA_V:MLA_V + 1]) for _, acc, _ in final], axis=-1)


def attention(q, k, vt, tq, tk):
    b, seq, _ = q.shape
    return pl.pallas_call(
        functools.partial(_attn_kernel, tk=tk),
        grid=(b, MLA_HEADS // 2, seq // tq),
        in_specs=[pl.BlockSpec((1, tq, 2 * HEAD_PAD), lambda bi, hp, qi: (bi, qi, hp)),
                  pl.BlockSpec((1, seq, 2 * HEAD_PAD), lambda bi, hp, qi: (bi, 0, hp)),
                  pl.BlockSpec((2 * MLA_V, seq), lambda bi, hp, qi: (hp, bi))],
        out_specs=pl.BlockSpec((1, tq, 2 * MLA_V), lambda bi, hp, qi: (bi, qi, hp)),
        out_shape=jax.ShapeDtypeStruct((b, seq, MLA_HEADS * MLA_V), F32),
        scratch_shapes=[pltpu.VMEM((2, 2, tk, tq), F32)],
        compiler_params=_cparams(("parallel", "parallel", "parallel")),
        name="attention",
    )(q, k, vt)


def _dwconv_kernel(x_ref, w_ref, b_ref, o_ref, *, width, act, rows):
    seq = x_ref.shape[1]
    pad = width // 2
    nchunks = seq // rows
    w = w_ref[...]
    bias = b_ref[...]

    def body(c, carry):
        r0 = pl.multiple_of(c * rows, rows)
        cur = x_ref[0, pl.ds(r0, rows), :]
        p0 = pl.multiple_of(jnp.maximum(r0 - SUBLANES, 0), SUBLANES)
        n0 = pl.multiple_of(jnp.minimum(r0 + rows, seq - SUBLANES), SUBLANES)
        prev = jnp.where(c > 0, x_ref[0, pl.ds(p0, SUBLANES), :], 0.0)
        nxt = jnp.where(c < nchunks - 1, x_ref[0, pl.ds(n0, SUBLANES), :], 0.0)
        ext = jnp.concatenate([prev, cur, nxt], axis=0)
        acc = bias + ext[SUBLANES - pad:SUBLANES - pad + rows] * w[0:1]
        for kk in range(1, width):
            s0 = SUBLANES - pad + kk
            acc = acc + ext[s0:s0 + rows] * w[kk:kk + 1]
        if act:
            acc = acc * jax.nn.sigmoid(acc)
        o_ref[0, pl.ds(r0, rows), :] = acc
        return carry

    lax.fori_loop(0, nchunks, body, 0)


def dwconv(x, w, bias, col0, ncols, act, rows):
    b, seq, _ = x.shape
    width = w.shape[0]
    cb0 = col0 // LANES
    return pl.pallas_call(
        functools.partial(_dwconv_kernel, width=width, act=act, rows=min(rows, seq)),
        grid=(b, ncols // LANES),
        in_specs=[pl.BlockSpec((1, seq, LANES), lambda bi, ci: (bi, 0, ci + cb0)),
                  pl.BlockSpec((width, LANES), lambda bi, ci: (0, ci)),
                  pl.BlockSpec((1, LANES), lambda bi, ci: (0, ci))],
        out_specs=pl.BlockSpec((1, seq, LANES), lambda bi, ci: (bi, 0, ci)),
        out_shape=jax.ShapeDtypeStruct((b, seq, ncols), F32),
        compiler_params=_cparams(("parallel", "parallel")),
        name="dwconv",
    )(x, w, bias)


def _hyfilt_kernel(mult_ref, w1_ref, b1_ref, fr_ref, w2_ref, b2_ref, w3_ref, dec_ref, bwd_ref, o_ref, *, seq, rows):
    i = pl.program_id(0)
    ridx = lax.broadcasted_iota(I32, (rows, LANES), 0) + i * rows
    lane = lax.broadcasted_iota(I32, (rows, LANES), 1)
    t = ridx.astype(F32) / seq
    ang = t * mult_ref[...]
    feats = jnp.where(lane == 0, t, jnp.where(lane <= HY_BANDS, jnp.sin(ang), jnp.cos(ang)))
    feats = jnp.where(lane < 1 + 2 * HY_BANDS, feats, 0.0)
    fr = fr_ref[...]
    hdn = jnp.sin(fr * (_dot(feats, w1_ref[...], HIGHEST) + b1_ref[...]))
    hdn = jnp.sin(fr * (_dot(hdn, w2_ref[...], HIGHEST) + b2_ref[...]))
    hh, hl = _split_bf16(hdn)
    wh, wl = _split_bf16(w3_ref[...])
    filt = (_dot(hh, wh) + _dot(hl, wh)) + _dot(hh, wl)
    window = jnp.exp(-t[:, 0:1] * jnp.abs(dec_ref[...]))
    out = filt * window
    keep = jnp.logical_or(ridx[:, 0:1] > 0, bwd_ref[...] < 0.5)
    o_ref[...] = jnp.where(keep, out, 0.0)


def hyena_filters(seq, mult, w1p, b1, fr, w2, b2, w3, dec, bwd_mask, rows):
    ncol = w3.shape[1]
    rows = min(rows, seq)
    args = (mult, w1p, b1, fr, w2, b2, w3, dec, bwd_mask)
    return pl.pallas_call(
        functools.partial(_hyfilt_kernel, seq=seq, rows=rows),
        grid=(seq // rows,),
        in_specs=[_full(a.shape) for a in args],
        out_specs=pl.BlockSpec((rows, ncol), lambda i: (i, 0)),
        out_shape=jax.ShapeDtypeStruct((seq, ncol), F32),
        compiler_params=_cparams(("parallel",)),
        name="hyena_filters",
    )(*args)


def _leftmm_kernel(m_ref, x_ref, o_ref):
    o_ref[...] = _dot3(m_ref[...], x_ref[...])


def leftmm(m, x2d, cb):
    r, kdim = m.shape
    n = x2d.shape[1]
    cb = min(cb, n)
    return pl.pallas_call(
        _leftmm_kernel,
        grid=(n // cb,),
        in_specs=[_full((r, kdim)), pl.BlockSpec((kdim, cb), lambda i: (0, i))],
        out_specs=pl.BlockSpec((r, cb), lambda i: (0, i)),
        out_shape=jax.ShapeDtypeStruct((r, n), F32),
        compiler_params=_cparams(("parallel",)),
        name="dft_outer",
    )(m, x2d)


def _stack3_lhs(f):
    hi, lo = _split_bf16(f)
    return jnp.concatenate([hi, hi, lo], axis=1)


def _stack3_rhs(a):
    hi, lo = _split_bf16(a)
    return jnp.concatenate([hi, lo, hi], axis=0)


def _twiddle(ar, ai, twr, twi, conj):
    if conj:
        return ar * twr + ai * twi, ai * twr - ar * twi
    return ar * twr - ai * twi, ai * twr + ar * twi


def _specfilt_kernel(a_ref, fblk_ref, twr_ref, twi_ref, o_ref, lhs_ref):
    n2 = FFT_N2
    c = a_ref.shape[-1] // 2

    @pl.when((pl.program_id(0) == 0) & (pl.program_id(1) == 0))
    def _():
        lhs_ref[...] = _stack3_lhs(fblk_ref[...])

    for kk in range(a_ref.shape[1]):
        br, bi = _twiddle(a_ref[0, kk], a_ref[1, kk], twr_ref[kk], twi_ref[kk], False)
        x = _dot(lhs_ref[...], _stack3_rhs(jnp.concatenate([br, bi], axis=0)))
        o_ref[0, 0, kk] = x[:n2, :c] + x[:n2, c:]
        o_ref[0, 1, kk] = x[n2:, :c] - x[n2:, c:]


def filter_spectrum(a4, fblk, twr_col, twi_col, k1s):
    _, n1, n2, ctot = a4.shape
    c = HY_WIDTH
    return pl.pallas_call(
        _specfilt_kernel,
        grid=(HY_ORDER, n1 // k1s),
        in_specs=[pl.BlockSpec((2, k1s, n2, 2 * c), lambda o, k: (0, k, 0, o)),
                  _full(fblk.shape),
                  pl.BlockSpec((k1s, n2, 1), lambda o, k: (k, 0, 0)),
                  pl.BlockSpec((k1s, n2, 1), lambda o, k: (k, 0, 0))],
        out_specs=pl.BlockSpec((1, 2, k1s, n2, c), lambda o, k: (o, 0, k, 0, 0)),
        out_shape=jax.ShapeDtypeStruct((HY_ORDER, 2, n1, n2, c), F32),
        scratch_shapes=[pltpu.VMEM((2 * n2, 6 * n2), BF16)],
        compiler_params=_cparams(("arbitrary", "arbitrary")),
        name="filter_spectrum",
    )(a4, fblk, twr_col, twi_col)


def _specmul_kernel(a_ref, k_ref, fblk_ref, fblk_t_ref, twr_ref, twi_ref, o_ref, lhs_ref):
    n2 = FFT_N2

    @pl.when(pl.program_id(0) == 0)
    def _():
        lhs_ref[0] = _stack3_lhs(fblk_ref[...])
        lhs_ref[1] = _stack3_lhs(fblk_t_ref[...])

    for kk in range(a_ref.shape[1]):
        twr, twi = twr_ref[kk], twi_ref[kk]
        br, bi = _twiddle(a_ref[0, kk], a_ref[1, kk], twr, twi, False)
        x = _dot(lhs_ref[0], _stack3_rhs(jnp.concatenate([br, bi], axis=0)))
        xr, xi = x[:n2], x[n2:]
        kr, ki = k_ref[0, 0, kk], k_ref[0, 1, kk]
        p = jnp.concatenate([xr * kr - xi * ki, xr * ki + xi * kr], axis=0)
        y = _dot(lhs_ref[1], _stack3_rhs(p))
        yr, yi = _twiddle(y[:n2], y[n2:], twr, twi, True)
        o_ref[0, kk] = yr
        o_ref[1, kk] = yi


def spectrum_multiply(a4, kspec, order, fblk, fblk_t, twr_col, twi_col, k1s):
    _, n1, n2, c = a4.shape
    return pl.pallas_call(
        _specmul_kernel,
        grid=(n1 // k1s,),
        in_specs=[pl.BlockSpec((2, k1s, n2, c), lambda k: (0, k, 0, 0)),
                  pl.BlockSpec((1, 2, k1s, n2, c), lambda k: (order, 0, k, 0, 0)),
                  _full(fblk.shape), _full(fblk_t.shape),
                  pl.BlockSpec((k1s, n2, 1), lambda k: (k, 0, 0)),
                  pl.BlockSpec((k1s, n2, 1), lambda k: (k, 0, 0))],
        out_specs=pl.BlockSpec((2, k1s, n2, c), lambda k: (0, k, 0, 0)),
        out_shape=jax.ShapeDtypeStruct((2, n1, n2, c), F32),
        scratch_shapes=[pltpu.VMEM((2, 2 * n2, 6 * n2), BF16)],
        compiler_params=_cparams(("arbitrary",)),
        name="spectrum_multiply",
    )(a4, kspec, fblk, fblk_t, twr_col, twi_col)


def _convout_kernel(m_ref, y_ref, v_ref, g_ref, bias_ref, o_ref, *, reps):
    y = _dot3(m_ref[...], y_ref[...])
    bias = jnp.concatenate([bias_ref[...]] * reps, axis=-1)
    v = v_ref[...]
    o_ref[...] = (y + v * bias) * g_ref[...]


def conv_output(m3, y2d, v2d, g2d, bias_row, cb):
    r, kdim = m3.shape
    n = y2d.shape[1]
    cb = min(cb, n)
    c = bias_row.shape[1]
    return pl.pallas_call(
        functools.partial(_convout_kernel, reps=cb // c),
        grid=(n // cb,),
        in_specs=[_full((r, kdim)), pl.BlockSpec((kdim, cb), lambda i: (0, i)),
                  pl.BlockSpec((r, cb), lambda i: (0, i)), pl.BlockSpec((r, cb), lambda i: (0, i)),
                  _full((1, c))],
        out_specs=pl.BlockSpec((r, cb), lambda i: (0, i)),
        out_shape=jax.ShapeDtypeStruct((r, n), F32),
        compiler_params=_cparams(("parallel",)),
        name="dft_outer_inverse",
    )(m3, y2d, v2d, g2d, bias_row)


def _dft_tables(seq, batch):
    n2 = FFT_N2
    half = seq // n2
    n1 = 2 * half
    n = n1 * n2
    k1 = np.arange(n1, dtype=np.float64)[:, None]
    nn1 = np.arange(half, dtype=np.float64)[None, :]
    th = 2.0 * np.pi * k1 * nn1 / n1
    c1, s1 = np.cos(th), np.sin(th)
    assert batch == 2, "the two batch entries are packed as real / imaginary parts"
    m1 = np.block([[c1, s1], [-s1, c1]])
    m3 = np.block([[c1.T, -s1.T], [s1.T, c1.T]]) / n
    mk = np.concatenate([c1, -s1], axis=0)
    kk2 = np.arange(n2, dtype=np.float64)
    th2 = 2.0 * np.pi * np.outer(kk2, kk2) / n2
    c2, s2 = np.cos(th2), np.sin(th2)
    fblk = np.block([[c2, s2], [-s2, c2]])
    tht = 2.0 * np.pi * np.outer(np.arange(n1, dtype=np.float64), kk2) / n
    twr, twi = np.cos(tht), -np.sin(tht)
    f = lambda a: jnp.asarray(a, F32)
    return dict(m1=f(m1), m3=f(m3), mk=f(mk), fblk=f(fblk), fblk_t=f(fblk.T),
                twr_col=f(twr[:, :, None]), twi_col=f(twi[:, :, None]), n1=n1, half=half)


def _ssd_kernel(xf_ref, bf_ref, cf_ref, dtf_ref, xb_ref, bb_ref, cb_ref, dtb_ref, dtbias_ref, a_ref, tril_ref,
                yf_ref, yb_ref, state_ref):
    q = SSM_CHUNK
    hd = SSM_HEADDIM
    ns = SSM_STATE

    @pl.when(pl.program_id(1) == 0)
    def _():
        state_ref[...] = jnp.zeros_like(state_ref)

    tril = tril_ref[...]
    rows = lax.broadcasted_iota(I32, (q, q), 0)
    cols = lax.broadcasted_iota(I32, (q, q), 1)
    a_row = a_ref[...]
    bias = dtbias_ref[...]

    def direction(x_ref, b_ref, c_ref, dt_ref, y_ref, d):
        dt = jax.nn.softplus(dt_ref[0] + bias)
        dta = dt * a_row
        cs = _dot(tril, dta, HIGHEST)
        ecs = cs - dta
        base = ecs if d else cs
        base_t = jnp.transpose(base)
        total = cs[q - 1:q, :]
        x = x_ref[0]
        ys = []
        for g in range(SSM_GROUPS):
            bm = b_ref[0, :, g * ns:(g + 1) * ns]
            cm = c_ref[0, :, g * ns:(g + 1) * ns]
            cb = _dot_nt(cm.astype(BF16), bm.astype(BF16))
            for hh in range(SSM_HEADS // SSM_GROUPS):
                h = g * (SSM_HEADS // SSM_GROUPS) + hh
                j = d * SSM_HEADS + h
                col = jnp.broadcast_to(base[:, j:j + 1], (q, q))
                coln = col if ns == q else jnp.broadcast_to(base[:, j:j + 1], (q, ns))
                row = base_t[j:j + 1, :]
                tot = total[:, j:j + 1]
                if d == 0:
                    seg = jnp.where(rows >= cols, col - row, -jnp.inf)
                    c_scale = jnp.exp(coln)
                    b_scale = jnp.exp(tot - coln)
                else:
                    seg = jnp.where(cols >= rows, row - col, -jnp.inf)
                    c_scale = jnp.exp(tot - coln)
                    b_scale = jnp.exp(coln)
                scores = cb * jnp.exp(seg)
                xdt = (x[:, h * hd:(h + 1) * hd] * jnp.broadcast_to(dt[:, j:j + 1], (q, hd))).astype(BF16)
                st = state_ref[j]
                y = _dot(scores.astype(BF16), xdt) + _dot_nt((cm * c_scale).astype(BF16), st.astype(BF16))
                state_ref[j] = st * jnp.exp(tot) + _dot_tn(xdt, (bm * b_scale).astype(BF16))
                ys.append(y)
        y_ref[0] = jnp.concatenate(ys, axis=-1)

    direction(xf_ref, bf_ref, cf_ref, dtf_ref, yf_ref, 0)
    direction(xb_ref, bb_ref, cb_ref, dtb_ref, yb_ref, 1)


def ssd_scan(xbc, dt_raw, dtbias_row, a_row, tril):
    b, seq, _ = xbc.shape
    q = SSM_CHUNK
    nc = seq // q
    w = SSM_WIDTH
    fwd = lambda col: pl.BlockSpec((1, q, w), lambda bi, i: (bi, i, col))
    bwd = lambda col: pl.BlockSpec((1, q, w), lambda bi, i: (bi, nc - 1 - i, col))
    return pl.pallas_call(
        _ssd_kernel,
        grid=(b, nc),
        in_specs=[fwd(0), fwd(1), fwd(2), pl.BlockSpec((1, q, LANES), lambda bi, i: (bi, i, 0)),
                  bwd(0), bwd(1), bwd(2), pl.BlockSpec((1, q, LANES), lambda bi, i: (bi, nc - 1 - i, 0)),
                  _full((1, LANES)), _full((1, LANES)), _full((q, q))],
        out_specs=[pl.BlockSpec((1, q, w), lambda bi, i: (bi, i, 0)),
                   pl.BlockSpec((1, q, w), lambda bi, i: (bi, nc - 1 - i, 0))],
        out_shape=[jax.ShapeDtypeStruct((b, seq, w), F32)] * 2,
        scratch_shapes=[pltpu.VMEM((2 * SSM_HEADS, SSM_HEADDIM, SSM_STATE), F32)],
        compiler_params=_cparams(("parallel", "arbitrary")),
        name="ssd_scan",
    )(xbc, xbc, xbc, dt_raw, xbc, xbc, xbc, dt_raw, dtbias_row, a_row, tril)


def _outproj_kernel(om_ref, hy_ref, yf_ref, yb_ref, xs_ref, z_ref, x_ref, gm_ref, gh_ref, dsk_ref, gs_ref, wout_ref,
                    gffn_ref, rt_ref, x1_ref, hffn_ref, aff_ref):
    o1 = _rms(om_ref[...], gm_ref[...])
    o2 = _rms(hy_ref[...], gh_ref[...])
    z = z_ref[...]
    y = (yf_ref[...] + yb_ref[...] + xs_ref[...] * dsk_ref[...]) * (z * jax.nn.sigmoid(z))
    gw = SSM_WIDTH // SSM_GROUPS
    gs = gs_ref[...]
    o3 = jnp.concatenate([_rms(y[:, g * gw:(g + 1) * gw], gs[:, g * gw:(g + 1) * gw]) for g in range(SSM_GROUPS)],
                         axis=-1)
    mix = jnp.concatenate([o1, o2, o3], axis=-1).astype(BF16)
    x1 = x_ref[...] + _dot(mix, wout_ref[...])
    x1_ref[...] = x1
    hf = _rms(x1, gffn_ref[...])
    hi, lo = _split_bf16(hf)
    hffn_ref[...] = hi
    logits = _dot_nt(rt_ref[...], jnp.concatenate([hi, lo, hi], axis=1))
    mx = jnp.max(logits, axis=0, keepdims=True)
    ex = jnp.exp(logits - mx)
    aff_ref[0] = ex / jnp.sum(ex, axis=0, keepdims=True)


def outproj(om, hy, yf, yb, xbc, z, x2d, gm, gh, dsk, gs, wout, gffn, router_t, batch, tm):
    t = x2d.shape[0]
    seq = t // batch
    nb = seq // tm
    row = lambda w: pl.BlockSpec((tm, w), lambda i: (i, 0))
    return pl.pallas_call(
        _outproj_kernel,
        grid=(t // tm,),
        in_specs=[row(om.shape[1]), row(HY_WIDTH), row(SSM_WIDTH), row(SSM_WIDTH), row(SSM_WIDTH), row(SSM_WIDTH),
                  row(D_MODEL), _full(gm.shape), _full(gh.shape), _full(dsk.shape), _full(gs.shape),
                  _full(wout.shape), _full(gffn.shape), _full(router_t.shape)],
        out_specs=[row(D_MODEL), row(D_MODEL),
                   pl.BlockSpec((1, N_EXPERTS, tm), lambda i: (i // nb, 0, i % nb))],
        out_shape=[jax.ShapeDtypeStruct((t, D_MODEL), F32), jax.ShapeDtypeStruct((t, D_MODEL), BF16),
                   jax.ShapeDtypeStruct((batch, N_EXPERTS, seq), F32)],
        compiler_params=_cparams(("parallel",)),
        name="outproj_router",
    )(om, hy, yf, yb, xbc, z, x2d, gm, gh, dsk, gs, wout, gffn, router_t)


def _select_kernel(aff_ref, tri_ref, ones_ref, blk_ref, pos_ref, g_ref, off_ref, *, cap, nrows):
    aff = aff_ref[0]
    er = aff.shape[0]
    ne = er // nrows
    aff3 = aff.reshape(ne, nrows, LANES)
    capf = jnp.float32(cap)

    def count(mask3):
        return jnp.sum(jnp.where(mask3, 1.0, 0.0), axis=(1, 2), keepdims=True)

    def enough(cand):
        return count(aff3 >= cand) >= capf

    top = jnp.full((ne, 1, 1), 2.0, F32)
    for shift in (64, 32, 16, 8, 4, 2, 1):
        cand = top * (2.0 ** -shift)
        top = jnp.where(enough(cand), top, cand)
    p = top * 0.5

    def refine(_, carry):
        lo, step = carry
        cand = lo + step
        return jnp.where(enough(cand), cand, lo), step * 0.5

    lo, _ = lax.fori_loop(0, MANTISSA_STEPS, refine, (p, p * 0.5))
    thr = jnp.min(jnp.where(aff3 >= lo, aff3, jnp.inf), axis=(1, 2), keepdims=True)
    gt3 = aff3 > thr
    eq3 = aff3 == thr
    need = capf - count(gt3)

    tri = tri_ref[...]
    ones = ones_ref[...]
    blk = blk_ref[...]

    def prefix(maskf):
        mb = maskf.astype(BF16)
        within = _dot(mb, tri)
        rowtot = _dot(mb, ones)
        before = _dot(blk, rowtot.astype(BF16))
        return within + before, before

    eqf = jnp.where(eq3, 1.0, 0.0).reshape(er, LANES)
    tie_incl, _ = prefix(eqf)
    tie_rank = (tie_incl - eqf).reshape(ne, nrows, LANES)
    sel3 = jnp.logical_or(gt3, jnp.logical_and(eq3, tie_rank < need))
    self_ = jnp.where(sel3, 1.0, 0.0).reshape(er, LANES)
    incl, before = prefix(self_)
    sel = self_ > 0.5
    pos_ref[0] = jnp.where(sel, (incl - self_).astype(I32), -1)
    g_ref[0] = jnp.where(sel, aff, 0.0)
    off_ref[0] = before.astype(I32)


def moe_select(aff, cap):
    b, ne, seq = aff.shape
    nrows = seq // LANES
    er = ne * nrows
    tri = jnp.asarray(np.triu(np.ones((LANES, LANES), np.float32)), BF16)
    ones = jnp.ones((LANES, LANES), BF16)
    ridx = np.arange(er)
    blk = (ridx[:, None] // nrows == ridx[None, :] // nrows) & (ridx[None, :] < ridx[:, None])
    blk = jnp.asarray(blk.astype(np.float32), BF16)
    spec = pl.BlockSpec((1, er, LANES), lambda bi: (bi, 0, 0))
    pos, gsel, off = pl.pallas_call(
        functools.partial(_select_kernel, cap=cap, nrows=nrows),
        grid=(b,),
        in_specs=[spec, _full(tri.shape), _full(ones.shape), _full(blk.shape)],
        out_specs=[spec] * 3,
        out_shape=[jax.ShapeDtypeStruct((b, er, LANES), I32), jax.ShapeDtypeStruct((b, er, LANES), F32),
                   jax.ShapeDtypeStruct((b, er, LANES), I32)],
        compiler_params=_cparams(("parallel",)),
        name="moe_select",
    )(aff.reshape(b, er, LANES), tri, ones, blk)
    return pos.reshape(b, ne, seq), gsel.reshape(b, ne, seq), off[:, :, 0].reshape(b, ne, nrows)


def _gather_kernel(offs_ref, h_ref, pos_ref, g_ref, o_ref, og_ref, *, tb, cap):
    bi = pl.program_id(0)
    gi = pl.program_id(1)
    seq, d = h_ref.shape[1], h_ref.shape[2]
    ng = o_ref.shape[1]
    al = COMBINE_ALIGN
    for e in range(ng):
        o_ref[0, e, 0:al, :] = jnp.zeros((al, d), BF16)
        og_ref[0, e, 0:al, :] = jnp.zeros((al, 1), F32)
    wins = sorted({min(64, tb + al), min(128, tb + al), tb + al})

    def body(j, carry):
        t0 = pl.multiple_of(j * tb, tb)
        offs = [pl.multiple_of(offs_ref[bi, gi * ng + e, j] * al, al) for e in range(ng)]
        needs = [offs_ref[bi, gi * ng + e, j + 1] * al + al - offs[e] for e in range(ng)]
        need = functools.reduce(jnp.maximum, needs)

        def place(win):
            riota = lax.broadcasted_iota(I32, (win, tb), 0)
            hits = [riota + offs[e] == pos_ref[0, 0, e:e + 1, pl.ds(t0, tb)] for e in range(ng)]
            onehot = jnp.concatenate([jnp.where(hit, 1.0, 0.0).astype(BF16) for hit in hits], axis=0)
            rows = _dot(onehot, h_ref[0, pl.ds(t0, tb), :])
            for e in range(ng):
                r = rows[e * win:(e + 1) * win]
                gates = jnp.sum(jnp.where(hits[e], g_ref[0, 0, e:e + 1, pl.ds(t0, tb)], 0.0), axis=1, keepdims=True)
                head = o_ref[0, e, pl.ds(offs[e], al), :].astype(F32) + r[0:al]
                o_ref[0, e, pl.ds(offs[e], al), :] = head.astype(BF16)
                o_ref[0, e, pl.ds(offs[e] + al, win - al), :] = r[al:].astype(BF16)
                og_ref[0, e, pl.ds(offs[e], al), :] += gates[0:al]
                og_ref[0, e, pl.ds(offs[e] + al, win - al), :] = gates[al:]

        lo = 0
        for win in wins:
            fits = need <= win if win != wins[-1] else True
            pl.when(jnp.logical_and(need > lo, fits))(functools.partial(place, win))
            lo = win
        return carry

    lax.fori_loop(0, seq // tb, body, 0)
    tail = o_ref.shape[2] - cap
    o_ref[0, :, cap:, :] = jnp.zeros((ng, tail, d), BF16)
    og_ref[0, :, cap:, :] = jnp.zeros((ng, tail, 1), F32)


def moe_gather(hffn, pos, gsel, offs, cap, tb, ng):
    b, seq, d = hffn.shape
    ne = pos.shape[1]
    rows = cap + tb + COMBINE_ALIGN
    assert cap % COMBINE_ALIGN == 0 and ne % ng == 0
    row_spec = pl.BlockSpec((1, 1, ng, seq), lambda bi, gi, offs: (bi, gi, 0, 0))
    grid_spec = pltpu.PrefetchScalarGridSpec(
        num_scalar_prefetch=1,
        grid=(b, ne // ng),
        in_specs=[pl.BlockSpec((1, seq, d), lambda bi, gi, offs: (bi, 0, 0)), row_spec, row_spec],
        out_specs=[pl.BlockSpec((1, ng, rows, d), lambda bi, gi, offs: (bi, gi, 0, 0)),
                   pl.BlockSpec((1, ng, rows, 1), lambda bi, gi, offs: (bi, gi, 0, 0))],
    )
    return pl.pallas_call(
        functools.partial(_gather_kernel, tb=tb, cap=cap),
        grid_spec=grid_spec,
        out_shape=[jax.ShapeDtypeStruct((b, ne, rows, d), BF16), jax.ShapeDtypeStruct((b, ne, rows, 1), F32)],
        compiler_params=_cparams(("parallel", "arbitrary")),
        name="moe_gather",
    )(offs, hffn, pos.reshape(b, ne // ng, ng, seq), gsel.reshape(b, ne // ng, ng, seq))


def _ffn_kernel(xe_ref, gs_ref, wg_ref, wu_ref, wd_ref, o_ref, acc_ref, *, cap):
    f = pl.program_id(1)
    nb = xe_ref.shape[0]
    d = xe_ref.shape[-1]
    xe = xe_ref[...].reshape(nb * cap, d)
    a = _dot(xe, wg_ref[...].astype(BF16))
    u = _dot(xe, wu_ref[...].astype(BF16))
    hid = (a * jax.nn.sigmoid(a) * u).astype(BF16)
    part = _dot(hid, wd_ref[...].astype(BF16))

    @pl.when(f == 0)
    def _():
        acc_ref[...] = part

    @pl.when(f > 0)
    def _():
        acc_ref[...] += part

    @pl.when(f == pl.num_programs(1) - 1)
    def _():
        gated = acc_ref[...] * gs_ref[...].reshape(nb * cap, 1)
        o_ref[:, 0:cap, :] = gated.reshape(nb, cap, d).astype(BF16)
        o_ref[:, cap:, :] = jnp.zeros((nb, o_ref.shape[1] - cap, d), BF16)


def moe_ffn(xe, gslot, w_gate, w_up, w_down, layer, cap, cap_pad, tf):
    b, ne, _, d = xe.shape
    ff = w_gate.shape[-1]
    return pl.pallas_call(
        functools.partial(_ffn_kernel, cap=cap),
        grid=(ne, ff // tf),
        in_specs=[pl.BlockSpec((b, None, cap, d), lambda e, f: (0, e, 0, 0)),
                  pl.BlockSpec((b, None, cap, 1), lambda e, f: (0, e, 0, 0)),
                  pl.BlockSpec((None, None, d, tf), lambda e, f: (layer, e, 0, f)),
                  pl.BlockSpec((None, None, d, tf), lambda e, f: (layer, e, 0, f)),
                  pl.BlockSpec((None, None, tf, d), lambda e, f: (layer, e, f, 0))],
        out_specs=pl.BlockSpec((b, None, cap_pad, d), lambda e, f: (0, e, 0, 0)),
        out_shape=jax.ShapeDtypeStruct((b, ne, cap_pad, d), BF16),
        scratch_shapes=[pltpu.VMEM((b * cap, d), F32)],
        compiler_params=_cparams(("parallel", "arbitrary")),
        name="moe_ffn",
    )(xe, gslot, w_gate, w_up, w_down)


def _combine_kernel(offs_ref, ye_hbm, pos_ref, x1_ref, p_ref, gple_ref, wgate_ref, wproj_ref, gfin_ref, o_ref,
                    buf_ref, sem_ref, *, tb, win, final):
    bi = pl.program_id(0)
    j = pl.program_id(1)
    nj = pl.num_programs(1)
    ne = pos_ref.shape[1]
    step = bi * nj + j
    slot = step % 2

    def window_copy(b_, j_, e, slot_):
        off = pl.multiple_of(offs_ref[b_, e, j_] * COMBINE_ALIGN, COMBINE_ALIGN)
        return pltpu.make_async_copy(ye_hbm.at[b_, e, pl.ds(off, win), :],
                                     buf_ref.at[slot_, pl.ds(e * win, win), :], sem_ref.at[slot_, e])

    @pl.when(step == 0)
    def _():
        for e in range(ne):
            window_copy(bi, j, e, slot).start()

    @pl.when(step + 1 < pl.num_programs(0) * nj)
    def _():
        wrap = j + 1 == nj
        b_next = jnp.where(wrap, bi + 1, bi)
        j_next = jnp.where(wrap, 0, j + 1)
        for e in range(ne):
            window_copy(b_next, j_next, e, 1 - slot).start()

    def token_major(a):
        pad = jnp.zeros((tb - ne, tb), a.dtype)
        return jnp.transpose(jnp.concatenate([a, pad], axis=0))[:, :ne]

    pos_all = token_major(pos_ref[0])
    lane = lax.broadcasted_iota(I32, (tb, LANES), 1)
    targets = []
    for e in range(ne):
        col = pos_all[:, e:e + 1]
        delta = e * win - offs_ref[bi, e, j] * COMBINE_ALIGN
        targets.append(jnp.broadcast_to(jnp.where(col >= 0, col + delta, -1), (tb, LANES)))
    tiles = []
    for k in range(ne * win // LANES):
        hits = [jnp.where(targets[e] == lane + k * LANES, 1.0, 0.0) for e in range(ne)
                if e * win < (k + 1) * LANES and (e + 1) * win > k * LANES]
        tiles.append(functools.reduce(lambda a, b: a + b, hits).astype(BF16))
    sel = jnp.concatenate(tiles, axis=1)
    for e in range(ne):
        window_copy(bi, j, e, slot).wait()
    x2 = x1_ref[0] + _dot(sel, buf_ref[slot])
    hp = _rms(x2, gple_ref[...]).astype(BF16)
    gt = jax.nn.sigmoid(_dot(hp, wgate_ref[...]))
    x3 = x2 + _dot(p_ref[0].astype(BF16), wproj_ref[...]) * gt
    o_ref[0] = _rms(x3, gfin_ref[...]) if final else x3


def moe_combine(ye, pos, offs, x1, p, layer, gple, wgate, wproj, gfin, final, tb):
    b, ne, _, d = ye.shape
    seq = x1.shape[1]
    win = tb + COMBINE_ALIGN
    assert (ne * win) % LANES == 0
    grid_spec = pltpu.PrefetchScalarGridSpec(
        num_scalar_prefetch=1,
        grid=(b, seq // tb),
        in_specs=[pl.BlockSpec(memory_space=pl.ANY),
                  pl.BlockSpec((1, ne, tb), lambda bi, j, offs: (bi, 0, j)),
                  pl.BlockSpec((1, tb, d), lambda bi, j, offs: (bi, j, 0)),
                  pl.BlockSpec((None, 1, tb, p.shape[-1]), lambda bi, j, offs: (layer, bi, j, 0)),
                  pl.BlockSpec(gple.shape, lambda bi, j, offs: (0, 0)),
                  pl.BlockSpec(wgate.shape, lambda bi, j, offs: (0, 0)),
                  pl.BlockSpec(wproj.shape, lambda bi, j, offs: (0, 0)),
                  pl.BlockSpec(gfin.shape, lambda bi, j, offs: (0, 0))],
        out_specs=pl.BlockSpec((1, tb, d), lambda bi, j, offs: (bi, j, 0)),
        scratch_shapes=[pltpu.VMEM((2, ne * win, d), BF16), pltpu.SemaphoreType.DMA((2, ne))],
    )
    return pl.pallas_call(
        functools.partial(_combine_kernel, tb=tb, win=win, final=final),
        grid_spec=grid_spec,
        out_shape=jax.ShapeDtypeStruct((b, seq, d), F32),
        compiler_params=_cparams(("arbitrary", "arbitrary")),
        name="moe_combine_ple",
    )(offs, ye, pos, x1, p, gple, wgate, wproj, gfin)


def _pad_cols(a, width):
    return jnp.pad(a, ((0, 0), (0, width - a.shape[1])))


def _pack_inproj(w_in):
    offs = np.cumsum((0,) + IN_SPLITS)
    cq, ckv, kr, hy, z, xbc, dt = [w_in[:, offs[i]:offs[i + 1]] for i in range(len(IN_SPLITS))]
    d = w_in.shape[0]
    half = MLA_ROPE // 2
    zeros = lambda n: jnp.zeros((d, n), w_in.dtype)
    kr_pad = jnp.concatenate([zeros(MLA_NOPE), kr, zeros(HEAD_PAD - MLA_NOPE - MLA_ROPE)], axis=1)
    kr_swap = jnp.concatenate([zeros(MLA_NOPE), -kr[:, half:], kr[:, :half], zeros(HEAD_PAD - MLA_NOPE - MLA_ROPE)],
                              axis=1)
    wall = jnp.concatenate([cq, ckv, kr_pad, kr_swap, hy, z, xbc, _pad_cols(dt, LANES)], axis=1)
    assert wall.shape[1] == _C_END
    return wall.astype(BF16)


def _pack_mla(w_uq, w_ukv):
    lq = w_uq.shape[0]
    lkv = w_ukv.shape[0]
    half = MLA_ROPE // 2
    padw = HEAD_PAD - MLA_NOPE - MLA_ROPE
    q3 = w_uq.reshape(lq, MLA_HEADS, MLA_NOPE + MLA_ROPE)
    nope, rope = q3[..., :MLA_NOPE], q3[..., MLA_NOPE:]
    zq = jnp.zeros((lq, MLA_HEADS, padw), w_uq.dtype)
    wq = jnp.concatenate([nope, rope, zq], axis=-1).reshape(lq, MLA_HEADS * HEAD_PAD)
    wqs = jnp.concatenate([jnp.zeros_like(nope), -rope[..., half:], rope[..., :half], zq], axis=-1)
    wqs = wqs.reshape(lq, MLA_HEADS * HEAD_PAD)
    kv3 = w_ukv.reshape(lkv, MLA_HEADS, MLA_NOPE + MLA_V)
    knope, vv = kv3[..., :MLA_NOPE], kv3[..., MLA_NOPE:]
    wk = jnp.concatenate([knope, jnp.zeros((lkv, MLA_HEADS, HEAD_PAD - MLA_NOPE), w_ukv.dtype)], axis=-1)
    wk = wk.reshape(lkv, MLA_HEADS * HEAD_PAD)
    wv = vv.reshape(lkv, MLA_HEADS * MLA_V).T
    return wq.astype(BF16), wqs.astype(BF16), wk.astype(BF16), wv.astype(BF16)


def _row(a):
    return a.reshape(1, -1).astype(F32)


TM_PROJ = 512
TQ_ATTN = 1024
TK_ATTN = 512
ROWS_CONV = 512
ROWS_FILT = 512
CB_DFT = 2048
TM_OUT = 512
K1_PER_STEP = 4
TB_MOE = 256
TB_COMBINE = 128
GATHER_GROUP = 2
TF_FFN = 512


def _hyena(hy_u, tabs, kspec, conv_w, conv_b, bias):
    b, seq, _ = hy_u.shape
    c = HY_WIDTH
    parts = [dwconv(hy_u, conv_w[:, i * c:(i + 1) * c], _row(conv_b[i * c:(i + 1) * c]), i * c, c, False, ROWS_CONV)
             for i in range(HY_ORDER + 1)]
    gates, v = parts[:-1], parts[-1]
    half, n1 = tabs["half"], tabs["n1"]
    flat = lambda a: a.reshape(b * half, FFT_N2 * c)
    for o in range(HY_ORDER):
        a = leftmm(tabs["m1"], flat(v), CB_DFT).reshape(2, n1, FFT_N2, c)
        y = spectrum_multiply(a, kspec, o, tabs["fblk"], tabs["fblk_t"], tabs["twr_col"], tabs["twi_col"],
                              K1_PER_STEP)
        v = conv_output(tabs["m3"], y.reshape(2 * n1, FFT_N2 * c), flat(v), flat(gates[o]), _row(bias[o]), CB_DFT)
        v = v.reshape(b, seq, c)
    return v


def _hyena_kspec(seq, tabs, w1, b1, freq, w2, b2, w3, decay):
    bands = np.arange(1, HY_BANDS + 1, dtype=np.float64) * 2.0 * np.pi
    mult = np.zeros((1, LANES), np.float32)
    mult[0, 1:1 + HY_BANDS] = bands
    mult[0, 1 + HY_BANDS:1 + 2 * HY_BANDS] = bands
    w1p = jnp.pad(w1.astype(F32), ((0, LANES - w1.shape[0]), (0, 0)))
    ncol = HY_ORDER * 2 * HY_WIDTH
    bwd = (np.arange(ncol) // HY_WIDTH) % 2
    kf = hyena_filters(seq, jnp.asarray(mult), w1p, _row(b1), _row(freq), w2.astype(F32), _row(b2), w3.astype(F32),
                       _row(decay), jnp.asarray(bwd.astype(np.float32)).reshape(1, ncol), ROWS_FILT)
    a = leftmm(tabs["mk"], kf.reshape(tabs["half"], FFT_N2 * ncol), CB_DFT)
    a4 = a.reshape(2, tabs["n1"], FFT_N2, ncol)
    return filter_spectrum(a4, tabs["fblk"], tabs["twr_col"], tabs["twi_col"], K1_PER_STEP)


def kernel(x, p, positions, norm_mix, w_in, mla_q_norm, mla_w_uq, mla_kv_norm, mla_w_ukv, mla_out_norm, hy_conv_w,
           hy_conv_b, hy_filt_w1, hy_filt_b1, hy_filt_freq, hy_filt_w2, hy_filt_b2, hy_filt_w3, hy_decay, hy_bias,
           hy_out_norm, ssm_conv_w, ssm_conv_b, ssm_dt_bias, ssm_a_log, ssm_d, ssm_norm, w_out, norm_ffn, moe_router,
           moe_w_gate, moe_w_up, moe_w_down, ple_norm, ple_gate_w, ple_proj, final_norm_g):
    batch, seq, d = x.shape
    depth = w_in.shape[0]
    t = batch * seq
    cap = EC_CAPACITY_FACTOR * seq // N_EXPERTS
    tb = min(TB_MOE, seq)
    tbc = min(TB_COMBINE, seq)
    cap_pad = cap + tb
    tm_proj = min(TM_PROJ, seq)
    tm_out = min(TM_OUT, seq)

    freq = np.zeros((1, HEAD_PAD), np.float32)
    inv = ROPE_THETA ** (-np.arange(0, MLA_ROPE, 2, dtype=np.float32) / MLA_ROPE)
    freq[0, MLA_NOPE:MLA_NOPE + MLA_ROPE // 2] = inv
    freq[0, MLA_NOPE + MLA_ROPE // 2:MLA_NOPE + MLA_ROPE] = inv
    cos_t, sin_t = rope_tables(positions.reshape(t, 1), jnp.asarray(freq), tm_proj)

    tabs = _dft_tables(seq, batch)
    tril = jnp.asarray(np.tril(np.ones((SSM_CHUNK, SSM_CHUNK), np.float32)))

    x2d = x.reshape(t, d)
    for i in range(depth):
        wall = _pack_inproj(w_in[i])
        wq, wqs, wk, wv = _pack_mla(mla_w_uq[i], mla_w_ukv[i])
        q, k, v, hy_u, z, xbc_raw, dt_raw = inproj(x2d, _row(norm_mix[i]), wall, _row(mla_q_norm[i]), wq, wqs,
                                                    _row(mla_kv_norm[i]), wk, wv, cos_t, sin_t, tm_proj)
        o_mla = attention(q.reshape(batch, seq, -1), k.reshape(batch, seq, -1), v, min(TQ_ATTN, seq),
                          min(TK_ATTN, seq))

        kspec = _hyena_kspec(seq, tabs, hy_filt_w1[i], hy_filt_b1[i], hy_filt_freq[i], hy_filt_w2[i], hy_filt_b2[i],
                             hy_filt_w3[i], hy_decay[i])
        o_hy = _hyena(hy_u.reshape(batch, seq, -1), tabs, kspec, hy_conv_w[i], hy_conv_b[i], hy_bias[i])

        xbc = dwconv(xbc_raw.reshape(batch, seq, -1), ssm_conv_w[i], _row(ssm_conv_b[i]), 0, SSM_CONV_DIM, True,
                     ROWS_CONV)
        dtbias_row = _pad_cols(_row(ssm_dt_bias[i]), LANES)
        a_row = _pad_cols(_row(-jnp.exp(ssm_a_log[i].astype(F32))), LANES)
        y_f, y_b = ssd_scan(xbc, dt_raw.reshape(batch, seq, -1), dtbias_row, a_row, tril)

        dsk = _row(jnp.repeat(ssm_d[i].astype(F32), SSM_HEADDIM))
        x1, hffn, aff = outproj(o_mla.reshape(t, -1), o_hy.reshape(t, -1), y_f.reshape(t, -1), y_b.reshape(t, -1),
                                xbc.reshape(t, -1), z, x2d, _row(mla_out_norm[i]), _row(hy_out_norm[i]), dsk,
                                _row(ssm_norm[i]), w_out[i].astype(BF16), _row(norm_ffn[i]),
                                _stack3_lhs(moe_router[i].astype(F32).T), batch, tm_out)

        pos, gsel, rowoff = moe_select(aff, cap)
        offs_g = jnp.concatenate([rowoff[:, :, ::tb // LANES] // COMBINE_ALIGN,
                                  jnp.full((batch, N_EXPERTS, 1), pl.cdiv(cap, COMBINE_ALIGN), I32)], axis=-1)
        offs_c = rowoff[:, :, ::tbc // LANES] // COMBINE_ALIGN
        xe, gslot = moe_gather(hffn.reshape(batch, seq, d), pos, gsel, offs_g, cap, tb, GATHER_GROUP)
        ye = moe_ffn(xe, gslot, moe_w_gate, moe_w_up, moe_w_down, i, cap, cap_pad, TF_FFN)
        x3 = moe_combine(ye, pos, offs_c, x1.reshape(batch, seq, d), p, i, _row(ple_norm[i]),
                         ple_gate_w[i].astype(BF16), ple_proj[i].astype(BF16), _row(final_norm_g), i == depth - 1, tbc)
        x2d = x3.reshape(t, d)
    return x2d.reshape(batch, seq, d)
```

```python
import functools
import math

import numpy as np
import jax
import jax.numpy as jnp
from jax import lax
from jax.experimental import pallas as pl
from jax.experimental.pallas import tpu as pltpu

F32 = jnp.float32
BF16 = jnp.bfloat16
I32 = jnp.int32
HIGHEST = lax.Precision.HIGHEST

EPS = 1e-6
LANES = 128
SUBLANES = 8
COMBINE_ALIGN = 16
VMEM_LIMIT = 56 * 1024 * 1024

D_MODEL = 1024
MLA_HEADS = 8
MLA_NOPE = 64
MLA_ROPE = 32
MLA_V = 64
MLA_Q_LORA = 256
MLA_KV_LORA = 128
HEAD_PAD = 128
ATTN_SUM_ROWS = 16
ROPE_THETA = 10000.0
HY_WIDTH = 256
HY_ORDER = 2
HY_SHORT = 3
HY_BANDS = 8
HY_FILT_HID = 64
SSM_WIDTH = 256
SSM_HEADDIM = 64
SSM_HEADS = 4
SSM_GROUPS = 2
SSM_STATE = 128
SSM_CONV = 5
SSM_CHUNK = 128
SSM_CONV_DIM = 768
N_EXPERTS = 16
EXPERT_FF = 2048
EC_CAPACITY_FACTOR = 2
PLE_DIM = 256
FFT_N2 = 128
MANTISSA_STEPS = 40
IN_SPLITS = (MLA_Q_LORA, MLA_KV_LORA, MLA_ROPE, 3 * HY_WIDTH, SSM_WIDTH, SSM_CONV_DIM, 2 * SSM_HEADS)


def _cparams(sem, vmem=None):
    return pltpu.CompilerParams(dimension_semantics=sem, vmem_limit_bytes=vmem or VMEM_LIMIT)


def _rms(x, g):
    ms = jnp.mean(x * x, axis=-1, keepdims=True)
    return x * lax.rsqrt(ms + EPS) * g


def _dot(a, b, precision=None):
    return jnp.dot(a, b, preferred_element_type=F32, precision=precision)


def _dot_nt(a, b, precision=None):
    return lax.dot_general(a, b, (((1,), (1,)), ((), ())), preferred_element_type=F32, precision=precision)


def _dot_tn(a, b, precision=None):
    return lax.dot_general(a, b, (((0,), (0,)), ((), ())), preferred_element_type=F32, precision=precision)


def _split_bf16(x):
    hi = x.astype(BF16)
    return hi, (x - hi.astype(F32)).astype(BF16)


def _dot3(a, b):
    ah, al = _split_bf16(a)
    bh, bl = _split_bf16(b)
    return _dot(jnp.concatenate([ah, ah, al], axis=1), jnp.concatenate([bh, bl, bh], axis=0))


def _full(shape):
    n = len(shape)
    return pl.BlockSpec(shape, lambda *_: (0,) * n)


def _rope_kernel(pos_ref, freq_ref, cos_ref, sin_ref):
    ang = pos_ref[...].astype(F32) * freq_ref[...]
    cos_ref[...] = jnp.cos(ang)
    sin_ref[...] = jnp.sin(ang)


def rope_tables(pos_col, freq_row, tm):
    t = pos_col.shape[0]
    return pl.pallas_call(
        _rope_kernel,
        grid=(t // tm,),
        in_specs=[pl.BlockSpec((tm, 1), lambda i: (i, 0)), _full((1, HEAD_PAD))],
        out_specs=[pl.BlockSpec((tm, HEAD_PAD), lambda i: (i, 0))] * 2,
        out_shape=[jax.ShapeDtypeStruct((t, HEAD_PAD), F32)] * 2,
        compiler_params=_cparams(("parallel",)),
        name="rope_tables",
    )(pos_col, freq_row)


_C_CQ = 0
_C_CKV = 256
_C_KR = 384
_C_KRS = 512
_C_HY = 640
_C_Z = 1408
_C_XBC = 1664
_C_DT = 2432
_C_END = 2560


def _inproj_kernel(x_ref, gmix_ref, wall_ref, qn_ref, wq_ref, wqs_ref, kvn_ref, wk_ref, wv_ref, cos_ref, sin_ref,
                   q_ref, k_ref, v_ref, hy_ref, z_ref, xbc_ref, dt_ref, *, scale):
    h = _rms(x_ref[...], gmix_ref[...]).astype(BF16)
    proj = _dot(h, wall_ref[...])
    hy_ref[...] = proj[:, _C_HY:_C_Z]
    z_ref[...] = proj[:, _C_Z:_C_XBC]
    xbc_ref[...] = proj[:, _C_XBC:_C_DT]
    dt_ref[...] = proj[:, _C_DT:_C_END]
    cos = cos_ref[...]
    sin = sin_ref[...]
    cos8 = jnp.concatenate([cos] * MLA_HEADS, axis=-1)
    sin8 = jnp.concatenate([sin] * MLA_HEADS, axis=-1)
    cqn = _rms(proj[:, _C_CQ:_C_CKV], qn_ref[...]).astype(BF16)
    q = _dot(cqn, wq_ref[...])
    qs = _dot(cqn, wqs_ref[...])
    q_ref[...] = ((q * cos8 + qs * sin8) * scale).astype(BF16)
    ckvn = _rms(proj[:, _C_CKV:_C_KR], kvn_ref[...]).astype(BF16)
    kn = _dot(ckvn, wk_ref[...])
    v_ref[...] = _dot_nt(wv_ref[...], ckvn).astype(BF16)
    kr = proj[:, _C_KR:_C_KRS] * cos + proj[:, _C_KRS:_C_HY] * sin
    k_ref[...] = (kn + jnp.concatenate([kr] * MLA_HEADS, axis=-1)).astype(BF16)


def inproj(x2d, gmix, wall, qn, wq, wqs, kvn, wk, wv, cos_t, sin_t, tm):
    t = x2d.shape[0]
    hq = MLA_HEADS * HEAD_PAD
    row = lambda w: pl.BlockSpec((tm, w), lambda i: (i, 0))
    outs = [(hq, BF16), (hq, BF16), None, (3 * HY_WIDTH, F32), (SSM_WIDTH, F32), (SSM_CONV_DIM, F32), (LANES, F32)]
    hv = MLA_HEADS * MLA_V
    return pl.pallas_call(
        functools.partial(_inproj_kernel, scale=(MLA_NOPE + MLA_ROPE) ** -0.5 * math.log2(math.e)),
        grid=(t // tm,),
        in_specs=[row(D_MODEL), _full(gmix.shape), _full(wall.shape), _full(qn.shape), _full(wq.shape),
                  _full(wqs.shape), _full(kvn.shape), _full(wk.shape), _full(wv.shape), row(HEAD_PAD), row(HEAD_PAD)],
        out_specs=[row(o[0]) if o else pl.BlockSpec((hv, tm), lambda i: (0, i)) for o in outs],
        out_shape=[jax.ShapeDtypeStruct((t, o[0]), o[1]) if o else jax.ShapeDtypeStruct((hv, t), BF16)
                   for o in outs],
        compiler_params=_cparams(("parallel",)),
        name="inproj",
    )(x2d, gmix, wall, qn, wq, wqs, kvn, wk, wv, cos_t, sin_t)


def _attn_kernel(q_ref, k_ref, vt_ref, o_ref, st_ref, *, tk):
    seq = k_ref.shape[1]
    tq = q_ref.shape[1]
    nh = st_ref.shape[0]
    npairs = seq // (2 * tk)

    def scores(hh, c, slot):
        off = pl.multiple_of(c * tk, tk)
        st = _dot_nt(k_ref[0, pl.ds(off, tk), hh * HEAD_PAD:(hh + 1) * HEAD_PAD],
                     q_ref[0, :, hh * HEAD_PAD:(hh + 1) * HEAD_PAD])
        st_ref[hh, slot] = st
        return jnp.max(st, axis=0, keepdims=True)

    ones_rows = jnp.ones((ATTN_SUM_ROWS, tk), BF16)

    def update(hh, c, slot, m, acc, smax):
        off = pl.multiple_of(c * tk, tk)
        vtc = jnp.concatenate([vt_ref[hh * MLA_V:(hh + 1) * MLA_V, pl.ds(off, tk)], ones_rows], axis=0)
        m_new = jnp.maximum(m, smax)
        alpha = jnp.exp2(m - m_new)
        p = jnp.exp2(st_ref[hh, slot] - m_new)
        acc = acc * alpha + _dot(vtc, p.astype(BF16))
        return m_new, acc

    def pair(i, carry, last):
        new = []
        for hh in range(nh):
            m, acc, smax0 = carry[hh]
            smax1 = scores(hh, 2 * i + 1, 1)
            m, acc = update(hh, 2 * i, 0, m, acc, smax0)
            smax0 = smax1 if last else scores(hh, 2 * i + 2, 0)
            m, acc = update(hh, 2 * i + 1, 1, m, acc, smax1)
            new.append((m, acc, smax0))
        return tuple(new)

    init = tuple((jnp.full((1, tq), -jnp.inf, F32), jnp.zeros((MLA_V + ATTN_SUM_ROWS, tq), F32), scores(hh, 0, 0))
                 for hh in range(nh))
    carry = lax.fori_loop(0, npairs - 1, functools.partial(pair, last=False), init)
    final = pair(npairs - 1, carry, True)
    o_ref[0] = jnp.concatenate([jnp.transpose(acc[:MLA_V] / acc[MLA_V:MLA_V + 1]) for _, acc, _ in final], axis=-1)


def attention(q, k, vt, tq, tk):
    b, seq, _ = q.shape
    return pl.pallas_call(
        functools.partial(_attn_kernel, tk=tk),
        grid=(b, MLA_HEADS // 2, seq // tq),
        in_specs=[pl.BlockSpec((1, tq, 2 * HEAD_PAD), lambda bi, hp, qi: (bi, qi, hp)),
                  pl.BlockSpec((1, seq, 2 * HEAD_PAD), lambda bi, hp, qi: (bi, 0, hp)),
                  pl.BlockSpec((2 * MLA_V, seq), lambda bi, hp, qi: (hp, bi))],
        out_specs=pl.BlockSpec((1, tq, 2 * MLA_V), lambda bi, hp, qi: (bi, qi, hp)),
        out_shape=jax.ShapeDtypeStruct((b, seq, MLA_HEADS * MLA_V), F32),
        scratch_shapes=[pltpu.VMEM((2, 2, tk, tq), F32)],
        compiler_params=_cparams(("parallel", "parallel", "parallel")),
        name="attention",
    )(q, k, vt)


def _dwconv_kernel(x_ref, w_ref, b_ref, o_ref, *, width, act, rows):
    seq = x_ref.shape[1]
    pad = width // 2
    nchunks = seq // rows
    w = w_ref[...]
    bias = b_ref[...]

    def body(c, carry):
        r0 = pl.multiple_of(c * rows, rows)
        cur = x_ref[0, pl.ds(r0, rows), :]
        p0 = pl.multiple_of(jnp.maximum(r0 - SUBLANES, 0), SUBLANES)
        n0 = pl.multiple_of(jnp.minimum(r0 + rows, seq - SUBLANES), SUBLANES)
        prev = jnp.where(c > 0, x_ref[0, pl.ds(p0, SUBLANES), :], 0.0)
        nxt = jnp.where(c < nchunks - 1, x_ref[0, pl.ds(n0, SUBLANES), :], 0.0)
        ext = jnp.concatenate([prev, cur, nxt], axis=0)
        acc = bias + ext[SUBLANES - pad:SUBLANES - pad + rows] * w[0:1]
        for kk in range(1, width):
            s0 = SUBLANES - pad + kk
            acc = acc + ext[s0:s0 + rows] * w[kk:kk + 1]
        if act:
            acc = acc * jax.nn.sigmoid(acc)
        o_ref[0, pl.ds(r0, rows), :] = acc
        return carry

    lax.fori_loop(0, nchunks, body, 0)


def dwconv(x, w, bias, col0, ncols, act, rows):
    b, seq, _ = x.shape
    width = w.shape[0]
    cb0 = col0 // LANES
    return pl.pallas_call(
        functools.partial(_dwconv_kernel, width=width, act=act, rows=min(rows, seq)),
        grid=(b, ncols // LANES),
        in_specs=[pl.BlockSpec((1, seq, LANES), lambda bi, ci: (bi, 0, ci + cb0)),
                  pl.BlockSpec((width, LANES), lambda bi, ci: (0, ci)),
                  pl.BlockSpec((1, LANES), lambda bi, ci: (0, ci))],
        out_specs=pl.BlockSpec((1, seq, LANES), lambda bi, ci: (bi, 0, ci)),
        out_shape=jax.ShapeDtypeStruct((b, seq, ncols), F32),
        compiler_params=_cparams(("parallel", "parallel")),
        name="dwconv",
    )(x, w, bias)


def _hyfilt_kernel(mult_ref, w1_ref, b1_ref, fr_ref, w2_ref, b2_ref, w3_ref, dec_ref, bwd_ref, o_ref, *, seq, rows):
    i = pl.program_id(0)
    ridx = lax.broadcasted_iota(I32, (rows, LANES), 0) + i * rows
    lane = lax.broadcasted_iota(I32, (rows, LANES), 1)
    t = ridx.astype(F32) / seq
    ang = t * mult_ref[...]
    feats = jnp.where(lane == 0, t, jnp.where(lane <= HY_BANDS, jnp.sin(ang), jnp.cos(ang)))
    feats = jnp.where(lane < 1 + 2 * HY_BANDS, feats, 0.0)
    fr = fr_ref[...]
    hdn = jnp.sin(fr * (_dot(feats, w1_ref[...], HIGHEST) + b1_ref[...]))
    hdn = jnp.sin(fr * (_dot(hdn, w2_ref[...], HIGHEST) + b2_ref[...]))
    hh, hl = _split_bf16(hdn)
    wh, wl = _split_bf16(w3_ref[...])
    filt = (_dot(hh, wh) + _dot(hl, wh)) + _dot(hh, wl)
    window = jnp.exp(-t[:, 0:1] * jnp.abs(dec_ref[...]))
    out = filt * window
    keep = jnp.logical_or(ridx[:, 0:1] > 0, bwd_ref[...] < 0.5)
    o_ref[...] = jnp.where(keep, out, 0.0)


def hyena_filters(seq, mult, w1p, b1, fr, w2, b2, w3, dec, bwd_mask, rows):
    ncol = w3.shape[1]
    rows = min(rows, seq)
    args = (mult, w1p, b1, fr, w2, b2, w3, dec, bwd_mask)
    return pl.pallas_call(
        functools.partial(_hyfilt_kernel, seq=seq, rows=rows),
        grid=(seq // rows,),
        in_specs=[_full(a.shape) for a in args],
        out_specs=pl.BlockSpec((rows, ncol), lambda i: (i, 0)),
        out_shape=jax.ShapeDtypeStruct((seq, ncol), F32),
        compiler_params=_cparams(("parallel",)),
        name="hyena_filters",
    )(*args)


def _leftmm_kernel(m_ref, x_ref, o_ref):
    o_ref[...] = _dot3(m_ref[...], x_ref[...])


def leftmm(m, x2d, cb):
    r, kdim = m.shape
    n = x2d.shape[1]
    cb = min(cb, n)
    return pl.pallas_call(
        _leftmm_kernel,
        grid=(n // cb,),
        in_specs=[_full((r, kdim)), pl.BlockSpec((kdim, cb), lambda i: (0, i))],
        out_specs=pl.BlockSpec((r, cb), lambda i: (0, i)),
        out_shape=jax.ShapeDtypeStruct((r, n), F32),
        compiler_params=_cparams(("parallel",)),
        name="dft_outer",
    )(m, x2d)


def _stack3_lhs(f):
    hi, lo = _split_bf16(f)
    return jnp.concatenate([hi, hi, lo], axis=1)


def _stack3_rhs(a):
    hi, lo = _split_bf16(a)
    return jnp.concatenate([hi, lo, hi], axis=0)


def _twiddle(ar, ai, twr, twi, conj):
    if conj:
        return ar * twr + ai * twi, ai * twr - ar * twi
    return ar * twr - ai * twi, ai * twr + ar * twi


def _specfilt_kernel(a_ref, fblk_ref, twr_ref, twi_ref, o_ref, lhs_ref):
    n2 = FFT_N2
    c = a_ref.shape[-1] // 2

    @pl.when((pl.program_id(0) == 0) & (pl.program_id(1) == 0))
    def _():
        lhs_ref[...] = _stack3_lhs(fblk_ref[...])

    for kk in range(a_ref.shape[1]):
        br, bi = _twiddle(a_ref[0, kk], a_ref[1, kk], twr_ref[kk], twi_ref[kk], False)
        x = _dot(lhs_ref[...], _stack3_rhs(jnp.concatenate([br, bi], axis=0)))
        o_ref[0, 0, kk] = x[:n2, :c] + x[:n2, c:]
        o_ref[0, 1, kk] = x[n2:, :c] - x[n2:, c:]


def filter_spectrum(a4, fblk, twr_col, twi_col, k1s):
    _, n1, n2, ctot = a4.shape
    c = HY_WIDTH
    return pl.pallas_call(
        _specfilt_kernel,
        grid=(HY_ORDER, n1 // k1s),
        in_specs=[pl.BlockSpec((2, k1s, n2, 2 * c), lambda o, k: (0, k, 0, o)),
                  _full(fblk.shape),
                  pl.BlockSpec((k1s, n2, 1), lambda o, k: (k, 0, 0)),
                  pl.BlockSpec((k1s, n2, 1), lambda o, k: (k, 0, 0))],
        out_specs=pl.BlockSpec((1, 2, k1s, n2, c), lambda o, k: (o, 0, k, 0, 0)),
        out_shape=jax.ShapeDtypeStruct((HY_ORDER, 2, n1, n2, c), F32),
        scratch_shapes=[pltpu.VMEM((2 * n2, 6 * n2), BF16)],
        compiler_params=_cparams(("arbitrary", "arbitrary")),
        name="filter_spectrum",
    )(a4, fblk, twr_col, twi_col)


def _specmul_kernel(a_ref, k_ref, fblk_ref, fblk_t_ref, twr_ref, twi_ref, o_ref, lhs_ref):
    n2 = FFT_N2

    @pl.when(pl.program_id(0) == 0)
    def _():
        lhs_ref[0] = _stack3_lhs(fblk_ref[...])
        lhs_ref[1] = _stack3_lhs(fblk_t_ref[...])

    for kk in range(a_ref.shape[1]):
        twr, twi = twr_ref[kk], twi_ref[kk]
        br, bi = _twiddle(a_ref[0, kk], a_ref[1, kk], twr, twi, False)
        x = _dot(lhs_ref[0], _stack3_rhs(jnp.concatenate([br, bi], axis=0)))
        xr, xi = x[:n2], x[n2:]
        kr, ki = k_ref[0, 0, kk], k_ref[0, 1, kk]
        p = jnp.concatenate([xr * kr - xi * ki, xr * ki + xi * kr], axis=0)
        y = _dot(lhs_ref[1], _stack3_rhs(p))
        yr, yi = _twiddle(y[:n2], y[n2:], twr, twi, True)
        o_ref[0, kk] = yr
        o_ref[1, kk] = yi


def spectrum_multiply(a4, kspec, order, fblk, fblk_t, twr_col, twi_col, k1s):
    _, n1, n2, c = a4.shape
    return pl.pallas_call(
        _specmul_kernel,
        grid=(n1 // k1s,),
        in_specs=[pl.BlockSpec((2, k1s, n2, c), lambda k: (0, k, 0, 0)),
                  pl.BlockSpec((1, 2, k1s, n2, c), lambda k: (order, 0, k, 0, 0)),
                  _full(fblk.shape), _full(fblk_t.shape),
                  pl.BlockSpec((k1s, n2, 1), lambda k: (k, 0, 0)),
                  pl.BlockSpec((k1s, n2, 1), lambda k: (k, 0, 0))],
        out_specs=pl.BlockSpec((2, k1s, n2, c), lambda k: (0, k, 0, 0)),
        out_shape=jax.ShapeDtypeStruct((2, n1, n2, c), F32),
        scratch_shapes=[pltpu.VMEM((2, 2 * n2, 6 * n2), BF16)],
        compiler_params=_cparams(("arbitrary",)),
        name="spectrum_multiply",
    )(a4, kspec, fblk, fblk_t, twr_col, twi_col)


def _convout_kernel(m_ref, y_ref, v_ref, g_ref, bias_ref, *rest, reps):
    y = _dot3(m_ref[...], y_ref[...])
    bias = jnp.concatenate([bias_ref[...]] * reps, axis=-1)
    out = (y + v_ref[...] * bias) * g_ref[...]
    if len(rest) == 1:
        rest[0][...] = out
    else:
        m1_ref, o_ref, a_ref = rest
        o_ref[...] = out
        a_ref[...] = _dot3(m1_ref[...], out)


def conv_output(m3, y2d, v2d, g2d, bias_row, cb, m1_next=None):
    r, kdim = m3.shape
    n = y2d.shape[1]
    cb = min(cb, n)
    c = bias_row.shape[1]
    col = lambda rows: pl.BlockSpec((rows, cb), lambda i: (0, i))
    in_specs = [_full((r, kdim)), col(kdim), col(r), col(r), _full((1, c))]
    out_specs = [col(r)]
    out_shape = [jax.ShapeDtypeStruct((r, n), F32)]
    args = [m3, y2d, v2d, g2d, bias_row]
    if m1_next is not None:
        in_specs.append(_full(m1_next.shape))
        out_specs.append(col(m1_next.shape[0]))
        out_shape.append(jax.ShapeDtypeStruct((m1_next.shape[0], n), F32))
        args.append(m1_next)
    return pl.pallas_call(
        functools.partial(_convout_kernel, reps=cb // c),
        grid=(n // cb,),
        in_specs=in_specs,
        out_specs=out_specs,
        out_shape=out_shape,
        compiler_params=_cparams(("parallel",)),
        name="dft_outer_inverse",
    )(*args)


def _dft_tables(seq, batch):
    n2 = FFT_N2
    half = seq // n2
    n1 = 2 * half
    n = n1 * n2
    k1 = np.arange(n1, dtype=np.float64)[:, None]
    nn1 = np.arange(half, dtype=np.float64)[None, :]
    th = 2.0 * np.pi * k1 * nn1 / n1
    c1, s1 = np.cos(th), np.sin(th)
    assert batch == 2, "the two batch entries are packed as real / imaginary parts"
    m1 = np.block([[c1, s1], [-s1, c1]])
    m3 = np.block([[c1.T, -s1.T], [s1.T, c1.T]]) / n
    mk = np.concatenate([c1, -s1], axis=0)
    kk2 = np.arange(n2, dtype=np.float64)
    th2 = 2.0 * np.pi * np.outer(kk2, kk2) / n2
    c2, s2 = np.cos(th2), np.sin(th2)
    fblk = np.block([[c2, s2], [-s2, c2]])
    tht = 2.0 * np.pi * np.outer(np.arange(n1, dtype=np.float64), kk2) / n
    twr, twi = np.cos(tht), -np.sin(tht)
    f = lambda a: jnp.asarray(a, F32)
    return dict(m1=f(m1), m3=f(m3), mk=f(mk), fblk=f(fblk), fblk_t=f(fblk.T),
                twr_col=f(twr[:, :, None]), twi_col=f(twi[:, :, None]), n1=n1, half=half)


def _ssd_kernel(xf_ref, bf_ref, cf_ref, dtf_ref, xb_ref, bb_ref, cb_ref, dtb_ref, dtbias_ref, a_ref, tril_ref,
                yf_ref, yb_ref, state_ref):
    q = SSM_CHUNK
    hd = SSM_HEADDIM
    ns = SSM_STATE

    @pl.when(pl.program_id(1) == 0)
    def _():
        state_ref[...] = jnp.zeros_like(state_ref)

    tril = tril_ref[...]
    rows = lax.broadcasted_iota(I32, (q, q), 0)
    cols = lax.broadcasted_iota(I32, (q, q), 1)
    a_row = a_ref[...]
    bias = dtbias_ref[...]

    def direction(x_ref, b_ref, c_ref, dt_ref, y_ref, d):
        dt = jax.nn.softplus(dt_ref[0] + bias)
        dta = dt * a_row
        cs = _dot(tril, dta, HIGHEST)
        ecs = cs - dta
        base = ecs if d else cs
        base_t = jnp.transpose(base)
        total = cs[q - 1:q, :]
        x = x_ref[0]
        ys = []
        for g in range(SSM_GROUPS):
            bm = b_ref[0, :, g * ns:(g + 1) * ns]
            cm = c_ref[0, :, g * ns:(g + 1) * ns]
            cb = _dot_nt(cm.astype(BF16), bm.astype(BF16))
            for hh in range(SSM_HEADS // SSM_GROUPS):
                h = g * (SSM_HEADS // SSM_GROUPS) + hh
                j = d * SSM_HEADS + h
                col = jnp.broadcast_to(base[:, j:j + 1], (q, q))
                coln = col if ns == q else jnp.broadcast_to(base[:, j:j + 1], (q, ns))
                row = base_t[j:j + 1, :]
                tot = total[:, j:j + 1]
                if d == 0:
                    seg = jnp.where(rows >= cols, col - row, -jnp.inf)
                    c_scale = jnp.exp(coln)
                    b_scale = jnp.exp(tot - coln)
                else:
                    seg = jnp.where(cols >= rows, row - col, -jnp.inf)
                    c_scale = jnp.exp(tot - coln)
                    b_scale = jnp.exp(coln)
                scores = cb * jnp.exp(seg)
                xdt = (x[:, h * hd:(h + 1) * hd] * jnp.broadcast_to(dt[:, j:j + 1], (q, hd))).astype(BF16)
                st = state_ref[j]
                y = _dot(scores.astype(BF16), xdt) + _dot_nt((cm * c_scale).astype(BF16), st.astype(BF16))
                state_ref[j] = st * jnp.exp(tot) + _dot_tn(xdt, (bm * b_scale).astype(BF16))
                ys.append(y)
        y_ref[0] = jnp.concatenate(ys, axis=-1)

    direction(xf_ref, bf_ref, cf_ref, dtf_ref, yf_ref, 0)
    direction(xb_ref, bb_ref, cb_ref, dtb_ref, yb_ref, 1)


def ssd_scan(xbc, dt_raw, dtbias_row, a_row, tril):
    b, seq, _ = xbc.shape
    q = SSM_CHUNK
    nc = seq // q
    w = SSM_WIDTH
    fwd = lambda col: pl.BlockSpec((1, q, w), lambda bi, i: (bi, i, col))
    bwd = lambda col: pl.BlockSpec((1, q, w), lambda bi, i: (bi, nc - 1 - i, col))
    return pl.pallas_call(
        _ssd_kernel,
        grid=(b, nc),
        in_specs=[fwd(0), fwd(1), fwd(2), pl.BlockSpec((1, q, LANES), lambda bi, i: (bi, i, 0)),
                  bwd(0), bwd(1), bwd(2), pl.BlockSpec((1, q, LANES), lambda bi, i: (bi, nc - 1 - i, 0)),
                  _full((1, LANES)), _full((1, LANES)), _full((q, q))],
        out_specs=[pl.BlockSpec((1, q, w), lambda bi, i: (bi, i, 0)),
                   pl.BlockSpec((1, q, w), lambda bi, i: (bi, nc - 1 - i, 0))],
        out_shape=[jax.ShapeDtypeStruct((b, seq, w), F32)] * 2,
        scratch_shapes=[pltpu.VMEM((2 * SSM_HEADS, SSM_HEADDIM, SSM_STATE), F32)],
        compiler_params=_cparams(("parallel", "arbitrary")),
        name="ssd_scan",
    )(xbc, xbc, xbc, dt_raw, xbc, xbc, xbc, dt_raw, dtbias_row, a_row, tril)


def _outproj_kernel(om_ref, hy_ref, yf_ref, yb_ref, xs_ref, z_ref, x_ref, gm_ref, gh_ref, dsk_ref, gs_ref, wout_ref,
                    gffn_ref, rt_ref, x1_ref, hffn_ref, aff_ref):
    o1 = _rms(om_ref[...], gm_ref[...])
    o2 = _rms(hy_ref[...], gh_ref[...])
    z = z_ref[...]
    y = (yf_ref[...] + yb_ref[...] + xs_ref[...] * dsk_ref[...]) * (z * jax.nn.sigmoid(z))
    gw = SSM_WIDTH // SSM_GROUPS
    gs = gs_ref[...]
    o3 = jnp.concatenate([_rms(y[:, g * gw:(g + 1) * gw], gs[:, g * gw:(g + 1) * gw]) for g in range(SSM_GROUPS)],
                         axis=-1)
    mix = jnp.concatenate([o1, o2, o3], axis=-1).astype(BF16)
    x1 = x_ref[...] + _dot(mix, wout_ref[...])
    x1_ref[...] = x1
    hf = _rms(x1, gffn_ref[...])
    hi, lo = _split_bf16(hf)
    hffn_ref[...] = hi
    logits = _dot_nt(rt_ref[...], jnp.concatenate([hi, lo, hi], axis=1))
    mx = jnp.max(logits, axis=0, keepdims=True)
    ex = jnp.exp(logits - mx)
    aff_ref[0] = ex / jnp.sum(ex, axis=0, keepdims=True)


def outproj(om, hy, yf, yb, xbc, z, x2d, gm, gh, dsk, gs, wout, gffn, router_t, batch, tm):
    t = x2d.shape[0]
    seq = t // batch
    nb = seq // tm
    row = lambda w: pl.BlockSpec((tm, w), lambda i: (i, 0))
    return pl.pallas_call(
        _outproj_kernel,
        grid=(t // tm,),
        in_specs=[row(om.shape[1]), row(HY_WIDTH), row(SSM_WIDTH), row(SSM_WIDTH), row(SSM_WIDTH), row(SSM_WIDTH),
                  row(D_MODEL), _full(gm.shape), _full(gh.shape), _full(dsk.shape), _full(gs.shape),
                  _full(wout.shape), _full(gffn.shape), _full(router_t.shape)],
        out_specs=[row(D_MODEL), row(D_MODEL),
                   pl.BlockSpec((1, N_EXPERTS, tm), lambda i: (i // nb, 0, i % nb))],
        out_shape=[jax.ShapeDtypeStruct((t, D_MODEL), F32), jax.ShapeDtypeStruct((t, D_MODEL), BF16),
                   jax.ShapeDtypeStruct((batch, N_EXPERTS, seq), F32)],
        compiler_params=_cparams(("parallel",)),
        name="outproj_router",
    )(om, hy, yf, yb, xbc, z, x2d, gm, gh, dsk, gs, wout, gffn, router_t)


def _select_kernel(aff_ref, tri_ref, ones_ref, blk_ref, pos_ref, g_ref, off_ref, *, cap, nrows):
    aff = aff_ref[0]
    er = aff.shape[0]
    ne = er // nrows
    aff3 = aff.reshape(ne, nrows, LANES)
    capf = jnp.float32(cap)

    def count(mask3):
        return jnp.sum(jnp.where(mask3, 1.0, 0.0), axis=(1, 2), keepdims=True)

    def enough(cand):
        return count(aff3 >= cand) >= capf

    top = jnp.full((ne, 1, 1), 2.0, F32)
    for shift in (64, 32, 16, 8, 4, 2, 1):
        cand = top * (2.0 ** -shift)
        top = jnp.where(enough(cand), top, cand)
    p = top * 0.5

    def refine(_, carry):
        lo, step = carry
        cand = lo + step
        return jnp.where(enough(cand), cand, lo), step * 0.5

    lo, _ = lax.fori_loop(0, MANTISSA_STEPS, refine, (p, p * 0.5))
    thr = jnp.min(jnp.where(aff3 >= lo, aff3, jnp.inf), axis=(1, 2), keepdims=True)
    gt3 = aff3 > thr
    eq3 = aff3 == thr
    need = capf - count(gt3)

    tri = tri_ref[...]
    ones = ones_ref[...]
    blk = blk_ref[...]

    def prefix(maskf):
        mb = maskf.astype(BF16)
        within = _dot(mb, tri)
        rowtot = _dot(mb, ones)
        before = _dot(blk, rowtot.astype(BF16))
        return within + before, before

    eqf = jnp.where(eq3, 1.0, 0.0).reshape(er, LANES)
    tie_incl, _ = prefix(eqf)
    tie_rank = (tie_incl - eqf).reshape(ne, nrows, LANES)
    sel3 = jnp.logical_or(gt3, jnp.logical_and(eq3, tie_rank < need))
    self_ = jnp.where(sel3, 1.0, 0.0).reshape(er, LANES)
    incl, before = prefix(self_)
    sel = self_ > 0.5
    pos_ref[0] = jnp.where(sel, (incl - self_).astype(I32), -1)
    g_ref[0] = jnp.where(sel, aff, 0.0)
    off_ref[0] = before.astype(I32)


def moe_select(aff, cap):
    b, ne, seq = aff.shape
    nrows = seq // LANES
    er = ne * nrows
    tri = jnp.asarray(np.triu(np.ones((LANES, LANES), np.float32)), BF16)
    ones = jnp.ones((LANES, LANES), BF16)
    ridx = np.arange(er)
    blk = (ridx[:, None] // nrows == ridx[None, :] // nrows) & (ridx[None, :] < ridx[:, None])
    blk = jnp.asarray(blk.astype(np.float32), BF16)
    spec = pl.BlockSpec((1, er, LANES), lambda bi: (bi, 0, 0))
    pos, gsel, off = pl.pallas_call(
        functools.partial(_select_kernel, cap=cap, nrows=nrows),
        grid=(b,),
        in_specs=[spec, _full(tri.shape), _full(ones.shape), _full(blk.shape)],
        out_specs=[spec] * 3,
        out_shape=[jax.ShapeDtypeStruct((b, er, LANES), I32), jax.ShapeDtypeStruct((b, er, LANES), F32),
                   jax.ShapeDtypeStruct((b, er, LANES), I32)],
        compiler_params=_cparams(("parallel",)),
        name="moe_select",
    )(aff.reshape(b, er, LANES), tri, ones, blk)
    return pos.reshape(b, ne, seq), gsel.reshape(b, ne, seq), off[:, :, 0].reshape(b, ne, nrows)


def _gather_kernel(offs_ref, h_hbm, pos_ref, g_ref, o_ref, og_ref, h_ref, sem_ref, *, tb, cap):
    bi = pl.program_id(0)
    gi = pl.program_id(1)
    seq, d = h_ref.shape
    ng = o_ref.shape[1]
    al = COMBINE_ALIGN

    @pl.when(gi == 0)
    def _():
        copy = pltpu.make_async_copy(h_hbm.at[bi], h_ref, sem_ref)
        copy.start()
        copy.wait()

    for e in range(ng):
        o_ref[0, e, 0:al, :] = jnp.zeros((al, d), BF16)
        og_ref[0, e, 0:al, :] = jnp.zeros((al, 1), F32)
    wins = sorted({min(64, tb + al), min(128, tb + al), tb + al})

    def body(j, carry):
        t0 = pl.multiple_of(j * tb, tb)
        offs = [pl.multiple_of(offs_ref[bi, gi * ng + e, j] * al, al) for e in range(ng)]
        needs = [offs_ref[bi, gi * ng + e, j + 1] * al + al - offs[e] for e in range(ng)]
        need = functools.reduce(jnp.maximum, needs)

        def place(win):
            riota = lax.broadcasted_iota(I32, (win, tb), 0)
            hits = [riota + offs[e] == pos_ref[0, 0, e:e + 1, pl.ds(t0, tb)] for e in range(ng)]
            onehot = jnp.concatenate([jnp.where(hit, 1.0, 0.0).astype(BF16) for hit in hits], axis=0)
            rows = _dot(onehot, h_ref[pl.ds(t0, tb), :])
            for e in range(ng):
                r = rows[e * win:(e + 1) * win]
                gates = jnp.sum(jnp.where(hits[e], g_ref[0, 0, e:e + 1, pl.ds(t0, tb)], 0.0), axis=1, keepdims=True)
                head = o_ref[0, e, pl.ds(offs[e], al), :].astype(F32) + r[0:al]
                o_ref[0, e, pl.ds(offs[e], al), :] = head.astype(BF16)
                o_ref[0, e, pl.ds(offs[e] + al, win - al), :] = r[al:].astype(BF16)
                og_ref[0, e, pl.ds(offs[e], al), :] += gates[0:al]
                og_ref[0, e, pl.ds(offs[e] + al, win - al), :] = gates[al:]

        lo = 0
        for win in wins:
            fits = need <= win if win != wins[-1] else True
            pl.when(jnp.logical_and(need > lo, fits))(functools.partial(place, win))
            lo = win
        return carry

    lax.fori_loop(0, seq // tb, body, 0)
    tail = o_ref.shape[2] - cap
    o_ref[0, :, cap:, :] = jnp.zeros((ng, tail, d), BF16)
    og_ref[0, :, cap:, :] = jnp.zeros((ng, tail, 1), F32)


def moe_gather(hffn, pos, gsel, offs, cap, tb, ng):
    b, seq, d = hffn.shape
    ne = pos.shape[1]
    rows = cap + tb + COMBINE_ALIGN
    assert cap % COMBINE_ALIGN == 0 and ne % ng == 0
    row_spec = pl.BlockSpec((1, 1, ng, seq), lambda bi, gi, offs: (bi, gi, 0, 0))
    grid_spec = pltpu.PrefetchScalarGridSpec(
        num_scalar_prefetch=1,
        grid=(b, ne // ng),
        in_specs=[pl.BlockSpec(memory_space=pl.ANY), row_spec, row_spec],
        out_specs=[pl.BlockSpec((1, ng, rows, d), lambda bi, gi, offs: (bi, gi, 0, 0)),
                   pl.BlockSpec((1, ng, rows, 1), lambda bi, gi, offs: (bi, gi, 0, 0))],
        scratch_shapes=[pltpu.VMEM((seq, d), BF16), pltpu.SemaphoreType.DMA(())],
    )
    return pl.pallas_call(
        functools.partial(_gather_kernel, tb=tb, cap=cap),
        grid_spec=grid_spec,
        out_shape=[jax.ShapeDtypeStruct((b, ne, rows, d), BF16), jax.ShapeDtypeStruct((b, ne, rows, 1), F32)],
        compiler_params=_cparams(("arbitrary", "arbitrary")),
        name="moe_gather",
    )(offs, hffn, pos.reshape(b, ne // ng, ng, seq), gsel.reshape(b, ne // ng, ng, seq))


def _ffn_kernel(xe_ref, gs_ref, wg_ref, wu_ref, wd_ref, o_ref, acc_ref, *, cap):
    f = pl.program_id(1)
    nb = xe_ref.shape[0]
    d = xe_ref.shape[-1]
    xe = xe_ref[...].reshape(nb * cap, d)
    a = _dot(xe, wg_ref[...].astype(BF16))
    u = _dot(xe, wu_ref[...].astype(BF16))
    hid = (a * jax.nn.sigmoid(a) * u).astype(BF16)
    part = _dot(hid, wd_ref[...].astype(BF16))

    @pl.when(f == 0)
    def _():
        acc_ref[...] = part

    @pl.when(f > 0)
    def _():
        acc_ref[...] += part

    @pl.when(f == pl.num_programs(1) - 1)
    def _():
        gated = acc_ref[...] * gs_ref[...].reshape(nb * cap, 1)
        o_ref[:, 0:cap, :] = gated.reshape(nb, cap, d).astype(BF16)
        o_ref[:, cap:, :] = jnp.zeros((nb, o_ref.shape[1] - cap, d), BF16)


def moe_ffn(xe, gslot, w_gate, w_up, w_down, layer, cap, cap_pad, tf):
    b, ne, _, d = xe.shape
    ff = w_gate.shape[-1]
    return pl.pallas_call(
        functools.partial(_ffn_kernel, cap=cap),
        grid=(ne, ff // tf),
        in_specs=[pl.BlockSpec((b, None, cap, d), lambda e, f: (0, e, 0, 0)),
                  pl.BlockSpec((b, None, cap, 1), lambda e, f: (0, e, 0, 0)),
                  pl.BlockSpec((None, None, d, tf), lambda e, f: (layer, e, 0, f)),
                  pl.BlockSpec((None, None, d, tf), lambda e, f: (layer, e, 0, f)),
                  pl.BlockSpec((None, None, tf, d), lambda e, f: (layer, e, f, 0))],
        out_specs=pl.BlockSpec((b, None, cap_pad, d), lambda e, f: (0, e, 0, 0)),
        out_shape=jax.ShapeDtypeStruct((b, ne, cap_pad, d), BF16),
        scratch_shapes=[pltpu.VMEM((b * cap, d), F32)],
        compiler_params=_cparams(("parallel", "arbitrary")),
        name="moe_ffn",
    )(xe, gslot, w_gate, w_up, w_down)


def _combine_kernel(offs_ref, ye_hbm, pos_ref, x1_ref, p_ref, gple_ref, wgate_ref, wproj_ref, gfin_ref, o_ref,
                    buf_ref, sem_ref, *, tb, win, final):
    bi = pl.program_id(0)
    j = pl.program_id(1)
    nj = pl.num_programs(1)
    ne = pos_ref.shape[1]
    step = bi * nj + j
    slot = step % 2

    def window_copy(b_, j_, e, slot_):
        off = pl.multiple_of(offs_ref[b_, e, j_] * COMBINE_ALIGN, COMBINE_ALIGN)
        return pltpu.make_async_copy(ye_hbm.at[b_, e, pl.ds(off, win), :],
                                     buf_ref.at[slot_, pl.ds(e * win, win), :], sem_ref.at[slot_, e])

    @pl.when(step == 0)
    def _():
        for e in range(ne):
            window_copy(bi, j, e, slot).start()

    @pl.when(step + 1 < pl.num_programs(0) * nj)
    def _():
        wrap = j + 1 == nj
        b_next = jnp.where(wrap, bi + 1, bi)
        j_next = jnp.where(wrap, 0, j + 1)
        for e in range(ne):
            window_copy(b_next, j_next, e, 1 - slot).start()

    def token_major(a):
        pad = jnp.zeros((tb - ne, tb), a.dtype)
        return jnp.transpose(jnp.concatenate([a, pad], axis=0))[:, :ne]

    pos_all = token_major(pos_ref[0])
    lane = lax.broadcasted_iota(I32, (tb, LANES), 1)
    targets = []
    for e in range(ne):
        col = pos_all[:, e:e + 1]
        delta = e * win - offs_ref[bi, e, j] * COMBINE_ALIGN
        targets.append(jnp.broadcast_to(jnp.where(col >= 0, col + delta, -1), (tb, LANES)))
    tiles = []
    for k in range(ne * win // LANES):
        hits = [jnp.where(targets[e] == lane + k * LANES, 1.0, 0.0) for e in range(ne)
                if e * win < (k + 1) * LANES and (e + 1) * win > k * LANES]
        tiles.append(functools.reduce(lambda a, b: a + b, hits).astype(BF16))
    sel = jnp.concatenate(tiles, axis=1)
    for e in range(ne):
        window_copy(bi, j, e, slot).wait()
    x2 = x1_ref[0] + _dot(sel, buf_ref[slot])
    hp = _rms(x2, gple_ref[...]).astype(BF16)
    gt = jax.nn.sigmoid(_dot(hp, wgate_ref[...]))
    x3 = x2 + _dot(p_ref[0].astype(BF16), wproj_ref[...]) * gt
    o_ref[0] = _rms(x3, gfin_ref[...]) if final else x3


def moe_combine(ye, pos, offs, x1, p, layer, gple, wgate, wproj, gfin, final, tb):
    b, ne, _, d = ye.shape
    seq = x1.shape[1]
    win = tb + COMBINE_ALIGN
    assert (ne * win) % LANES == 0
    grid_spec = pltpu.PrefetchScalarGridSpec(
        num_scalar_prefetch=1,
        grid=(b, seq // tb),
        in_specs=[pl.BlockSpec(memory_space=pl.ANY),
                  pl.BlockSpec((1, ne, tb), lambda bi, j, offs: (bi, 0, j)),
                  pl.BlockSpec((1, tb, d), lambda bi, j, offs: (bi, j, 0)),
                  pl.BlockSpec((None, 1, tb, p.shape[-1]), lambda bi, j, offs: (layer, bi, j, 0)),
                  pl.BlockSpec(gple.shape, lambda bi, j, offs: (0, 0)),
                  pl.BlockSpec(wgate.shape, lambda bi, j, offs: (0, 0)),
                  pl.BlockSpec(wproj.shape, lambda bi, j, offs: (0, 0)),
                  pl.BlockSpec(gfin.shape, lambda bi, j, offs: (0, 0))],
        out_specs=pl.BlockSpec((1, tb, d), lambda bi, j, offs: (bi, j, 0)),
        scratch_shapes=[pltpu.VMEM((2, ne * win, d), BF16), pltpu.SemaphoreType.DMA((2, ne))],
    )
    return pl.pallas_call(
        functools.partial(_combine_kernel, tb=tb, win=win, final=final),
        grid_spec=grid_spec,
        out_shape=jax.ShapeDtypeStruct((b, seq, d), F32),
        compiler_params=_cparams(("arbitrary", "arbitrary")),
        name="moe_combine_ple",
    )(offs, ye, pos, x1, p, gple, wgate, wproj, gfin)


def _pad_cols(a, width):
    return jnp.pad(a, ((0, 0), (0, width - a.shape[1])))


def _pack_inproj(w_in):
    offs = np.cumsum((0,) + IN_SPLITS)
    cq, ckv, kr, hy, z, xbc, dt = [w_in[:, offs[i]:offs[i + 1]] for i in range(len(IN_SPLITS))]
    d = w_in.shape[0]
    half = MLA_ROPE // 2
    zeros = lambda n: jnp.zeros((d, n), w_in.dtype)
    kr_pad = jnp.concatenate([zeros(MLA_NOPE), kr, zeros(HEAD_PAD - MLA_NOPE - MLA_ROPE)], axis=1)
    kr_swap = jnp.concatenate([zeros(MLA_NOPE), -kr[:, half:], kr[:, :half], zeros(HEAD_PAD - MLA_NOPE - MLA_ROPE)],
                              axis=1)
    wall = jnp.concatenate([cq, ckv, kr_pad, kr_swap, hy, z, xbc, _pad_cols(dt, LANES)], axis=1)
    assert wall.shape[1] == _C_END
    return wall.astype(BF16)


def _pack_mla(w_uq, w_ukv):
    lq = w_uq.shape[0]
    lkv = w_ukv.shape[0]
    half = MLA_ROPE // 2
    padw = HEAD_PAD - MLA_NOPE - MLA_ROPE
    q3 = w_uq.reshape(lq, MLA_HEADS, MLA_NOPE + MLA_ROPE)
    nope, rope = q3[..., :MLA_NOPE], q3[..., MLA_NOPE:]
    zq = jnp.zeros((lq, MLA_HEADS, padw), w_uq.dtype)
    wq = jnp.concatenate([nope, rope, zq], axis=-1).reshape(lq, MLA_HEADS * HEAD_PAD)
    wqs = jnp.concatenate([jnp.zeros_like(nope), -rope[..., half:], rope[..., :half], zq], axis=-1)
    wqs = wqs.reshape(lq, MLA_HEADS * HEAD_PAD)
    kv3 = w_ukv.reshape(lkv, MLA_HEADS, MLA_NOPE + MLA_V)
    knope, vv = kv3[..., :MLA_NOPE], kv3[..., MLA_NOPE:]
    wk = jnp.concatenate([knope, jnp.zeros((lkv, MLA_HEADS, HEAD_PAD - MLA_NOPE), w_ukv.dtype)], axis=-1)
    wk = wk.reshape(lkv, MLA_HEADS * HEAD_PAD)
    wv = vv.reshape(lkv, MLA_HEADS * MLA_V).T
    return wq.astype(BF16), wqs.astype(BF16), wk.astype(BF16), wv.astype(BF16)


def _row(a):
    return a.reshape(1, -1).astype(F32)


TM_PROJ = 512
TQ_ATTN = 1024
TK_ATTN = 512
ROWS_CONV = 512
ROWS_FILT = 512
CB_DFT = 2048
TM_OUT = 512
K1_PER_STEP = 8
TB_MOE = 256
TB_COMBINE = 128
GATHER_GROUP = 4
TF_FFN = 512


def _hyena(hy_u, tabs, kspec, conv_w, conv_b, bias):
    b, seq, _ = hy_u.shape
    c = HY_WIDTH
    parts = [dwconv(hy_u, conv_w[:, i * c:(i + 1) * c], _row(conv_b[i * c:(i + 1) * c]), i * c, c, False, ROWS_CONV)
             for i in range(HY_ORDER + 1)]
    gates, v = parts[:-1], parts[-1]
    half, n1 = tabs["half"], tabs["n1"]
    flat = lambda a: a.reshape(b * half, FFT_N2 * c)
    a = leftmm(tabs["m1"], flat(v), CB_DFT)
    for o in range(HY_ORDER):
        y = spectrum_multiply(a.reshape(2, n1, FFT_N2, c), kspec, o, tabs["fblk"], tabs["fblk_t"], tabs["twr_col"],
                              tabs["twi_col"], K1_PER_STEP)
        last = o == HY_ORDER - 1
        outs = conv_output(tabs["m3"], y.reshape(2 * n1, FFT_N2 * c), flat(v), flat(gates[o]), _row(bias[o]), CB_DFT,
                           None if last else tabs["m1"])
        v = outs[0].reshape(b, seq, c)
        a = None if last else outs[1]
    return v


def _hyena_kspec(seq, tabs, w1, b1, freq, w2, b2, w3, decay):
    bands = np.arange(1, HY_BANDS + 1, dtype=np.float64) * 2.0 * np.pi
    mult = np.zeros((1, LANES), np.float32)
    mult[0, 1:1 + HY_BANDS] = bands
    mult[0, 1 + HY_BANDS:1 + 2 * HY_BANDS] = bands
    w1p = jnp.pad(w1.astype(F32), ((0, LANES - w1.shape[0]), (0, 0)))
    ncol = HY_ORDER * 2 * HY_WIDTH
    bwd = (np.arange(ncol) // HY_WIDTH) % 2
    kf = hyena_filters(seq, jnp.asarray(mult), w1p, _row(b1), _row(freq), w2.astype(F32), _row(b2), w3.astype(F32),
                       _row(decay), jnp.asarray(bwd.astype(np.float32)).reshape(1, ncol), ROWS_FILT)
    a = leftmm(tabs["mk"], kf.reshape(tabs["half"], FFT_N2 * ncol), CB_DFT)
    a4 = a.reshape(2, tabs["n1"], FFT_N2, ncol)
    return filter_spectrum(a4, tabs["fblk"], tabs["twr_col"], tabs["twi_col"], K1_PER_STEP)


def kernel(x, p, positions, norm_mix, w_in, mla_q_norm, mla_w_uq, mla_kv_norm, mla_w_ukv, mla_out_norm, hy_conv_w,
           hy_conv_b, hy_filt_w1, hy_filt_b1, hy_filt_freq, hy_filt_w2, hy_filt_b2, hy_filt_w3, hy_decay, hy_bias,
           hy_out_norm, ssm_conv_w, ssm_conv_b, ssm_dt_bias, ssm_a_log, ssm_d, ssm_norm, w_out, norm_ffn, moe_router,
           moe_w_gate, moe_w_up, moe_w_down, ple_norm, ple_gate_w, ple_proj, final_norm_g):
    batch, seq, d = x.shape
    depth = w_in.shape[0]
    t = batch * seq
    cap = EC_CAPACITY_FACTOR * seq // N_EXPERTS
    tb = min(TB_MOE, seq)
    tbc = min(TB_COMBINE, seq)
    cap_pad = cap + tb
    tm_proj = min(TM_PROJ, seq)
    tm_out = min(TM_OUT, seq)

    freq = np.zeros((1, HEAD_PAD), np.float32)
    inv = ROPE_THETA ** (-np.arange(0, MLA_ROPE, 2, dtype=np.float32) / MLA_ROPE)
    freq[0, MLA_NOPE:MLA_NOPE + MLA_ROPE // 2] = inv
    freq[0, MLA_NOPE + MLA_ROPE // 2:MLA_NOPE + MLA_ROPE] = inv
    cos_t, sin_t = rope_tables(positions.reshape(t, 1), jnp.asarray(freq), tm_proj)

    tabs = _dft_tables(seq, batch)
    tril = jnp.asarray(np.tril(np.ones((SSM_CHUNK, SSM_CHUNK), np.float32)))

    x2d = x.reshape(t, d)
    for i in range(depth):
        wall = _pack_inproj(w_in[i])
        wq, wqs, wk, wv = _pack_mla(mla_w_uq[i], mla_w_ukv[i])
        q, k, v, hy_u, z, xbc_raw, dt_raw = inproj(x2d, _row(norm_mix[i]), wall, _row(mla_q_norm[i]), wq, wqs,
                                                    _row(mla_kv_norm[i]), wk, wv, cos_t, sin_t, tm_proj)
        o_mla = attention(q.reshape(batch, seq, -1), k.reshape(batch, seq, -1), v, min(TQ_ATTN, seq),
                          min(TK_ATTN, seq))

        kspec = _hyena_kspec(seq, tabs, hy_filt_w1[i], hy_filt_b1[i], hy_filt_freq[i], hy_filt_w2[i], hy_filt_b2[i],
                             hy_filt_w3[i], hy_decay[i])
        o_hy = _hyena(hy_u.reshape(batch, seq, -1), tabs, kspec, hy_conv_w[i], hy_conv_b[i], hy_bias[i])

        xbc = dwconv(xbc_raw.reshape(batch, seq, -1), ssm_conv_w[i], _row(ssm_conv_b[i]), 0, SSM_CONV_DIM, True,
                     ROWS_CONV)
        dtbias_row = _pad_cols(_row(ssm_dt_bias[i]), LANES)
        a_row = _pad_cols(_row(-jnp.exp(ssm_a_log[i].astype(F32))), LANES)
        y_f, y_b = ssd_scan(xbc, dt_raw.reshape(batch, seq, -1), dtbias_row, a_row, tril)

        dsk = _row(jnp.repeat(ssm_d[i].astype(F32), SSM_HEADDIM))
        x1, hffn, aff = outproj(o_mla.reshape(t, -1), o_hy.reshape(t, -1), y_f.reshape(t, -1), y_b.reshape(t, -1),
                                xbc.reshape(t, -1), z, x2d, _row(mla_out_norm[i]), _row(hy_out_norm[i]), dsk,
                                _row(ssm_norm[i]), w_out[i].astype(BF16), _row(norm_ffn[i]),
                                _stack3_lhs(moe_router[i].astype(F32).T), batch, tm_out)

        pos, gsel, rowoff = moe_select(aff, cap)
        offs_g = jnp.concatenate([rowoff[:, :, ::tb // LANES] // COMBINE_ALIGN,
                                  jnp.full((batch, N_EXPERTS, 1), pl.cdiv(cap, COMBINE_ALIGN), I32)], axis=-1)
        offs_c = rowoff[:, :, ::tbc // LANES] // COMBINE_ALIGN
        xe, gslot = moe_gather(hffn.reshape(batch, seq, d), pos, gsel, offs_g, cap, tb, GATHER_GROUP)
        ye = moe_ffn(xe, gslot, moe_w_gate, moe_w_up, moe_w_down, i, cap, cap_pad, TF_FFN)
        x3 = moe_combine(ye, pos, offs_c, x1.reshape(batch, seq, d), p, i, _row(ple_norm[i]),
                         ple_gate_w[i].astype(BF16), ple_proj[i].astype(BF16), _row(final_norm_g), i == depth - 1, tbc)
        x2d = x3.reshape(t, d)
    return x2d.reshape(batch, seq, d)
```

```python
import functools
import math

import numpy as np
import jax
import jax.numpy as jnp
from jax import lax
from jax.experimental import pallas as pl
from jax.experimental.pallas import tpu as pltpu

F32 = jnp.float32
BF16 = jnp.bfloat16
I32 = jnp.int32
HIGHEST = lax.Precision.HIGHEST

EPS = 1e-6
LANES = 128
SUBLANES = 8
COMBINE_ALIGN = 16
VMEM_LIMIT = 56 * 1024 * 1024

D_MODEL = 1024
MLA_HEADS = 8
MLA_NOPE = 64
MLA_ROPE = 32
MLA_V = 64
MLA_Q_LORA = 256
MLA_KV_LORA = 128
HEAD_PAD = 128
ATTN_SUM_ROWS = 16
ROPE_THETA = 10000.0
HY_WIDTH = 256
HY_ORDER = 2
HY_SHORT = 3
HY_BANDS = 8
HY_FILT_HID = 64
SSM_WIDTH = 256
SSM_HEADDIM = 64
SSM_HEADS = 4
SSM_GROUPS = 2
SSM_STATE = 128
SSM_CONV = 5
SSM_CHUNK = 128
SSM_CONV_DIM = 768
N_EXPERTS = 16
EXPERT_FF = 2048
EC_CAPACITY_FACTOR = 2
PLE_DIM = 256
FFT_N2 = 128
MANTISSA_STEPS = 40
IN_SPLITS = (MLA_Q_LORA, MLA_KV_LORA, MLA_ROPE, 3 * HY_WIDTH, SSM_WIDTH, SSM_CONV_DIM, 2 * SSM_HEADS)


def _cparams(sem, vmem=None):
    return pltpu.CompilerParams(dimension_semantics=sem, vmem_limit_bytes=vmem or VMEM_LIMIT)


def _rms(x, g):
    ms = jnp.mean(x * x, axis=-1, keepdims=True)
    return x * lax.rsqrt(ms + EPS) * g


def _dot(a, b, precision=None):
    return jnp.dot(a, b, preferred_element_type=F32, precision=precision)


def _dot_nt(a, b, precision=None):
    return lax.dot_general(a, b, (((1,), (1,)), ((), ())), preferred_element_type=F32, precision=precision)


def _dot_tn(a, b, precision=None):
    return lax.dot_general(a, b, (((0,), (0,)), ((), ())), preferred_element_type=F32, precision=precision)


def _split_bf16(x):
    hi = x.astype(BF16)
    return hi, (x - hi.astype(F32)).astype(BF16)


def _dot3(a, b):
    ah, al = _split_bf16(a)
    bh, bl = _split_bf16(b)
    return _dot(jnp.concatenate([ah, ah, al], axis=1), jnp.concatenate([bh, bl, bh], axis=0))


def _full(shape):
    n = len(shape)
    return pl.BlockSpec(shape, lambda *_: (0,) * n)


def _rope_kernel(pos_ref, freq_ref, cos_ref, sin_ref):
    ang = pos_ref[...].astype(F32) * freq_ref[...]
    cos_ref[...] = jnp.cos(ang)
    sin_ref[...] = jnp.sin(ang)


def rope_tables(pos_col, freq_row, tm):
    t = pos_col.shape[0]
    return pl.pallas_call(
        _rope_kernel,
        grid=(t // tm,),
        in_specs=[pl.BlockSpec((tm, 1), lambda i: (i, 0)), _full((1, HEAD_PAD))],
        out_specs=[pl.BlockSpec((tm, HEAD_PAD), lambda i: (i, 0))] * 2,
        out_shape=[jax.ShapeDtypeStruct((t, HEAD_PAD), F32)] * 2,
        compiler_params=_cparams(("parallel",)),
        name="rope_tables",
    )(pos_col, freq_row)


_C_CQ = 0
_C_CKV = 256
_C_KR = 384
_C_KRS = 512
_C_HY = 640
_C_Z = 1408
_C_XBC = 1664
_C_DT = 2432
_C_END = 2560


def _inproj_kernel(x_ref, gmix_ref, wall_ref, qn_ref, wq_ref, wqs_ref, kvn_ref, wk_ref, wv_ref, cos_ref, sin_ref,
                   q_ref, k_ref, v_ref, hy_ref, z_ref, xbc_ref, dt_ref, *, scale):
    h = _rms(x_ref[...], gmix_ref[...]).astype(BF16)
    proj = _dot(h, wall_ref[...])
    hy_ref[...] = proj[:, _C_HY:_C_Z]
    z_ref[...] = proj[:, _C_Z:_C_XBC]
    xbc_ref[...] = proj[:, _C_XBC:_C_DT]
    dt_ref[...] = proj[:, _C_DT:_C_END]
    cos = cos_ref[...]
    sin = sin_ref[...]
    cos8 = jnp.concatenate([cos] * MLA_HEADS, axis=-1)
    sin8 = jnp.concatenate([sin] * MLA_HEADS, axis=-1)
    cqn = _rms(proj[:, _C_CQ:_C_CKV], qn_ref[...]).astype(BF16)
    q = _dot(cqn, wq_ref[...])
    qs = _dot(cqn, wqs_ref[...])
    q_ref[...] = ((q * cos8 + qs * sin8) * scale).astype(BF16)
    ckvn = _rms(proj[:, _C_CKV:_C_KR], kvn_ref[...]).astype(BF16)
    kn = _dot(ckvn, wk_ref[...])
    v_ref[...] = _dot_nt(wv_ref[...], ckvn).astype(BF16)
    kr = proj[:, _C_KR:_C_KRS] * cos + proj[:, _C_KRS:_C_HY] * sin
    k_ref[...] = (kn + jnp.concatenate([kr] * MLA_HEADS, axis=-1)).astype(BF16)


def inproj(x2d, gmix, wall, qn, wq, wqs, kvn, wk, wv, cos_t, sin_t, tm):
    t = x2d.shape[0]
    hq = MLA_HEADS * HEAD_PAD
    row = lambda w: pl.BlockSpec((tm, w), lambda i: (i, 0))
    outs = [(hq, BF16), (hq, BF16), None, (3 * HY_WIDTH, F32), (SSM_WIDTH, F32), (SSM_CONV_DIM, F32), (LANES, F32)]
    hv = MLA_HEADS * MLA_V
    return pl.pallas_call(
        functools.partial(_inproj_kernel, scale=(MLA_NOPE + MLA_ROPE) ** -0.5 * math.log2(math.e)),
        grid=(t // tm,),
        in_specs=[row(D_MODEL), _full(gmix.shape), _full(wall.shape), _full(qn.shape), _full(wq.shape),
                  _full(wqs.shape), _full(kvn.shape), _full(wk.shape), _full(wv.shape), row(HEAD_PAD), row(HEAD_PAD)],
        out_specs=[row(o[0]) if o else pl.BlockSpec((hv, tm), lambda i: (0, i)) for o in outs],
        out_shape=[jax.ShapeDtypeStruct((t, o[0]), o[1]) if o else jax.ShapeDtypeStruct((hv, t), BF16)
                   for o in outs],
        compiler_params=_cparams(("parallel",)),
        name="inproj",
    )(x2d, gmix, wall, qn, wq, wqs, kvn, wk, wv, cos_t, sin_t)


def _attn_kernel(q_ref, k_ref, vt_ref, o_ref, st_ref, *, tk):
    seq = k_ref.shape[1]
    tq = q_ref.shape[1]
    nh = st_ref.shape[0]
    npairs = seq // (2 * tk)

    def scores(hh, c, slot):
        off = pl.multiple_of(c * tk, tk)
        st = _dot_nt(k_ref[0, pl.ds(off, tk), hh * HEAD_PAD:(hh + 1) * HEAD_PAD],
                     q_ref[0, :, hh * HEAD_PAD:(hh + 1) * HEAD_PAD])
        st_ref[hh, slot] = st
        return jnp.max(st, axis=0, keepdims=True)

    ones_rows = jnp.ones((ATTN_SUM_ROWS, tk), BF16)

    def update(hh, c, slot, m, acc, smax):
        off = pl.multiple_of(c * tk, tk)
        vtc = jnp.concatenate([vt_ref[hh * MLA_V:(hh + 1) * MLA_V, pl.ds(off, tk)], ones_rows], axis=0)
        m_new = jnp.maximum(m, smax)
        alpha = jnp.exp2(m - m_new)
        p = jnp.exp2(st_ref[hh, slot] - m_new)
        acc = acc * alpha + _dot(vtc, p.astype(BF16))
        return m_new, acc

    def pair(i, carry, last):
        new = []
        for hh in range(nh):
            m, acc, smax0 = carry[hh]
            smax1 = scores(hh, 2 * i + 1, 1)
            m, acc = update(hh, 2 * i, 0, m, acc, smax0)
            smax0 = smax1 if last else scores(hh, 2 * i + 2, 0)
            m, acc = update(hh, 2 * i + 1, 1, m, acc, smax1)
            new.append((m, acc, smax0))
        return tuple(new)

    init = tuple((jnp.full((1, tq), -jnp.inf, F32), jnp.zeros((MLA_V + ATTN_SUM_ROWS, tq), F32), scores(hh, 0, 0))
                 for hh in range(nh))
    carry = lax.fori_loop(0, npairs - 1, functools.partial(pair, last=False), init)
    final = pair(npairs - 1, carry, True)
    o_ref[0] = jnp.concatenate([jnp.transpose(acc[:MLA_V] / acc[MLA_V:MLA_V + 1]) for _, acc, _ in final], axis=-1)


def attention(q, k, vt, tq, tk):
    b, seq, _ = q.shape
    return pl.pallas_call(
        functools.partial(_attn_kernel, tk=tk),
        grid=(b, MLA_HEADS // 2, seq // tq),
        in_specs=[pl.BlockSpec((1, tq, 2 * HEAD_PAD), lambda bi, hp, qi: (bi, qi, hp)),
                  pl.BlockSpec((1, seq, 2 * HEAD_PAD), lambda bi, hp, qi: (bi, 0, hp)),
                  pl.BlockSpec((2 * MLA_V, seq), lambda bi, hp, qi: (hp, bi))],
        out_specs=pl.BlockSpec((1, tq, 2 * MLA_V), lambda bi, hp, qi: (bi, qi, hp)),
        out_shape=jax.ShapeDtypeStruct((b, seq, MLA_HEADS * MLA_V), F32),
        scratch_shapes=[pltpu.VMEM((2, 2, tk, tq), F32)],
        compiler_params=_cparams(("parallel", "parallel", "parallel")),
        name="attention",
    )(q, k, vt)


def _dwconv_kernel(x_ref, w_ref, b_ref, o_ref, *, width, act, rows):
    seq = x_ref.shape[1]
    pad = width // 2
    nchunks = seq // rows
    w = w_ref[...]
    bias = b_ref[...]

    def body(c, carry):
        r0 = pl.multiple_of(c * rows, rows)
        cur = x_ref[0, pl.ds(r0, rows), :]
        p0 = pl.multiple_of(jnp.maximum(r0 - SUBLANES, 0), SUBLANES)
        n0 = pl.multiple_of(jnp.minimum(r0 + rows, seq - SUBLANES), SUBLANES)
        prev = jnp.where(c > 0, x_ref[0, pl.ds(p0, SUBLANES), :], 0.0)
        nxt = jnp.where(c < nchunks - 1, x_ref[0, pl.ds(n0, SUBLANES), :], 0.0)
        ext = jnp.concatenate([prev, cur, nxt], axis=0)
        acc = bias + ext[SUBLANES - pad:SUBLANES - pad + rows] * w[0:1]
        for kk in range(1, width):
            s0 = SUBLANES - pad + kk
            acc = acc + ext[s0:s0 + rows] * w[kk:kk + 1]
        if act:
            acc = acc * jax.nn.sigmoid(acc)
        o_ref[0, pl.ds(r0, rows), :] = acc
        return carry

    lax.fori_loop(0, nchunks, body, 0)


def dwconv(x, w, bias, col0, ncols, act, rows):
    b, seq, _ = x.shape
    width = w.shape[0]
    cb0 = col0 // LANES
    return pl.pallas_call(
        functools.partial(_dwconv_kernel, width=width, act=act, rows=min(rows, seq)),
        grid=(b, ncols // LANES),
        in_specs=[pl.BlockSpec((1, seq, LANES), lambda bi, ci: (bi, 0, ci + cb0)),
                  pl.BlockSpec((width, LANES), lambda bi, ci: (0, ci)),
                  pl.BlockSpec((1, LANES), lambda bi, ci: (0, ci))],
        out_specs=pl.BlockSpec((1, seq, LANES), lambda bi, ci: (bi, 0, ci)),
        out_shape=jax.ShapeDtypeStruct((b, seq, ncols), F32),
        compiler_params=_cparams(("parallel", "parallel")),
        name="dwconv",
    )(x, w, bias)


def _hyfilt_kernel(mult_ref, w1_ref, b1_ref, fr_ref, w2_ref, b2_ref, w3_ref, dec_ref, bwd_ref, o_ref, *, seq, rows):
    i = pl.program_id(0)
    ridx = lax.broadcasted_iota(I32, (rows, LANES), 0) + i * rows
    lane = lax.broadcasted_iota(I32, (rows, LANES), 1)
    t = ridx.astype(F32) / seq
    ang = t * mult_ref[...]
    feats = jnp.where(lane == 0, t, jnp.where(lane <= HY_BANDS, jnp.sin(ang), jnp.cos(ang)))
    feats = jnp.where(lane < 1 + 2 * HY_BANDS, feats, 0.0)
    fr = fr_ref[...]
    hdn = jnp.sin(fr * (_dot(feats, w1_ref[...], HIGHEST) + b1_ref[...]))
    hdn = jnp.sin(fr * (_dot(hdn, w2_ref[...], HIGHEST) + b2_ref[...]))
    hh, hl = _split_bf16(hdn)
    wh, wl = _split_bf16(w3_ref[...])
    filt = (_dot(hh, wh) + _dot(hl, wh)) + _dot(hh, wl)
    window = jnp.exp(-t[:, 0:1] * jnp.abs(dec_ref[...]))
    out = filt * window
    keep = jnp.logical_or(ridx[:, 0:1] > 0, bwd_ref[...] < 0.5)
    o_ref[...] = jnp.where(keep, out, 0.0)


def hyena_filters(seq, mult, w1p, b1, fr, w2, b2, w3, dec, bwd_mask, rows):
    ncol = w3.shape[1]
    rows = min(rows, seq)
    args = (mult, w1p, b1, fr, w2, b2, w3, dec, bwd_mask)
    return pl.pallas_call(
        functools.partial(_hyfilt_kernel, seq=seq, rows=rows),
        grid=(seq // rows,),
        in_specs=[_full(a.shape) for a in args],
        out_specs=pl.BlockSpec((rows, ncol), lambda i: (i, 0)),
        out_shape=jax.ShapeDtypeStruct((seq, ncol), F32),
        compiler_params=_cparams(("parallel",)),
        name="hyena_filters",
    )(*args)


def _leftmm_kernel(m_ref, x_ref, o_ref):
    o_ref[...] = _dot3(m_ref[...], x_ref[...])


def leftmm(m, x2d, cb):
    r, kdim = m.shape
    n = x2d.shape[1]
    cb = min(cb, n)
    return pl.pallas_call(
        _leftmm_kernel,
        grid=(n // cb,),
        in_specs=[_full((r, kdim)), pl.BlockSpec((kdim, cb), lambda i: (0, i))],
        out_specs=pl.BlockSpec((r, cb), lambda i: (0, i)),
        out_shape=jax.ShapeDtypeStruct((r, n), F32),
        compiler_params=_cparams(("parallel",)),
        name="dft_outer",
    )(m, x2d)


def _stack3_lhs(f):
    hi, lo = _split_bf16(f)
    return jnp.concatenate([hi, hi, lo], axis=1)


def _stack3_rhs(a):
    hi, lo = _split_bf16(a)
    return jnp.concatenate([hi, lo, hi], axis=0)


def _twiddle(ar, ai, twr, twi, conj):
    if conj:
        return ar * twr + ai * twi, ai * twr - ar * twi
    return ar * twr - ai * twi, ai * twr + ar * twi


def _specfilt_kernel(a_ref, fblk_ref, twr_ref, twi_ref, o_ref, lhs_ref):
    n2 = FFT_N2
    c = a_ref.shape[-1] // 2

    @pl.when((pl.program_id(0) == 0) & (pl.program_id(1) == 0))
    def _():
        lhs_ref[...] = _stack3_lhs(fblk_ref[...])

    for kk in range(a_ref.shape[1]):
        br, bi = _twiddle(a_ref[0, kk], a_ref[1, kk], twr_ref[kk], twi_ref[kk], False)
        x = _dot(lhs_ref[...], _stack3_rhs(jnp.concatenate([br, bi], axis=0)))
        o_ref[0, 0, kk] = x[:n2, :c] + x[:n2, c:]
        o_ref[0, 1, kk] = x[n2:, :c] - x[n2:, c:]


def filter_spectrum(a4, fblk, twr_col, twi_col, k1s):
    _, n1, n2, ctot = a4.shape
    c = HY_WIDTH
    return pl.pallas_call(
        _specfilt_kernel,
        grid=(HY_ORDER, n1 // k1s),
        in_specs=[pl.BlockSpec((2, k1s, n2, 2 * c), lambda o, k: (0, k, 0, o)),
                  _full(fblk.shape),
                  pl.BlockSpec((k1s, n2, 1), lambda o, k: (k, 0, 0)),
                  pl.BlockSpec((k1s, n2, 1), lambda o, k: (k, 0, 0))],
        out_specs=pl.BlockSpec((1, 2, k1s, n2, c), lambda o, k: (o, 0, k, 0, 0)),
        out_shape=jax.ShapeDtypeStruct((HY_ORDER, 2, n1, n2, c), F32),
        scratch_shapes=[pltpu.VMEM((2 * n2, 6 * n2), BF16)],
        compiler_params=_cparams(("arbitrary", "arbitrary")),
        name="filter_spectrum",
    )(a4, fblk, twr_col, twi_col)


def _specmul_kernel(a_ref, k_ref, fblk_ref, fblk_t_ref, twr_ref, twi_ref, o_ref, lhs_ref):
    n2 = FFT_N2

    @pl.when(pl.program_id(0) == 0)
    def _():
        lhs_ref[0] = _stack3_lhs(fblk_ref[...])
        lhs_ref[1] = _stack3_lhs(fblk_t_ref[...])

    for kk in range(a_ref.shape[1]):
        twr, twi = twr_ref[kk], twi_ref[kk]
        br, bi = _twiddle(a_ref[0, kk], a_ref[1, kk], twr, twi, False)
        x = _dot(lhs_ref[0], _stack3_rhs(jnp.concatenate([br, bi], axis=0)))
        xr, xi = x[:n2], x[n2:]
        kr, ki = k_ref[0, 0, kk], k_ref[0, 1, kk]
        p = jnp.concatenate([xr * kr - xi * ki, xr * ki + xi * kr], axis=0)
        y = _dot(lhs_ref[1], _stack3_rhs(p))
        yr, yi = _twiddle(y[:n2], y[n2:], twr, twi, True)
        o_ref[0, kk] = yr
        o_ref[1, kk] = yi


def spectrum_multiply(a4, kspec, order, fblk, fblk_t, twr_col, twi_col, k1s):
    _, n1, n2, c = a4.shape
    return pl.pallas_call(
        _specmul_kernel,
        grid=(n1 // k1s,),
        in_specs=[pl.BlockSpec((2, k1s, n2, c), lambda k: (0, k, 0, 0)),
                  pl.BlockSpec((1, 2, k1s, n2, c), lambda k: (order, 0, k, 0, 0)),
                  _full(fblk.shape), _full(fblk_t.shape),
                  pl.BlockSpec((k1s, n2, 1), lambda k: (k, 0, 0)),
                  pl.BlockSpec((k1s, n2, 1), lambda k: (k, 0, 0))],
        out_specs=pl.BlockSpec((2, k1s, n2, c), lambda k: (0, k, 0, 0)),
        out_shape=jax.ShapeDtypeStruct((2, n1, n2, c), F32),
        scratch_shapes=[pltpu.VMEM((2, 2 * n2, 6 * n2), BF16)],
        compiler_params=_cparams(("arbitrary",)),
        name="spectrum_multiply",
    )(a4, kspec, fblk, fblk_t, twr_col, twi_col)


def _convout_kernel(m_ref, y_ref, v_ref, g_ref, bias_ref, *rest, reps):
    y = _dot3(m_ref[...], y_ref[...])
    bias = jnp.concatenate([bias_ref[...]] * reps, axis=-1)
    out = (y + v_ref[...] * bias) * g_ref[...]
    if len(rest) == 1:
        rest[0][...] = out
    else:
        m1_ref, o_ref, a_ref = rest
        o_ref[...] = out
        a_ref[...] = _dot3(m1_ref[...], out)


def conv_output(m3, y2d, v2d, g2d, bias_row, cb, m1_next=None):
    r, kdim = m3.shape
    n = y2d.shape[1]
    cb = min(cb, n)
    c = bias_row.shape[1]
    col = lambda rows: pl.BlockSpec((rows, cb), lambda i: (0, i))
    in_specs = [_full((r, kdim)), col(kdim), col(r), col(r), _full((1, c))]
    out_specs = [col(r)]
    out_shape = [jax.ShapeDtypeStruct((r, n), F32)]
    args = [m3, y2d, v2d, g2d, bias_row]
    if m1_next is not None:
        in_specs.append(_full(m1_next.shape))
        out_specs.append(col(m1_next.shape[0]))
        out_shape.append(jax.ShapeDtypeStruct((m1_next.shape[0], n), F32))
        args.append(m1_next)
    return pl.pallas_call(
        functools.partial(_convout_kernel, reps=cb // c),
        grid=(n // cb,),
        in_specs=in_specs,
        out_specs=out_specs,
        out_shape=out_shape,
        compiler_params=_cparams(("parallel",)),
        name="dft_outer_inverse",
    )(*args)


def _dft_tables(seq, batch):
    n2 = FFT_N2
    half = seq // n2
    n1 = 2 * half
    n = n1 * n2
    k1 = np.arange(n1, dtype=np.float64)[:, None]
    nn1 = np.arange(half, dtype=np.float64)[None, :]
    th = 2.0 * np.pi * k1 * nn1 / n1
    c1, s1 = np.cos(th), np.sin(th)
    assert batch == 2, "the two batch entries are packed as real / imaginary parts"
    m1 = np.block([[c1, s1], [-s1, c1]])
    m3 = np.block([[c1.T, -s1.T], [s1.T, c1.T]]) / n
    mk = np.concatenate([c1, -s1], axis=0)
    kk2 = np.arange(n2, dtype=np.float64)
    th2 = 2.0 * np.pi * np.outer(kk2, kk2) / n2
    c2, s2 = np.cos(th2), np.sin(th2)
    fblk = np.block([[c2, s2], [-s2, c2]])
    tht = 2.0 * np.pi * np.outer(np.arange(n1, dtype=np.float64), kk2) / n
    twr, twi = np.cos(tht), -np.sin(tht)
    f = lambda a: jnp.asarray(a, F32)
    return dict(m1=f(m1), m3=f(m3), mk=f(mk), fblk=f(fblk), fblk_t=f(fblk.T),
                twr_col=f(twr[:, :, None]), twi_col=f(twi[:, :, None]), n1=n1, half=half)


def _ssd_kernel(xf_ref, bf_ref, cf_ref, dtf_ref, xb_ref, bb_ref, cb_ref, dtb_ref, dtbias_ref, a_ref, tril_ref,
                yf_ref, yb_ref, state_ref):
    q = SSM_CHUNK
    hd = SSM_HEADDIM
    ns = SSM_STATE

    @pl.when(pl.program_id(1) == 0)
    def _():
        state_ref[...] = jnp.zeros_like(state_ref)

    tril = tril_ref[...]
    rows = lax.broadcasted_iota(I32, (q, q), 0)
    cols = lax.broadcasted_iota(I32, (q, q), 1)
    a_row = a_ref[...]
    bias = dtbias_ref[...]

    def direction(x_ref, b_ref, c_ref, dt_ref, y_ref, d):
        dt = jax.nn.softplus(dt_ref[0] + bias)
        dta = dt * a_row
        cs = _dot(tril, dta, HIGHEST)
        ecs = cs - dta
        base = ecs if d else cs
        base_t = jnp.transpose(base)
        total = cs[q - 1:q, :]
        x = x_ref[0]
        ys = []
        for g in range(SSM_GROUPS):
            bm = b_ref[0, :, g * ns:(g + 1) * ns]
            cm = c_ref[0, :, g * ns:(g + 1) * ns]
            cb = _dot_nt(cm.astype(BF16), bm.astype(BF16))
            for hh in range(SSM_HEADS // SSM_GROUPS):
                h = g * (SSM_HEADS // SSM_GROUPS) + hh
                j = d * SSM_HEADS + h
                col = jnp.broadcast_to(base[:, j:j + 1], (q, q))
                coln = col if ns == q else jnp.broadcast_to(base[:, j:j + 1], (q, ns))
                row = base_t[j:j + 1, :]
                tot = total[:, j:j + 1]
                if d == 0:
                    seg = jnp.where(rows >= cols, col - row, -jnp.inf)
                    c_scale = jnp.exp(coln)
                    b_scale = jnp.exp(tot - coln)
                else:
                    seg = jnp.where(cols >= rows, row - col, -jnp.inf)
                    c_scale = jnp.exp(tot - coln)
                    b_scale = jnp.exp(coln)
                scores = cb * jnp.exp(seg)
                xdt = (x[:, h * hd:(h + 1) * hd] * jnp.broadcast_to(dt[:, j:j + 1], (q, hd))).astype(BF16)
                st = state_ref[j]
                y = _dot(scores.astype(BF16), xdt) + _dot_nt((cm * c_scale).astype(BF16), st.astype(BF16))
                state_ref[j] = st * jnp.exp(tot) + _dot_tn(xdt, (bm * b_scale).astype(BF16))
                ys.append(y)
        y_ref[0] = jnp.concatenate(ys, axis=-1)

    direction(xf_ref, bf_ref, cf_ref, dtf_ref, yf_ref, 0)
    direction(xb_ref, bb_ref, cb_ref, dtb_ref, yb_ref, 1)


def ssd_scan(xbc, dt_raw, dtbias_row, a_row, tril):
    b, seq, _ = xbc.shape
    q = SSM_CHUNK
    nc = seq // q
    w = SSM_WIDTH
    fwd = lambda col: pl.BlockSpec((1, q, w), lambda bi, i: (bi, i, col))
    bwd = lambda col: pl.BlockSpec((1, q, w), lambda bi, i: (bi, nc - 1 - i, col))
    return pl.pallas_call(
        _ssd_kernel,
        grid=(b, nc),
        in_specs=[fwd(0), fwd(1), fwd(2), pl.BlockSpec((1, q, LANES), lambda bi, i: (bi, i, 0)),
                  bwd(0), bwd(1), bwd(2), pl.BlockSpec((1, q, LANES), lambda bi, i: (bi, nc - 1 - i, 0)),
                  _full((1, LANES)), _full((1, LANES)), _full((q, q))],
        out_specs=[pl.BlockSpec((1, q, w), lambda bi, i: (bi, i, 0)),
                   pl.BlockSpec((1, q, w), lambda bi, i: (bi, nc - 1 - i, 0))],
        out_shape=[jax.ShapeDtypeStruct((b, seq, w), F32)] * 2,
        scratch_shapes=[pltpu.VMEM((2 * SSM_HEADS, SSM_HEADDIM, SSM_STATE), F32)],
        compiler_params=_cparams(("parallel", "arbitrary")),
        name="ssd_scan",
    )(xbc, xbc, xbc, dt_raw, xbc, xbc, xbc, dt_raw, dtbias_row, a_row, tril)


def _outproj_kernel(om_ref, hy_ref, yf_ref, yb_ref, xs_ref, z_ref, x_ref, gm_ref, gh_ref, dsk_ref, gs_ref, wout_ref,
                    gffn_ref, rt_ref, x1_ref, hffn_ref, aff_ref):
    o1 = _rms(om_ref[...], gm_ref[...])
    o2 = _rms(hy_ref[...], gh_ref[...])
    z = z_ref[...]
    y = (yf_ref[...] + yb_ref[...] + xs_ref[...] * dsk_ref[...]) * (z * jax.nn.sigmoid(z))
    gw = SSM_WIDTH // SSM_GROUPS
    gs = gs_ref[...]
    o3 = jnp.concatenate([_rms(y[:, g * gw:(g + 1) * gw], gs[:, g * gw:(g + 1) * gw]) for g in range(SSM_GROUPS)],
                         axis=-1)
    mix = jnp.concatenate([o1, o2, o3], axis=-1).astype(BF16)
    x1 = x_ref[...] + _dot(mix, wout_ref[...])
    x1_ref[...] = x1
    hf = _rms(x1, gffn_ref[...])
    hi, lo = _split_bf16(hf)
    hffn_ref[...] = hi
    logits = _dot_nt(rt_ref[...], jnp.concatenate([hi, lo, hi], axis=1))
    mx = jnp.max(logits, axis=0, keepdims=True)
    ex = jnp.exp(logits - mx)
    aff_ref[0] = ex / jnp.sum(ex, axis=0, keepdims=True)


def outproj(om, hy, yf, yb, xbc, z, x2d, gm, gh, dsk, gs, wout, gffn, router_t, batch, tm):
    t = x2d.shape[0]
    seq = t // batch
    nb = seq // tm
    row = lambda w: pl.BlockSpec((tm, w), lambda i: (i, 0))
    return pl.pallas_call(
        _outproj_kernel,
        grid=(t // tm,),
        in_specs=[row(om.shape[1]), row(HY_WIDTH), row(SSM_WIDTH), row(SSM_WIDTH), row(SSM_WIDTH), row(SSM_WIDTH),
                  row(D_MODEL), _full(gm.shape), _full(gh.shape), _full(dsk.shape), _full(gs.shape),
                  _full(wout.shape), _full(gffn.shape), _full(router_t.shape)],
        out_specs=[row(D_MODEL), row(D_MODEL),
                   pl.BlockSpec((1, N_EXPERTS, tm), lambda i: (i // nb, 0, i % nb))],
        out_shape=[jax.ShapeDtypeStruct((t, D_MODEL), F32), jax.ShapeDtypeStruct((t, D_MODEL), BF16),
                   jax.ShapeDtypeStruct((batch, N_EXPERTS, seq), F32)],
        compiler_params=_cparams(("parallel",)),
        name="outproj_router",
    )(om, hy, yf, yb, xbc, z, x2d, gm, gh, dsk, gs, wout, gffn, router_t)


def _select_kernel(aff_ref, tri_ref, ones_ref, blk_ref, pos_ref, g_ref, off_ref, *, cap, nrows):
    aff = aff_ref[0]
    er = aff.shape[0]
    ne = er // nrows
    aff3 = aff.reshape(ne, nrows, LANES)
    capf = jnp.float32(cap)

    def count(mask3):
        return jnp.sum(jnp.where(mask3, 1.0, 0.0), axis=(1, 2), keepdims=True)

    def enough(cand):
        return count(aff3 >= cand) >= capf

    top = jnp.full((ne, 1, 1), 2.0, F32)
    for shift in (64, 32, 16, 8, 4, 2, 1):
        cand = top * (2.0 ** -shift)
        top = jnp.where(enough(cand), top, cand)
    p = top * 0.5

    def refine(_, carry):
        lo, step = carry
        cand = lo + step
        return jnp.where(enough(cand), cand, lo), step * 0.5

    lo, _ = lax.fori_loop(0, MANTISSA_STEPS, refine, (p, p * 0.5))
    thr = jnp.min(jnp.where(aff3 >= lo, aff3, jnp.inf), axis=(1, 2), keepdims=True)
    gt3 = aff3 > thr
    eq3 = aff3 == thr
    need = capf - count(gt3)

    tri = tri_ref[...]
    ones = ones_ref[...]
    blk = blk_ref[...]

    def prefix(maskf):
        mb = maskf.astype(BF16)
        within = _dot(mb, tri)
        rowtot = _dot(mb, ones)
        before = _dot(blk, rowtot.astype(BF16))
        return within + before, before

    eqf = jnp.where(eq3, 1.0, 0.0).reshape(er, LANES)
    tie_incl, _ = prefix(eqf)
    tie_rank = (tie_incl - eqf).reshape(ne, nrows, LANES)
    sel3 = jnp.logical_or(gt3, jnp.logical_and(eq3, tie_rank < need))
    self_ = jnp.where(sel3, 1.0, 0.0).reshape(er, LANES)
    incl, before = prefix(self_)
    sel = self_ > 0.5
    pos_ref[0] = jnp.where(sel, (incl - self_).astype(I32), -1)
    g_ref[0] = jnp.where(sel, aff, 0.0)
    off_ref[0] = before.astype(I32)


def moe_select(aff, cap):
    b, ne, seq = aff.shape
    nrows = seq // LANES
    er = ne * nrows
    tri = jnp.asarray(np.triu(np.ones((LANES, LANES), np.float32)), BF16)
    ones = jnp.ones((LANES, LANES), BF16)
    ridx = np.arange(er)
    blk = (ridx[:, None] // nrows == ridx[None, :] // nrows) & (ridx[None, :] < ridx[:, None])
    blk = jnp.asarray(blk.astype(np.float32), BF16)
    spec = pl.BlockSpec((1, er, LANES), lambda bi: (bi, 0, 0))
    pos, gsel, off = pl.pallas_call(
        functools.partial(_select_kernel, cap=cap, nrows=nrows),
        grid=(b,),
        in_specs=[spec, _full(tri.shape), _full(ones.shape), _full(blk.shape)],
        out_specs=[spec] * 3,
        out_shape=[jax.ShapeDtypeStruct((b, er, LANES), I32), jax.ShapeDtypeStruct((b, er, LANES), F32),
                   jax.ShapeDtypeStruct((b, er, LANES), I32)],
        compiler_params=_cparams(("parallel",)),
        name="moe_select",
    )(aff.reshape(b, er, LANES), tri, ones, blk)
    return pos.reshape(b, ne, seq), gsel.reshape(b, ne, seq), off[:, :, 0].reshape(b, ne, nrows)


def _gather_kernel(offs_ref, h_hbm, pos_ref, g_ref, o_ref, og_ref, h_ref, sem_ref, *, tb, cap):
    bi = pl.program_id(0)
    gi = pl.program_id(1)
    seq, d = h_ref.shape
    ng = o_ref.shape[1]
    al = COMBINE_ALIGN

    @pl.when(gi == 0)
    def _():
        copy = pltpu.make_async_copy(h_hbm.at[bi], h_ref, sem_ref)
        copy.start()
        copy.wait()

    for e in range(ng):
        o_ref[0, e, 0:al, :] = jnp.zeros((al, d), BF16)
        og_ref[0, e, 0:al, :] = jnp.zeros((al, 1), F32)
    wins = sorted({min(64, tb + al), min(128, tb + al), tb + al})

    def body(j, carry):
        t0 = pl.multiple_of(j * tb, tb)
        offs = [pl.multiple_of(offs_ref[bi, gi * ng + e, j] * al, al) for e in range(ng)]
        needs = [offs_ref[bi, gi * ng + e, j + 1] * al + al - offs[e] for e in range(ng)]
        need = functools.reduce(jnp.maximum, needs)

        def place(win):
            riota = lax.broadcasted_iota(I32, (win, tb), 0)
            hits = [riota + offs[e] == pos_ref[0, 0, e:e + 1, pl.ds(t0, tb)] for e in range(ng)]
            onehot = jnp.concatenate([jnp.where(hit, 1.0, 0.0).astype(BF16) for hit in hits], axis=0)
            rows = _dot(onehot, h_ref[pl.ds(t0, tb), :])
            for e in range(ng):
                r = rows[e * win:(e + 1) * win]
                gates = jnp.sum(jnp.where(hits[e], g_ref[0, 0, e:e + 1, pl.ds(t0, tb)], 0.0), axis=1, keepdims=True)
                head = o_ref[0, e, pl.ds(offs[e], al), :].astype(F32) + r[0:al]
                o_ref[0, e, pl.ds(offs[e], al), :] = head.astype(BF16)
                o_ref[0, e, pl.ds(offs[e] + al, win - al), :] = r[al:].astype(BF16)
                og_ref[0, e, pl.ds(offs[e], al), :] += gates[0:al]
                og_ref[0, e, pl.ds(offs[e] + al, win - al), :] = gates[al:]

        lo = 0
        for win in wins:
            fits = need <= win if win != wins[-1] else True
            pl.when(jnp.logical_and(need > lo, fits))(functools.partial(place, win))
            lo = win
        return carry

    lax.fori_loop(0, seq // tb, body, 0)
    tail = o_ref.shape[2] - cap
    o_ref[0, :, cap:, :] = jnp.zeros((ng, tail, d), BF16)
    og_ref[0, :, cap:, :] = jnp.zeros((ng, tail, 1), F32)


def moe_gather(hffn, pos, gsel, offs, cap, tb, ng):
    b, seq, d = hffn.shape
    ne = pos.shape[1]
    rows = cap + tb + COMBINE_ALIGN
    assert cap % COMBINE_ALIGN == 0 and ne % ng == 0
    row_spec = pl.BlockSpec((1, 1, ng, seq), lambda bi, gi, offs: (bi, gi, 0, 0))
    grid_spec = pltpu.PrefetchScalarGridSpec(
        num_scalar_prefetch=1,
        grid=(b, ne // ng),
        in_specs=[pl.BlockSpec(memory_space=pl.ANY), row_spec, row_spec],
        out_specs=[pl.BlockSpec((1, ng, rows, d), lambda bi, gi, offs: (bi, gi, 0, 0)),
                   pl.BlockSpec((1, ng, rows, 1), lambda bi, gi, offs: (bi, gi, 0, 0))],
        scratch_shapes=[pltpu.VMEM((seq, d), BF16), pltpu.SemaphoreType.DMA(())],
    )
    return pl.pallas_call(
        functools.partial(_gather_kernel, tb=tb, cap=cap),
        grid_spec=grid_spec,
        out_shape=[jax.ShapeDtypeStruct((b, ne, rows, d), BF16), jax.ShapeDtypeStruct((b, ne, rows, 1), F32)],
        compiler_params=_cparams(("arbitrary", "arbitrary")),
        name="moe_gather",
    )(offs, hffn, pos.reshape(b, ne // ng, ng, seq), gsel.reshape(b, ne // ng, ng, seq))


def _ffn_kernel(xe_ref, gs_ref, wg_ref, wu_ref, wd_ref, o_ref, acc_ref, *, cap):
    f = pl.program_id(1)
    nb = xe_ref.shape[0]
    d = xe_ref.shape[-1]
    xe = xe_ref[...].reshape(nb * cap, d)
    a = _dot(xe, wg_ref[...].astype(BF16))
    u = _dot(xe, wu_ref[...].astype(BF16))
    hid = (a * jax.nn.sigmoid(a) * u).astype(BF16)
    part = _dot(hid, wd_ref[...].astype(BF16))

    @pl.when(f == 0)
    def _():
        acc_ref[...] = part

    @pl.when(f > 0)
    def _():
        acc_ref[...] += part

    @pl.when(f == pl.num_programs(1) - 1)
    def _():
        gated = acc_ref[...] * gs_ref[...].reshape(nb * cap, 1)
        o_ref[:, 0:cap, :] = gated.reshape(nb, cap, d).astype(BF16)
        o_ref[:, cap:, :] = jnp.zeros((nb, o_ref.shape[1] - cap, d), BF16)


def moe_ffn(xe, gslot, w_gate, w_up, w_down, layer, cap, cap_pad, tf):
    b, ne, _, d = xe.shape
    ff = w_gate.shape[-1]
    return pl.pallas_call(
        functools.partial(_ffn_kernel, cap=cap),
        grid=(ne, ff // tf),
        in_specs=[pl.BlockSpec((b, None, cap, d), lambda e, f: (0, e, 0, 0)),
                  pl.BlockSpec((b, None, cap, 1), lambda e, f: (0, e, 0, 0)),
                  pl.BlockSpec((None, None, d, tf), lambda e, f: (layer, e, 0, f)),
                  pl.BlockSpec((None, None, d, tf), lambda e, f: (layer, e, 0, f)),
                  pl.BlockSpec((None, None, tf, d), lambda e, f: (layer, e, f, 0))],
        out_specs=pl.BlockSpec((b, None, cap_pad, d), lambda e, f: (0, e, 0, 0)),
        out_shape=jax.ShapeDtypeStruct((b, ne, cap_pad, d), BF16),
        scratch_shapes=[pltpu.VMEM((b * cap, d), F32)],
        compiler_params=_cparams(("parallel", "arbitrary")),
        name="moe_ffn",
    )(xe, gslot, w_gate, w_up, w_down)


def _combine_kernel(offs_ref, ye_hbm, pos_ref, x1_ref, p_ref, gple_ref, wgate_ref, wproj_ref, gfin_ref, o_ref,
                    buf_ref, sem_ref, *, tb, win, final):
    bi = pl.program_id(0)
    j = pl.program_id(1)
    nj = pl.num_programs(1)
    ne = pos_ref.shape[1]
    step = bi * nj + j
    slot = step % 2

    def window_copy(b_, j_, e, slot_):
        off = pl.multiple_of(offs_ref[b_, e, j_] * COMBINE_ALIGN, COMBINE_ALIGN)
        return pltpu.make_async_copy(ye_hbm.at[b_, e, pl.ds(off, win), :],
                                     buf_ref.at[slot_, pl.ds(e * win, win), :], sem_ref.at[slot_, e])

    @pl.when(step == 0)
    def _():
        for e in range(ne):
            window_copy(bi, j, e, slot).start()

    @pl.when(step + 1 < pl.num_programs(0) * nj)
    def _():
        wrap = j + 1 == nj
        b_next = jnp.where(wrap, bi + 1, bi)
        j_next = jnp.where(wrap, 0, j + 1)
        for e in range(ne):
            window_copy(b_next, j_next, e, 1 - slot).start()

    def token_major(a):
        pad = jnp.zeros((tb - ne, tb), a.dtype)
        return jnp.transpose(jnp.concatenate([a, pad], axis=0))[:, :ne]

    pos_all = token_major(pos_ref[0])
    lane = lax.broadcasted_iota(I32, (tb, LANES), 1)
    targets = []
    for e in range(ne):
        col = pos_all[:, e:e + 1]
        delta = e * win - offs_ref[bi, e, j] * COMBINE_ALIGN
        targets.append(jnp.broadcast_to(jnp.where(col >= 0, col + delta, -1), (tb, LANES)))
    tiles = []
    for k in range(ne * win // LANES):
        hits = [jnp.where(targets[e] == lane + k * LANES, 1.0, 0.0) for e in range(ne)
                if e * win < (k + 1) * LANES and (e + 1) * win > k * LANES]
        tiles.append(functools.reduce(lambda a, b: a + b, hits).astype(BF16))
    sel = jnp.concatenate(tiles, axis=1)
    for e in range(ne):
        window_copy(bi, j, e, slot).wait()
    x2 = x1_ref[0] + _dot(sel, buf_ref[slot])
    hp = _rms(x2, gple_ref[...]).astype(BF16)
    gt = jax.nn.sigmoid(_dot(hp, wgate_ref[...]))
    x3 = x2 + _dot(p_ref[0].astype(BF16), wproj_ref[...]) * gt
    o_ref[0] = _rms(x3, gfin_ref[...]) if final else x3


def moe_combine(ye, pos, offs, x1, p, layer, gple, wgate, wproj, gfin, final, tb):
    b, ne, _, d = ye.shape
    seq = x1.shape[1]
    win = tb + COMBINE_ALIGN
    assert (ne * win) % LANES == 0
    grid_spec = pltpu.PrefetchScalarGridSpec(
        num_scalar_prefetch=1,
        grid=(b, seq // tb),
        in_specs=[pl.BlockSpec(memory_space=pl.ANY),
                  pl.BlockSpec((1, ne, tb), lambda bi, j, offs: (bi, 0, j)),
                  pl.BlockSpec((1, tb, d), lambda bi, j, offs: (bi, j, 0)),
                  pl.BlockSpec((None, 1, tb, p.shape[-1]), lambda bi, j, offs: (layer, bi, j, 0)),
                  pl.BlockSpec(gple.shape, lambda bi, j, offs: (0, 0)),
                  pl.BlockSpec(wgate.shape, lambda bi, j, offs: (0, 0)),
                  pl.BlockSpec(wproj.shape, lambda bi, j, offs: (0, 0)),
                  pl.BlockSpec(gfin.shape, lambda bi, j, offs: (0, 0))],
        out_specs=pl.BlockSpec((1, tb, d), lambda bi, j, offs: (bi, j, 0)),
        scratch_shapes=[pltpu.VMEM((2, ne * win, d), BF16), pltpu.SemaphoreType.DMA((2, ne))],
    )
    return pl.pallas_call(
        functools.partial(_combine_kernel, tb=tb, win=win, final=final),
        grid_spec=grid_spec,
        out_shape=jax.ShapeDtypeStruct((b, seq, d), F32),
        compiler_params=_cparams(("arbitrary", "arbitrary")),
        name="moe_combine_ple",
    )(offs, ye, pos, x1, p, gple, wgate, wproj, gfin)


def _pad_cols(a, width):
    return jnp.pad(a, ((0, 0), (0, width - a.shape[1])))


def _pack_inproj(w_in):
    offs = np.cumsum((0,) + IN_SPLITS)
    cq, ckv, kr, hy, z, xbc, dt = [w_in[:, offs[i]:offs[i + 1]] for i in range(len(IN_SPLITS))]
    d = w_in.shape[0]
    half = MLA_ROPE // 2
    zeros = lambda n: jnp.zeros((d, n), w_in.dtype)
    kr_pad = jnp.concatenate([zeros(MLA_NOPE), kr, zeros(HEAD_PAD - MLA_NOPE - MLA_ROPE)], axis=1)
    kr_swap = jnp.concatenate([zeros(MLA_NOPE), -kr[:, half:], kr[:, :half], zeros(HEAD_PAD - MLA_NOPE - MLA_ROPE)],
                              axis=1)
    wall = jnp.concatenate([cq, ckv, kr_pad, kr_swap, hy, z, xbc, _pad_cols(dt, LANES)], axis=1)
    assert wall.shape[1] == _C_END
    return wall.astype(BF16)


def _pack_mla(w_uq, w_ukv):
    lq = w_uq.shape[0]
    lkv = w_ukv.shape[0]
    half = MLA_ROPE // 2
    padw = HEAD_PAD - MLA_NOPE - MLA_ROPE
    q3 = w_uq.reshape(lq, MLA_HEADS, MLA_NOPE + MLA_ROPE)
    nope, rope = q3[..., :MLA_NOPE], q3[..., MLA_NOPE:]
    zq = jnp.zeros((lq, MLA_HEADS, padw), w_uq.dtype)
    wq = jnp.concatenate([nope, rope, zq], axis=-1).reshape(lq, MLA_HEADS * HEAD_PAD)
    wqs = jnp.concatenate([jnp.zeros_like(nope), -rope[..., half:], rope[..., :half], zq], axis=-1)
    wqs = wqs.reshape(lq, MLA_HEADS * HEAD_PAD)
    kv3 = w_ukv.reshape(lkv, MLA_HEADS, MLA_NOPE + MLA_V)
    knope, vv = kv3[..., :MLA_NOPE], kv3[..., MLA_NOPE:]
    wk = jnp.concatenate([knope, jnp.zeros((lkv, MLA_HEADS, HEAD_PAD - MLA_NOPE), w_ukv.dtype)], axis=-1)
    wk = wk.reshape(lkv, MLA_HEADS * HEAD_PAD)
    wv = vv.reshape(lkv, MLA_HEADS * MLA_V).T
    return wq.astype(BF16), wqs.astype(BF16), wk.astype(BF16), wv.astype(BF16)


def _row(a):
    return a.reshape(1, -1).astype(F32)


TM_PROJ = 512
TQ_ATTN = 512
TK_ATTN = 512
ROWS_CONV = 512
ROWS_FILT = 512
CB_DFT = 2048
TM_OUT = 512
K1_PER_STEP = 16
TB_MOE = 256
TB_COMBINE = 128
GATHER_GROUP = 4
TF_FFN = 512


def _hyena(hy_u, tabs, kspec, conv_w, conv_b, bias):
    b, seq, _ = hy_u.shape
    c = HY_WIDTH
    parts = [dwconv(hy_u, conv_w[:, i * c:(i + 1) * c], _row(conv_b[i * c:(i + 1) * c]), i * c, c, False, ROWS_CONV)
             for i in range(HY_ORDER + 1)]
    gates, v = parts[:-1], parts[-1]
    half, n1 = tabs["half"], tabs["n1"]
    flat = lambda a: a.reshape(b * half, FFT_N2 * c)
    a = leftmm(tabs["m1"], flat(v), CB_DFT)
    for o in range(HY_ORDER):
        y = spectrum_multiply(a.reshape(2, n1, FFT_N2, c), kspec, o, tabs["fblk"], tabs["fblk_t"], tabs["twr_col"],
                              tabs["twi_col"], K1_PER_STEP)
        last = o == HY_ORDER - 1
        outs = conv_output(tabs["m3"], y.reshape(2 * n1, FFT_N2 * c), flat(v), flat(gates[o]), _row(bias[o]), CB_DFT,
                           None if last else tabs["m1"])
        v = outs[0].reshape(b, seq, c)
        a = None if last else outs[1]
    return v


def _hyena_kspec(seq, tabs, w1, b1, freq, w2, b2, w3, decay):
    bands = np.arange(1, HY_BANDS + 1, dtype=np.float64) * 2.0 * np.pi
    mult = np.zeros((1, LANES), np.float32)
    mult[0, 1:1 + HY_BANDS] = bands
    mult[0, 1 + HY_BANDS:1 + 2 * HY_BANDS] = bands
    w1p = jnp.pad(w1.astype(F32), ((0, LANES - w1.shape[0]), (0, 0)))
    ncol = HY_ORDER * 2 * HY_WIDTH
    bwd = (np.arange(ncol) // HY_WIDTH) % 2
    kf = hyena_filters(seq, jnp.asarray(mult), w1p, _row(b1), _row(freq), w2.astype(F32), _row(b2), w3.astype(F32),
                       _row(decay), jnp.asarray(bwd.astype(np.float32)).reshape(1, ncol), ROWS_FILT)
    a = leftmm(tabs["mk"], kf.reshape(tabs["half"], FFT_N2 * ncol), CB_DFT)
    a4 = a.reshape(2, tabs["n1"], FFT_N2, ncol)
    return filter_spectrum(a4, tabs["fblk"], tabs["twr_col"], tabs["twi_col"], K1_PER_STEP)


def kernel(x, p, positions, norm_mix, w_in, mla_q_norm, mla_w_uq, mla_kv_norm, mla_w_ukv, mla_out_norm, hy_conv_w,
           hy_conv_b, hy_filt_w1, hy_filt_b1, hy_filt_freq, hy_filt_w2, hy_filt_b2, hy_filt_w3, hy_decay, hy_bias,
           hy_out_norm, ssm_conv_w, ssm_conv_b, ssm_dt_bias, ssm_a_log, ssm_d, ssm_norm, w_out, norm_ffn, moe_router,
           moe_w_gate, moe_w_up, moe_w_down, ple_norm, ple_gate_w, ple_proj, final_norm_g):
    batch, seq, d = x.shape
    depth = w_in.shape[0]
    t = batch * seq
    cap = EC_CAPACITY_FACTOR * seq // N_EXPERTS
    tb = min(TB_MOE, seq)
    tbc = min(TB_COMBINE, seq)
    cap_pad = cap + tb
    tm_proj = min(TM_PROJ, seq)
    tm_out = min(TM_OUT, seq)

    freq = np.zeros((1, HEAD_PAD), np.float32)
    inv = ROPE_THETA ** (-np.arange(0, MLA_ROPE, 2, dtype=np.float32) / MLA_ROPE)
    freq[0, MLA_NOPE:MLA_NOPE + MLA_ROPE // 2] = inv
    freq[0, MLA_NOPE + MLA_ROPE // 2:MLA_NOPE + MLA_ROPE] = inv
    cos_t, sin_t = rope_tables(positions.reshape(t, 1), jnp.asarray(freq), tm_proj)

    tabs = _dft_tables(seq, batch)
    tril = jnp.asarray(np.tril(np.ones((SSM_CHUNK, SSM_CHUNK), np.float32)))

    x2d = x.reshape(t, d)
    for i in range(depth):
        wall = _pack_inproj(w_in[i])
        wq, wqs, wk, wv = _pack_mla(mla_w_uq[i], mla_w_ukv[i])
        q, k, v, hy_u, z, xbc_raw, dt_raw = inproj(x2d, _row(norm_mix[i]), wall, _row(mla_q_norm[i]), wq, wqs,
                                                    _row(mla_kv_norm[i]), wk, wv, cos_t, sin_t, tm_proj)
        o_mla = attention(q.reshape(batch, seq, -1), k.reshape(batch, seq, -1), v, min(TQ_ATTN, seq),
                          min(TK_ATTN, seq))

        kspec = _hyena_kspec(seq, tabs, hy_filt_w1[i], hy_filt_b1[i], hy_filt_freq[i], hy_filt_w2[i], hy_filt_b2[i],
                             hy_filt_w3[i], hy_decay[i])
        o_hy = _hyena(hy_u.reshape(batch, seq, -1), tabs, kspec, hy_conv_w[i], hy_conv_b[i], hy_bias[i])

        xbc = dwconv(xbc_raw.reshape(batch, seq, -1), ssm_conv_w[i], _row(ssm_conv_b[i]), 0, SSM_CONV_DIM, True,
                     ROWS_CONV)
        dtbias_row = _pad_cols(_row(ssm_dt_bias[i]), LANES)
        a_row = _pad_cols(_row(-jnp.exp(ssm_a_log[i].astype(F32))), LANES)
        y_f, y_b = ssd_scan(xbc, dt_raw.reshape(batch, seq, -1), dtbias_row, a_row, tril)

        dsk = _row(jnp.repeat(ssm_d[i].astype(F32), SSM_HEADDIM))
        x1, hffn, aff = outproj(o_mla.reshape(t, -1), o_hy.reshape(t, -1), y_f.reshape(t, -1), y_b.reshape(t, -1),
                                xbc.reshape(t, -1), z, x2d, _row(mla_out_norm[i]), _row(hy_out_norm[i]), dsk,
                                _row(ssm_norm[i]), w_out[i].astype(BF16), _row(norm_ffn[i]),
                                _stack3_lhs(moe_router[i].astype(F32).T), batch, tm_out)

        pos, gsel, rowoff = moe_select(aff, cap)
        offs_g = jnp.concatenate([rowoff[:, :, ::tb // LANES] // COMBINE_ALIGN,
                                  jnp.full((batch, N_EXPERTS, 1), pl.cdiv(cap, COMBINE_ALIGN), I32)], axis=-1)
        offs_c = rowoff[:, :, ::tbc // LANES] // COMBINE_ALIGN
        xe, gslot = moe_gather(hffn.reshape(batch, seq, d), pos, gsel, offs_g, cap, tb, GATHER_GROUP)
        ye = moe_ffn(xe, gslot, moe_w_gate, moe_w_up, moe_w_down, i, cap, cap_pad, TF_FFN)
        x3 = moe_combine(ye, pos, offs_c, x1.reshape(batch, seq, d), p, i, _row(ple_norm[i]),
                         ple_gate_w[i].astype(BF16), ple_proj[i].astype(BF16), _row(final_norm_g), i == depth - 1, tbc)
        x2d = x3.reshape(t, d)
    return x2d.reshape(batch, seq, d)
```

```python
import functools
import math

import numpy as np
import jax
import jax.numpy as jnp
from jax import lax
from jax.experimental import pallas as pl
from jax.experimental.pallas import tpu as pltpu

F32 = jnp.float32
BF16 = jnp.bfloat16
I32 = jnp.int32
HIGHEST = lax.Precision.HIGHEST

EPS = 1e-6
LANES = 128
SUBLANES = 8
COMBINE_ALIGN = 16
VMEM_LIMIT = 56 * 1024 * 1024

D_MODEL = 1024
MLA_HEADS = 8
MLA_NOPE = 64
MLA_ROPE = 32
MLA_V = 64
MLA_Q_LORA = 256
MLA_KV_LORA = 128
HEAD_PAD = 128
ATTN_SUM_ROWS = 16
ROPE_THETA = 10000.0
HY_WIDTH = 256
HY_ORDER = 2
HY_SHORT = 3
HY_BANDS = 8
HY_FILT_HID = 64
SSM_WIDTH = 256
SSM_HEADDIM = 64
SSM_HEADS = 4
SSM_GROUPS = 2
SSM_STATE = 128
SSM_CONV = 5
SSM_CHUNK = 128
SSM_CONV_DIM = 768
N_EXPERTS = 16
EXPERT_FF = 2048
EC_CAPACITY_FACTOR = 2
PLE_DIM = 256
FFT_N2 = 128
MANTISSA_STEPS = 40
IN_SPLITS = (MLA_Q_LORA, MLA_KV_LORA, MLA_ROPE, 3 * HY_WIDTH, SSM_WIDTH, SSM_CONV_DIM, 2 * SSM_HEADS)


def _cparams(sem, vmem=None):
    return pltpu.CompilerParams(dimension_semantics=sem, vmem_limit_bytes=vmem or VMEM_LIMIT)


def _rms(x, g):
    ms = jnp.mean(x * x, axis=-1, keepdims=True)
    return x * lax.rsqrt(ms + EPS) * g


def _dot(a, b, precision=None):
    return jnp.dot(a, b, preferred_element_type=F32, precision=precision)


def _dot_nt(a, b, precision=None):
    return lax.dot_general(a, b, (((1,), (1,)), ((), ())), preferred_element_type=F32, precision=precision)


def _dot_tn(a, b, precision=None):
    return lax.dot_general(a, b, (((0,), (0,)), ((), ())), preferred_element_type=F32, precision=precision)


def _split_bf16(x):
    hi = x.astype(BF16)
    return hi, (x - hi.astype(F32)).astype(BF16)


def _dot3(a, b):
    ah, al = _split_bf16(a)
    bh, bl = _split_bf16(b)
    return _dot(jnp.concatenate([ah, ah, al], axis=1), jnp.concatenate([bh, bl, bh], axis=0))


def _full(shape):
    n = len(shape)
    return pl.BlockSpec(shape, lambda *_: (0,) * n)


def _rope_kernel(pos_ref, freq_ref, cos_ref, sin_ref):
    ang = pos_ref[...].astype(F32) * freq_ref[...]
    cos_ref[...] = jnp.cos(ang)
    sin_ref[...] = jnp.sin(ang)


def rope_tables(pos_col, freq_row, tm):
    t = pos_col.shape[0]
    return pl.pallas_call(
        _rope_kernel,
        grid=(t // tm,),
        in_specs=[pl.BlockSpec((tm, 1), lambda i: (i, 0)), _full((1, HEAD_PAD))],
        out_specs=[pl.BlockSpec((tm, HEAD_PAD), lambda i: (i, 0))] * 2,
        out_shape=[jax.ShapeDtypeStruct((t, HEAD_PAD), F32)] * 2,
        compiler_params=_cparams(("parallel",)),
        name="rope_tables",
    )(pos_col, freq_row)


_C_CQ = 0
_C_CKV = 256
_C_KR = 384
_C_KRS = 512
_C_HY = 640
_C_Z = 1408
_C_XBC = 1664
_C_DT = 2432
_C_END = 2560


def _inproj_kernel(x_ref, gmix_ref, wall_ref, qn_ref, wq_ref, wqs_ref, kvn_ref, wk_ref, wv_ref, cos_ref, sin_ref,
                   q_ref, k_ref, v_ref, hy_ref, z_ref, xbc_ref, dt_ref, *, scale):
    h = _rms(x_ref[...], gmix_ref[...]).astype(BF16)
    proj = _dot(h, wall_ref[...])
    hy_ref[...] = proj[:, _C_HY:_C_Z]
    z_ref[...] = proj[:, _C_Z:_C_XBC]
    xbc_ref[...] = proj[:, _C_XBC:_C_DT]
    dt_ref[...] = proj[:, _C_DT:_C_END]
    cos = cos_ref[...]
    sin = sin_ref[...]
    cos8 = jnp.concatenate([cos] * MLA_HEADS, axis=-1)
    sin8 = jnp.concatenate([sin] * MLA_HEADS, axis=-1)
    cqn = _rms(proj[:, _C_CQ:_C_CKV], qn_ref[...]).astype(BF16)
    q = _dot(cqn, wq_ref[...])
    qs = _dot(cqn, wqs_ref[...])
    q_ref[...] = ((q * cos8 + qs * sin8) * scale).astype(BF16)
    ckvn = _rms(proj[:, _C_CKV:_C_KR], kvn_ref[...]).astype(BF16)
    kn = _dot(ckvn, wk_ref[...])
    v_ref[...] = _dot_nt(wv_ref[...], ckvn).astype(BF16)
    kr = proj[:, _C_KR:_C_KRS] * cos + proj[:, _C_KRS:_C_HY] * sin
    k_ref[...] = (kn + jnp.concatenate([kr] * MLA_HEADS, axis=-1)).astype(BF16)


def inproj(x2d, gmix, wall, qn, wq, wqs, kvn, wk, wv, cos_t, sin_t, tm):
    t = x2d.shape[0]
    hq = MLA_HEADS * HEAD_PAD
    row = lambda w: pl.BlockSpec((tm, w), lambda i: (i, 0))
    outs = [(hq, BF16), (hq, BF16), None, (3 * HY_WIDTH, F32), (SSM_WIDTH, F32), (SSM_CONV_DIM, F32), (LANES, F32)]
    hv = MLA_HEADS * MLA_V
    return pl.pallas_call(
        functools.partial(_inproj_kernel, scale=(MLA_NOPE + MLA_ROPE) ** -0.5 * math.log2(math.e)),
        grid=(t // tm,),
        in_specs=[row(D_MODEL), _full(gmix.shape), _full(wall.shape), _full(qn.shape), _full(wq.shape),
                  _full(wqs.shape), _full(kvn.shape), _full(wk.shape), _full(wv.shape), row(HEAD_PAD), row(HEAD_PAD)],
        out_specs=[row(o[0]) if o else pl.BlockSpec((hv, tm), lambda i: (0, i)) for o in outs],
        out_shape=[jax.ShapeDtypeStruct((t, o[0]), o[1]) if o else jax.ShapeDtypeStruct((hv, t), BF16)
                   for o in outs],
        compiler_params=_cparams(("parallel",)),
        name="inproj",
    )(x2d, gmix, wall, qn, wq, wqs, kvn, wk, wv, cos_t, sin_t)


def _attn_kernel(q_ref, k_ref, vt_ref, o_ref, st_ref, *, tk):
    seq = k_ref.shape[1]
    tq = q_ref.shape[1]
    nh = st_ref.shape[0]
    npairs = seq // (2 * tk)

    def scores(hh, c, slot):
        off = pl.multiple_of(c * tk, tk)
        st = _dot_nt(k_ref[0, pl.ds(off, tk), hh * HEAD_PAD:(hh + 1) * HEAD_PAD],
                     q_ref[0, :, hh * HEAD_PAD:(hh + 1) * HEAD_PAD])
        st_ref[hh, slot] = st
        return jnp.max(st, axis=0, keepdims=True)

    ones_rows = jnp.ones((ATTN_SUM_ROWS, tk), BF16)

    def update(hh, c, slot, m, acc, smax):
        off = pl.multiple_of(c * tk, tk)
        vtc = jnp.concatenate([vt_ref[hh * MLA_V:(hh + 1) * MLA_V, pl.ds(off, tk)], ones_rows], axis=0)
        m_new = jnp.maximum(m, smax)
        alpha = jnp.exp2(m - m_new)
        p = jnp.exp2(st_ref[hh, slot] - m_new)
        acc = acc * alpha + _dot(vtc, p.astype(BF16))
        return m_new, acc

    def pair(i, carry, last):
        new = []
        for hh in range(nh):
            m, acc, smax0 = carry[hh]
            smax1 = scores(hh, 2 * i + 1, 1)
            m, acc = update(hh, 2 * i, 0, m, acc, smax0)
            smax0 = smax1 if last else scores(hh, 2 * i + 2, 0)
            m, acc = update(hh, 2 * i + 1, 1, m, acc, smax1)
            new.append((m, acc, smax0))
        return tuple(new)

    init = tuple((jnp.full((1, tq), -jnp.inf, F32), jnp.zeros((MLA_V + ATTN_SUM_ROWS, tq), F32), scores(hh, 0, 0))
                 for hh in range(nh))
    carry = lax.fori_loop(0, npairs - 1, functools.partial(pair, last=False), init)
    final = pair(npairs - 1, carry, True)
    o_ref[0] = jnp.concatenate([jnp.transpose(acc[:MLA_V] / acc[MLA_V:MLA_V + 1]) for _, acc, _ in final], axis=-1)


def attention(q, k, vt, tq, tk):
    b, seq, _ = q.shape
    return pl.pallas_call(
        functools.partial(_attn_kernel, tk=tk),
        grid=(b, MLA_HEADS // 2, seq // tq),
        in_specs=[pl.BlockSpec((1, tq, 2 * HEAD_PAD), lambda bi, hp, qi: (bi, qi, hp)),
                  pl.BlockSpec((1, seq, 2 * HEAD_PAD), lambda bi, hp, qi: (bi, 0, hp)),
                  pl.BlockSpec((2 * MLA_V, seq), lambda bi, hp, qi: (hp, bi))],
        out_specs=pl.BlockSpec((1, tq, 2 * MLA_V), lambda bi, hp, qi: (bi, qi, hp)),
        out_shape=jax.ShapeDtypeStruct((b, seq, MLA_HEADS * MLA_V), F32),
        scratch_shapes=[pltpu.VMEM((2, 2, tk, tq), F32)],
        compiler_params=_cparams(("parallel", "parallel", "parallel")),
        name="attention",
    )(q, k, vt)


def _dwconv_kernel(x_ref, w_ref, b_ref, o_ref, *, width, act, rows):
    seq = x_ref.shape[1]
    pad = width // 2
    nchunks = seq // rows
    w = w_ref[...]
    bias = b_ref[...]

    def body(c, carry):
        r0 = pl.multiple_of(c * rows, rows)
        cur = x_ref[0, pl.ds(r0, rows), :]
        p0 = pl.multiple_of(jnp.maximum(r0 - SUBLANES, 0), SUBLANES)
        n0 = pl.multiple_of(jnp.minimum(r0 + rows, seq - SUBLANES), SUBLANES)
        prev = jnp.where(c > 0, x_ref[0, pl.ds(p0, SUBLANES), :], 0.0)
        nxt = jnp.where(c < nchunks - 1, x_ref[0, pl.ds(n0, SUBLANES), :], 0.0)
        ext = jnp.concatenate([prev, cur, nxt], axis=0)
        acc = bias + ext[SUBLANES - pad:SUBLANES - pad + rows] * w[0:1]
        for kk in range(1, width):
            s0 = SUBLANES - pad + kk
            acc = acc + ext[s0:s0 + rows] * w[kk:kk + 1]
        if act:
            acc = acc * jax.nn.sigmoid(acc)
        o_ref[0, pl.ds(r0, rows), :] = acc
        return carry

    lax.fori_loop(0, nchunks, body, 0)


def dwconv(x, w, bias, col0, ncols, act, rows):
    b, seq, _ = x.shape
    width = w.shape[0]
    cb0 = col0 // LANES
    return pl.pallas_call(
        functools.partial(_dwconv_kernel, width=width, act=act, rows=min(rows, seq)),
        grid=(b, ncols // LANES),
        in_specs=[pl.BlockSpec((1, seq, LANES), lambda bi, ci: (bi, 0, ci + cb0)),
                  pl.BlockSpec((width, LANES), lambda bi, ci: (0, ci)),
                  pl.BlockSpec((1, LANES), lambda bi, ci: (0, ci))],
        out_specs=pl.BlockSpec((1, seq, LANES), lambda bi, ci: (bi, 0, ci)),
        out_shape=jax.ShapeDtypeStruct((b, seq, ncols), F32),
        compiler_params=_cparams(("parallel", "parallel")),
        name="dwconv",
    )(x, w, bias)


def _hyfilt_kernel(mult_ref, w1_ref, b1_ref, fr_ref, w2_ref, b2_ref, w3_ref, dec_ref, bwd_ref, o_ref, *, seq, rows):
    i = pl.program_id(0)
    ridx = lax.broadcasted_iota(I32, (rows, LANES), 0) + i * rows
    lane = lax.broadcasted_iota(I32, (rows, LANES), 1)
    t = ridx.astype(F32) / seq
    ang = t * mult_ref[...]
    feats = jnp.where(lane == 0, t, jnp.where(lane <= HY_BANDS, jnp.sin(ang), jnp.cos(ang)))
    feats = jnp.where(lane < 1 + 2 * HY_BANDS, feats, 0.0)
    fr = fr_ref[...]
    hdn = jnp.sin(fr * (_dot(feats, w1_ref[...], HIGHEST) + b1_ref[...]))
    hdn = jnp.sin(fr * (_dot(hdn, w2_ref[...], HIGHEST) + b2_ref[...]))
    hh, hl = _split_bf16(hdn)
    wh, wl = _split_bf16(w3_ref[...])
    filt = (_dot(hh, wh) + _dot(hl, wh)) + _dot(hh, wl)
    window = jnp.exp(-t[:, 0:1] * jnp.abs(dec_ref[...]))
    out = filt * window
    keep = jnp.logical_or(ridx[:, 0:1] > 0, bwd_ref[...] < 0.5)
    o_ref[...] = jnp.where(keep, out, 0.0)


def hyena_filters(seq, mult, w1p, b1, fr, w2, b2, w3, dec, bwd_mask, rows):
    ncol = w3.shape[1]
    rows = min(rows, seq)
    args = (mult, w1p, b1, fr, w2, b2, w3, dec, bwd_mask)
    return pl.pallas_call(
        functools.partial(_hyfilt_kernel, seq=seq, rows=rows),
        grid=(seq // rows,),
        in_specs=[_full(a.shape) for a in args],
        out_specs=pl.BlockSpec((rows, ncol), lambda i: (i, 0)),
        out_shape=jax.ShapeDtypeStruct((seq, ncol), F32),
        compiler_params=_cparams(("parallel",)),
        name="hyena_filters",
    )(*args)


def _leftmm_kernel(m_ref, x_ref, o_ref):
    o_ref[...] = _dot3(m_ref[...], x_ref[...])


def leftmm(m, x2d, cb):
    r, kdim = m.shape
    n = x2d.shape[1]
    cb = min(cb, n)
    return pl.pallas_call(
        _leftmm_kernel,
        grid=(n // cb,),
        in_specs=[_full((r, kdim)), pl.BlockSpec((kdim, cb), lambda i: (0, i))],
        out_specs=pl.BlockSpec((r, cb), lambda i: (0, i)),
        out_shape=jax.ShapeDtypeStruct((r, n), F32),
        compiler_params=_cparams(("parallel",)),
        name="dft_outer",
    )(m, x2d)


def _stack3_lhs(f):
    hi, lo = _split_bf16(f)
    return jnp.concatenate([hi, hi, lo], axis=1)


def _stack3_rhs(a):
    hi, lo = _split_bf16(a)
    return jnp.concatenate([hi, lo, hi], axis=0)


def _twiddle(ar, ai, twr, twi, conj):
    if conj:
        return ar * twr + ai * twi, ai * twr - ar * twi
    return ar * twr - ai * twi, ai * twr + ar * twi


def _specfilt_kernel(a_ref, fblk_ref, twr_ref, twi_ref, o_ref, lhs_ref):
    n2 = FFT_N2
    c = a_ref.shape[-1] // 2

    @pl.when((pl.program_id(0) == 0) & (pl.program_id(1) == 0))
    def _():
        lhs_ref[...] = _stack3_lhs(fblk_ref[...])

    for kk in range(a_ref.shape[1]):
        br, bi = _twiddle(a_ref[0, kk], a_ref[1, kk], twr_ref[kk], twi_ref[kk], False)
        x = _dot(lhs_ref[...], _stack3_rhs(jnp.concatenate([br, bi], axis=0)))
        o_ref[0, 0, kk] = x[:n2, :c] + x[:n2, c:]
        o_ref[0, 1, kk] = x[n2:, :c] - x[n2:, c:]


def filter_spectrum(a4, fblk, twr_col, twi_col, k1s):
    _, n1, n2, ctot = a4.shape
    c = HY_WIDTH
    return pl.pallas_call(
        _specfilt_kernel,
        grid=(HY_ORDER, n1 // k1s),
        in_specs=[pl.BlockSpec((2, k1s, n2, 2 * c), lambda o, k: (0, k, 0, o)),
                  _full(fblk.shape),
                  pl.BlockSpec((k1s, n2, 1), lambda o, k: (k, 0, 0)),
                  pl.BlockSpec((k1s, n2, 1), lambda o, k: (k, 0, 0))],
        out_specs=pl.BlockSpec((1, 2, k1s, n2, c), lambda o, k: (o, 0, k, 0, 0)),
        out_shape=jax.ShapeDtypeStruct((HY_ORDER, 2, n1, n2, c), F32),
        scratch_shapes=[pltpu.VMEM((2 * n2, 6 * n2), BF16)],
        compiler_params=_cparams(("arbitrary", "arbitrary")),
        name="filter_spectrum",
    )(a4, fblk, twr_col, twi_col)


def _specmul_kernel(a_ref, k_ref, fblk_ref, fblk_t_ref, twr_ref, twi_ref, o_ref, lhs_ref):
    n2 = FFT_N2

    @pl.when(pl.program_id(0) == 0)
    def _():
        lhs_ref[0] = _stack3_lhs(fblk_ref[...])
        lhs_ref[1] = _stack3_lhs(fblk_t_ref[...])

    for kk in range(a_ref.shape[1]):
        twr, twi = twr_ref[kk], twi_ref[kk]
        br, bi = _twiddle(a_ref[0, kk], a_ref[1, kk], twr, twi, False)
        x = _dot(lhs_ref[0], _stack3_rhs(jnp.concatenate([br, bi], axis=0)))
        xr, xi = x[:n2], x[n2:]
        kr, ki = k_ref[0, 0, kk], k_ref[0, 1, kk]
        p = jnp.concatenate([xr * kr - xi * ki, xr * ki + xi * kr], axis=0)
        y = _dot(lhs_ref[1], _stack3_rhs(p))
        yr, yi = _twiddle(y[:n2], y[n2:], twr, twi, True)
        o_ref[0, kk] = yr
        o_ref[1, kk] = yi


def spectrum_multiply(a4, kspec, order, fblk, fblk_t, twr_col, twi_col, k1s):
    _, n1, n2, c = a4.shape
    return pl.pallas_call(
        _specmul_kernel,
        grid=(n1 // k1s,),
        in_specs=[pl.BlockSpec((2, k1s, n2, c), lambda k: (0, k, 0, 0)),
                  pl.BlockSpec((1, 2, k1s, n2, c), lambda k: (order, 0, k, 0, 0)),
                  _full(fblk.shape), _full(fblk_t.shape),
                  pl.BlockSpec((k1s, n2, 1), lambda k: (k, 0, 0)),
                  pl.BlockSpec((k1s, n2, 1), lambda k: (k, 0, 0))],
        out_specs=pl.BlockSpec((2, k1s, n2, c), lambda k: (0, k, 0, 0)),
        out_shape=jax.ShapeDtypeStruct((2, n1, n2, c), F32),
        scratch_shapes=[pltpu.VMEM((2, 2 * n2, 6 * n2), BF16)],
        compiler_params=_cparams(("arbitrary",)),
        name="spectrum_multiply",
    )(a4, kspec, fblk, fblk_t, twr_col, twi_col)


def _convout_kernel(m_ref, y_ref, v_ref, g_ref, bias_ref, *rest, reps):
    y = _dot3(m_ref[...], y_ref[...])
    bias = jnp.concatenate([bias_ref[...]] * reps, axis=-1)
    out = (y + v_ref[...] * bias) * g_ref[...]
    if len(rest) == 1:
        rest[0][...] = out
    else:
        m1_ref, o_ref, a_ref = rest
        o_ref[...] = out
        a_ref[...] = _dot3(m1_ref[...], out)


def conv_output(m3, y2d, v2d, g2d, bias_row, cb, m1_next=None):
    r, kdim = m3.shape
    n = y2d.shape[1]
    cb = min(cb, n)
    c = bias_row.shape[1]
    col = lambda rows: pl.BlockSpec((rows, cb), lambda i: (0, i))
    in_specs = [_full((r, kdim)), col(kdim), col(r), col(r), _full((1, c))]
    out_specs = [col(r)]
    out_shape = [jax.ShapeDtypeStruct((r, n), F32)]
    args = [m3, y2d, v2d, g2d, bias_row]
    if m1_next is not None:
        in_specs.append(_full(m1_next.shape))
        out_specs.append(col(m1_next.shape[0]))
        out_shape.append(jax.ShapeDtypeStruct((m1_next.shape[0], n), F32))
        args.append(m1_next)
    return pl.pallas_call(
        functools.partial(_convout_kernel, reps=cb // c),
        grid=(n // cb,),
        in_specs=in_specs,
        out_specs=out_specs,
        out_shape=out_shape,
        compiler_params=_cparams(("parallel",)),
        name="dft_outer_inverse",
    )(*args)


def _dft_tables(seq, batch):
    n2 = FFT_N2
    half = seq // n2
    n1 = 2 * half
    n = n1 * n2
    k1 = np.arange(n1, dtype=np.float64)[:, None]
    nn1 = np.arange(half, dtype=np.float64)[None, :]
    th = 2.0 * np.pi * k1 * nn1 / n1
    c1, s1 = np.cos(th), np.sin(th)
    assert batch == 2, "the two batch entries are packed as real / imaginary parts"
    m1 = np.block([[c1, s1], [-s1, c1]])
    m3 = np.block([[c1.T, -s1.T], [s1.T, c1.T]]) / n
    mk = np.concatenate([c1, -s1], axis=0)
    kk2 = np.arange(n2, dtype=np.float64)
    th2 = 2.0 * np.pi * np.outer(kk2, kk2) / n2
    c2, s2 = np.cos(th2), np.sin(th2)
    fblk = np.block([[c2, s2], [-s2, c2]])
    tht = 2.0 * np.pi * np.outer(np.arange(n1, dtype=np.float64), kk2) / n
    twr, twi = np.cos(tht), -np.sin(tht)
    f = lambda a: jnp.asarray(a, F32)
    return dict(m1=f(m1), m3=f(m3), mk=f(mk), fblk=f(fblk), fblk_t=f(fblk.T),
                twr_col=f(twr[:, :, None]), twi_col=f(twi[:, :, None]), n1=n1, half=half)


def _ssd_kernel(xf_ref, bf_ref, cf_ref, dtf_ref, xb_ref, bb_ref, cb_ref, dtb_ref, dtbias_ref, a_ref, tril_ref,
                yf_ref, yb_ref, state_ref):
    q = SSM_CHUNK
    hd = SSM_HEADDIM
    ns = SSM_STATE

    @pl.when(pl.program_id(1) == 0)
    def _():
        state_ref[...] = jnp.zeros_like(state_ref)

    tril = tril_ref[...]
    rows = lax.broadcasted_iota(I32, (q, q), 0)
    cols = lax.broadcasted_iota(I32, (q, q), 1)
    a_row = a_ref[...]
    bias = dtbias_ref[...]

    def direction(x_ref, b_ref, c_ref, dt_ref, y_ref, d):
        dt = jax.nn.softplus(dt_ref[0] + bias)
        dta = dt * a_row
        cs = _dot(tril, dta, HIGHEST)
        ecs = cs - dta
        base = ecs if d else cs
        base_t = jnp.transpose(base)
        total = cs[q - 1:q, :]
        x = x_ref[0]
        ys = []
        for g in range(SSM_GROUPS):
            bm = b_ref[0, :, g * ns:(g + 1) * ns]
            cm = c_ref[0, :, g * ns:(g + 1) * ns]
            cb = _dot_nt(cm.astype(BF16), bm.astype(BF16))
            for hh in range(SSM_HEADS // SSM_GROUPS):
                h = g * (SSM_HEADS // SSM_GROUPS) + hh
                j = d * SSM_HEADS + h
                col = jnp.broadcast_to(base[:, j:j + 1], (q, q))
                coln = col if ns == q else jnp.broadcast_to(base[:, j:j + 1], (q, ns))
                row = base_t[j:j + 1, :]
                tot = total[:, j:j + 1]
                if d == 0:
                    seg = jnp.where(rows >= cols, col - row, -jnp.inf)
                    c_scale = jnp.exp(coln)
                    b_scale = jnp.exp(tot - coln)
                else:
                    seg = jnp.where(cols >= rows, row - col, -jnp.inf)
                    c_scale = jnp.exp(tot - coln)
                    b_scale = jnp.exp(coln)
                scores = cb * jnp.exp(seg)
                xdt = (x[:, h * hd:(h + 1) * hd] * jnp.broadcast_to(dt[:, j:j + 1], (q, hd))).astype(BF16)
                st = state_ref[j]
                y = _dot(scores.astype(BF16), xdt) + _dot_nt((cm * c_scale).astype(BF16), st.astype(BF16))
                state_ref[j] = st * jnp.exp(tot) + _dot_tn(xdt, (bm * b_scale).astype(BF16))
                ys.append(y)
        y_ref[0] = jnp.concatenate(ys, axis=-1)

    direction(xf_ref, bf_ref, cf_ref, dtf_ref, yf_ref, 0)
    direction(xb_ref, bb_ref, cb_ref, dtb_ref, yb_ref, 1)


def ssd_scan(xbc, dt_raw, dtbias_row, a_row, tril):
    b, seq, _ = xbc.shape
    q = SSM_CHUNK
    nc = seq // q
    w = SSM_WIDTH
    fwd = lambda col: pl.BlockSpec((1, q, w), lambda bi, i: (bi, i, col))
    bwd = lambda col: pl.BlockSpec((1, q, w), lambda bi, i: (bi, nc - 1 - i, col))
    return pl.pallas_call(
        _ssd_kernel,
        grid=(b, nc),
        in_specs=[fwd(0), fwd(1), fwd(2), pl.BlockSpec((1, q, LANES), lambda bi, i: (bi, i, 0)),
                  bwd(0), bwd(1), bwd(2), pl.BlockSpec((1, q, LANES), lambda bi, i: (bi, nc - 1 - i, 0)),
                  _full((1, LANES)), _full((1, LANES)), _full((q, q))],
        out_specs=[pl.BlockSpec((1, q, w), lambda bi, i: (bi, i, 0)),
                   pl.BlockSpec((1, q, w), lambda bi, i: (bi, nc - 1 - i, 0))],
        out_shape=[jax.ShapeDtypeStruct((b, seq, w), F32)] * 2,
        scratch_shapes=[pltpu.VMEM((2 * SSM_HEADS, SSM_HEADDIM, SSM_STATE), F32)],
        compiler_params=_cparams(("parallel", "arbitrary")),
        name="ssd_scan",
    )(xbc, xbc, xbc, dt_raw, xbc, xbc, xbc, dt_raw, dtbias_row, a_row, tril)


def _outproj_kernel(om_ref, hy_ref, yf_ref, yb_ref, xs_ref, z_ref, x_ref, gm_ref, gh_ref, dsk_ref, gs_ref, wout_ref,
                    gffn_ref, rt_ref, x1_ref, hffn_ref, aff_ref):
    o1 = _rms(om_ref[...], gm_ref[...])
    o2 = _rms(hy_ref[...], gh_ref[...])
    z = z_ref[...]
    y = (yf_ref[...] + yb_ref[...] + xs_ref[...] * dsk_ref[...]) * (z * jax.nn.sigmoid(z))
    gw = SSM_WIDTH // SSM_GROUPS
    gs = gs_ref[...]
    o3 = jnp.concatenate([_rms(y[:, g * gw:(g + 1) * gw], gs[:, g * gw:(g + 1) * gw]) for g in range(SSM_GROUPS)],
                         axis=-1)
    mix = jnp.concatenate([o1, o2, o3], axis=-1).astype(BF16)
    x1 = x_ref[...] + _dot(mix, wout_ref[...])
    x1_ref[...] = x1
    hf = _rms(x1, gffn_ref[...])
    hi, lo = _split_bf16(hf)
    hffn_ref[...] = hi
    logits = _dot_nt(rt_ref[...], jnp.concatenate([hi, lo, hi], axis=1))
    mx = jnp.max(logits, axis=0, keepdims=True)
    ex = jnp.exp(logits - mx)
    aff_ref[0] = ex / jnp.sum(ex, axis=0, keepdims=True)


def outproj(om, hy, yf, yb, xbc, z, x2d, gm, gh, dsk, gs, wout, gffn, router_t, batch, tm):
    t = x2d.shape[0]
    seq = t // batch
    nb = seq // tm
    row = lambda w: pl.BlockSpec((tm, w), lambda i: (i, 0))
    return pl.pallas_call(
        _outproj_kernel,
        grid=(t // tm,),
        in_specs=[row(om.shape[1]), row(HY_WIDTH), row(SSM_WIDTH), row(SSM_WIDTH), row(SSM_WIDTH), row(SSM_WIDTH),
                  row(D_MODEL), _full(gm.shape), _full(gh.shape), _full(dsk.shape), _full(gs.shape),
                  _full(wout.shape), _full(gffn.shape), _full(router_t.shape)],
        out_specs=[row(D_MODEL), row(D_MODEL),
                   pl.BlockSpec((1, N_EXPERTS, tm), lambda i: (i // nb, 0, i % nb))],
        out_shape=[jax.ShapeDtypeStruct((t, D_MODEL), F32), jax.ShapeDtypeStruct((t, D_MODEL), BF16),
                   jax.ShapeDtypeStruct((batch, N_EXPERTS, seq), F32)],
        compiler_params=_cparams(("parallel",)),
        name="outproj_router",
    )(om, hy, yf, yb, xbc, z, x2d, gm, gh, dsk, gs, wout, gffn, router_t)


def _select_kernel(aff_ref, tri_ref, ones_ref, blk_ref, pos_ref, g_ref, off_ref, *, cap, nrows):
    aff = aff_ref[0]
    er = aff.shape[0]
    ne = er // nrows
    aff3 = aff.reshape(ne, nrows, LANES)
    capf = jnp.float32(cap)

    def count(mask3):
        return jnp.sum(jnp.where(mask3, 1.0, 0.0), axis=(1, 2), keepdims=True)

    def enough(cand):
        return count(aff3 >= cand) >= capf

    top = jnp.full((ne, 1, 1), 2.0, F32)
    for shift in (64, 32, 16, 8, 4, 2, 1):
        cand = top * (2.0 ** -shift)
        top = jnp.where(enough(cand), top, cand)
    p = top * 0.5

    def refine(_, carry):
        lo, step = carry
        cand = lo + step
        return jnp.where(enough(cand), cand, lo), step * 0.5

    lo, _ = lax.fori_loop(0, MANTISSA_STEPS, refine, (p, p * 0.5))
    thr = jnp.min(jnp.where(aff3 >= lo, aff3, jnp.inf), axis=(1, 2), keepdims=True)
    gt3 = aff3 > thr
    eq3 = aff3 == thr
    need = capf - count(gt3)

    tri = tri_ref[...]
    ones = ones_ref[...]
    blk = blk_ref[...]

    def prefix(maskf):
        mb = maskf.astype(BF16)
        within = _dot(mb, tri)
        rowtot = _dot(mb, ones)
        before = _dot(blk, rowtot.astype(BF16))
        return within + before, before

    eqf = jnp.where(eq3, 1.0, 0.0).reshape(er, LANES)
    tie_incl, _ = prefix(eqf)
    tie_rank = (tie_incl - eqf).reshape(ne, nrows, LANES)
    sel3 = jnp.logical_or(gt3, jnp.logical_and(eq3, tie_rank < need))
    self_ = jnp.where(sel3, 1.0, 0.0).reshape(er, LANES)
    incl, before = prefix(self_)
    sel = self_ > 0.5
    pos_ref[0] = jnp.where(sel, (incl - self_).astype(I32), -1)
    g_ref[0] = jnp.where(sel, aff, 0.0)
    off_ref[0] = before.astype(I32)


def moe_select(aff, cap):
    b, ne, seq = aff.shape
    nrows = seq // LANES
    er = ne * nrows
    tri = jnp.asarray(np.triu(np.ones((LANES, LANES), np.float32)), BF16)
    ones = jnp.ones((LANES, LANES), BF16)
    ridx = np.arange(er)
    blk = (ridx[:, None] // nrows == ridx[None, :] // nrows) & (ridx[None, :] < ridx[:, None])
    blk = jnp.asarray(blk.astype(np.float32), BF16)
    spec = pl.BlockSpec((1, er, LANES), lambda bi: (bi, 0, 0))
    pos, gsel, off = pl.pallas_call(
        functools.partial(_select_kernel, cap=cap, nrows=nrows),
        grid=(b,),
        in_specs=[spec, _full(tri.shape), _full(ones.shape), _full(blk.shape)],
        out_specs=[spec] * 3,
        out_shape=[jax.ShapeDtypeStruct((b, er, LANES), I32), jax.ShapeDtypeStruct((b, er, LANES), F32),
                   jax.ShapeDtypeStruct((b, er, LANES), I32)],
        compiler_params=_cparams(("parallel",)),
        name="moe_select",
    )(aff.reshape(b, er, LANES), tri, ones, blk)
    return pos.reshape(b, ne, seq), gsel.reshape(b, ne, seq), off[:, :, 0].reshape(b, ne, nrows)


def _gather_kernel(offs_ref, h_hbm, pos_ref, g_ref, o_ref, og_ref, h_ref, sem_ref, *, tb, cap):
    bi = pl.program_id(0)
    gi = pl.program_id(1)
    seq, d = h_ref.shape
    ng = o_ref.shape[1]
    al = COMBINE_ALIGN

    @pl.when(gi == 0)
    def _():
        copy = pltpu.make_async_copy(h_hbm.at[bi], h_ref, sem_ref)
        copy.start()
        copy.wait()

    for e in range(ng):
        o_ref[0, e, 0:al, :] = jnp.zeros((al, d), BF16)
        og_ref[0, e, 0:al, :] = jnp.zeros((al, 1), F32)
    wins = sorted({min(64, tb + al), min(128, tb + al), tb + al})

    def body(j, carry):
        t0 = pl.multiple_of(j * tb, tb)
        offs = [pl.multiple_of(offs_ref[bi, gi * ng + e, j] * al, al) for e in range(ng)]
        needs = [offs_ref[bi, gi * ng + e, j + 1] * al + al - offs[e] for e in range(ng)]
        need = functools.reduce(jnp.maximum, needs)

        def place(win):
            riota = lax.broadcasted_iota(I32, (win, tb), 0)
            hits = [riota + offs[e] == pos_ref[0, 0, e:e + 1, pl.ds(t0, tb)] for e in range(ng)]
            onehot = jnp.concatenate([jnp.where(hit, 1.0, 0.0).astype(BF16) for hit in hits], axis=0)
            rows = _dot(onehot, h_ref[pl.ds(t0, tb), :])
            for e in range(ng):
                r = rows[e * win:(e + 1) * win]
                gates = jnp.sum(jnp.where(hits[e], g_ref[0, 0, e:e + 1, pl.ds(t0, tb)], 0.0), axis=1, keepdims=True)
                head = o_ref[0, e, pl.ds(offs[e], al), :].astype(F32) + r[0:al]
                o_ref[0, e, pl.ds(offs[e], al), :] = head.astype(BF16)
                o_ref[0, e, pl.ds(offs[e] + al, win - al), :] = r[al:].astype(BF16)
                og_ref[0, e, pl.ds(offs[e], al), :] += gates[0:al]
                og_ref[0, e, pl.ds(offs[e] + al, win - al), :] = gates[al:]

        lo = 0
        for win in wins:
            fits = need <= win if win != wins[-1] else True
            pl.when(jnp.logical_and(need > lo, fits))(functools.partial(place, win))
            lo = win
        return carry

    lax.fori_loop(0, seq // tb, body, 0)
    tail = o_ref.shape[2] - cap
    o_ref[0, :, cap:, :] = jnp.zeros((ng, tail, d), BF16)
    og_ref[0, :, cap:, :] = jnp.zeros((ng, tail, 1), F32)


def moe_gather(hffn, pos, gsel, offs, cap, tb, ng):
    b, seq, d = hffn.shape
    ne = pos.shape[1]
    rows = cap + tb + COMBINE_ALIGN
    assert cap % COMBINE_ALIGN == 0 and ne % ng == 0
    row_spec = pl.BlockSpec((1, 1, ng, seq), lambda bi, gi, offs: (bi, gi, 0, 0))
    grid_spec = pltpu.PrefetchScalarGridSpec(
        num_scalar_prefetch=1,
        grid=(b, ne // ng),
        in_specs=[pl.BlockSpec(memory_space=pl.ANY), row_spec, row_spec],
        out_specs=[pl.BlockSpec((1, ng, rows, d), lambda bi, gi, offs: (bi, gi, 0, 0)),
                   pl.BlockSpec((1, ng, rows, 1), lambda bi, gi, offs: (bi, gi, 0, 0))],
        scratch_shapes=[pltpu.VMEM((seq, d), BF16), pltpu.SemaphoreType.DMA(())],
    )
    return pl.pallas_call(
        functools.partial(_gather_kernel, tb=tb, cap=cap),
        grid_spec=grid_spec,
        out_shape=[jax.ShapeDtypeStruct((b, ne, rows, d), BF16), jax.ShapeDtypeStruct((b, ne, rows, 1), F32)],
        compiler_params=_cparams(("arbitrary", "arbitrary")),
        name="moe_gather",
    )(offs, hffn, pos.reshape(b, ne // ng, ng, seq), gsel.reshape(b, ne // ng, ng, seq))


def _ffn_kernel(xe_ref, gs_ref, wg_ref, wu_ref, wd_ref, o_ref, acc_ref, *, cap):
    f = pl.program_id(1)
    nb = xe_ref.shape[0]
    d = xe_ref.shape[-1]
    xe = xe_ref[...].reshape(nb * cap, d)
    a = _dot(xe, wg_ref[...].astype(BF16))
    u = _dot(xe, wu_ref[...].astype(BF16))
    hid = (a * jax.nn.sigmoid(a) * u).astype(BF16)
    part = _dot(hid, wd_ref[...].astype(BF16))

    @pl.when(f == 0)
    def _():
        acc_ref[...] = part

    @pl.when(f > 0)
    def _():
        acc_ref[...] += part

    @pl.when(f == pl.num_programs(1) - 1)
    def _():
        gated = acc_ref[...] * gs_ref[...].reshape(nb * cap, 1)
        o_ref[:, 0:cap, :] = gated.reshape(nb, cap, d).astype(BF16)
        o_ref[:, cap:, :] = jnp.zeros((nb, o_ref.shape[1] - cap, d), BF16)


def moe_ffn(xe, gslot, w_gate, w_up, w_down, layer, cap, cap_pad, tf):
    b, ne, _, d = xe.shape
    ff = w_gate.shape[-1]
    return pl.pallas_call(
        functools.partial(_ffn_kernel, cap=cap),
        grid=(ne, ff // tf),
        in_specs=[pl.BlockSpec((b, None, cap, d), lambda e, f: (0, e, 0, 0)),
                  pl.BlockSpec((b, None, cap, 1), lambda e, f: (0, e, 0, 0)),
                  pl.BlockSpec((None, None, d, tf), lambda e, f: (layer, e, 0, f)),
                  pl.BlockSpec((None, None, d, tf), lambda e, f: (layer, e, 0, f)),
                  pl.BlockSpec((None, None, tf, d), lambda e, f: (layer, e, f, 0))],
        out_specs=pl.BlockSpec((b, None, cap_pad, d), lambda e, f: (0, e, 0, 0)),
        out_shape=jax.ShapeDtypeStruct((b, ne, cap_pad, d), BF16),
        scratch_shapes=[pltpu.VMEM((b * cap, d), F32)],
        compiler_params=_cparams(("parallel", "arbitrary")),
        name="moe_ffn",
    )(xe, gslot, w_gate, w_up, w_down)


def _combine_kernel(offs_ref, ye_hbm, pos_ref, x1_ref, p_ref, gple_ref, wgate_ref, wproj_ref, gfin_ref, o_ref,
                    buf_ref, sem_ref, *, tb, win, final):
    bi = pl.program_id(0)
    j = pl.program_id(1)
    nj = pl.num_programs(1)
    ne = pos_ref.shape[1]
    step = bi * nj + j
    slot = step % 2

    def window_copy(b_, j_, e, slot_):
        off = pl.multiple_of(offs_ref[b_, e, j_] * COMBINE_ALIGN, COMBINE_ALIGN)
        return pltpu.make_async_copy(ye_hbm.at[b_, e, pl.ds(off, win), :],
                                     buf_ref.at[slot_, pl.ds(e * win, win), :], sem_ref.at[slot_, e])

    @pl.when(step == 0)
    def _():
        for e in range(ne):
            window_copy(bi, j, e, slot).start()

    @pl.when(step + 1 < pl.num_programs(0) * nj)
    def _():
        wrap = j + 1 == nj
        b_next = jnp.where(wrap, bi + 1, bi)
        j_next = jnp.where(wrap, 0, j + 1)
        for e in range(ne):
            window_copy(b_next, j_next, e, 1 - slot).start()

    def token_major(a):
        pad = jnp.zeros((tb - ne, tb), a.dtype)
        return jnp.transpose(jnp.concatenate([a, pad], axis=0))[:, :ne]

    pos_all = token_major(pos_ref[0])
    lane = lax.broadcasted_iota(I32, (tb, LANES), 1)
    targets = []
    for e in range(ne):
        col = pos_all[:, e:e + 1]
        delta = e * win - offs_ref[bi, e, j] * COMBINE_ALIGN
        targets.append(jnp.broadcast_to(jnp.where(col >= 0, col + delta, -1), (tb, LANES)))
    tiles = []
    for k in range(ne * win // LANES):
        hits = [jnp.where(targets[e] == lane + k * LANES, 1.0, 0.0) for e in range(ne)
                if e * win < (k + 1) * LANES and (e + 1) * win > k * LANES]
        tiles.append(functools.reduce(lambda a, b: a + b, hits).astype(BF16))
    sel = jnp.concatenate(tiles, axis=1)
    for e in range(ne):
        window_copy(bi, j, e, slot).wait()
    x2 = x1_ref[0] + _dot(sel, buf_ref[slot])
    hp = _rms(x2, gple_ref[...]).astype(BF16)
    gt = jax.nn.sigmoid(_dot(hp, wgate_ref[...]))
    x3 = x2 + _dot(p_ref[0].astype(BF16), wproj_ref[...]) * gt
    o_ref[0] = _rms(x3, gfin_ref[...]) if final else x3


def moe_combine(ye, pos, offs, x1, p, layer, gple, wgate, wproj, gfin, final, tb):
    b, ne, _, d = ye.shape
    seq = x1.shape[1]
    win = tb + COMBINE_ALIGN
    assert (ne * win) % LANES == 0
    grid_spec = pltpu.PrefetchScalarGridSpec(
        num_scalar_prefetch=1,
        grid=(b, seq // tb),
        in_specs=[pl.BlockSpec(memory_space=pl.ANY),
                  pl.BlockSpec((1, ne, tb), lambda bi, j, offs: (bi, 0, j)),
                  pl.BlockSpec((1, tb, d), lambda bi, j, offs: (bi, j, 0)),
                  pl.BlockSpec((None, 1, tb, p.shape[-1]), lambda bi, j, offs: (layer, bi, j, 0)),
                  pl.BlockSpec(gple.shape, lambda bi, j, offs: (0, 0)),
                  pl.BlockSpec(wgate.shape, lambda bi, j, offs: (0, 0)),
                  pl.BlockSpec(wproj.shape, lambda bi, j, offs: (0, 0)),
                  pl.BlockSpec(gfin.shape, lambda bi, j, offs: (0, 0))],
        out_specs=pl.BlockSpec((1, tb, d), lambda bi, j, offs: (bi, j, 0)),
        scratch_shapes=[pltpu.VMEM((2, ne * win, d), BF16), pltpu.SemaphoreType.DMA((2, ne))],
    )
    return pl.pallas_call(
        functools.partial(_combine_kernel, tb=tb, win=win, final=final),
        grid_spec=grid_spec,
        out_shape=jax.ShapeDtypeStruct((b, seq, d), F32),
        compiler_params=_cparams(("arbitrary", "arbitrary")),
        name="moe_combine_ple",
    )(offs, ye, pos, x1, p, gple, wgate, wproj, gfin)


def _pad_cols(a, width):
    return jnp.pad(a, ((0, 0), (0, width - a.shape[1])))


def _pack_inproj(w_in):
    offs = np.cumsum((0,) + IN_SPLITS)
    cq, ckv, kr, hy, z, xbc, dt = [w_in[:, offs[i]:offs[i + 1]] for i in range(len(IN_SPLITS))]
    d = w_in.shape[0]
    half = MLA_ROPE // 2
    zeros = lambda n: jnp.zeros((d, n), w_in.dtype)
    kr_pad = jnp.concatenate([zeros(MLA_NOPE), kr, zeros(HEAD_PAD - MLA_NOPE - MLA_ROPE)], axis=1)
    kr_swap = jnp.concatenate([zeros(MLA_NOPE), -kr[:, half:], kr[:, :half], zeros(HEAD_PAD - MLA_NOPE - MLA_ROPE)],
                              axis=1)
    wall = jnp.concatenate([cq, ckv, kr_pad, kr_swap, hy, z, xbc, _pad_cols(dt, LANES)], axis=1)
    assert wall.shape[1] == _C_END
    return wall.astype(BF16)


def _pack_mla(w_uq, w_ukv):
    lq = w_uq.shape[0]
    lkv = w_ukv.shape[0]
    half = MLA_ROPE // 2
    padw = HEAD_PAD - MLA_NOPE - MLA_ROPE
    q3 = w_uq.reshape(lq, MLA_HEADS, MLA_NOPE + MLA_ROPE)
    nope, rope = q3[..., :MLA_NOPE], q3[..., MLA_NOPE:]
    zq = jnp.zeros((lq, MLA_HEADS, padw), w_uq.dtype)
    wq = jnp.concatenate([nope, rope, zq], axis=-1).reshape(lq, MLA_HEADS * HEAD_PAD)
    wqs = jnp.concatenate([jnp.zeros_like(nope), -rope[..., half:], rope[..., :half], zq], axis=-1)
    wqs = wqs.reshape(lq, MLA_HEADS * HEAD_PAD)
    kv3 = w_ukv.reshape(lkv, MLA_HEADS, MLA_NOPE + MLA_V)
    knope, vv = kv3[..., :MLA_NOPE], kv3[..., MLA_NOPE:]
    wk = jnp.concatenate([knope, jnp.zeros((lkv, MLA_HEADS, HEAD_PAD - MLA_NOPE), w_ukv.dtype)], axis=-1)
    wk = wk.reshape(lkv, MLA_HEADS * HEAD_PAD)
    wv = vv.reshape(lkv, MLA_HEADS * MLA_V).T
    return wq.astype(BF16), wqs.astype(BF16), wk.astype(BF16), wv.astype(BF16)


def _row(a):
    return a.reshape(1, -1).astype(F32)


TM_PROJ = 512
TQ_ATTN = 1024
TK_ATTN = 512
ROWS_CONV = 512
ROWS_FILT = 512
CB_DFT = 2048
TM_OUT = 512
K1_PER_STEP = 16
TB_MOE = 256
TB_COMBINE = 128
GATHER_GROUP = 4
TF_FFN = 512


def _hyena(hy_u, tabs, kspec, conv_w, conv_b, bias):
    b, seq, _ = hy_u.shape
    c = HY_WIDTH
    parts = [dwconv(hy_u, conv_w[:, i * c:(i + 1) * c], _row(conv_b[i * c:(i + 1) * c]), i * c, c, False, ROWS_CONV)
             for i in range(HY_ORDER + 1)]
    gates, v = parts[:-1], parts[-1]
    half, n1 = tabs["half"], tabs["n1"]
    flat = lambda a: a.reshape(b * half, FFT_N2 * c)
    a = leftmm(tabs["m1"], flat(v), CB_DFT)
    for o in range(HY_ORDER):
        y = spectrum_multiply(a.reshape(2, n1, FFT_N2, c), kspec, o, tabs["fblk"], tabs["fblk_t"], tabs["twr_col"],
                              tabs["twi_col"], K1_PER_STEP)
        last = o == HY_ORDER - 1
        outs = conv_output(tabs["m3"], y.reshape(2 * n1, FFT_N2 * c), flat(v), flat(gates[o]), _row(bias[o]), CB_DFT,
                           None if last else tabs["m1"])
        v = outs[0].reshape(b, seq, c)
        a = None if last else outs[1]
    return v


def _hyena_kspec(seq, tabs, w1, b1, freq, w2, b2, w3, decay):
    bands = np.arange(1, HY_BANDS + 1, dtype=np.float64) * 2.0 * np.pi
    mult = np.zeros((1, LANES), np.float32)
    mult[0, 1:1 + HY_BANDS] = bands
    mult[0, 1 + HY_BANDS:1 + 2 * HY_BANDS] = bands
    w1p = jnp.pad(w1.astype(F32), ((0, LANES - w1.shape[0]), (0, 0)))
    ncol = HY_ORDER * 2 * HY_WIDTH
    bwd = (np.arange(ncol) // HY_WIDTH) % 2
    kf = hyena_filters(seq, jnp.asarray(mult), w1p, _row(b1), _row(freq), w2.astype(F32), _row(b2), w3.astype(F32),
                       _row(decay), jnp.asarray(bwd.astype(np.float32)).reshape(1, ncol), ROWS_FILT)
    a = leftmm(tabs["mk"], kf.reshape(tabs["half"], FFT_N2 * ncol), CB_DFT)
    a4 = a.reshape(2, tabs["n1"], FFT_N2, ncol)
    return filter_spectrum(a4, tabs["fblk"], tabs["twr_col"], tabs["twi_col"], K1_PER_STEP)


def kernel(x, p, positions, norm_mix, w_in, mla_q_norm, mla_w_uq, mla_kv_norm, mla_w_ukv, mla_out_norm, hy_conv_w,
           hy_conv_b, hy_filt_w1, hy_filt_b1, hy_filt_freq, hy_filt_w2, hy_filt_b2, hy_filt_w3, hy_decay, hy_bias,
           hy_out_norm, ssm_conv_w, ssm_conv_b, ssm_dt_bias, ssm_a_log, ssm_d, ssm_norm, w_out, norm_ffn, moe_router,
           moe_w_gate, moe_w_up, moe_w_down, ple_norm, ple_gate_w, ple_proj, final_norm_g):
    batch, seq, d = x.shape
    depth = w_in.shape[0]
    t = batch * seq
    cap = EC_CAPACITY_FACTOR * seq // N_EXPERTS
    tb = min(TB_MOE, seq)
    tbc = min(TB_COMBINE, seq)
    cap_pad = cap + tb
    tm_proj = min(TM_PROJ, seq)
    tm_out = min(TM_OUT, seq)

    freq = np.zeros((1, HEAD_PAD), np.float32)
    inv = ROPE_THETA ** (-np.arange(0, MLA_ROPE, 2, dtype=np.float32) / MLA_ROPE)
    freq[0, MLA_NOPE:MLA_NOPE + MLA_ROPE // 2] = inv
    freq[0, MLA_NOPE + MLA_ROPE // 2:MLA_NOPE + MLA_ROPE] = inv
    cos_t, sin_t = rope_tables(positions.reshape(t, 1), jnp.asarray(freq), tm_proj)

    tabs = _dft_tables(seq, batch)
    tril = jnp.asarray(np.tril(np.ones((SSM_CHUNK, SSM_CHUNK), np.float32)))

    x2d = x.reshape(t, d)
    for i in range(depth):
        wall = _pack_inproj(w_in[i])
        wq, wqs, wk, wv = _pack_mla(mla_w_uq[i], mla_w_ukv[i])
        q, k, v, hy_u, z, xbc_raw, dt_raw = inproj(x2d, _row(norm_mix[i]), wall, _row(mla_q_norm[i]), wq, wqs,
                                                    _row(mla_kv_norm[i]), wk, wv, cos_t, sin_t, tm_proj)
        o_mla = attention(q.reshape(batch, seq, -1), k.reshape(batch, seq, -1), v, min(TQ_ATTN, seq),
                          min(TK_ATTN, seq))

        kspec = _hyena_kspec(seq, tabs, hy_filt_w1[i], hy_filt_b1[i], hy_filt_freq[i], hy_filt_w2[i], hy_filt_b2[i],
                             hy_filt_w3[i], hy_decay[i])
        o_hy = _hyena(hy_u.reshape(batch, seq, -1), tabs, kspec, hy_conv_w[i], hy_conv_b[i], hy_bias[i])

        xbc = dwconv(xbc_raw.reshape(batch, seq, -1), ssm_conv_w[i], _row(ssm_conv_b[i]), 0, SSM_CONV_DIM, True,
                     ROWS_CONV)
        dtbias_row = _pad_cols(_row(ssm_dt_bias[i]), LANES)
        a_row = _pad_cols(_row(-jnp.exp(ssm_a_log[i].astype(F32))), LANES)
        y_f, y_b = ssd_scan(xbc, dt_raw.reshape(batch, seq, -1), dtbias_row, a_row, tril)

        dsk = _row(jnp.repeat(ssm_d[i].astype(F32), SSM_HEADDIM))
        x1, hffn, aff = outproj(o_mla.reshape(t, -1), o_hy.reshape(t, -1), y_f.reshape(t, -1), y_b.reshape(t, -1),
                                xbc.reshape(t, -1), z, x2d, _row(mla_out_norm[i]), _row(hy_out_norm[i]), dsk,
                                _row(ssm_norm[i]), w_out[i].astype(BF16), _row(norm_ffn[i]),
                                _stack3_lhs(moe_router[i].astype(F32).T), batch, tm_out)

        pos, gsel, rowoff = moe_select(aff, cap)
        offs_g = jnp.concatenate([rowoff[:, :, ::tb // LANES] // COMBINE_ALIGN,
                                  jnp.full((batch, N_EXPERTS, 1), pl.cdiv(cap, COMBINE_ALIGN), I32)], axis=-1)
        offs_c = rowoff[:, :, ::tbc // LANES] // COMBINE_ALIGN
        xe, gslot = moe_gather(hffn.reshape(batch, seq, d), pos, gsel, offs_g, cap, tb, GATHER_GROUP)
        ye = moe_ffn(xe, gslot, moe_w_gate, moe_w_up, moe_w_down, i, cap, cap_pad, TF_FFN)
        x3 = moe_combine(ye, pos, offs_c, x1.reshape(batch, seq, d), p, i, _row(ple_norm[i]),
                         ple_gate_w[i].astype(BF16), ple_proj[i].astype(BF16), _row(final_norm_g), i == depth - 1, tbc)
        x2d = x3.reshape(t, d)
    return x2d.reshape(batch, seq, d)
```
